```python
import math
import jax, jax.numpy as jnp
from jax import lax
import numpy as np

D_MODEL = 1024
BATCH = 2
SEQ = 8192
DEPTH = 2
DEC_BATCH = 32
DEC_SEQ = 2048
PAST_LEN = 128

SSM_GROUP = 16
SSM_WIDTH = D_MODEL // 2
SSM_GROUPS = SSM_WIDTH // SSM_GROUP
SSM_STATE = 64
DT_MIN = 1e-3
DT_MAX = 1e-1
MLA_HEADS = 8
MLA_NOPE = 64
MLA_ROPE = 32
MLA_V = 64
MLA_QK = MLA_NOPE + MLA_ROPE
MLA_WIDTH = MLA_HEADS * MLA_V
Q_LORA = D_MODEL // 4
KV_LORA = D_MODEL // 8
ROPE_BASE = 10000.0
Q_BLOCK = 128
MEM_TOKENS = 256
MEM_HEADS = 4
MEM_HEAD_DIM = 128
MEM_WIDTH = MEM_HEADS * MEM_HEAD_DIM
N_BRANCH = 3
BRANCH_WIDTH = 512
D_FF = 2816
N_EXPERTS = 8
TOP_K = 2
D_FF_EXPERT = 3584
N_DENSE = (DEPTH + 1) // 2
N_MOE = DEPTH // 2
EPS = 1e-6

IN_SIZES = (SSM_WIDTH, Q_LORA, KV_LORA, MLA_ROPE, MEM_WIDTH, N_BRANCH * D_MODEL)
IN_OFFSETS = tuple(int(v) for v in np.cumsum(IN_SIZES)[:-1])
N_IN = int(sum(IN_SIZES))

kernel_name = "hybrid_s5_mla_memory_encoder"


def rms_norm(x, g):
    xf = x.astype(jnp.float32)
    y = xf * lax.rsqrt(jnp.mean(xf * xf, axis=-1, keepdims=True) + EPS)
    return (y * g.astype(jnp.float32)).astype(x.dtype)


def rope_tables(seq_len):
    inv = 1.0 / (ROPE_BASE ** (jnp.arange(0, MLA_ROPE, 2, dtype=jnp.float32) / MLA_ROPE))
    ang = jnp.arange(seq_len, dtype=jnp.float32)[:, None] * inv[None, :]
    return jnp.cos(ang), jnp.sin(ang)


def apply_rope(x, cos, sin):
    x1, x2 = jnp.split(x, 2, axis=-1)
    c = cos.astype(x.dtype)
    s = sin.astype(x.dtype)
    return jnp.concatenate([x1 * c - x2 * s, x1 * s + x2 * c], axis=-1)


def _ssm_combine(left, right):
    a_l, b_l = left
    a_r, b_r = right
    return a_l * a_r, a_r * b_l + b_r


def s5_bidirectional(u, a_re, a_im, log_dt, b_re, b_im, c_re, c_im, d_skip, w_glu, b_glu):
    f32 = jnp.float32
    bsz, seq, _ = u.shape
    uf = u.astype(f32)
    ug = uf.reshape(bsz, seq, SSM_GROUPS, SSM_GROUP).astype(jnp.complex64)
    lam = lax.complex(a_re.astype(f32), a_im.astype(f32))
    dt = jnp.exp(log_dt.astype(f32))[..., None]
    lam_bar = jnp.exp(lam * dt)
    b_bar = ((lam_bar - 1.0) / lam)[..., None] * lax.complex(b_re.astype(f32), b_im.astype(f32))
    c_mat = lax.complex(c_re.astype(f32), c_im.astype(f32))
    y = d_skip.astype(f32) * uf
    for direction in range(2):
        u_dir = ug if direction == 0 else jnp.flip(ug, axis=1)
        bu = jnp.einsum('gpn,blgn->blgp', b_bar[direction], u_dir)
        a = jnp.broadcast_to(lam_bar[direction], (1, seq, SSM_GROUPS, SSM_STATE))
        _, states = lax.associative_scan(_ssm_combine, (a, bu), axis=1)
        out = jnp.einsum('gnp,blgp->blgn', c_mat[direction], states).real
        if direction == 1:
            out = jnp.flip(out, axis=1)
        y = y + out.reshape(bsz, seq, SSM_WIDTH)
    z = jax.nn.gelu(y).astype(u.dtype) @ w_glu + b_glu
    val, gate = jnp.split(z, 2, axis=-1)
    return val * jax.nn.sigmoid(gate)


def mla_attention(c_q, c_kv, k_rope, g_q_lora, w_uq, g_kv_lora, w_ukv, g_q, g_k, cos, sin):
    bsz, seq, _ = c_q.shape
    q = (rms_norm(c_q, g_q_lora) @ w_uq).reshape(bsz, seq, MLA_HEADS, MLA_QK)
    kv = (rms_norm(c_kv, g_kv_lora) @ w_ukv).reshape(bsz, seq, MLA_HEADS, MLA_NOPE + MLA_V)
    k_nope, v = kv[..., :MLA_NOPE], kv[..., MLA_NOPE:]
    k_r = jnp.broadcast_to(k_rope[:, :, None, :], (bsz, seq, MLA_HEADS, MLA_ROPE))
    k = jnp.concatenate([k_nope, k_r], axis=-1)
    q = rms_norm(q, g_q)
    k = rms_norm(k, g_k)
    cos_h, sin_h = cos[:, None, :], sin[:, None, :]
    q = jnp.concatenate([q[..., :MLA_NOPE], apply_rope(q[..., MLA_NOPE:], cos_h, sin_h)], axis=-1)
    k = jnp.concatenate([k[..., :MLA_NOPE], apply_rope(k[..., MLA_NOPE:], cos_h, sin_h)], axis=-1)
    scale = MLA_QK ** -0.5
    n_blk = seq // Q_BLOCK
    q_blocks = q.reshape(bsz, n_blk, Q_BLOCK, MLA_HEADS, MLA_QK).transpose(1, 0, 2, 3, 4)

    def attend(qb):
        s = jnp.einsum('bqhd,bkhd->bhqk', qb, k).astype(jnp.float32) * scale
        p = jax.nn.softmax(s, axis=-1).astype(v.dtype)
        return jnp.einsum('bhqk,bkhd->bqhd', p, v)

    out = lax.map(attend, q_blocks)
    return out.transpose(1, 0, 2, 3, 4).reshape(bsz, seq, MLA_WIDTH)


def memory_attention(q_in, mem, g_mem, w_mem_kv, g_q, g_k):
    bsz, seq, _ = q_in.shape
    n_mem = mem.shape[1]
    q = rms_norm(q_in.reshape(bsz, seq, MEM_HEADS, MEM_HEAD_DIM), g_q)
    kv = rms_norm(mem, g_mem) @ w_mem_kv
    k, v = jnp.split(kv, 2, axis=-1)
    k = rms_norm(k.reshape(bsz, n_mem, MEM_HEADS, MEM_HEAD_DIM), g_k)
    v = v.reshape(bsz, n_mem, MEM_HEADS, MEM_HEAD_DIM)
    s = jnp.einsum('bqhd,bkhd->bhqk', q, k).astype(jnp.float32) * (MEM_HEAD_DIM ** -0.5)
    p = jax.nn.softmax(s, axis=-1).astype(v.dtype)
    return jnp.einsum('bhqk,bkhd->bqhd', p, v).reshape(bsz, seq, MEM_WIDTH)


def swiglu(h, w13, w2):
    g, u = jnp.split(h @ w13, 2, axis=-1)
    return (jax.nn.silu(g) * u) @ w2


def moe_swiglu(h, w_router, w13, w2):
    logits = (h @ w_router).astype(jnp.float32)
    top_val, top_idx = lax.top_k(logits, TOP_K)
    top_w = jax.nn.softmax(top_val, axis=-1)
    combine = jnp.sum(jax.nn.one_hot(top_idx, N_EXPERTS, dtype=jnp.float32) * top_w[..., None], axis=-2)
    out = jnp.zeros_like(h)
    for e in range(N_EXPERTS):
        out = out + combine[..., e:e + 1].astype(h.dtype) * swiglu(h, w13[e], w2[e])
    return out


def setup_inputs(seed: int = 0) -> dict:
    key = jax.random.key(seed)
    ks = jax.random.split(key, 40)
    f32 = jnp.float32

    def nrm(k, shape, scale):
        return jax.random.normal(k, shape, f32) * scale

    def gain(k, shape):
        return 1.0 + 0.01 * jax.random.normal(k, shape, f32)

    G, P, N = SSM_GROUPS, SSM_STATE, SSM_GROUP
    a_im_init = jnp.pi * jnp.arange(P, dtype=f32)
    return {
        "x_prompt": nrm(ks[0], (BATCH, SEQ, D_MODEL), 1.0),
        "x_sample": nrm(ks[1], (DEC_BATCH, DEC_SEQ, D_MODEL), 1.0),
        "mem_prompt": nrm(ks[2], (BATCH, MEM_TOKENS, D_MODEL), 1.0),
        "mem_sample": nrm(ks[3], (DEC_BATCH, MEM_TOKENS, D_MODEL), 1.0),
        "g_mix": gain(ks[4], (DEPTH, D_MODEL)),
        "w_in": nrm(ks[5], (DEPTH, D_MODEL, N_IN), D_MODEL ** -0.5),
        "ssm_a_re": -0.5 + 0.01 * jax.random.normal(ks[6], (DEPTH, 2, G, P), f32),
        "ssm_a_im": a_im_init + 0.01 * jax.random.normal(ks[7], (DEPTH, 2, G, P), f32),
        "ssm_log_dt": jax.random.uniform(ks[8], (DEPTH, 2, G), f32, math.log(DT_MIN), math.log(DT_MAX)),
        "ssm_b_re": nrm(ks[9], (DEPTH, 2, G, P, N), (2.0 * N) ** -0.5),
        "ssm_b_im": nrm(ks[10], (DEPTH, 2, G, P, N), (2.0 * N) ** -0.5),
        "ssm_c_re": nrm(ks[11], (DEPTH, 2, G, N, P), (2.0 * P) ** -0.5),
        "ssm_c_im": nrm(ks[12], (DEPTH, 2, G, N, P), (2.0 * P) ** -0.5),
        "ssm_d": nrm(ks[13], (DEPTH, SSM_WIDTH), 1.0),
        "ssm_w_glu": nrm(ks[14], (DEPTH, SSM_WIDTH, 2 * SSM_WIDTH), SSM_WIDTH ** -0.5),
        "ssm_b_glu": nrm(ks[15], (DEPTH, 2 * SSM_WIDTH), 0.01),
        "g_q_lora": gain(ks[16], (DEPTH, Q_LORA)),
        "w_uq": nrm(ks[17], (DEPTH, Q_LORA, MLA_HEADS * MLA_QK), Q_LORA ** -0.5),
        "g_kv_lora": gain(ks[18], (DEPTH, KV_LORA)),
        "w_ukv": nrm(ks[19], (DEPTH, KV_LORA, MLA_HEADS * (MLA_NOPE + MLA_V)), KV_LORA ** -0.5),
        "g_mla_q": gain(ks[20], (DEPTH, MLA_QK)),
        "g_mla_k": gain(ks[21], (DEPTH, MLA_QK)),
        "g_mem": gain(ks[22], (DEPTH, D_MODEL)),
        "w_mem_kv": nrm(ks[23], (DEPTH, D_MODEL, 2 * MEM_WIDTH), D_MODEL ** -0.5),
        "g_mem_q": gain(ks[24], (DEPTH, MEM_HEAD_DIM)),
        "g_mem_k": gain(ks[25], (DEPTH, MEM_HEAD_DIM)),
        "w_branch": nrm(ks[26], (DEPTH, N_BRANCH, BRANCH_WIDTH, D_MODEL), BRANCH_WIDTH ** -0.5),
        "w_out": nrm(ks[27], (DEPTH, D_MODEL, D_MODEL), D_MODEL ** -0.5),
        "g_ffn": gain(ks[28], (DEPTH, D_MODEL)),
        "ffn_w13": nrm(ks[29], (N_DENSE, D_MODEL, 2 * D_FF), D_MODEL ** -0.5),
        "ffn_w2": nrm(ks[30], (N_DENSE, D_FF, D_MODEL), D_FF ** -0.5),
        "moe_router": nrm(ks[31], (N_MOE, D_MODEL, N_EXPERTS), D_MODEL ** -0.5),
        "moe_w13": nrm(ks[32], (N_MOE, N_EXPERTS, D_MODEL, 2 * D_FF_EXPERT), D_MODEL ** -0.5),
        "moe_w2": nrm(ks[33], (N_MOE, N_EXPERTS, D_FF_EXPERT, D_MODEL), D_FF_EXPERT ** -0.5),
    }


def reference(x_prompt, x_sample, mem_prompt, mem_sample, g_mix, w_in, ssm_a_re, ssm_a_im,
              ssm_log_dt, ssm_b_re, ssm_b_im, ssm_c_re, ssm_c_im, ssm_d, ssm_w_glu, ssm_b_glu,
              g_q_lora, w_uq, g_kv_lora, w_ukv, g_mla_q, g_mla_k, g_mem, w_mem_kv, g_mem_q,
              g_mem_k, w_branch, w_out, g_ffn, ffn_w13, ffn_w2, moe_router, moe_w13, moe_w2):

    def trunk(x, mem):
        bsz, seq, _ = x.shape
        cos, sin = rope_tables(seq)
        for i in range(DEPTH):
            h = rms_norm(x, g_mix[i])
            u, c_q, c_kv, k_rope, q_mem, gates = jnp.split(h @ w_in[i], IN_OFFSETS, axis=-1)
            y_ssm = s5_bidirectional(u, ssm_a_re[i], ssm_a_im[i], ssm_log_dt[i], ssm_b_re[i],
                                     ssm_b_im[i], ssm_c_re[i], ssm_c_im[i], ssm_d[i],
                                     ssm_w_glu[i], ssm_b_glu[i])
            y_mla = mla_attention(c_q, c_kv, k_rope, g_q_lora[i], w_uq[i], g_kv_lora[i],
                                  w_ukv[i], g_mla_q[i], g_mla_k[i], cos, sin)
            y_mem = memory_attention(q_mem, mem, g_mem[i], w_mem_kv[i], g_mem_q[i], g_mem_k[i])
            merged = jnp.zeros_like(x)
            for b, y_b in enumerate((y_ssm, y_mla, y_mem)):
                gate_b = jax.nn.sigmoid(gates[..., b * D_MODEL:(b + 1) * D_MODEL])
                merged = merged + gate_b * (y_b @ w_branch[i, b])
            x = x + merged @ w_out[i]
            h2 = rms_norm(x, g_ffn[i])
            if i % 2 == 0:
                x = x + swiglu(h2, ffn_w13[i // 2], ffn_w2[i // 2])
            else:
                x = x + moe_swiglu(h2, moe_router[i // 2], moe_w13[i // 2], moe_w2[i // 2])
        return x

    y_prompt = trunk(x_prompt, mem_prompt)
    y_sample = trunk(x_sample, mem_sample)
    return (y_prompt, y_sample)
```

```python
import functools

import numpy as np
import jax
import jax.numpy as jnp
from jax import lax
from jax.experimental import pallas as pl
from jax.experimental.pallas import tpu as pltpu

F32 = jnp.float32
BF16 = jnp.bfloat16
EPS = 1e-6

D_MODEL = 1024
SSM_WIDTH = 512
SSM_GROUPS = 32
SSM_GROUP = 16
SSM_STATE = 64
MLA_HEADS = 8
MLA_NOPE = 64
MLA_ROPE = 32
MLA_V = 64
MLA_QK = MLA_NOPE + MLA_ROPE
Q_LORA = 256
KV_LORA = 128
ROPE_BASE = 10000.0
MEM_TOKENS = 256
MEM_HEADS = 4
MEM_HEAD_DIM = 128
MEM_WIDTH = 512
N_BRANCH = 3
D_FF = 2816
N_EXPERTS = 8
D_FF_EXPERT = 3584
IN_SIZES = (SSM_WIDTH, Q_LORA, KV_LORA, MLA_ROPE, MEM_WIDTH, N_BRANCH * D_MODEL)
IN_OFFSETS = tuple(int(v) for v in np.cumsum(IN_SIZES)[:-1])

LANES = 128
HEAD_PAD = 128
HALF_ROPE = MLA_ROPE // 2
VMEM_LIMIT = 56 * 1024 * 1024

TOKEN_TILE = 512
ATTN_TILE = 1024
SSM_CHUNK = 16
SSM_GROUP_BLOCK = 4
SSM_SLOTS = 8
SSM_MAX_ROWS = 4096
MOE_BLOCK = 512
FFN_CHUNK = 1408
MOE_CHUNK = 512


def _cparams(sem):
    return pltpu.CompilerParams(dimension_semantics=sem, vmem_limit_bytes=VMEM_LIMIT)


def _const_spec(shape):
    nd = len(shape)
    return pl.BlockSpec(shape, lambda *_: (0,) * nd)


def _sigmoid(x):
    return 1.0 / (1.0 + jnp.exp(-x))


def _rms_scale(x, n):
    return lax.rsqrt(jnp.sum(x * x, axis=-1, keepdims=True) * (1.0 / n) + EPS)


def _inproj_kernel(x_ref, gmix_ref, wu_ref, wcq_ref, wckv_ref, wqm_ref, wg_ref,
                   gql_ref, wuq_ref, gkvl_ref, wkn_ref, wv_ref, gq_ref, gk_ref, gmq_ref,
                   ra_ref, rb1_ref, rb2_ref,
                   u_ref, q_ref, k_ref, v_ref, qm_ref, sg_ref):
    x = x_ref[...]
    h = (x * _rms_scale(x, D_MODEL) * gmix_ref[...]).astype(BF16)

    u_ref[...] = jnp.dot(h, wu_ref[...], preferred_element_type=F32).astype(BF16)

    for b in range(N_BRANCH):
        zg = jnp.dot(h, wg_ref[:, b * D_MODEL:(b + 1) * D_MODEL], preferred_element_type=F32)
        sg_ref[:, b * D_MODEL:(b + 1) * D_MODEL] = _sigmoid(zg).astype(BF16)

    zq = jnp.dot(h, wqm_ref[...], preferred_element_type=F32)
    for hh in range(MEM_HEADS):
        blk = zq[:, hh * MEM_HEAD_DIM:(hh + 1) * MEM_HEAD_DIM]
        r = blk * (_rms_scale(blk, MEM_HEAD_DIM) * MEM_HEAD_DIM ** -0.5) * gmq_ref[...]
        qm_ref[:, hh * MEM_HEAD_DIM:(hh + 1) * MEM_HEAD_DIM] = r.astype(BF16)

    cq = jnp.dot(h, wcq_ref[...], preferred_element_type=F32)
    cqn = (cq * _rms_scale(cq, Q_LORA) * gql_ref[...]).astype(BF16)
    qf = jnp.dot(cqn, wuq_ref[...], preferred_element_type=F32)
    z2 = jnp.dot(h, wckv_ref[...], preferred_element_type=F32)
    ckv = z2[:, :KV_LORA]
    kr = z2[:, KV_LORA:]
    ckvn = (ckv * _rms_scale(ckv, KV_LORA) * gkvl_ref[...]).astype(BF16)
    kn = jnp.dot(ckvn, wkn_ref[...], preferred_element_type=F32)
    v_ref[...] = jnp.dot(ckvn, wv_ref[...], preferred_element_type=F32).astype(BF16)

    ra = ra_ref[...]
    rb1 = rb1_ref[...]
    rb2 = rb2_ref[...]

    def norm_rope(blk, gain, scale):
        rs = _rms_scale(blk, MLA_QK) * scale
        ng = blk * gain
        r = ng * ra + pltpu.roll(ng, HEAD_PAD - HALF_ROPE, 1) * rb1 + pltpu.roll(ng, HALF_ROPE, 1) * rb2
        return (r * rs).astype(BF16)

    for hh in range(MLA_HEADS):
        sl = slice(hh * HEAD_PAD, (hh + 1) * HEAD_PAD)
        q_ref[:, sl] = norm_rope(qf[:, sl], gq_ref[...], MLA_QK ** -0.5)
        k_ref[:, sl] = norm_rope(kn[:, sl] + kr, gk_ref[...], 1.0)


def _in_projection(x, lw, tiles):
    t_tok = x.shape[0]
    tm = tiles["tm"]
    n_tiles = t_tok // tm
    pos_map = tiles["pos_map"]

    def row(i):
        return (i, 0)

    weights = [lw["g_mix"], lw["w_u"], lw["w_cq"], lw["w_ckv"], lw["w_qm"], lw["w_gate"],
               lw["g_q_lora"], lw["w_uq"], lw["g_kv_lora"], lw["w_kn"], lw["w_v"],
               lw["g_q"], lw["g_k"], lw["g_mem_q"]]
    in_specs = [pl.BlockSpec((tm, D_MODEL), row)] + [_const_spec(w.shape) for w in weights]
    in_specs += [pl.BlockSpec((tm, HEAD_PAD), lambda i: (pos_map(i), 0))] * 3
    out_widths = (SSM_WIDTH, MLA_HEADS * HEAD_PAD, MLA_HEADS * HEAD_PAD, MLA_HEADS * MLA_V,
                  MEM_WIDTH, N_BRANCH * D_MODEL)
    return pl.pallas_call(
        _inproj_kernel,
        grid=(n_tiles,),
        in_specs=in_specs,
        out_specs=[pl.BlockSpec((tm, w), row) for w in out_widths],
        out_shape=[jax.ShapeDtypeStruct((t_tok, w), BF16) for w in out_widths],
        compiler_params=_cparams(("parallel",)),
        name="in_projection",
    )(x, *weights, tiles["rope_a"], tiles["rope_b1"], tiles["rope_b2"])


def _ssm_kernel(u_ref, t_ref, win_ref, wout_ref, a_ref, y_ref,
                sf_ref, sfs_ref, sb_ref, sbs_ref, yacc_ref, *, n_chunks, n_slots):
    gb = u_ref.shape[1]
    for j in range(gb):
        u = u_ref[0, j]
        yacc_ref[j] = jnp.dot(u, t_ref[j], preferred_element_type=F32)
        s4 = jnp.dot(u, win_ref[j], preferred_element_type=F32)
        sl = slice(j * LANES, (j + 1) * LANES)
        sf_ref[:, sl] = s4[:, 0 * LANES:1 * LANES]
        sfs_ref[:, sl] = s4[:, 1 * LANES:2 * LANES]
        sb_ref[:, sl] = s4[:, 2 * LANES:3 * LANES]
        sbs_ref[:, sl] = s4[:, 3 * LANES:4 * LANES]

    a1f = a_ref[0:1, :]
    a2f = a_ref[1:2, :]
    a1b = a_ref[2:3, :]
    a2b = a_ref[3:4, :]

    def step(i, carry):
        cf, cfs, cb, cbs = carry
        rf = pl.ds(pl.multiple_of(i * n_slots, n_slots), n_slots)
        rb = pl.ds(pl.multiple_of((n_chunks - 1 - i) * n_slots, n_slots), n_slots)
        tf = sf_ref[rf, :]
        tfs = sfs_ref[rf, :]
        tb = sb_ref[rb, :]
        tbs = sbs_ref[rb, :]
        sf_ref[rf, :] = cf
        sb_ref[rb, :] = cb
        return (a1f * cf + a2f * cfs + tf, a1f * cfs - a2f * cf + tfs,
                a1b * cb + a2b * cbs + tb, a1b * cbs - a2b * cb + tbs)

    zero = jnp.zeros((n_slots, gb * LANES), F32)
    lax.fori_loop(0, n_chunks, step, (zero, zero, zero, zero))

    for j in range(gb):
        sl = slice(j * LANES, (j + 1) * LANES)
        xs = jnp.concatenate([sf_ref[:, sl], sb_ref[:, sl]], axis=1).astype(BF16)
        y = yacc_ref[j] + jnp.dot(xs, wout_ref[j], preferred_element_type=F32)
        y_ref[0, j] = y.astype(BF16)


def _ssm_call(u_r, tabs, n_chunks):
    n_rg, n_g, rows, width = u_r.shape
    gb = max(1, min(SSM_GROUP_BLOCK, SSM_MAX_ROWS // rows))
    kern = functools.partial(_ssm_kernel, n_chunks=n_chunks, n_slots=SSM_SLOTS)
    return pl.pallas_call(
        kern,
        grid=(n_rg, n_g // gb),
        in_specs=[
            pl.BlockSpec((1, gb, rows, width), lambda r, g: (r, g, 0, 0)),
            pl.BlockSpec((gb, width, width), lambda r, g: (g, 0, 0)),
            pl.BlockSpec((gb, width, 4 * LANES), lambda r, g: (g, 0, 0)),
            pl.BlockSpec((gb, 2 * LANES, width), lambda r, g: (g, 0, 0)),
            pl.BlockSpec((4, gb * LANES), lambda r, g: (0, g)),
        ],
        out_specs=pl.BlockSpec((1, gb, rows, width), lambda r, g: (r, g, 0, 0)),
        out_shape=jax.ShapeDtypeStruct(u_r.shape, BF16),
        scratch_shapes=[pltpu.VMEM((rows, gb * LANES), F32)] * 4
        + [pltpu.VMEM((gb, rows, width), F32)],
        compiler_params=_cparams(("parallel", "parallel")),
        name="ssm_chunked",
    )(u_r, tabs["t"], tabs["w_in"], tabs["w_out"], tabs["a"])


def _to_chunks(u, n_seq, seq_len):
    n_rg = -(-n_seq // SSM_SLOTS)
    n_chunks = seq_len // SSM_CHUNK
    v = u.reshape(n_seq, seq_len, SSM_WIDTH)
    v = jnp.pad(v, ((0, n_rg * SSM_SLOTS - n_seq), (0, 0), (0, 0)))
    v = v.reshape(n_rg, SSM_SLOTS, n_chunks, SSM_CHUNK, SSM_GROUPS, SSM_GROUP)
    v = v.transpose(0, 4, 2, 1, 3, 5)
    return v.reshape(n_rg, SSM_GROUPS, n_chunks * SSM_SLOTS, SSM_CHUNK * SSM_GROUP)


def _from_chunks(y_r, n_seq, seq_len):
    n_rg = y_r.shape[0]
    n_chunks = seq_len // SSM_CHUNK
    v = y_r.reshape(n_rg, SSM_GROUPS, n_chunks, SSM_SLOTS, SSM_CHUNK, SSM_GROUP)
    v = v.transpose(0, 3, 2, 4, 1, 5)
    v = v.reshape(n_rg * SSM_SLOTS, seq_len, SSM_WIDTH)[:n_seq]
    return v.reshape(n_seq * seq_len, SSM_WIDTH)


def _ssm_tables(a_re, a_im, log_dt, b_re, b_im, c_re, c_im, d_skip):
    q = SSM_CHUNK
    hp = lax.Precision.HIGHEST
    dt = jnp.exp(log_dt)[..., None, None]
    ks = jnp.arange(q + 1, dtype=F32)
    mag = jnp.exp(a_re[..., None] * dt * ks)
    ang = a_im[..., None] * dt * ks
    pw_re = mag * jnp.cos(ang)
    pw_im = mag * jnp.sin(ang)
    lb_re, lb_im = pw_re[..., 1], pw_im[..., 1]
    den = a_re * a_re + a_im * a_im
    f_re = ((lb_re - 1.0) * a_re + lb_im * a_im) / den
    f_im = (lb_im * a_re - (lb_re - 1.0) * a_im) / den
    bb_re = f_re[..., None] * b_re - f_im[..., None] * b_im
    bb_im = f_re[..., None] * b_im + f_im[..., None] * b_re

    cl_re = c_re[..., None] * pw_re[:, :, None] - c_im[..., None] * pw_im[:, :, None]
    cl_im = c_re[..., None] * pw_im[:, :, None] + c_im[..., None] * pw_re[:, :, None]

    kern = (jnp.einsum("dgnpk,dgpm->dgknm", cl_re[..., :q], bb_re, precision=hp)
            - jnp.einsum("dgnpk,dgpm->dgknm", cl_im[..., :q], bb_im, precision=hp))
    s_idx = jnp.arange(q)[:, None]
    t_idx = jnp.arange(q)[None, :]
    lag_f = jnp.clip(t_idx - s_idx, 0, q - 1)
    lag_b = jnp.clip(s_idx - t_idx, 0, q - 1)
    tf = kern[0][:, lag_f] * (t_idx >= s_idx)[None, :, :, None, None]
    tb = kern[1][:, lag_b] * (s_idx >= t_idx)[None, :, :, None, None]
    t_mat = (tf + tb).transpose(0, 1, 4, 2, 3)
    t_mat = t_mat.reshape(SSM_GROUPS, q * SSM_GROUP, q * SSM_GROUP)
    d_rep = jnp.tile(d_skip.reshape(SSM_GROUPS, 1, SSM_GROUP), (1, q, 1)).reshape(SSM_GROUPS, -1)
    t_mat = t_mat + d_rep[:, :, None] * jnp.eye(q * SSM_GROUP, dtype=F32)[None]

    def w_in_dir(d, exps):
        pr = pw_re[d][:, :, exps]
        pi = pw_im[d][:, :, exps]
        re = pr[..., None] * bb_re[d][:, :, None, :] - pi[..., None] * bb_im[d][:, :, None, :]
        im = pr[..., None] * bb_im[d][:, :, None, :] + pi[..., None] * bb_re[d][:, :, None, :]
        re = re.transpose(0, 2, 3, 1).reshape(SSM_GROUPS, q * SSM_GROUP, SSM_STATE)
        im = im.transpose(0, 2, 3, 1).reshape(SSM_GROUPS, q * SSM_GROUP, SSM_STATE)
        return jnp.concatenate([re, im, im, re], axis=-1)

    w_in = jnp.concatenate([w_in_dir(0, q - 1 - jnp.arange(q)), w_in_dir(1, jnp.arange(q))], axis=-1)

    def w_out_dir(d, exps):
        re = cl_re[d][..., exps]
        im = cl_im[d][..., exps]
        re = re.transpose(0, 2, 3, 1).reshape(SSM_GROUPS, SSM_STATE, q * SSM_GROUP)
        im = im.transpose(0, 2, 3, 1).reshape(SSM_GROUPS, SSM_STATE, q * SSM_GROUP)
        return jnp.concatenate([re, -im], axis=1)

    w_out = jnp.concatenate([w_out_dir(0, 1 + jnp.arange(q)), w_out_dir(1, q - jnp.arange(q))], axis=1)

    def a_rows(d):
        ar = pw_re[d][..., q]
        ai = pw_im[d][..., q]
        a1 = jnp.concatenate([ar, ar], axis=-1).reshape(-1)
        a2 = jnp.concatenate([-ai, ai], axis=-1).reshape(-1)
        return [a1, a2]

    a_tab = jnp.stack(a_rows(0) + a_rows(1), axis=0)
    return {"t": t_mat.astype(BF16), "w_in": w_in.astype(BF16), "w_out": w_out.astype(BF16),
            "a": a_tab.astype(F32)}


def _flash_kernel(qb_ref, kb_ref, first_ref, last_ref, q_ref, k_ref, v_ref, o_ref,
                  m_ref, l_ref, acc_ref):
    del qb_ref, kb_ref
    s_id = pl.program_id(1)

    @pl.when(first_ref[s_id] == 1)
    def _():
        m_ref[...] = jnp.full(m_ref.shape, -jnp.inf, F32)
        l_ref[...] = jnp.zeros(l_ref.shape, F32)
        acc_ref[...] = jnp.zeros(acc_ref.shape, F32)

    v = v_ref[...]
    for hh in range(2):
        sl = slice(hh * HEAD_PAD, (hh + 1) * HEAD_PAD)
        s = lax.dot_general(q_ref[:, sl], k_ref[:, sl], (((1,), (1,)), ((), ())),
                            preferred_element_type=F32)
        m_prev = m_ref[hh]
        m_new = jnp.maximum(m_prev, jnp.max(s, axis=1, keepdims=True))
        alpha = jnp.exp(m_prev - m_new)
        p = jnp.exp(s - m_new[:, :1])
        l_ref[hh] = alpha * l_ref[hh] + jnp.sum(p, axis=1, keepdims=True)
        acc_ref[hh] = alpha * acc_ref[hh] + jnp.dot(p.astype(BF16), v, preferred_element_type=F32)
        m_ref[hh] = m_new

    @pl.when(last_ref[s_id] == 1)
    def _():
        lane = lax.broadcasted_iota(jnp.int32, o_ref.shape, 1)
        o0 = acc_ref[0] / l_ref[0]
        o1 = acc_ref[1] / l_ref[1]
        o_ref[...] = jnp.where(lane < MLA_V, o0, o1).astype(BF16)


def _flash_attention(q, k, v, seqs, tile):
    qb, kb, first, last = [], [], [], []
    base = 0
    for n_seq, seq_len in seqs:
        nb = seq_len // tile
        for b in range(n_seq):
            for qi in range(nb):
                for ki in range(nb):
                    qb.append(base + qi)
                    kb.append(base + ki)
                    first.append(int(ki == 0))
                    last.append(int(ki == nb - 1))
            base += nb
    tabs = [jnp.asarray(np.asarray(a, np.int32)) for a in (qb, kb, first, last)]
    n_steps = len(qb)
    t_tok = q.shape[0]
    pair_w = 2 * HEAD_PAD
    grid_spec = pltpu.PrefetchScalarGridSpec(
        num_scalar_prefetch=4,
        grid=(MLA_HEADS // 2, n_steps),
        in_specs=[
            pl.BlockSpec((tile, pair_w), lambda p, s, qb, kb, fi, la: (qb[s], p)),
            pl.BlockSpec((tile, pair_w), lambda p, s, qb, kb, fi, la: (kb[s], p)),
            pl.BlockSpec((tile, 2 * MLA_V), lambda p, s, qb, kb, fi, la: (kb[s], p)),
        ],
        out_specs=pl.BlockSpec((tile, 2 * MLA_V), lambda p, s, qb, kb, fi, la: (qb[s], p)),
        scratch_shapes=[pltpu.VMEM((2, tile, LANES), F32)] * 3,
    )
    return pl.pallas_call(
        _flash_kernel,
        grid_spec=grid_spec,
        out_shape=jax.ShapeDtypeStruct((t_tok, MLA_HEADS * MLA_V), BF16),
        compiler_params=_cparams(("parallel", "arbitrary")),
        name="mla_flash",
    )(*tabs, q, k, v)


def _memkv_kernel(mem_ref, gmem_ref, w_ref, gk_ref, k_ref, v_ref):
    x = mem_ref[...]
    h = (x * _rms_scale(x, D_MODEL) * gmem_ref[...]).astype(BF16)
    kv = jnp.dot(h, w_ref[...], preferred_element_type=F32)
    for hh in range(MEM_HEADS):
        sl = slice(hh * MEM_HEAD_DIM, (hh + 1) * MEM_HEAD_DIM)
        blk = kv[:, sl]
        k_ref[:, sl] = (blk * _rms_scale(blk, MEM_HEAD_DIM) * gk_ref[...]).astype(BF16)
    v_ref[...] = kv[:, MEM_WIDTH:].astype(BF16)


def _memory_kv(mem, lw):
    rows = mem.shape[0]
    tm = MEM_TOKENS
    return pl.pallas_call(
        _memkv_kernel,
        grid=(rows // tm,),
        in_specs=[pl.BlockSpec((tm, D_MODEL), lambda i: (i, 0)),
                  _const_spec(lw["g_mem"].shape), _const_spec(lw["w_mem_kv"].shape),
                  _const_spec(lw["g_mem_k"].shape)],
        out_specs=[pl.BlockSpec((tm, MEM_WIDTH), lambda i: (i, 0))] * 2,
        out_shape=[jax.ShapeDtypeStruct((rows, MEM_WIDTH), BF16)] * 2,
        compiler_params=_cparams(("parallel",)),
        name="memory_kv",
    )(mem, lw["g_mem"], lw["w_mem_kv"], lw["g_mem_k"])


def _gelu_tanh(x):
    return 0.5 * x * (1.0 + jnp.tanh(0.7978845608028654 * (x + 0.044715 * x * x * x)))


def _split_bf16(x):
    hi = x.astype(BF16)
    lo = (x - hi.astype(F32)).astype(BF16)
    return hi, lo


def _merge_kernel(*refs, with_router):
    if with_router:
        (x_ref, ys_ref, ya_ref, qm_ref, km_ref, vm_ref, sg_ref, wglu_ref, bglu_ref, wb_ref,
         wo_ref, gffn_ref, wr_hi_ref, wr_lo_ref, x1_ref, h2_ref, route_ref) = refs
    else:
        (x_ref, ys_ref, ya_ref, qm_ref, km_ref, vm_ref, sg_ref, wglu_ref, bglu_ref, wb_ref,
         wo_ref, gffn_ref, x1_ref, h2_ref) = refs

    ya = _gelu_tanh(ys_ref[...].astype(F32)).astype(BF16)
    z = jnp.dot(ya, wglu_ref[...], preferred_element_type=F32) + bglu_ref[...]
    y_ssm = (z[:, :SSM_WIDTH] * _sigmoid(z[:, SSM_WIDTH:])).astype(BF16)

    mem_parts = []
    for hh in range(MEM_HEADS):
        sl = slice(hh * MEM_HEAD_DIM, (hh + 1) * MEM_HEAD_DIM)
        s = lax.dot_general(qm_ref[:, sl], km_ref[:, sl], (((1,), (1,)), ((), ())),
                            preferred_element_type=F32)
        p = jnp.exp(s - jnp.max(s, axis=1, keepdims=True))
        p = p / jnp.sum(p, axis=1, keepdims=True)
        mem_parts.append(jnp.dot(p.astype(BF16), vm_ref[:, sl], preferred_element_type=F32))
    y_mem = jnp.concatenate(mem_parts, axis=1).astype(BF16)

    merged = None
    for b, yb in enumerate((y_ssm, ya_ref[...], y_mem)):
        gate = sg_ref[:, b * D_MODEL:(b + 1) * D_MODEL].astype(F32)
        term = gate * jnp.dot(yb, wb_ref[b], preferred_element_type=F32)
        merged = term if merged is None else merged + term

    x1 = x_ref[...] + jnp.dot(merged.astype(BF16), wo_ref[...], preferred_element_type=F32)
    x1_ref[...] = x1
    h2 = x1 * _rms_scale(x1, D_MODEL) * gffn_ref[...]
    h2_ref[...] = h2.astype(BF16)

    if with_router:
        h_hi, h_lo = _split_bf16(h2)
        logits = (jnp.dot(h_hi, wr_hi_ref[...], preferred_element_type=F32)
                  + jnp.dot(h_hi, wr_lo_ref[...], preferred_element_type=F32)
                  + jnp.dot(h_lo, wr_hi_ref[...], preferred_element_type=F32))
        lane = lax.broadcasted_iota(jnp.int32, logits.shape, 1)
        lane_f = lane.astype(F32)
        neg = jnp.float32(-jnp.inf)
        big = jnp.float32(LANES)
        logits = jnp.where(lane < N_EXPERTS, logits, neg)
        m1 = jnp.max(logits, axis=1, keepdims=True)
        i1 = jnp.min(jnp.where(logits == m1, lane_f, big), axis=1, keepdims=True)
        rest = jnp.where(lane_f == i1, neg, logits)
        m2 = jnp.max(rest, axis=1, keepdims=True)
        i2 = jnp.min(jnp.where(rest == m2, lane_f, big), axis=1, keepdims=True)
        e2 = jnp.exp(m2 - m1)
        w1 = 1.0 / (1.0 + e2)
        w2 = e2 / (1.0 + e2)
        out = jnp.where(lane == 0, i1, jnp.where(lane == 1, i2, jnp.where(lane == 2, w1, w2)))
        route_ref[...] = jnp.where(lane < 4, out, 0.0)


def _merge(x, y_ssm_raw, y_mla, qm, km, vm, sg, lw, tiles, with_router):
    t_tok = x.shape[0]
    tm = tiles["tm"]
    seq_map = tiles["seq_map"]

    def row(i):
        return (i, 0)

    weights = [lw["w_glu"], lw["b_glu"], lw["w_branch"], lw["w_out"], lw["g_ffn"]]
    if with_router:
        weights += [lw["w_router_hi"], lw["w_router_lo"]]
    in_specs = [
        pl.BlockSpec((tm, D_MODEL), row),
        pl.BlockSpec((tm, SSM_WIDTH), row),
        pl.BlockSpec((tm, MLA_HEADS * MLA_V), row),
        pl.BlockSpec((tm, MEM_WIDTH), row),
        pl.BlockSpec((MEM_TOKENS, MEM_WIDTH), lambda i: (seq_map(i), 0)),
        pl.BlockSpec((MEM_TOKENS, MEM_WIDTH), lambda i: (seq_map(i), 0)),
        pl.BlockSpec((tm, N_BRANCH * D_MODEL), row),
    ] + [_const_spec(w.shape) for w in weights]
    out_specs = [pl.BlockSpec((tm, D_MODEL), row), pl.BlockSpec((tm, D_MODEL), row)]
    out_shape = [jax.ShapeDtypeStruct((t_tok, D_MODEL), F32),
                 jax.ShapeDtypeStruct((t_tok, D_MODEL), BF16)]
    if with_router:
        out_specs.append(pl.BlockSpec((tm, LANES), row))
        out_shape.append(jax.ShapeDtypeStruct((t_tok, LANES), F32))
    return pl.pallas_call(
        functools.partial(_merge_kernel, with_router=with_router),
        grid=(t_tok // tm,),
        in_specs=in_specs,
        out_specs=out_specs,
        out_shape=out_shape,
        compiler_params=_cparams(("parallel",)),
        name="merge_router" if with_router else "merge",
    )(x, y_ssm_raw, y_mla, qm, km, vm, sg, *weights)


def _ffn_kernel(h_ref, x_ref, w1_ref, w3_ref, w2_ref, o_ref):
    h = h_ref[...]
    acc = x_ref[...]
    d_ff = w1_ref.shape[1]
    for c in range(d_ff // FFN_CHUNK):
        sl = slice(c * FFN_CHUNK, (c + 1) * FFN_CHUNK)
        g = jnp.dot(h, w1_ref[:, sl], preferred_element_type=F32)
        u = jnp.dot(h, w3_ref[:, sl], preferred_element_type=F32)
        a = (g * _sigmoid(g) * u).astype(BF16)
        acc = acc + jnp.dot(a, w2_ref[sl, :], preferred_element_type=F32)
    o_ref[...] = acc


def _dense_ffn(h2, x1, lw, tiles):
    t_tok = x1.shape[0]
    tm = tiles["tm"]

    def row(i):
        return (i, 0)

    return pl.pallas_call(
        _ffn_kernel,
        grid=(t_tok // tm,),
        in_specs=[pl.BlockSpec((tm, D_MODEL), row), pl.BlockSpec((tm, D_MODEL), row),
                  _const_spec(lw["ffn_w1"].shape), _const_spec(lw["ffn_w3"].shape),
                  _const_spec(lw["ffn_w2"].shape)],
        out_specs=pl.BlockSpec((tm, D_MODEL), row),
        out_shape=jax.ShapeDtypeStruct((t_tok, D_MODEL), F32),
        compiler_params=_cparams(("parallel",)),
        name="dense_ffn",
    )(h2, x1, lw["ffn_w1"], lw["ffn_w3"], lw["ffn_w2"])


def _gather_rows_kernel(src_ref, h_hbm, o_ref, sem):
    n_rows = o_ref.shape[0]

    def row_copy(r, src_row):
        return pltpu.make_async_copy(h_hbm.at[pl.ds(src_row, 1)], o_ref.at[pl.ds(r, 1)], sem)

    def start(r, c):
        row_copy(r, src_ref[0, 0, r]).start()
        return c

    def wait(r, c):
        row_copy(r, 0).wait()
        return c

    lax.fori_loop(0, n_rows, start, 0)
    lax.fori_loop(0, n_rows, wait, 0)


def _gather_rows(h, src3, bm):
    n_blocks = src3.shape[0]
    width = h.shape[1]
    return pl.pallas_call(
        _gather_rows_kernel,
        grid=(n_blocks,),
        in_specs=[pl.BlockSpec((1, 1, bm), lambda i: (i, 0, 0), memory_space=pltpu.SMEM),
                  pl.BlockSpec(memory_space=pl.ANY)],
        out_specs=pl.BlockSpec((bm, width), lambda i: (i, 0)),
        out_shape=jax.ShapeDtypeStruct((n_blocks * bm, width), h.dtype),
        scratch_shapes=[pltpu.SemaphoreType.DMA(())],
        compiler_params=_cparams(("arbitrary",)),
        name="moe_gather",
    )(src3, h)


def _expert_kernel(be_ref, nu_ref, x_ref, w1_ref, w3_ref, w2_ref, y_ref):
    del be_ref
    i = pl.program_id(0)

    @pl.when(i < nu_ref[0])
    def _():
        x = x_ref[...].astype(BF16)
        acc = jnp.zeros(y_ref.shape, F32)
        d_ff = w1_ref.shape[1]
        for c in range(d_ff // MOE_CHUNK):
            sl = slice(c * MOE_CHUNK, (c + 1) * MOE_CHUNK)
            g = jnp.dot(x, w1_ref[:, sl], preferred_element_type=F32)
            u = jnp.dot(x, w3_ref[:, sl], preferred_element_type=F32)
            a = (g * _sigmoid(g) * u).astype(BF16)
            acc = acc + jnp.dot(a, w2_ref[sl, :], preferred_element_type=F32)
        y_ref[...] = acc

    @pl.when(i >= nu_ref[0])
    def _():
        y_ref[...] = jnp.zeros(y_ref.shape, F32)


def _expert_ffn(xs, blk_expert, n_used, lw, bm):
    n_rows = xs.shape[0]
    grid_spec = pltpu.PrefetchScalarGridSpec(
        num_scalar_prefetch=2,
        grid=(n_rows // bm,),
        in_specs=[
            pl.BlockSpec((bm, D_MODEL), lambda i, be, nu: (i, 0)),
            pl.BlockSpec((None, D_MODEL, D_FF_EXPERT), lambda i, be, nu: (be[i], 0, 0),
                         pipeline_mode=pl.Buffered(1)),
            pl.BlockSpec((None, D_MODEL, D_FF_EXPERT), lambda i, be, nu: (be[i], 0, 0),
                         pipeline_mode=pl.Buffered(1)),
            pl.BlockSpec((None, D_FF_EXPERT, D_MODEL), lambda i, be, nu: (be[i], 0, 0),
                         pipeline_mode=pl.Buffered(1)),
        ],
        out_specs=pl.BlockSpec((bm, D_MODEL), lambda i, be, nu: (i, 0)),
    )
    return pl.pallas_call(
        _expert_kernel,
        grid_spec=grid_spec,
        out_shape=jax.ShapeDtypeStruct((n_rows, D_MODEL), F32),
        compiler_params=_cparams(("arbitrary",)),
        name="moe_experts",
    )(blk_expert, n_used, xs, lw["moe_w1"], lw["moe_w3"], lw["moe_w2"])


def _combine_kernel(pos_ref, ys_hbm, x_ref, route_ref, o_ref, buf_ref, sem):
    tm = o_ref.shape[0]

    def row_copy(k, r, src_row):
        return pltpu.make_async_copy(ys_hbm.at[pl.ds(src_row, 1)], buf_ref.at[k, pl.ds(r, 1)], sem)

    def start(r, c):
        row_copy(0, r, pos_ref[0, 0, 2 * r]).start()
        row_copy(1, r, pos_ref[0, 0, 2 * r + 1]).start()
        return c

    def wait(r, c):
        row_copy(0, r, 0).wait()
        row_copy(1, r, 0).wait()
        return c

    lax.fori_loop(0, tm, start, 0)
    lax.fori_loop(0, tm, wait, 0)
    w1 = route_ref[:, 2:3]
    w2 = route_ref[:, 3:4]
    o_ref[...] = x_ref[...] + w1 * buf_ref[0] + w2 * buf_ref[1]


def _combine(ys, pos3, x1, route, tm):
    t_tok = x1.shape[0]
    return pl.pallas_call(
        _combine_kernel,
        grid=(t_tok // tm,),
        in_specs=[pl.BlockSpec((1, 1, 2 * tm), lambda i: (i, 0, 0), memory_space=pltpu.SMEM),
                  pl.BlockSpec(memory_space=pl.ANY),
                  pl.BlockSpec((tm, D_MODEL), lambda i: (i, 0)),
                  pl.BlockSpec((tm, LANES), lambda i: (i, 0))],
        out_specs=pl.BlockSpec((tm, D_MODEL), lambda i: (i, 0)),
        out_shape=jax.ShapeDtypeStruct((t_tok, D_MODEL), F32),
        scratch_shapes=[pltpu.VMEM((2, tm, D_MODEL), F32), pltpu.SemaphoreType.DMA(())],
        compiler_params=_cparams(("arbitrary",)),
        name="moe_combine",
    )(pos3, ys, x1, route)


def _moe_ffn(h2_f32, x1, route, lw, tiles):
    t_tok = x1.shape[0]
    tm = tiles["tm"]
    bm = tiles["moe_block"]
    n_assign = 2 * t_tok
    experts = route[:, :2].astype(jnp.int32).reshape(n_assign)
    onehot = (experts[:, None] == jnp.arange(N_EXPERTS, dtype=jnp.int32)[None, :]).astype(jnp.int32)
    csum = jnp.cumsum(onehot, axis=0)
    rank = jnp.sum((csum - 1) * onehot, axis=1)
    counts = csum[-1]
    padded = ((counts + bm - 1) // bm) * bm
    ends = jnp.cumsum(padded)
    starts = ends - padded
    pos = starts[experts] + rank
    n_blocks = n_assign // bm + N_EXPERTS
    src = jnp.zeros((n_blocks * bm,), jnp.int32).at[pos].set(jnp.arange(n_assign, dtype=jnp.int32) // 2)
    blk_start = jnp.arange(n_blocks, dtype=jnp.int32) * bm
    blk_expert = jnp.minimum(jnp.searchsorted(ends, blk_start, side="right"), N_EXPERTS - 1).astype(jnp.int32)
    n_used = (ends[-1:] // bm).astype(jnp.int32)

    xs = _gather_rows(h2_f32, src.reshape(n_blocks, 1, bm), bm)
    ys = _expert_ffn(xs, blk_expert, n_used, lw, bm)
    return _combine(ys, pos.reshape(t_tok // tm, 1, 2 * tm), x1, route, tm)


def _pad_heads(w, per_head, n_heads):
    k = w.shape[0]
    w = w.reshape(k, n_heads, per_head)
    w = jnp.pad(w, ((0, 0), (0, 0), (0, HEAD_PAD - per_head)))
    return w.reshape(k, n_heads * HEAD_PAD)


def _pad_gain(g, per_head):
    return jnp.pad(g, (0, HEAD_PAD - per_head)).reshape(1, HEAD_PAD).astype(F32)


def _layer_weights(i, p):
    o = IN_OFFSETS
    w_in = p["w_in"][i]
    lw = {}
    lw["g_mix"] = p["g_mix"][i].reshape(1, D_MODEL)
    lw["w_u"] = w_in[:, :o[0]].astype(BF16)
    lw["w_cq"] = w_in[:, o[0]:o[1]].astype(BF16)
    w_ckv = jnp.zeros((D_MODEL, 2 * LANES), F32)
    w_ckv = w_ckv.at[:, :KV_LORA].set(w_in[:, o[1]:o[2]])
    w_ckv = w_ckv.at[:, LANES + MLA_NOPE:LANES + MLA_QK].set(w_in[:, o[2]:o[3]])
    lw["w_ckv"] = w_ckv.astype(BF16)
    lw["w_qm"] = w_in[:, o[3]:o[4]].astype(BF16)
    lw["w_gate"] = w_in[:, o[4]:].astype(BF16)
    lw["g_q_lora"] = p["g_q_lora"][i].reshape(1, Q_LORA)
    lw["w_uq"] = _pad_heads(p["w_uq"][i], MLA_QK, MLA_HEADS).astype(BF16)
    lw["g_kv_lora"] = p["g_kv_lora"][i].reshape(1, KV_LORA)
    w_ukv = p["w_ukv"][i].reshape(KV_LORA, MLA_HEADS, MLA_NOPE + MLA_V)
    lw["w_kn"] = _pad_heads(w_ukv[:, :, :MLA_NOPE].reshape(KV_LORA, -1), MLA_NOPE, MLA_HEADS).astype(BF16)
    lw["w_v"] = w_ukv[:, :, MLA_NOPE:].reshape(KV_LORA, MLA_HEADS * MLA_V).astype(BF16)
    lw["g_q"] = _pad_gain(p["g_mla_q"][i], MLA_QK)
    lw["g_k"] = _pad_gain(p["g_mla_k"][i], MLA_QK)
    lw["g_mem_q"] = p["g_mem_q"][i].reshape(1, MEM_HEAD_DIM)
    lw["g_mem"] = p["g_mem"][i].reshape(1, D_MODEL)
    lw["w_mem_kv"] = p["w_mem_kv"][i].astype(BF16)
    lw["g_mem_k"] = p["g_mem_k"][i].reshape(1, MEM_HEAD_DIM)
    lw["w_glu"] = p["ssm_w_glu"][i].astype(BF16)
    lw["b_glu"] = p["ssm_b_glu"][i].reshape(1, 2 * SSM_WIDTH)
    lw["w_branch"] = p["w_branch"][i].astype(BF16)
    lw["w_out"] = p["w_out"][i].astype(BF16)
    lw["g_ffn"] = p["g_ffn"][i].reshape(1, D_MODEL)
    lw["ssm"] = _ssm_tables(p["ssm_a_re"][i], p["ssm_a_im"][i], p["ssm_log_dt"][i], p["ssm_b_re"][i],
                            p["ssm_b_im"][i], p["ssm_c_re"][i], p["ssm_c_im"][i], p["ssm_d"][i])
    if i % 2 == 0:
        w13 = p["ffn_w13"][i // 2]
        lw["ffn_w1"] = w13[:, :D_FF].astype(BF16)
        lw["ffn_w3"] = w13[:, D_FF:].astype(BF16)
        lw["ffn_w2"] = p["ffn_w2"][i // 2].astype(BF16)
    else:
        w13 = p["moe_w13"][i // 2]
        lw["moe_w1"] = w13[:, :, :D_FF_EXPERT].astype(BF16)
        lw["moe_w3"] = w13[:, :, D_FF_EXPERT:].astype(BF16)
        lw["moe_w2"] = p["moe_w2"][i // 2].astype(BF16)
        wr = jnp.pad(p["moe_router"][i // 2], ((0, 0), (0, LANES - N_EXPERTS)))
        hi = wr.astype(BF16)
        lw["w_router_hi"] = hi
        lw["w_router_lo"] = (wr - hi.astype(F32)).astype(BF16)
    return lw


def _rope_tables(max_len):
    inv = 1.0 / (ROPE_BASE ** (jnp.arange(0, MLA_ROPE, 2, dtype=F32) / MLA_ROPE))
    ang = jnp.arange(max_len, dtype=F32)[:, None] * inv[None, :]
    cos, sin = jnp.cos(ang), jnp.sin(ang)
    ones = jnp.ones((max_len, MLA_NOPE), F32)
    zn = jnp.zeros((max_len, MLA_NOPE), F32)
    zr = jnp.zeros((max_len, HALF_ROPE), F32)
    zp = jnp.zeros((max_len, HEAD_PAD - MLA_QK), F32)
    ra = jnp.concatenate([ones, cos, cos, zp], axis=1)
    rb1 = jnp.concatenate([zn, -sin, zr, zp], axis=1)
    rb2 = jnp.concatenate([zn, zr, sin, zp], axis=1)
    return ra, rb1, rb2


def _gcd_tile(limit, *sizes):
    t = limit
    while any(s % t for s in sizes):
        t //= 2
    return t


def kernel(x_prompt, x_sample, mem_prompt, mem_sample, g_mix, w_in, ssm_a_re, ssm_a_im, ssm_log_dt, ssm_b_re, ssm_b_im, ssm_c_re, ssm_c_im, ssm_d, ssm_w_glu, ssm_b_glu, g_q_lora, w_uq, g_kv_lora, w_ukv, g_mla_q, g_mla_k, g_mem, w_mem_kv, g_mem_q, g_mem_k, w_branch, w_out, g_ffn, ffn_w13, ffn_w2, moe_router, moe_w13, moe_w2):
    params = dict(g_mix=g_mix, w_in=w_in, ssm_a_re=ssm_a_re, ssm_a_im=ssm_a_im, ssm_log_dt=ssm_log_dt,
                  ssm_b_re=ssm_b_re, ssm_b_im=ssm_b_im, ssm_c_re=ssm_c_re, ssm_c_im=ssm_c_im, ssm_d=ssm_d,
                  ssm_w_glu=ssm_w_glu, ssm_b_glu=ssm_b_glu, g_q_lora=g_q_lora, w_uq=w_uq,
                  g_kv_lora=g_kv_lora, w_ukv=w_ukv, g_mla_q=g_mla_q, g_mla_k=g_mla_k, g_mem=g_mem,
                  w_mem_kv=w_mem_kv, g_mem_q=g_mem_q, g_mem_k=g_mem_k, w_branch=w_branch, w_out=w_out,
                  g_ffn=g_ffn, ffn_w13=ffn_w13, ffn_w2=ffn_w2, moe_router=moe_router, moe_w13=moe_w13,
                  moe_w2=moe_w2)
    depth = g_mix.shape[0]
    bp, lp, _ = x_prompt.shape
    bs, ls, _ = x_sample.shape
    tp, ts = bp * lp, bs * ls
    t_tok = tp + ts

    tm = _gcd_tile(TOKEN_TILE, lp, ls)
    attn_tile = _gcd_tile(ATTN_TILE, lp, ls)
    npt = tp // tm
    lp_t, ls_t = lp // tm, ls // tm

    def pos_map(i):
        return jnp.where(i < npt, i % lp_t, (i - npt) % ls_t)

    def seq_map(i):
        return jnp.where(i < npt, i // lp_t, bp + (i - npt) // ls_t)

    ra, rb1, rb2 = _rope_tables(max(lp, ls))
    tiles = dict(tm=tm, pos_map=pos_map, seq_map=seq_map, rope_a=ra, rope_b1=rb1, rope_b2=rb2,
                 moe_block=_gcd_tile(MOE_BLOCK, 2 * t_tok))

    x = jnp.concatenate([x_prompt.reshape(tp, D_MODEL), x_sample.reshape(ts, D_MODEL)], axis=0)
    mem = jnp.concatenate([mem_prompt.reshape(bp * MEM_TOKENS, D_MODEL),
                           mem_sample.reshape(bs * MEM_TOKENS, D_MODEL)], axis=0)

    for i in range(depth):
        lw = _layer_weights(i, params)
        u, q, k, v, qm, sg = _in_projection(x, lw, tiles)

        y_parts = []
        for (n_seq, seq_len, lo, hi) in ((bp, lp, 0, tp), (bs, ls, tp, t_tok)):
            u_r = _to_chunks(u[lo:hi], n_seq, seq_len)
            y_r = _ssm_call(u_r, lw["ssm"], seq_len // SSM_CHUNK)
            y_parts.append(_from_chunks(y_r, n_seq, seq_len))
        y_ssm_raw = jnp.concatenate(y_parts, axis=0)

        y_mla = _flash_attention(q, k, v, [(bp, lp), (bs, ls)], attn_tile)
        km, vm = _memory_kv(mem, lw)

        with_router = i % 2 == 1
        outs = _merge(x, y_ssm_raw, y_mla, qm, km, vm, sg, lw, tiles, with_router)
        if with_router:
            x1, h2, route = outs
            h2_f32 = h2.astype(F32)
            x = _moe_ffn(h2_f32, x1, route, lw, tiles)
        else:
            x1, h2 = outs
            x = _dense_ffn(h2, x1, lw, tiles)

    return (x[:tp].reshape(bp, lp, D_MODEL), x[tp:].reshape(bs, ls, D_MODEL))
```

```python
import functools
import math

import numpy as np
import jax
import jax.numpy as jnp
from jax import lax
from jax.experimental import pallas as pl
from jax.experimental.pallas import tpu as pltpu

F32 = jnp.float32
BF16 = jnp.bfloat16
EPS = 1e-6
LOG2E = math.log2(math.e)

D_MODEL = 1024
SSM_WIDTH = 512
SSM_GROUPS = 32
SSM_GROUP = 16
SSM_STATE = 64
MLA_HEADS = 8
MLA_NOPE = 64
MLA_ROPE = 32
MLA_V = 64
MLA_QK = MLA_NOPE + MLA_ROPE
Q_LORA = 256
KV_LORA = 128
ROPE_BASE = 10000.0
MEM_TOKENS = 256
MEM_HEADS = 4
MEM_HEAD_DIM = 128
MEM_WIDTH = 512
N_BRANCH = 3
D_FF = 2816
N_EXPERTS = 8
D_FF_EXPERT = 3584
IN_SIZES = (SSM_WIDTH, Q_LORA, KV_LORA, MLA_ROPE, MEM_WIDTH, N_BRANCH * D_MODEL)
IN_OFFSETS = tuple(int(v) for v in np.cumsum(IN_SIZES)[:-1])

LANES = 128
SUBLANES = 8
HEAD_PAD = 128
HALF_ROPE = MLA_ROPE // 2
MLA_WIDTH_PAD = MLA_HEADS * HEAD_PAD
ROW_TILES = D_MODEL // LANES
VMEM_LIMIT = 56 * 1024 * 1024
MAX_SAFE_LOG2_SCORE = 100.0

TOKEN_TILE = 512
ATTN_TILE = 1024
SSM_CHUNK = 16
SSM_GROUP_BLOCK = 4
SSM_SLOTS = 8
SSM_MAX_ROWS = 4096
MOE_BLOCK = 512
FFN_CHUNK = 1408
MOE_CHUNK = 512
DMA_UNROLL = 8


def _cparams(sem):
    return pltpu.CompilerParams(dimension_semantics=sem, vmem_limit_bytes=VMEM_LIMIT)


def _const_spec(shape):
    nd = len(shape)
    return pl.BlockSpec(shape, lambda *_: (0,) * nd)


def _sigmoid(x):
    return 1.0 / (1.0 + jnp.exp(-x))


def _rms_scale(x, n):
    return lax.rsqrt(jnp.sum(x * x, axis=-1, keepdims=True) * (1.0 / n) + EPS)


def _split_bf16(x):
    hi = x.astype(BF16)
    lo = (x - hi.astype(F32)).astype(BF16)
    return hi, lo


def _dot(a, b):
    return jnp.dot(a, b, preferred_element_type=F32)


def _inproj_kernel(x_ref, gmix_ref, wu_ref, wg_ref, wqm_ref, wcq_ref, wckv_ref,
                   gql_ref, wuq_ref, gkvl_ref, wkn_ref, wv_ref, vone_ref, gmq_ref,
                   ones_ref, ind_ref, indt_ref, bd_ref,
                   aq_ref, bq_ref, ak_ref, bk_ref,
                   u_ref, q_ref, k_ref, v_ref, qm_ref, sg_ref):
    def row_ssq(val):
        n = val.shape[1]
        return _dot((val * val).astype(BF16), ones_ref[:n, :])

    def head_scale(val, scale):
        ss = _dot((val * val).astype(BF16), ind_ref[...])
        r = lax.rsqrt(ss * (1.0 / MLA_QK) + EPS) * scale
        r_hi, r_lo = _split_bf16(r)
        return _dot(r_hi, indt_ref[...]) + _dot(r_lo, indt_ref[...])

    x = x_ref[...]
    rs = lax.rsqrt(row_ssq(x) * (1.0 / D_MODEL) + EPS)
    h = (x * jnp.tile(rs, (1, D_MODEL // LANES)) * gmix_ref[...]).astype(BF16)

    u_ref[...] = _dot(h, wu_ref[...]).astype(BF16)

    for b in range(N_BRANCH):
        sl = slice(b * D_MODEL, (b + 1) * D_MODEL)
        sg_ref[:, sl] = _sigmoid(_dot(h, wg_ref[:, sl])).astype(BF16)

    zq = _dot(h, wqm_ref[...])
    zr = lax.rsqrt(_dot((zq * zq).astype(BF16), bd_ref[...]) * (1.0 / MEM_HEAD_DIM) + EPS)
    qm_ref[...] = (zq * zr * gmq_ref[...]).astype(BF16)

    cq = _dot(h, wcq_ref[...])
    cq_rs = lax.rsqrt(row_ssq(cq) * (1.0 / Q_LORA) + EPS)
    cqn = (cq * jnp.tile(cq_rs, (1, Q_LORA // LANES)) * gql_ref[...]).astype(BF16)
    qq = _dot(cqn, wuq_ref[...])
    qf = qq[:, :MLA_WIDTH_PAD]
    qsw = qq[:, MLA_WIDTH_PAD:]

    z3 = _dot(h, wckv_ref[...])
    ckv = z3[:, :LANES]
    kr = z3[:, LANES:2 * LANES]
    kr_sw = z3[:, 2 * LANES:]
    ckv_rs = lax.rsqrt(row_ssq(ckv) * (1.0 / KV_LORA) + EPS)
    ckvn = (ckv * ckv_rs * gkvl_ref[...]).astype(BF16)
    kf = _dot(ckvn, wkn_ref[...]) + jnp.tile(kr, (1, MLA_HEADS))
    v_ref[...] = (_dot(ckvn, wv_ref[...]) + vone_ref[...]).astype(BF16)

    q_rs = head_scale(qf, MLA_QK ** -0.5 * LOG2E)
    k_rs = head_scale(kf, 1.0)
    aq = aq_ref[...]
    bq = bq_ref[...]
    ak = ak_ref[...]
    k_rot = kr_sw * bk_ref[...]
    for hh in range(MLA_HEADS):
        sl = slice(hh * HEAD_PAD, (hh + 1) * HEAD_PAD)
        q_ref[:, sl] = ((qf[:, sl] * aq + qsw[:, sl] * bq) * q_rs[:, sl]).astype(BF16)
        k_ref[:, sl] = ((kf[:, sl] * ak + k_rot) * k_rs[:, sl]).astype(BF16)


def _in_projection(x, lw, tiles):
    t_tok = x.shape[0]
    tm = tiles["tm"]
    pos_map = tiles["pos_map"]

    def row(i):
        return (i, 0)

    weights = [lw["g_mix"], lw["w_u"], lw["w_gate"], lw["w_qm"], lw["w_cq"], lw["w_ckv"],
               lw["g_q_lora"], lw["w_uq"], lw["g_kv_lora"], lw["w_kn"], lw["w_v"], lw["v_one"],
               lw["g_mem_q"], tiles["ones"], tiles["head_ind"], tiles["head_ind_t"], tiles["mem_bd"]]
    rope = [lw["rope_aq"], lw["rope_bq"], lw["rope_ak"], lw["rope_bk"]]
    in_specs = [pl.BlockSpec((tm, D_MODEL), row)] + [_const_spec(w.shape) for w in weights]
    in_specs += [pl.BlockSpec((tm, HEAD_PAD), lambda i: (pos_map(i), 0))] * len(rope)
    out_widths = (SSM_WIDTH, MLA_WIDTH_PAD, MLA_WIDTH_PAD, MLA_WIDTH_PAD, MEM_WIDTH, N_BRANCH * D_MODEL)
    return pl.pallas_call(
        _inproj_kernel,
        grid=(t_tok // tm,),
        in_specs=in_specs,
        out_specs=[pl.BlockSpec((tm, w), row) for w in out_widths],
        out_shape=[jax.ShapeDtypeStruct((t_tok, w), BF16) for w in out_widths],
        compiler_params=_cparams(("parallel",)),
        name="in_projection",
    )(x, *weights, *rope)


def _ssm_kernel(u_ref, t_ref, win_ref, wout_ref, a_ref, y_ref,
                sf_ref, sfs_ref, sb_ref, sbs_ref, yacc_ref, *, n_chunks, n_slots):
    gb = u_ref.shape[1]
    for j in range(gb):
        u = u_ref[0, j]
        yacc_ref[j] = _dot(u, t_ref[j])
        s4 = _dot(u, win_ref[j])
        sl = slice(j * LANES, (j + 1) * LANES)
        sf_ref[:, sl] = s4[:, 0 * LANES:1 * LANES]
        sfs_ref[:, sl] = s4[:, 1 * LANES:2 * LANES]
        sb_ref[:, sl] = s4[:, 2 * LANES:3 * LANES]
        sbs_ref[:, sl] = s4[:, 3 * LANES:4 * LANES]

    a1f = a_ref[0:1, :]
    a2f = a_ref[1:2, :]
    a1b = a_ref[2:3, :]
    a2b = a_ref[3:4, :]

    def step(i, carry):
        cf, cfs, cb, cbs = carry
        rf = pl.ds(pl.multiple_of(i * n_slots, n_slots), n_slots)
        rb = pl.ds(pl.multiple_of((n_chunks - 1 - i) * n_slots, n_slots), n_slots)
        tf = sf_ref[rf, :]
        tfs = sfs_ref[rf, :]
        tb = sb_ref[rb, :]
        tbs = sbs_ref[rb, :]
        sf_ref[rf, :] = cf
        sb_ref[rb, :] = cb
        return (a1f * cf + a2f * cfs + tf, a1f * cfs - a2f * cf + tfs,
                a1b * cb + a2b * cbs + tb, a1b * cbs - a2b * cb + tbs)

    zero = jnp.zeros((n_slots, gb * LANES), F32)
    lax.fori_loop(0, n_chunks, step, (zero, zero, zero, zero))

    for j in range(gb):
        sl = slice(j * LANES, (j + 1) * LANES)
        xs = jnp.concatenate([sf_ref[:, sl], sb_ref[:, sl]], axis=1).astype(BF16)
        y = yacc_ref[j] + _dot(xs, wout_ref[j])
        y_ref[0, j] = y.astype(BF16)


def _ssm_call(u_r, tabs, n_chunks):
    n_rg, n_g, rows, width = u_r.shape
    gb = max(1, min(SSM_GROUP_BLOCK, SSM_MAX_ROWS // rows))
    kern = functools.partial(_ssm_kernel, n_chunks=n_chunks, n_slots=SSM_SLOTS)
    return pl.pallas_call(
        kern,
        grid=(n_rg, n_g // gb),
        in_specs=[
            pl.BlockSpec((1, gb, rows, width), lambda r, g: (r, g, 0, 0)),
            pl.BlockSpec((gb, width, width), lambda r, g: (g, 0, 0)),
            pl.BlockSpec((gb, width, 4 * LANES), lambda r, g: (g, 0, 0)),
            pl.BlockSpec((gb, 2 * LANES, width), lambda r, g: (g, 0, 0)),
            pl.BlockSpec((4, gb * LANES), lambda r, g: (0, g)),
        ],
        out_specs=pl.BlockSpec((1, gb, rows, width), lambda r, g: (r, g, 0, 0)),
        out_shape=jax.ShapeDtypeStruct(u_r.shape, BF16),
        scratch_shapes=[pltpu.VMEM((rows, gb * LANES), F32)] * 4
        + [pltpu.VMEM((gb, rows, width), F32)],
        compiler_params=_cparams(("parallel", "parallel")),
        name="ssm_chunked",
    )(u_r, tabs["t"], tabs["w_in"], tabs["w_out"], tabs["a"])


def _to_chunks(u, n_seq, seq_len):
    n_rg = -(-n_seq // SSM_SLOTS)
    n_chunks = seq_len // SSM_CHUNK
    v = u.reshape(n_seq, seq_len, SSM_WIDTH)
    v = jnp.pad(v, ((0, n_rg * SSM_SLOTS - n_seq), (0, 0), (0, 0)))
    v = v.reshape(n_rg, SSM_SLOTS, n_chunks, SSM_CHUNK, SSM_GROUPS, SSM_GROUP)
    v = v.transpose(0, 4, 2, 1, 3, 5)
    return v.reshape(n_rg, SSM_GROUPS, n_chunks * SSM_SLOTS, SSM_CHUNK * SSM_GROUP)


def _from_chunks(y_r, n_seq, seq_len):
    n_rg = y_r.shape[0]
    n_chunks = seq_len // SSM_CHUNK
    v = y_r.reshape(n_rg, SSM_GROUPS, n_chunks, SSM_SLOTS, SSM_CHUNK, SSM_GROUP)
    v = v.transpose(0, 3, 2, 4, 1, 5)
    v = v.reshape(n_rg * SSM_SLOTS, seq_len, SSM_WIDTH)[:n_seq]
    return v.reshape(n_seq * seq_len, SSM_WIDTH)


def _ssm_tables(a_re, a_im, log_dt, b_re, b_im, c_re, c_im, d_skip):
    q = SSM_CHUNK
    hp = lax.Precision.HIGHEST
    dt = jnp.exp(log_dt)[..., None, None]
    ks = jnp.arange(q + 1, dtype=F32)
    mag = jnp.exp(a_re[..., None] * dt * ks)
    ang = a_im[..., None] * dt * ks
    pw_re = mag * jnp.cos(ang)
    pw_im = mag * jnp.sin(ang)
    lb_re, lb_im = pw_re[..., 1], pw_im[..., 1]
    den = a_re * a_re + a_im * a_im
    f_re = ((lb_re - 1.0) * a_re + lb_im * a_im) / den
    f_im = (lb_im * a_re - (lb_re - 1.0) * a_im) / den
    bb_re = f_re[..., None] * b_re - f_im[..., None] * b_im
    bb_im = f_re[..., None] * b_im + f_im[..., None] * b_re

    cl_re = c_re[..., None] * pw_re[:, :, None] - c_im[..., None] * pw_im[:, :, None]
    cl_im = c_re[..., None] * pw_im[:, :, None] + c_im[..., None] * pw_re[:, :, None]

    kern = (jnp.einsum("dgnpk,dgpm->dgknm", cl_re[..., :q], bb_re, precision=hp)
            - jnp.einsum("dgnpk,dgpm->dgknm", cl_im[..., :q], bb_im, precision=hp))
    s_idx = jnp.arange(q)[:, None]
    t_idx = jnp.arange(q)[None, :]
    lag_f = jnp.clip(t_idx - s_idx, 0, q - 1)
    lag_b = jnp.clip(s_idx - t_idx, 0, q - 1)
    tf = kern[0][:, lag_f] * (t_idx >= s_idx)[None, :, :, None, None]
    tb = kern[1][:, lag_b] * (s_idx >= t_idx)[None, :, :, None, None]
    t_mat = (tf + tb).transpose(0, 1, 4, 2, 3)
    t_mat = t_mat.reshape(SSM_GROUPS, q * SSM_GROUP, q * SSM_GROUP)
    d_rep = jnp.tile(d_skip.reshape(SSM_GROUPS, 1, SSM_GROUP), (1, q, 1)).reshape(SSM_GROUPS, -1)
    t_mat = t_mat + d_rep[:, :, None] * jnp.eye(q * SSM_GROUP, dtype=F32)[None]

    def w_in_dir(d, exps):
        pr = pw_re[d][:, :, exps]
        pi = pw_im[d][:, :, exps]
        re = pr[..., None] * bb_re[d][:, :, None, :] - pi[..., None] * bb_im[d][:, :, None, :]
        im = pr[..., None] * bb_im[d][:, :, None, :] + pi[..., None] * bb_re[d][:, :, None, :]
        re = re.transpose(0, 2, 3, 1).reshape(SSM_GROUPS, q * SSM_GROUP, SSM_STATE)
        im = im.transpose(0, 2, 3, 1).reshape(SSM_GROUPS, q * SSM_GROUP, SSM_STATE)
        return jnp.concatenate([re, im, im, re], axis=-1)

    w_in = jnp.concatenate([w_in_dir(0, q - 1 - jnp.arange(q)), w_in_dir(1, jnp.arange(q))], axis=-1)

    def w_out_dir(d, exps):
        re = cl_re[d][..., exps]
        im = cl_im[d][..., exps]
        re = re.transpose(0, 2, 3, 1).reshape(SSM_GROUPS, SSM_STATE, q * SSM_GROUP)
        im = im.transpose(0, 2, 3, 1).reshape(SSM_GROUPS, SSM_STATE, q * SSM_GROUP)
        return jnp.concatenate([re, -im], axis=1)

    w_out = jnp.concatenate([w_out_dir(0, 1 + jnp.arange(q)), w_out_dir(1, q - jnp.arange(q))], axis=1)

    def a_rows(d):
        ar = pw_re[d][..., q]
        ai = pw_im[d][..., q]
        a1 = jnp.concatenate([ar, ar], axis=-1).reshape(-1)
        a2 = jnp.concatenate([-ai, ai], axis=-1).reshape(-1)
        return [a1, a2]

    a_tab = jnp.stack(a_rows(0) + a_rows(1), axis=0)
    return {"t": t_mat.astype(BF16), "w_in": w_in.astype(BF16), "w_out": w_out.astype(BF16),
            "a": a_tab.astype(F32)}


def _flash_kernel(qb_ref, kb_ref, first_ref, last_ref, q_ref, k_ref, v_ref, o_ref,
                  acc_ref, *m_scratch, running_max):
    del qb_ref, kb_ref
    s_id = pl.program_id(1)

    @pl.when(first_ref[s_id] == 1)
    def _():
        acc_ref[...] = jnp.zeros(acc_ref.shape, F32)
        if running_max:
            m_scratch[0][...] = jnp.full(m_scratch[0].shape, -jnp.inf, F32)

    for hh in range(2):
        sl = slice(hh * HEAD_PAD, (hh + 1) * HEAD_PAD)
        s = lax.dot_general(q_ref[:, sl], k_ref[:, sl], (((1,), (1,)), ((), ())),
                            preferred_element_type=F32)
        if running_max:
            m_ref = m_scratch[0]
            m_prev = m_ref[hh]
            m_new = jnp.maximum(m_prev, jnp.max(s, axis=1, keepdims=True))
            p = jnp.exp2(s - m_new[:, :1]).astype(BF16)
            acc_ref[hh] = jnp.exp2(m_prev - m_new) * acc_ref[hh] + _dot(p, v_ref[:, sl])
            m_ref[hh] = m_new
        else:
            acc_ref[hh] += _dot(jnp.exp2(s).astype(BF16), v_ref[:, sl])

    @pl.when(last_ref[s_id] == 1)
    def _():
        lane = lax.broadcasted_iota(jnp.int32, o_ref.shape, 1)
        a0 = acc_ref[0]
        a1 = acc_ref[1]
        o0 = a0 / a0[:, MLA_V:MLA_V + 1]
        o1 = a1 / a1[:, 0:1]
        o_ref[...] = jnp.where(lane < MLA_V, o0, o1).astype(BF16)


def _flash_attention(q, k, v, seqs, tile, running_max):
    qb, kb, first, last = [], [], [], []
    base = 0
    for n_seq, seq_len in seqs:
        nb = seq_len // tile
        for _ in range(n_seq):
            for qi in range(nb):
                for ki in range(nb):
                    qb.append(base + qi)
                    kb.append(base + ki)
                    first.append(int(ki == 0))
                    last.append(int(ki == nb - 1))
            base += nb
    tabs = [jnp.asarray(np.asarray(a, np.int32)) for a in (qb, kb, first, last)]
    n_steps = len(qb)
    t_tok = q.shape[0]
    pair_w = 2 * HEAD_PAD
    scratch = [pltpu.VMEM((2, tile, LANES), F32)] * (2 if running_max else 1)
    grid_spec = pltpu.PrefetchScalarGridSpec(
        num_scalar_prefetch=4,
        grid=(MLA_HEADS // 2, n_steps),
        in_specs=[
            pl.BlockSpec((tile, pair_w), lambda p, s, qb, kb, fi, la: (qb[s], p)),
            pl.BlockSpec((tile, pair_w), lambda p, s, qb, kb, fi, la: (kb[s], p)),
            pl.BlockSpec((tile, pair_w), lambda p, s, qb, kb, fi, la: (kb[s], p)),
        ],
        out_specs=pl.BlockSpec((tile, 2 * MLA_V), lambda p, s, qb, kb, fi, la: (qb[s], p)),
        scratch_shapes=scratch,
    )
    return pl.pallas_call(
        functools.partial(_flash_kernel, running_max=running_max),
        grid_spec=grid_spec,
        out_shape=jax.ShapeDtypeStruct((t_tok, MLA_HEADS * MLA_V), BF16),
        compiler_params=_cparams(("parallel", "arbitrary")),
        name="mla_flash_max" if running_max else "mla_flash",
    )(*tabs, q, k, v)


def _memkv_kernel(mem_ref, gmem_ref, w_ref, gk_ref, k_ref, v_ref):
    x = mem_ref[...]
    h = (x * _rms_scale(x, D_MODEL) * gmem_ref[...]).astype(BF16)
    kv = _dot(h, w_ref[...])
    for hh in range(MEM_HEADS):
        sl = slice(hh * MEM_HEAD_DIM, (hh + 1) * MEM_HEAD_DIM)
        blk = kv[:, sl]
        k_ref[:, sl] = (blk * _rms_scale(blk, MEM_HEAD_DIM) * gk_ref[...]).astype(BF16)
    v_ref[...] = kv[:, MEM_WIDTH:].astype(BF16)


def _memory_kv(mem, lw):
    rows = mem.shape[0]
    tm = MEM_TOKENS
    return pl.pallas_call(
        _memkv_kernel,
        grid=(rows // tm,),
        in_specs=[pl.BlockSpec((tm, D_MODEL), lambda i: (i, 0)),
                  _const_spec(lw["g_mem"].shape), _const_spec(lw["w_mem_kv"].shape),
                  _const_spec(lw["g_mem_k"].shape)],
        out_specs=[pl.BlockSpec((tm, MEM_WIDTH), lambda i: (i, 0))] * 2,
        out_shape=[jax.ShapeDtypeStruct((rows, MEM_WIDTH), BF16)] * 2,
        compiler_params=_cparams(("parallel",)),
        name="memory_kv",
    )(mem, lw["g_mem"], lw["w_mem_kv"], lw["g_mem_k"])


def _gelu_tanh(x):
    return 0.5 * x * (1.0 + jnp.tanh(0.7978845608028654 * (x + 0.044715 * x * x * x)))


def _merge_kernel(*refs, with_router):
    if with_router:
        (x_ref, ys_ref, ya_ref, qm_ref, km_ref, vm_ref, sg_ref, wglu_ref, bglu_ref, wb_ref,
         wo_ref, gffn_ref, wr_hi_ref, wr_lo_ref, x1_ref, h2_ref, route_ref) = refs
    else:
        (x_ref, ys_ref, ya_ref, qm_ref, km_ref, vm_ref, sg_ref, wglu_ref, bglu_ref, wb_ref,
         wo_ref, gffn_ref, x1_ref, h2_ref) = refs

    ya = _gelu_tanh(ys_ref[...].astype(F32)).astype(BF16)
    z = _dot(ya, wglu_ref[...]) + bglu_ref[...]
    y_ssm = (z[:, :SSM_WIDTH] * _sigmoid(z[:, SSM_WIDTH:])).astype(BF16)

    mem_parts = []
    for hh in range(MEM_HEADS):
        sl = slice(hh * MEM_HEAD_DIM, (hh + 1) * MEM_HEAD_DIM)
        s = lax.dot_general(qm_ref[:, sl], km_ref[:, sl], (((1,), (1,)), ((), ())),
                            preferred_element_type=F32)
        p = jnp.exp(s - jnp.max(s, axis=1, keepdims=True))
        p = p / jnp.sum(p, axis=1, keepdims=True)
        mem_parts.append(_dot(p.astype(BF16), vm_ref[:, sl]))
    y_mem = jnp.concatenate(mem_parts, axis=1).astype(BF16)

    merged = None
    for b, yb in enumerate((y_ssm, ya_ref[...], y_mem)):
        gate = sg_ref[:, b * D_MODEL:(b + 1) * D_MODEL].astype(F32)
        term = gate * _dot(yb, wb_ref[b])
        merged = term if merged is None else merged + term

    x1 = x_ref[...] + _dot(merged.astype(BF16), wo_ref[...])
    x1_ref[...] = x1
    h2 = x1 * _rms_scale(x1, D_MODEL) * gffn_ref[...]

    if not with_router:
        h2_ref[...] = h2.astype(BF16)
        return

    for j in range(ROW_TILES):
        h2_ref[:, j, :] = h2[:, j * LANES:(j + 1) * LANES]

    h_hi, h_lo = _split_bf16(h2)
    logits = _dot(h_hi, wr_hi_ref[...]) + _dot(h_hi, wr_lo_ref[...]) + _dot(h_lo, wr_hi_ref[...])
    lane = lax.broadcasted_iota(jnp.int32, logits.shape, 1)
    lane_f = lane.astype(F32)
    neg = jnp.float32(-jnp.inf)
    big = jnp.float32(LANES)
    logits = jnp.where(lane < N_EXPERTS, logits, neg)
    m1 = jnp.max(logits, axis=1, keepdims=True)
    i1 = jnp.min(jnp.where(logits == m1, lane_f, big), axis=1, keepdims=True)
    rest = jnp.where(lane_f == i1, neg, logits)
    m2 = jnp.max(rest, axis=1, keepdims=True)
    i2 = jnp.min(jnp.where(rest == m2, lane_f, big), axis=1, keepdims=True)
    e2 = jnp.exp(m2 - m1)
    w1 = 1.0 / (1.0 + e2)
    w2 = e2 / (1.0 + e2)
    out = jnp.where(lane == 0, i1, jnp.where(lane == 1, i2, jnp.where(lane == 2, w1, w2)))
    route_ref[...] = jnp.where(lane < 4, out, 0.0)


def _merge(x, y_ssm_raw, y_mla, qm, km, vm, sg, lw, tiles, with_router):
    t_tok = x.shape[0]
    tm = tiles["tm"]
    seq_map = tiles["seq_map"]

    def row(i):
        return (i, 0)

    weights = [lw["w_glu"], lw["b_glu"], lw["w_branch"], lw["w_out"], lw["g_ffn"]]
    if with_router:
        weights += [lw["w_router_hi"], lw["w_router_lo"]]
    in_specs = [
        pl.BlockSpec((tm, D_MODEL), row),
        pl.BlockSpec((tm, SSM_WIDTH), row),
        pl.BlockSpec((tm, MLA_HEADS * MLA_V), row),
        pl.BlockSpec((tm, MEM_WIDTH), row),
        pl.BlockSpec((MEM_TOKENS, MEM_WIDTH), lambda i: (seq_map(i), 0)),
        pl.BlockSpec((MEM_TOKENS, MEM_WIDTH), lambda i: (seq_map(i), 0)),
        pl.BlockSpec((tm, N_BRANCH * D_MODEL), row),
    ] + [_const_spec(w.shape) for w in weights]
    out_specs = [pl.BlockSpec((tm, D_MODEL), row)]
    out_shape = [jax.ShapeDtypeStruct((t_tok, D_MODEL), F32)]
    if with_router:
        out_specs += [pl.BlockSpec((tm, ROW_TILES, LANES), lambda i: (i, 0, 0)),
                      pl.BlockSpec((tm, LANES), row)]
        out_shape += [jax.ShapeDtypeStruct((t_tok, ROW_TILES, LANES), F32),
                      jax.ShapeDtypeStruct((t_tok, LANES), F32)]
    else:
        out_specs.append(pl.BlockSpec((tm, D_MODEL), row))
        out_shape.append(jax.ShapeDtypeStruct((t_tok, D_MODEL), BF16))
    return pl.pallas_call(
        functools.partial(_merge_kernel, with_router=with_router),
        grid=(t_tok // tm,),
        in_specs=in_specs,
        out_specs=out_specs,
        out_shape=out_shape,
        compiler_params=_cparams(("parallel",)),
        name="merge_router" if with_router else "merge",
    )(x, y_ssm_raw, y_mla, qm, km, vm, sg, *weights)


def _swiglu_acc(x, w1_ref, w3_ref, w2_ref, acc, chunk):
    d_ff = w1_ref.shape[1]
    for c in range(d_ff // chunk):
        sl = slice(c * chunk, (c + 1) * chunk)
        g = _dot(x, w1_ref[:, sl])
        u = _dot(x, w3_ref[:, sl])
        a = (g * _sigmoid(g) * u).astype(BF16)
        acc = acc + _dot(a, w2_ref[sl, :])
    return acc


def _ffn_kernel(h_ref, x_ref, w1_ref, w3_ref, w2_ref, o_ref):
    o_ref[...] = _swiglu_acc(h_ref[...], w1_ref, w3_ref, w2_ref, x_ref[...], FFN_CHUNK)


def _dense_ffn(h2, x1, lw, tiles):
    t_tok = x1.shape[0]
    tm = tiles["tm"]

    def row(i):
        return (i, 0)

    return pl.pallas_call(
        _ffn_kernel,
        grid=(t_tok // tm,),
        in_specs=[pl.BlockSpec((tm, D_MODEL), row), pl.BlockSpec((tm, D_MODEL), row),
                  _const_spec(lw["ffn_w1"].shape), _const_spec(lw["ffn_w3"].shape),
                  _const_spec(lw["ffn_w2"].shape)],
        out_specs=pl.BlockSpec((tm, D_MODEL), row),
        out_shape=jax.ShapeDtypeStruct((t_tok, D_MODEL), F32),
        compiler_params=_cparams(("parallel",)),
        name="dense_ffn",
    )(h2, x1, lw["ffn_w1"], lw["ffn_w3"], lw["ffn_w2"])


def _dispatch_kernel(pos_ref, h_ref, xs_in_ref, xs_ref, sem):
    del xs_in_ref
    tm = h_ref.shape[0]

    def tile_copy(r, dst):
        return pltpu.make_async_copy(h_ref.at[r], xs_ref.at[dst], sem)

    def start(r, c):
        tile_copy(r, pos_ref[0, 0, 2 * r]).start()
        tile_copy(r, pos_ref[0, 0, 2 * r + 1]).start()
        return c

    def wait(r, c):
        tile_copy(r, 0).wait()
        tile_copy(r, 0).wait()
        return c

    lax.fori_loop(0, tm, start, 0, unroll=DMA_UNROLL)
    lax.fori_loop(0, tm, wait, 0, unroll=DMA_UNROLL)


def _dispatch(h2t, pos3, n_rows, tm):
    t_tok = h2t.shape[0]
    xs0 = jnp.zeros((n_rows, ROW_TILES, LANES), F32)
    return pl.pallas_call(
        _dispatch_kernel,
        grid=(t_tok // tm,),
        in_specs=[pl.BlockSpec((1, 1, 2 * tm), lambda i: (i, 0, 0), memory_space=pltpu.SMEM),
                  pl.BlockSpec((tm, ROW_TILES, LANES), lambda i: (i, 0, 0)),
                  pl.BlockSpec(memory_space=pl.ANY)],
        out_specs=pl.BlockSpec(memory_space=pl.ANY),
        out_shape=jax.ShapeDtypeStruct(xs0.shape, F32),
        input_output_aliases={2: 0},
        scratch_shapes=[pltpu.SemaphoreType.DMA(())],
        compiler_params=_cparams(("arbitrary",)),
        name="moe_dispatch",
    )(pos3, h2t, xs0)


def _expert_kernel(be_ref, nu_ref, x_ref, w1_ref, w3_ref, w2_ref, y_ref):
    del be_ref
    i = pl.program_id(0)

    @pl.when(i < nu_ref[0])
    def _():
        x = jnp.concatenate([x_ref[:, j, :] for j in range(ROW_TILES)], axis=1).astype(BF16)
        acc = _swiglu_acc(x, w1_ref, w3_ref, w2_ref, jnp.zeros((x.shape[0], D_MODEL), F32), MOE_CHUNK)
        for j in range(ROW_TILES):
            y_ref[:, j, :] = acc[:, j * LANES:(j + 1) * LANES]

    @pl.when(i >= nu_ref[0])
    def _():
        y_ref[...] = jnp.zeros(y_ref.shape, F32)


def _expert_ffn(xs, blk_expert, n_used, lw, bm):
    n_rows = xs.shape[0]

    def w_spec(shape):
        return pl.BlockSpec((None,) + shape, lambda i, be, nu: (be[i], 0, 0),
                            pipeline_mode=pl.Buffered(1))

    grid_spec = pltpu.PrefetchScalarGridSpec(
        num_scalar_prefetch=2,
        grid=(n_rows // bm,),
        in_specs=[
            pl.BlockSpec((bm, ROW_TILES, LANES), lambda i, be, nu: (i, 0, 0)),
            w_spec((D_MODEL, D_FF_EXPERT)), w_spec((D_MODEL, D_FF_EXPERT)),
            w_spec((D_FF_EXPERT, D_MODEL)),
        ],
        out_specs=pl.BlockSpec((bm, ROW_TILES, LANES), lambda i, be, nu: (i, 0, 0)),
    )
    return pl.pallas_call(
        _expert_kernel,
        grid_spec=grid_spec,
        out_shape=jax.ShapeDtypeStruct(xs.shape, F32),
        compiler_params=_cparams(("arbitrary",)),
        name="moe_experts",
    )(blk_expert, n_used, xs, lw["moe_w1"], lw["moe_w3"], lw["moe_w2"])


def _combine_kernel(pos_ref, ys_hbm, x_ref, route_ref, o_ref, buf_ref, sem):
    tm = o_ref.shape[0]

    def tile_copy(k, r, src):
        return pltpu.make_async_copy(ys_hbm.at[src], buf_ref.at[k, r], sem)

    def start(r, c):
        tile_copy(0, r, pos_ref[0, 0, 2 * r]).start()
        tile_copy(1, r, pos_ref[0, 0, 2 * r + 1]).start()
        return c

    def wait(r, c):
        tile_copy(0, r, 0).wait()
        tile_copy(1, r, 0).wait()
        return c

    lax.fori_loop(0, tm, start, 0, unroll=DMA_UNROLL)
    lax.fori_loop(0, tm, wait, 0, unroll=DMA_UNROLL)
    w1 = route_ref[:, 2:3]
    w2 = route_ref[:, 3:4]
    for j in range(ROW_TILES):
        sl = slice(j * LANES, (j + 1) * LANES)
        o_ref[:, sl] = x_ref[:, sl] + w1 * buf_ref[0, :, j, :] + w2 * buf_ref[1, :, j, :]


def _combine(ys, pos3, x1, route, tm):
    t_tok = x1.shape[0]
    return pl.pallas_call(
        _combine_kernel,
        grid=(t_tok // tm,),
        in_specs=[pl.BlockSpec((1, 1, 2 * tm), lambda i: (i, 0, 0), memory_space=pltpu.SMEM),
                  pl.BlockSpec(memory_space=pl.ANY),
                  pl.BlockSpec((tm, D_MODEL), lambda i: (i, 0)),
                  pl.BlockSpec((tm, LANES), lambda i: (i, 0))],
        out_specs=pl.BlockSpec((tm, D_MODEL), lambda i: (i, 0)),
        out_shape=jax.ShapeDtypeStruct((t_tok, D_MODEL), F32),
        scratch_shapes=[pltpu.VMEM((2, tm, ROW_TILES, LANES), F32), pltpu.SemaphoreType.DMA(())],
        compiler_params=_cparams(("arbitrary",)),
        name="moe_combine",
    )(pos3, ys, x1, route)


def _moe_ffn(h2t, x1, route, lw, tiles):
    t_tok = x1.shape[0]
    tm = tiles["tm"]
    bm = tiles["moe_block"]
    n_assign = 2 * t_tok
    experts = route[:, :2].astype(jnp.int32).reshape(n_assign)
    onehot = (experts[:, None] == jnp.arange(N_EXPERTS, dtype=jnp.int32)[None, :]).astype(jnp.int32)
    csum = jnp.cumsum(onehot, axis=0)
    rank = jnp.sum((csum - 1) * onehot, axis=1)
    counts = csum[-1]
    padded = ((counts + bm - 1) // bm) * bm
    ends = jnp.cumsum(padded)
    starts = ends - padded
    pos = jnp.sum(onehot * starts[None, :], axis=1) + rank
    n_blocks = n_assign // bm + N_EXPERTS
    blk_start = jnp.arange(n_blocks, dtype=jnp.int32) * bm
    blk_expert = jnp.sum((blk_start[:, None] >= ends[None, :]).astype(jnp.int32), axis=1)
    blk_expert = jnp.minimum(blk_expert, N_EXPERTS - 1)
    n_used = (ends[-1:] // bm).astype(jnp.int32)
    pos3 = pos.astype(jnp.int32).reshape(t_tok // tm, 1, 2 * tm)

    xs = _dispatch(h2t, pos3, n_blocks * bm, tm)
    ys = _expert_ffn(xs, blk_expert, n_used, lw, bm)
    return _combine(ys, pos3, x1, route, tm)


def _head_cols(w, per_head, n_heads, offset=0):
    k = w.shape[0]
    w = w.reshape(k, n_heads, per_head)
    w = jnp.pad(w, ((0, 0), (0, 0), (offset, HEAD_PAD - per_head - offset)))
    return w.reshape(k, n_heads * HEAD_PAD)


def _swap_rope(w):
    z = jnp.zeros_like(w[..., :MLA_NOPE])
    return jnp.concatenate([z, w[..., MLA_NOPE + HALF_ROPE:], w[..., MLA_NOPE:MLA_NOPE + HALF_ROPE]], axis=-1)


def _pad_lane_block(v, offset=0):
    return jnp.pad(v, ((0, 0),) * (v.ndim - 1) + ((offset, LANES - v.shape[-1] - offset),))


def _layer_weights(i, p, rope_cos, rope_sin):
    o = IN_OFFSETS
    w_in = p["w_in"][i]
    lw = {}
    lw["g_mix"] = p["g_mix"][i].reshape(1, D_MODEL)
    lw["w_u"] = w_in[:, :o[0]].astype(BF16)
    lw["w_cq"] = w_in[:, o[0]:o[1]].astype(BF16)
    w_kr = w_in[:, o[2]:o[3]]
    w_kr_sw = jnp.concatenate([w_kr[:, HALF_ROPE:], w_kr[:, :HALF_ROPE]], axis=1)
    lw["w_ckv"] = jnp.concatenate([w_in[:, o[1]:o[2]], _pad_lane_block(w_kr, MLA_NOPE),
                                   _pad_lane_block(w_kr_sw, MLA_NOPE)], axis=1).astype(BF16)
    lw["w_qm"] = w_in[:, o[3]:o[4]].astype(BF16)
    lw["w_gate"] = w_in[:, o[4]:].astype(BF16)
    lw["g_q_lora"] = p["g_q_lora"][i].reshape(1, Q_LORA)
    w_uq = p["w_uq"][i].reshape(Q_LORA, MLA_HEADS, MLA_QK)
    lw["w_uq"] = jnp.concatenate(
        [_head_cols(w_uq.reshape(Q_LORA, -1), MLA_QK, MLA_HEADS),
         _head_cols(_swap_rope(w_uq).reshape(Q_LORA, -1), MLA_QK, MLA_HEADS)], axis=1).astype(BF16)
    lw["g_kv_lora"] = p["g_kv_lora"][i].reshape(1, KV_LORA)
    w_ukv = p["w_ukv"][i].reshape(KV_LORA, MLA_HEADS, MLA_NOPE + MLA_V)
    lw["w_kn"] = _head_cols(w_ukv[:, :, :MLA_NOPE].reshape(KV_LORA, -1), MLA_NOPE, MLA_HEADS).astype(BF16)
    w_v = w_ukv[:, :, MLA_NOPE:].reshape(KV_LORA, MLA_HEADS // 2, 2, MLA_V)
    w_v = jnp.stack([jnp.pad(w_v[:, :, 0], ((0, 0), (0, 0), (0, HEAD_PAD - MLA_V))),
                     jnp.pad(w_v[:, :, 1], ((0, 0), (0, 0), (HEAD_PAD - MLA_V, 0)))], axis=2)
    lw["w_v"] = w_v.reshape(KV_LORA, MLA_WIDTH_PAD).astype(BF16)
    one_even = jnp.zeros((HEAD_PAD,), F32).at[MLA_V].set(1.0)
    one_odd = jnp.zeros((HEAD_PAD,), F32).at[0].set(1.0)
    lw["v_one"] = jnp.tile(jnp.concatenate([one_even, one_odd]), MLA_HEADS // 2).reshape(1, MLA_WIDTH_PAD)
    for name, g in (("q", p["g_mla_q"][i]), ("k", p["g_mla_k"][i])):
        lw["rope_a" + name] = rope_cos * _pad_lane_block(g)[None, :]
        lw["rope_b" + name] = rope_sin * _pad_lane_block(_swap_rope(g))[None, :]
    lw["g_mem_q"] = jnp.tile(p["g_mem_q"][i], MEM_HEADS).reshape(1, MEM_WIDTH) * (MEM_HEAD_DIM ** -0.5)
    lw["g_mem"] = p["g_mem"][i].reshape(1, D_MODEL)
    lw["w_mem_kv"] = p["w_mem_kv"][i].astype(BF16)
    lw["g_mem_k"] = p["g_mem_k"][i].reshape(1, MEM_HEAD_DIM)
    lw["w_glu"] = p["ssm_w_glu"][i].astype(BF16)
    lw["b_glu"] = p["ssm_b_glu"][i].reshape(1, 2 * SSM_WIDTH)
    lw["w_branch"] = p["w_branch"][i].astype(BF16)
    lw["w_out"] = p["w_out"][i].astype(BF16)
    lw["g_ffn"] = p["g_ffn"][i].reshape(1, D_MODEL)
    lw["ssm"] = _ssm_tables(p["ssm_a_re"][i], p["ssm_a_im"][i], p["ssm_log_dt"][i], p["ssm_b_re"][i],
                            p["ssm_b_im"][i], p["ssm_c_re"][i], p["ssm_c_im"][i], p["ssm_d"][i])
    lw["score_bound"] = (1.02 * MLA_QK ** 0.5 * LOG2E
                         * jnp.max(jnp.abs(p["g_mla_q"][i])) * jnp.max(jnp.abs(p["g_mla_k"][i])))
    if i % 2 == 0:
        w13 = p["ffn_w13"][i // 2]
        lw["ffn_w1"] = w13[:, :D_FF].astype(BF16)
        lw["ffn_w3"] = w13[:, D_FF:].astype(BF16)
        lw["ffn_w2"] = p["ffn_w2"][i // 2].astype(BF16)
    else:
        w13 = p["moe_w13"][i // 2]
        lw["moe_w1"] = w13[:, :, :D_FF_EXPERT].astype(BF16)
        lw["moe_w3"] = w13[:, :, D_FF_EXPERT:].astype(BF16)
        lw["moe_w2"] = p["moe_w2"][i // 2].astype(BF16)
        wr = jnp.pad(p["moe_router"][i // 2], ((0, 0), (0, LANES - N_EXPERTS)))
        lw["w_router_hi"], lw["w_router_lo"] = _split_bf16(wr)
    return lw


def _rope_tables(max_len):
    inv = 1.0 / (ROPE_BASE ** (jnp.arange(0, MLA_ROPE, 2, dtype=F32) / MLA_ROPE))
    ang = jnp.arange(max_len, dtype=F32)[:, None] * inv[None, :]
    cos, sin = jnp.cos(ang), jnp.sin(ang)
    ones = jnp.ones((max_len, MLA_NOPE), F32)
    zn = jnp.zeros((max_len, MLA_NOPE), F32)
    zp = jnp.zeros((max_len, HEAD_PAD - MLA_QK), F32)
    return (jnp.concatenate([ones, cos, cos, zp], axis=1),
            jnp.concatenate([zn, -sin, sin, zp], axis=1))


def _gcd_tile(limit, *sizes):
    t = limit
    while any(s % t for s in sizes):
        t //= 2
    return t


def kernel(x_prompt, x_sample, mem_prompt, mem_sample, g_mix, w_in, ssm_a_re, ssm_a_im, ssm_log_dt, ssm_b_re, ssm_b_im, ssm_c_re, ssm_c_im, ssm_d, ssm_w_glu, ssm_b_glu, g_q_lora, w_uq, g_kv_lora, w_ukv, g_mla_q, g_mla_k, g_mem, w_mem_kv, g_mem_q, g_mem_k, w_branch, w_out, g_ffn, ffn_w13, ffn_w2, moe_router, moe_w13, moe_w2):
    params = dict(g_mix=g_mix, w_in=w_in, ssm_a_re=ssm_a_re, ssm_a_im=ssm_a_im, ssm_log_dt=ssm_log_dt,
                  ssm_b_re=ssm_b_re, ssm_b_im=ssm_b_im, ssm_c_re=ssm_c_re, ssm_c_im=ssm_c_im, ssm_d=ssm_d,
                  ssm_w_glu=ssm_w_glu, ssm_b_glu=ssm_b_glu, g_q_lora=g_q_lora, w_uq=w_uq,
                  g_kv_lora=g_kv_lora, w_ukv=w_ukv, g_mla_q=g_mla_q, g_mla_k=g_mla_k, g_mem=g_mem,
                  w_mem_kv=w_mem_kv, g_mem_q=g_mem_q, g_mem_k=g_mem_k, w_branch=w_branch, w_out=w_out,
                  g_ffn=g_ffn, ffn_w13=ffn_w13, ffn_w2=ffn_w2, moe_router=moe_router, moe_w13=moe_w13,
                  moe_w2=moe_w2)
    depth = g_mix.shape[0]
    bp, lp, _ = x_prompt.shape
    bs, ls, _ = x_sample.shape
    tp, ts = bp * lp, bs * ls
    t_tok = tp + ts

    tm = _gcd_tile(TOKEN_TILE, lp, ls)
    attn_tile = _gcd_tile(ATTN_TILE, lp, ls)
    npt = tp // tm
    lp_t, ls_t = lp // tm, ls // tm

    def pos_map(i):
        return jnp.where(i < npt, i % lp_t, (i - npt) % ls_t)

    def seq_map(i):
        return jnp.where(i < npt, i // lp_t, bp + (i - npt) // ls_t)

    rope_cos, rope_sin = _rope_tables(max(lp, ls))
    head_ind = (jnp.arange(MLA_WIDTH_PAD)[:, None] // HEAD_PAD == jnp.arange(LANES)[None, :])
    mem_bd = (jnp.arange(MEM_WIDTH)[:, None] // MEM_HEAD_DIM == jnp.arange(MEM_WIDTH)[None, :] // MEM_HEAD_DIM)
    tiles = dict(tm=tm, pos_map=pos_map, seq_map=seq_map,
                 ones=jnp.ones((D_MODEL, LANES), BF16),
                 head_ind=head_ind.astype(BF16), head_ind_t=head_ind.T.astype(BF16),
                 mem_bd=mem_bd.astype(BF16),
                 moe_block=_gcd_tile(MOE_BLOCK, 2 * t_tok))

    x = jnp.concatenate([x_prompt.reshape(tp, D_MODEL), x_sample.reshape(ts, D_MODEL)], axis=0)
    mem = jnp.concatenate([mem_prompt.reshape(bp * MEM_TOKENS, D_MODEL),
                           mem_sample.reshape(bs * MEM_TOKENS, D_MODEL)], axis=0)
    seqs = [(bp, lp), (bs, ls)]

    for i in range(depth):
        lw = _layer_weights(i, params, rope_cos, rope_sin)
        u, q, k, v, qm, sg = _in_projection(x, lw, tiles)

        y_parts = []
        for (n_seq, seq_len, lo, hi) in ((bp, lp, 0, tp), (bs, ls, tp, t_tok)):
            u_r = _to_chunks(u[lo:hi], n_seq, seq_len)
            y_r = _ssm_call(u_r, lw["ssm"], seq_len // SSM_CHUNK)
            y_parts.append(_from_chunks(y_r, n_seq, seq_len))
        y_ssm_raw = jnp.concatenate(y_parts, axis=0)

        y_mla = lax.cond(
            lw["score_bound"] <= MAX_SAFE_LOG2_SCORE,
            lambda q_, k_, v_: _flash_attention(q_, k_, v_, seqs, attn_tile, running_max=False),
            lambda q_, k_, v_: _flash_attention(q_, k_, v_, seqs, attn_tile, running_max=True),
            q, k, v)
        km, vm = _memory_kv(mem, lw)

        with_router = i % 2 == 1
        outs = _merge(x, y_ssm_raw, y_mla, qm, km, vm, sg, lw, tiles, with_router)
        if with_router:
            x1, h2t, route = outs
            x = _moe_ffn(h2t, x1, route, lw, tiles)
        else:
            x1, h2 = outs
            x = _dense_ffn(h2, x1, lw, tiles)

    return (x[:tp].reshape(bp, lp, D_MODEL), x[tp:].reshape(bs, ls, D_MODEL))
```

```python
import functools
import math

import numpy as np
import jax
import jax.numpy as jnp
from jax import lax
from jax.experimental import pallas as pl
from jax.experimental.pallas import tpu as pltpu

F32 = jnp.float32
BF16 = jnp.bfloat16
EPS = 1e-6
LOG2E = math.log2(math.e)

D_MODEL = 1024
SSM_WIDTH = 512
SSM_GROUPS = 32
SSM_GROUP = 16
SSM_STATE = 64
MLA_HEADS = 8
MLA_NOPE = 64
MLA_ROPE = 32
MLA_V = 64
MLA_QK = MLA_NOPE + MLA_ROPE
Q_LORA = 256
KV_LORA = 128
ROPE_BASE = 10000.0
MEM_TOKENS = 256
MEM_HEADS = 4
MEM_HEAD_DIM = 128
MEM_WIDTH = 512
N_BRANCH = 3
D_FF = 2816
N_EXPERTS = 8
D_FF_EXPERT = 3584
IN_SIZES = (SSM_WIDTH, Q_LORA, KV_LORA, MLA_ROPE, MEM_WIDTH, N_BRANCH * D_MODEL)
IN_OFFSETS = tuple(int(v) for v in np.cumsum(IN_SIZES)[:-1])

LANES = 128
SUBLANES = 8
HEAD_PAD = 128
HALF_ROPE = MLA_ROPE // 2
MLA_WIDTH_PAD = MLA_HEADS * HEAD_PAD
ROW_TILES = D_MODEL // LANES
VMEM_LIMIT = 56 * 1024 * 1024
MAX_SAFE_LOG2_SCORE = 100.0

TOKEN_TILE = 512
ATTN_TILE = 1024
SSM_CHUNK = 16
MOE_BLOCK = 512
FFN_CHUNK = 1408
MOE_CHUNK = 512
DMA_UNROLL = 8


def _cparams(sem):
    return pltpu.CompilerParams(dimension_semantics=sem, vmem_limit_bytes=VMEM_LIMIT)


def _const_spec(shape):
    nd = len(shape)
    return pl.BlockSpec(shape, lambda *_: (0,) * nd)


def _sigmoid(x):
    return 1.0 / (1.0 + jnp.exp(-x))


def _rms_scale(x, n):
    return lax.rsqrt(jnp.sum(x * x, axis=-1, keepdims=True) * (1.0 / n) + EPS)


def _split_bf16(x):
    hi = x.astype(BF16)
    lo = (x - hi.astype(F32)).astype(BF16)
    return hi, lo


def _dot(a, b):
    return jnp.dot(a, b, preferred_element_type=F32)


def _inproj_kernel(x_ref, gmix_ref, wu_ref, wg_ref, wqm_ref, wcq_ref, wckv_ref,
                   gql_ref, wuq_ref, gkvl_ref, wkn_ref, wv_ref, vone_ref, gmq_ref,
                   ones_ref, ind_ref, indt_ref, bd_ref,
                   aq_ref, bq_ref, ak_ref, bk_ref,
                   u_ref, q_ref, k_ref, v_ref, qm_ref, sg_ref):
    def row_ssq(val):
        n = val.shape[1]
        return _dot((val * val).astype(BF16), ones_ref[:n, :])

    def head_scale(val, scale):
        ss = _dot((val * val).astype(BF16), ind_ref[...])
        r = lax.rsqrt(ss * (1.0 / MLA_QK) + EPS) * scale
        r_hi, r_lo = _split_bf16(r)
        return _dot(r_hi, indt_ref[...]) + _dot(r_lo, indt_ref[...])

    x = x_ref[...]
    rs = lax.rsqrt(row_ssq(x) * (1.0 / D_MODEL) + EPS)
    h = (x * jnp.tile(rs, (1, D_MODEL // LANES)) * gmix_ref[...]).astype(BF16)

    u_ref[...] = _dot(h, wu_ref[...])

    for b in range(N_BRANCH):
        sl = slice(b * D_MODEL, (b + 1) * D_MODEL)
        sg_ref[:, sl] = _sigmoid(_dot(h, wg_ref[:, sl])).astype(BF16)

    zq = _dot(h, wqm_ref[...])
    zr = lax.rsqrt(_dot((zq * zq).astype(BF16), bd_ref[...]) * (1.0 / MEM_HEAD_DIM) + EPS)
    qm_ref[...] = (zq * zr * gmq_ref[...]).astype(BF16)

    cq = _dot(h, wcq_ref[...])
    cq_rs = lax.rsqrt(row_ssq(cq) * (1.0 / Q_LORA) + EPS)
    cqn = (cq * jnp.tile(cq_rs, (1, Q_LORA // LANES)) * gql_ref[...]).astype(BF16)
    qq = _dot(cqn, wuq_ref[...])
    qf = qq[:, :MLA_WIDTH_PAD]
    qsw = qq[:, MLA_WIDTH_PAD:]

    z3 = _dot(h, wckv_ref[...])
    ckv = z3[:, :LANES]
    kr = z3[:, LANES:2 * LANES]
    kr_sw = z3[:, 2 * LANES:]
    ckv_rs = lax.rsqrt(row_ssq(ckv) * (1.0 / KV_LORA) + EPS)
    ckvn = (ckv * ckv_rs * gkvl_ref[...]).astype(BF16)
    kf = _dot(ckvn, wkn_ref[...]) + jnp.tile(kr, (1, MLA_HEADS))
    v_ref[...] = (_dot(ckvn, wv_ref[...]) + vone_ref[...]).astype(BF16)

    q_rs = head_scale(qf, MLA_QK ** -0.5 * LOG2E)
    k_rs = head_scale(kf, 1.0)
    aq = aq_ref[...]
    bq = bq_ref[...]
    ak = ak_ref[...]
    k_rot = kr_sw * bk_ref[...]
    for hh in range(MLA_HEADS):
        sl = slice(hh * HEAD_PAD, (hh + 1) * HEAD_PAD)
        q_ref[:, sl] = ((qf[:, sl] * aq + qsw[:, sl] * bq) * q_rs[:, sl]).astype(BF16)
        k_ref[:, sl] = ((kf[:, sl] * ak + k_rot) * k_rs[:, sl]).astype(BF16)


def _in_projection(x, lw, tiles):
    t_tok = x.shape[0]
    tm = tiles["tm"]
    pos_map = tiles["pos_map"]

    def row(i):
        return (i, 0)

    weights = [lw["g_mix"], lw["w_u"], lw["w_gate"], lw["w_qm"], lw["w_cq"], lw["w_ckv"],
               lw["g_q_lora"], lw["w_uq"], lw["g_kv_lora"], lw["w_kn"], lw["w_v"], lw["v_one"],
               lw["g_mem_q"], tiles["ones"], tiles["head_ind"], tiles["head_ind_t"], tiles["mem_bd"]]
    rope = [lw["rope_aq"], lw["rope_bq"], lw["rope_ak"], lw["rope_bk"]]
    in_specs = [pl.BlockSpec((tm, D_MODEL), row)] + [_const_spec(w.shape) for w in weights]
    in_specs += [pl.BlockSpec((tm, HEAD_PAD), lambda i: (pos_map(i), 0))] * len(rope)
    out_widths = (SSM_WIDTH, MLA_WIDTH_PAD, MLA_WIDTH_PAD, MLA_WIDTH_PAD, MEM_WIDTH, N_BRANCH * D_MODEL)
    return pl.pallas_call(
        _inproj_kernel,
        grid=(t_tok // tm,),
        in_specs=in_specs,
        out_specs=[pl.BlockSpec((tm, w), row) for w in out_widths],
        out_shape=[jax.ShapeDtypeStruct((t_tok, w), F32 if j == 0 else BF16) for j, w in enumerate(out_widths)],
        compiler_params=_cparams(("parallel",)),
        name="in_projection",
    )(x, *weights, *rope)


SSM_LANE_GROUPS = LANES // SSM_GROUP
SSM_TAB_DIR = 11
(_TAB_STEP, _TAB_POW, _TAB_A8, _TAB_MASK) = (0, 6, 8, 10)


def _granule_transpose(vs, lane):
    for dist in (4, 2, 1):
        shift = dist * SSM_GROUP
        low = (lane // shift) % 2 == 0
        new = list(vs)
        for a in range(SSM_LANE_GROUPS):
            if a & dist == 0:
                b = a | dist
                new[a] = jnp.where(low, vs[a], pltpu.roll(vs[b], shift, 1))
                new[b] = jnp.where(low, pltpu.roll(vs[a], LANES - shift, 1), vs[b])
        vs = new
    return vs


def _chunk_scan(x, xs, tab_ref, base, keep, backward):
    nv, _, nl = x.shape
    for j, k in enumerate((1, 2, 4)):
        shift = SUBLANES - k if backward else k
        rx = pltpu.roll(x, shift, 1)
        rxs = pltpu.roll(xs, shift, 1)
        a1 = tab_ref[base + _TAB_STEP + 2 * j][None]
        a2 = tab_ref[base + _TAB_STEP + 2 * j + 1][None]
        x, xs = x + a1 * rx + a2 * rxs, xs + a1 * rxs - a2 * rx
    a1 = tab_ref[base + _TAB_A8]
    a2 = tab_ref[base + _TAB_A8 + 1]
    edge = 0 if backward else SUBLANES - 1
    c = jnp.zeros((SUBLANES, nl), F32)
    cs = c
    cb = [None] * nv
    csb = [None] * nv
    for v in (range(nv - 1, -1, -1) if backward else range(nv)):
        kp = keep(v)
        c = c * kp
        cs = cs * kp
        cb[v] = c
        csb[v] = cs
        ex = jnp.broadcast_to(x[v, edge:edge + 1, :], (SUBLANES, nl))
        exs = jnp.broadcast_to(xs[v, edge:edge + 1, :], (SUBLANES, nl))
        c, cs = ex + a1 * c + a2 * cs, exs + a1 * cs - a2 * c
    xe = pltpu.roll(x, SUBLANES - 1 if backward else 1, 1) * tab_ref[base + _TAB_MASK][None]
    return (xe + tab_ref[base + _TAB_POW][None] * jnp.stack(cb)
            + tab_ref[base + _TAB_POW + 1][None] * jnp.stack(csb))


def _ssm_kernel(keepf_ref, keepb_ref, u_ref, t_ref, win_ref, wout_ref, tab_ref, y_ref):
    blk = pl.program_id(1)
    rows = u_ref.shape[0] // SSM_CHUNK
    nv = rows // SUBLANES
    ng = SSM_LANE_GROUPS
    lane = lax.broadcasted_iota(jnp.int32, (rows, LANES), 1)

    pieces = [u_ref[pl.ds(t, rows, stride=SSM_CHUNK), :] for t in range(SSM_CHUNK)]
    halves = [_granule_transpose(pieces[ng * h:ng * (h + 1)], lane) for h in range(SSM_CHUNK // ng)]

    ys = []
    parts = [[], [], [], []]
    for g in range(ng):
        xg = jnp.concatenate([h[g] for h in halves], axis=1).astype(BF16)
        ys.append(_dot(xg, t_ref[g]))
        s4 = _dot(xg, win_ref[g])
        for j in range(4):
            parts[j].append(s4[:, j * LANES:(j + 1) * LANES])
    f, fs, b, bs = [jnp.concatenate(p, axis=1).reshape(nv, SUBLANES, ng * LANES) for p in parts]

    xin_f = _chunk_scan(f, fs, tab_ref, 0, lambda v: keepf_ref[blk * nv + v].astype(F32), False)
    xin_b = _chunk_scan(b, bs, tab_ref, SSM_TAB_DIR, lambda v: keepb_ref[blk * nv + v].astype(F32), True)
    xin_f = xin_f.reshape(rows, ng * LANES)
    xin_b = xin_b.reshape(rows, ng * LANES)

    outs = []
    for g in range(ng):
        sl = slice(g * LANES, (g + 1) * LANES)
        xs = jnp.concatenate([xin_f[:, sl], xin_b[:, sl]], axis=1).astype(BF16)
        outs.append(ys[g] + _dot(xs, wout_ref[g]))
    for h in range(SSM_CHUNK // ng):
        back = _granule_transpose([o[:, h * LANES:(h + 1) * LANES] for o in outs], lane)
        for tt in range(ng):
            y_ref[pl.ds(ng * h + tt, rows, stride=SSM_CHUNK), :] = back[tt]


def _ssm_call(u, tabs, seqs):
    t_tok = u.shape[0]
    block = int(np.lcm.reduce([ln for _, ln in seqs]))
    vreg_tokens = SUBLANES * SSM_CHUNK
    assert all((n * ln) % block == 0 and ln % vreg_tokens == 0 for n, ln in seqs), seqs
    nv = block // vreg_tokens
    keep_f, keep_b = [], []
    for n_seq, seq_len in seqs:
        for _ in range(n_seq * seq_len // block):
            for v in range(nv):
                keep_f.append(int((v * vreg_tokens) % seq_len != 0))
                keep_b.append(int(((v + 1) * vreg_tokens) % seq_len != 0))
    keep = [jnp.asarray(np.asarray(a, np.int32)) for a in (keep_f, keep_b)]
    ng = SSM_LANE_GROUPS
    width = SSM_CHUNK * SSM_GROUP
    n_tab = 2 * SSM_TAB_DIR
    grid_spec = pltpu.PrefetchScalarGridSpec(
        num_scalar_prefetch=2,
        grid=(SSM_WIDTH // LANES, t_tok // block),
        in_specs=[
            pl.BlockSpec((block, LANES), lambda g, b, kf, kb: (b, g)),
            pl.BlockSpec((ng, width, width), lambda g, b, kf, kb: (g, 0, 0)),
            pl.BlockSpec((ng, width, 4 * LANES), lambda g, b, kf, kb: (g, 0, 0)),
            pl.BlockSpec((ng, 2 * LANES, width), lambda g, b, kf, kb: (g, 0, 0)),
            pl.BlockSpec((n_tab, SUBLANES, ng * LANES), lambda g, b, kf, kb: (0, 0, g)),
        ],
        out_specs=pl.BlockSpec((block, LANES), lambda g, b, kf, kb: (b, g)),
    )
    return pl.pallas_call(
        _ssm_kernel,
        grid_spec=grid_spec,
        out_shape=jax.ShapeDtypeStruct(u.shape, F32),
        compiler_params=_cparams(("parallel", "parallel")),
        name="ssm_chunked",
    )(*keep, u, tabs["t"], tabs["w_in"], tabs["w_out"], tabs["scan"])


def _ssm_tables(a_re, a_im, log_dt, b_re, b_im, c_re, c_im, d_skip):
    q = SSM_CHUNK
    hp = lax.Precision.HIGHEST
    dt = jnp.exp(log_dt)[..., None, None]
    ks = jnp.arange(q + 1, dtype=F32)
    mag = jnp.exp(a_re[..., None] * dt * ks)
    ang = a_im[..., None] * dt * ks
    pw_re = mag * jnp.cos(ang)
    pw_im = mag * jnp.sin(ang)
    lb_re, lb_im = pw_re[..., 1], pw_im[..., 1]
    den = a_re * a_re + a_im * a_im
    f_re = ((lb_re - 1.0) * a_re + lb_im * a_im) / den
    f_im = (lb_im * a_re - (lb_re - 1.0) * a_im) / den
    bb_re = f_re[..., None] * b_re - f_im[..., None] * b_im
    bb_im = f_re[..., None] * b_im + f_im[..., None] * b_re

    cl_re = c_re[..., None] * pw_re[:, :, None] - c_im[..., None] * pw_im[:, :, None]
    cl_im = c_re[..., None] * pw_im[:, :, None] + c_im[..., None] * pw_re[:, :, None]

    kern = (jnp.einsum("dgnpk,dgpm->dgknm", cl_re[..., :q], bb_re, precision=hp)
            - jnp.einsum("dgnpk,dgpm->dgknm", cl_im[..., :q], bb_im, precision=hp))
    s_idx = jnp.arange(q)[:, None]
    t_idx = jnp.arange(q)[None, :]
    lag_f = jnp.clip(t_idx - s_idx, 0, q - 1)
    lag_b = jnp.clip(s_idx - t_idx, 0, q - 1)
    tf = kern[0][:, lag_f] * (t_idx >= s_idx)[None, :, :, None, None]
    tb = kern[1][:, lag_b] * (s_idx >= t_idx)[None, :, :, None, None]
    t_mat = (tf + tb).transpose(0, 1, 4, 2, 3)
    t_mat = t_mat.reshape(SSM_GROUPS, q * SSM_GROUP, q * SSM_GROUP)
    d_rep = jnp.tile(d_skip.reshape(SSM_GROUPS, 1, SSM_GROUP), (1, q, 1)).reshape(SSM_GROUPS, -1)
    t_mat = t_mat + d_rep[:, :, None] * jnp.eye(q * SSM_GROUP, dtype=F32)[None]

    def w_in_dir(d, exps):
        pr = pw_re[d][:, :, exps]
        pi = pw_im[d][:, :, exps]
        re = pr[..., None] * bb_re[d][:, :, None, :] - pi[..., None] * bb_im[d][:, :, None, :]
        im = pr[..., None] * bb_im[d][:, :, None, :] + pi[..., None] * bb_re[d][:, :, None, :]
        re = re.transpose(0, 2, 3, 1).reshape(SSM_GROUPS, q * SSM_GROUP, SSM_STATE)
        im = im.transpose(0, 2, 3, 1).reshape(SSM_GROUPS, q * SSM_GROUP, SSM_STATE)
        return jnp.concatenate([re, im, im, re], axis=-1)

    w_in = jnp.concatenate([w_in_dir(0, q - 1 - jnp.arange(q)), w_in_dir(1, jnp.arange(q))], axis=-1)

    def w_out_dir(d, exps):
        re = cl_re[d][..., exps]
        im = cl_im[d][..., exps]
        re = re.transpose(0, 2, 3, 1).reshape(SSM_GROUPS, SSM_STATE, q * SSM_GROUP)
        im = im.transpose(0, 2, 3, 1).reshape(SSM_GROUPS, SSM_STATE, q * SSM_GROUP)
        return jnp.concatenate([re, -im], axis=1)

    w_out = jnp.concatenate([w_out_dir(0, 1 + jnp.arange(q)), w_out_dir(1, q - jnp.arange(q))], axis=1)

    ms = jnp.arange(SUBLANES + 1, dtype=F32) * q
    cr = jnp.exp(a_re[..., None] * dt * ms) * jnp.cos(a_im[..., None] * dt * ms)
    ci = jnp.exp(a_re[..., None] * dt * ms) * jnp.sin(a_im[..., None] * dt * ms)

    def pat(d, m):
        ar, ai = cr[d][..., m], ci[d][..., m]
        return (jnp.concatenate([ar, ar], axis=-1).reshape(-1), jnp.concatenate([-ai, ai], axis=-1).reshape(-1))

    row = jnp.arange(SUBLANES)

    def dir_tables(d, backward):
        tabs = []
        for k in (1, 2, 4):
            valid = (row <= SUBLANES - 1 - k) if backward else (row >= k)
            tabs += [valid[:, None] * p[None, :] for p in pat(d, k)]
        pows = [pat(d, SUBLANES - 1 - i if backward else i) for i in range(SUBLANES)]
        tabs += [jnp.stack([p[0] for p in pows]), jnp.stack([p[1] for p in pows])]
        tabs += [jnp.tile(p[None, :], (SUBLANES, 1)) for p in pat(d, SUBLANES)]
        excl = (row <= SUBLANES - 2) if backward else (row >= 1)
        tabs.append(jnp.tile(excl[:, None].astype(F32), (1, SSM_GROUPS * LANES)))
        return tabs

    scan_tab = jnp.stack(dir_tables(0, False) + dir_tables(1, True), axis=0).astype(F32)
    return {"t": t_mat.astype(BF16), "w_in": w_in.astype(BF16), "w_out": w_out.astype(BF16),
            "scan": scan_tab}


def _flash_kernel(qb_ref, kb_ref, first_ref, last_ref, q_ref, k_ref, v_ref, o_ref,
                  acc_ref, *m_scratch, running_max):
    del qb_ref, kb_ref
    s_id = pl.program_id(1)

    @pl.when(first_ref[s_id] == 1)
    def _():
        acc_ref[...] = jnp.zeros(acc_ref.shape, F32)
        if running_max:
            m_scratch[0][...] = jnp.full(m_scratch[0].shape, -jnp.inf, F32)

    for hh in range(2):
        sl = slice(hh * HEAD_PAD, (hh + 1) * HEAD_PAD)
        s = lax.dot_general(q_ref[:, sl], k_ref[:, sl], (((1,), (1,)), ((), ())),
                            preferred_element_type=F32)
        if running_max:
            m_ref = m_scratch[0]
            m_prev = m_ref[hh]
            m_new = jnp.maximum(m_prev, jnp.max(s, axis=1, keepdims=True))
            p = jnp.exp2(s - m_new[:, :1]).astype(BF16)
            acc_ref[hh] = jnp.exp2(m_prev - m_new) * acc_ref[hh] + _dot(p, v_ref[:, sl])
            m_ref[hh] = m_new
        else:
            acc_ref[hh] += _dot(jnp.exp2(s).astype(BF16), v_ref[:, sl])

    @pl.when(last_ref[s_id] == 1)
    def _():
        lane = lax.broadcasted_iota(jnp.int32, o_ref.shape, 1)
        a0 = acc_ref[0]
        a1 = acc_ref[1]
        o0 = a0 / a0[:, MLA_V:MLA_V + 1]
        o1 = a1 / a1[:, 0:1]
        o_ref[...] = jnp.where(lane < MLA_V, o0, o1).astype(BF16)


def _flash_attention(q, k, v, seqs, tile, running_max):
    qb, kb, first, last = [], [], [], []
    base = 0
    for n_seq, seq_len in seqs:
        nb = seq_len // tile
        for _ in range(n_seq):
            for qi in range(nb):
                for ki in range(nb):
                    qb.append(base + qi)
                    kb.append(base + ki)
                    first.append(int(ki == 0))
                    last.append(int(ki == nb - 1))
            base += nb
    tabs = [jnp.asarray(np.asarray(a, np.int32)) for a in (qb, kb, first, last)]
    n_steps = len(qb)
    t_tok = q.shape[0]
    pair_w = 2 * HEAD_PAD
    scratch = [pltpu.VMEM((2, tile, LANES), F32)] * (2 if running_max else 1)
    grid_spec = pltpu.PrefetchScalarGridSpec(
        num_scalar_prefetch=4,
        grid=(MLA_HEADS // 2, n_steps),
        in_specs=[
            pl.BlockSpec((tile, pair_w), lambda p, s, qb, kb, fi, la: (qb[s], p)),
            pl.BlockSpec((tile, pair_w), lambda p, s, qb, kb, fi, la: (kb[s], p)),
            pl.BlockSpec((tile, pair_w), lambda p, s, qb, kb, fi, la: (kb[s], p)),
        ],
        out_specs=pl.BlockSpec((tile, 2 * MLA_V), lambda p, s, qb, kb, fi, la: (qb[s], p)),
        scratch_shapes=scratch,
    )
    return pl.pallas_call(
        functools.partial(_flash_kernel, running_max=running_max),
        grid_spec=grid_spec,
        out_shape=jax.ShapeDtypeStruct((t_tok, MLA_HEADS * MLA_V), BF16),
        compiler_params=_cparams(("parallel", "arbitrary")),
        name="mla_flash_max" if running_max else "mla_flash",
    )(*tabs, q, k, v)


def _memkv_kernel(mem_ref, gmem_ref, w_ref, gk_ref, k_ref, v_ref):
    x = mem_ref[...]
    h = (x * _rms_scale(x, D_MODEL) * gmem_ref[...]).astype(BF16)
    kv = _dot(h, w_ref[...])
    for hh in range(MEM_HEADS):
        sl = slice(hh * MEM_HEAD_DIM, (hh + 1) * MEM_HEAD_DIM)
        blk = kv[:, sl]
        k_ref[:, sl] = (blk * _rms_scale(blk, MEM_HEAD_DIM) * gk_ref[...]).astype(BF16)
    v_ref[...] = kv[:, MEM_WIDTH:].astype(BF16)


def _memory_kv(mem, lw):
    rows = mem.shape[0]
    tm = MEM_TOKENS
    return pl.pallas_call(
        _memkv_kernel,
        grid=(rows // tm,),
        in_specs=[pl.BlockSpec((tm, D_MODEL), lambda i: (i, 0)),
                  _const_spec(lw["g_mem"].shape), _const_spec(lw["w_mem_kv"].shape),
                  _const_spec(lw["g_mem_k"].shape)],
        out_specs=[pl.BlockSpec((tm, MEM_WIDTH), lambda i: (i, 0))] * 2,
        out_shape=[jax.ShapeDtypeStruct((rows, MEM_WIDTH), BF16)] * 2,
        compiler_params=_cparams(("parallel",)),
        name="memory_kv",
    )(mem, lw["g_mem"], lw["w_mem_kv"], lw["g_mem_k"])


def _gelu_tanh(x):
    return 0.5 * x * (1.0 + jnp.tanh(0.7978845608028654 * (x + 0.044715 * x * x * x)))


def _merge_kernel(*refs, with_router):
    if with_router:
        (x_ref, ys_ref, ya_ref, qm_ref, km_ref, vm_ref, sg_ref, wglu_ref, bglu_ref, wb_ref,
         wo_ref, gffn_ref, wr_hi_ref, wr_lo_ref, x1_ref, h2_ref, route_ref) = refs
    else:
        (x_ref, ys_ref, ya_ref, qm_ref, km_ref, vm_ref, sg_ref, wglu_ref, bglu_ref, wb_ref,
         wo_ref, gffn_ref, x1_ref, h2_ref) = refs

    ya = _gelu_tanh(ys_ref[...]).astype(BF16)
    z = _dot(ya, wglu_ref[...]) + bglu_ref[...]
    y_ssm = (z[:, :SSM_WIDTH] * _sigmoid(z[:, SSM_WIDTH:])).astype(BF16)

    mem_parts = []
    for hh in range(MEM_HEADS):
        sl = slice(hh * MEM_HEAD_DIM, (hh + 1) * MEM_HEAD_DIM)
        s = lax.dot_general(qm_ref[:, sl], km_ref[:, sl], (((1,), (1,)), ((), ())),
                            preferred_element_type=F32)
        p = jnp.exp(s - jnp.max(s, axis=1, keepdims=True))
        p = p / jnp.sum(p, axis=1, keepdims=True)
        mem_parts.append(_dot(p.astype(BF16), vm_ref[:, sl]))
    y_mem = jnp.concatenate(mem_parts, axis=1).astype(BF16)

    merged = None
    for b, yb in enumerate((y_ssm, ya_ref[...], y_mem)):
        gate = sg_ref[:, b * D_MODEL:(b + 1) * D_MODEL].astype(F32)
        term = gate * _dot(yb, wb_ref[b])
        merged = term if merged is None else merged + term

    x1 = x_ref[...] + _dot(merged.astype(BF16), wo_ref[...])
    x1_ref[...] = x1
    h2 = x1 * _rms_scale(x1, D_MODEL) * gffn_ref[...]

    if not with_router:
        h2_ref[...] = h2.astype(BF16)
        return

    for j in range(ROW_TILES):
        h2_ref[:, j, :] = h2[:, j * LANES:(j + 1) * LANES]

    h_hi, h_lo = _split_bf16(h2)
    logits = _dot(h_hi, wr_hi_ref[...]) + _dot(h_hi, wr_lo_ref[...]) + _dot(h_lo, wr_hi_ref[...])
    lane = lax.broadcasted_iota(jnp.int32, logits.shape, 1)
    lane_f = lane.astype(F32)
    neg = jnp.float32(-jnp.inf)
    big = jnp.float32(LANES)
    logits = jnp.where(lane < N_EXPERTS, logits, neg)
    m1 = jnp.max(logits, axis=1, keepdims=True)
    i1 = jnp.min(jnp.where(logits == m1, lane_f, big), axis=1, keepdims=True)
    rest = jnp.where(lane_f == i1, neg, logits)
    m2 = jnp.max(rest, axis=1, keepdims=True)
    i2 = jnp.min(jnp.where(rest == m2, lane_f, big), axis=1, keepdims=True)
    e2 = jnp.exp(m2 - m1)
    w1 = 1.0 / (1.0 + e2)
    w2 = e2 / (1.0 + e2)
    out = jnp.where(lane == 0, i1, jnp.where(lane == 1, i2, jnp.where(lane == 2, w1, w2)))
    route_ref[...] = jnp.where(lane < 4, out, 0.0)


def _merge(x, y_ssm_raw, y_mla, qm, km, vm, sg, lw, tiles, with_router):
    t_tok = x.shape[0]
    tm = tiles["tm"]
    seq_map = tiles["seq_map"]

    def row(i):
        return (i, 0)

    weights = [lw["w_glu"], lw["b_glu"], lw["w_branch"], lw["w_out"], lw["g_ffn"]]
    if with_router:
        weights += [lw["w_router_hi"], lw["w_router_lo"]]
    in_specs = [
        pl.BlockSpec((tm, D_MODEL), row),
        pl.BlockSpec((tm, SSM_WIDTH), row),
        pl.BlockSpec((tm, MLA_HEADS * MLA_V), row),
        pl.BlockSpec((tm, MEM_WIDTH), row),
        pl.BlockSpec((MEM_TOKENS, MEM_WIDTH), lambda i: (seq_map(i), 0)),
        pl.BlockSpec((MEM_TOKENS, MEM_WIDTH), lambda i: (seq_map(i), 0)),
        pl.BlockSpec((tm, N_BRANCH * D_MODEL), row),
    ] + [_const_spec(w.shape) for w in weights]
    out_specs = [pl.BlockSpec((tm, D_MODEL), row)]
    out_shape = [jax.ShapeDtypeStruct((t_tok, D_MODEL), F32)]
    if with_router:
        out_specs += [pl.BlockSpec((tm, ROW_TILES, LANES), lambda i: (i, 0, 0)),
                      pl.BlockSpec((tm, LANES), row)]
        out_shape += [jax.ShapeDtypeStruct((t_tok, ROW_TILES, LANES), F32),
                      jax.ShapeDtypeStruct((t_tok, LANES), F32)]
    else:
        out_specs.append(pl.BlockSpec((tm, D_MODEL), row))
        out_shape.append(jax.ShapeDtypeStruct((t_tok, D_MODEL), BF16))
    return pl.pallas_call(
        functools.partial(_merge_kernel, with_router=with_router),
        grid=(t_tok // tm,),
        in_specs=in_specs,
        out_specs=out_specs,
        out_shape=out_shape,
        compiler_params=_cparams(("parallel",)),
        name="merge_router" if with_router else "merge",
    )(x, y_ssm_raw, y_mla, qm, km, vm, sg, *weights)


def _swiglu_acc(x, w1_ref, w3_ref, w2_ref, acc, chunk):
    d_ff = w1_ref.shape[1]
    for c in range(d_ff // chunk):
        sl = slice(c * chunk, (c + 1) * chunk)
        g = _dot(x, w1_ref[:, sl])
        u = _dot(x, w3_ref[:, sl])
        a = (g * _sigmoid(g) * u).astype(BF16)
        acc = acc + _dot(a, w2_ref[sl, :])
    return acc


def _ffn_kernel(h_ref, x_ref, w1_ref, w3_ref, w2_ref, o_ref):
    o_ref[...] = _swiglu_acc(h_ref[...], w1_ref, w3_ref, w2_ref, x_ref[...], FFN_CHUNK)


def _dense_ffn(h2, x1, lw, tiles):
    t_tok = x1.shape[0]
    tm = tiles["tm"]

    def row(i):
        return (i, 0)

    return pl.pallas_call(
        _ffn_kernel,
        grid=(t_tok // tm,),
        in_specs=[pl.BlockSpec((tm, D_MODEL), row), pl.BlockSpec((tm, D_MODEL), row),
                  _const_spec(lw["ffn_w1"].shape), _const_spec(lw["ffn_w3"].shape),
                  _const_spec(lw["ffn_w2"].shape)],
        out_specs=pl.BlockSpec((tm, D_MODEL), row),
        out_shape=jax.ShapeDtypeStruct((t_tok, D_MODEL), F32),
        compiler_params=_cparams(("parallel",)),
        name="dense_ffn",
    )(h2, x1, lw["ffn_w1"], lw["ffn_w3"], lw["ffn_w2"])


def _dispatch_kernel(pos_ref, h_ref, xs_in_ref, xs_ref, sem):
    del xs_in_ref
    tm = h_ref.shape[0]

    def tile_copy(r, dst):
        return pltpu.make_async_copy(h_ref.at[r], xs_ref.at[dst], sem)

    def start(r, c):
        tile_copy(r, pos_ref[0, 0, 2 * r]).start()
        tile_copy(r, pos_ref[0, 0, 2 * r + 1]).start()
        return c

    def wait(r, c):
        tile_copy(r, 0).wait()
        tile_copy(r, 0).wait()
        return c

    lax.fori_loop(0, tm, start, 0, unroll=DMA_UNROLL)
    lax.fori_loop(0, tm, wait, 0, unroll=DMA_UNROLL)


def _dispatch(h2t, pos3, n_rows, tm):
    t_tok = h2t.shape[0]
    xs0 = jnp.zeros((n_rows, ROW_TILES, LANES), F32)
    return pl.pallas_call(
        _dispatch_kernel,
        grid=(t_tok // tm,),
        in_specs=[pl.BlockSpec((1, 1, 2 * tm), lambda i: (i, 0, 0), memory_space=pltpu.SMEM),
                  pl.BlockSpec((tm, ROW_TILES, LANES), lambda i: (i, 0, 0)),
                  pl.BlockSpec(memory_space=pl.ANY)],
        out_specs=pl.BlockSpec(memory_space=pl.ANY),
        out_shape=jax.ShapeDtypeStruct(xs0.shape, F32),
        input_output_aliases={2: 0},
        scratch_shapes=[pltpu.SemaphoreType.DMA(())],
        compiler_params=_cparams(("arbitrary",)),
        name="moe_dispatch",
    )(pos3, h2t, xs0)


def _expert_kernel(be_ref, nu_ref, x_ref, w1_ref, w3_ref, w2_ref, y_ref):
    del be_ref
    i = pl.program_id(0)

    @pl.when(i < nu_ref[0])
    def _():
        x = jnp.concatenate([x_ref[:, j, :] for j in range(ROW_TILES)], axis=1).astype(BF16)
        acc = _swiglu_acc(x, w1_ref, w3_ref, w2_ref, jnp.zeros((x.shape[0], D_MODEL), F32), MOE_CHUNK)
        for j in range(ROW_TILES):
            y_ref[:, j, :] = acc[:, j * LANES:(j + 1) * LANES]

    @pl.when(i >= nu_ref[0])
    def _():
        y_ref[...] = jnp.zeros(y_ref.shape, F32)


def _expert_ffn(xs, blk_expert, n_used, lw, bm):
    n_rows = xs.shape[0]

    def w_spec(shape):
        return pl.BlockSpec((None,) + shape, lambda i, be, nu: (be[i], 0, 0),
                            pipeline_mode=pl.Buffered(1))

    grid_spec = pltpu.PrefetchScalarGridSpec(
        num_scalar_prefetch=2,
        grid=(n_rows // bm,),
        in_specs=[
            pl.BlockSpec((bm, ROW_TILES, LANES), lambda i, be, nu: (i, 0, 0)),
            w_spec((D_MODEL, D_FF_EXPERT)), w_spec((D_MODEL, D_FF_EXPERT)),
            w_spec((D_FF_EXPERT, D_MODEL)),
        ],
        out_specs=pl.BlockSpec((bm, ROW_TILES, LANES), lambda i, be, nu: (i, 0, 0)),
    )
    return pl.pallas_call(
        _expert_kernel,
        grid_spec=grid_spec,
        out_shape=jax.ShapeDtypeStruct(xs.shape, F32),
        compiler_params=_cparams(("arbitrary",)),
        name="moe_experts",
    )(blk_expert, n_used, xs, lw["moe_w1"], lw["moe_w3"], lw["moe_w2"])


def _combine_kernel(pos_ref, ys_hbm, x_ref, route_ref, o_ref, buf_ref, sem):
    tm = o_ref.shape[0]

    def tile_copy(k, r, src):
        return pltpu.make_async_copy(ys_hbm.at[src], buf_ref.at[k, r], sem)

    def start(r, c):
        tile_copy(0, r, pos_ref[0, 0, 2 * r]).start()
        tile_copy(1, r, pos_ref[0, 0, 2 * r + 1]).start()
        return c

    def wait(r, c):
        tile_copy(0, r, 0).wait()
        tile_copy(1, r, 0).wait()
        return c

    lax.fori_loop(0, tm, start, 0, unroll=DMA_UNROLL)
    lax.fori_loop(0, tm, wait, 0, unroll=DMA_UNROLL)
    w1 = route_ref[:, 2:3]
    w2 = route_ref[:, 3:4]
    for j in range(ROW_TILES):
        sl = slice(j * LANES, (j + 1) * LANES)
        o_ref[:, sl] = x_ref[:, sl] + w1 * buf_ref[0, :, j, :] + w2 * buf_ref[1, :, j, :]


def _combine(ys, pos3, x1, route, tm):
    t_tok = x1.shape[0]
    return pl.pallas_call(
        _combine_kernel,
        grid=(t_tok // tm,),
        in_specs=[pl.BlockSpec((1, 1, 2 * tm), lambda i: (i, 0, 0), memory_space=pltpu.SMEM),
                  pl.BlockSpec(memory_space=pl.ANY),
                  pl.BlockSpec((tm, D_MODEL), lambda i: (i, 0)),
                  pl.BlockSpec((tm, LANES), lambda i: (i, 0))],
        out_specs=pl.BlockSpec((tm, D_MODEL), lambda i: (i, 0)),
        out_shape=jax.ShapeDtypeStruct((t_tok, D_MODEL), F32),
        scratch_shapes=[pltpu.VMEM((2, tm, ROW_TILES, LANES), F32), pltpu.SemaphoreType.DMA(())],
        compiler_params=_cparams(("arbitrary",)),
        name="moe_combine",
    )(pos3, ys, x1, route)


def _moe_ffn(h2t, x1, route, lw, tiles):
    t_tok = x1.shape[0]
    tm = tiles["tm"]
    bm = tiles["moe_block"]
    n_assign = 2 * t_tok
    experts = route[:, :2].astype(jnp.int32).reshape(n_assign)
    onehot = (experts[:, None] == jnp.arange(N_EXPERTS, dtype=jnp.int32)[None, :]).astype(jnp.int32)
    csum = jnp.cumsum(onehot, axis=0)
    rank = jnp.sum((csum - 1) * onehot, axis=1)
    counts = csum[-1]
    padded = ((counts + bm - 1) // bm) * bm
    ends = jnp.cumsum(padded)
    starts = ends - padded
    pos = jnp.sum(onehot * starts[None, :], axis=1) + rank
    n_blocks = n_assign // bm + N_EXPERTS
    blk_start = jnp.arange(n_blocks, dtype=jnp.int32) * bm
    blk_expert = jnp.sum((blk_start[:, None] >= ends[None, :]).astype(jnp.int32), axis=1)
    blk_expert = jnp.minimum(blk_expert, N_EXPERTS - 1)
    n_used = (ends[-1:] // bm).astype(jnp.int32)
    pos3 = pos.astype(jnp.int32).reshape(t_tok // tm, 1, 2 * tm)

    xs = _dispatch(h2t, pos3, n_blocks * bm, tm)
    ys = _expert_ffn(xs, blk_expert, n_used, lw, bm)
    return _combine(ys, pos3, x1, route, tm)


def _head_cols(w, per_head, n_heads, offset=0):
    k = w.shape[0]
    w = w.reshape(k, n_heads, per_head)
    w = jnp.pad(w, ((0, 0), (0, 0), (offset, HEAD_PAD - per_head - offset)))
    return w.reshape(k, n_heads * HEAD_PAD)


def _swap_rope(w):
    z = jnp.zeros_like(w[..., :MLA_NOPE])
    return jnp.concatenate([z, w[..., MLA_NOPE + HALF_ROPE:], w[..., MLA_NOPE:MLA_NOPE + HALF_ROPE]], axis=-1)


def _pad_lane_block(v, offset=0):
    return jnp.pad(v, ((0, 0),) * (v.ndim - 1) + ((offset, LANES - v.shape[-1] - offset),))


def _layer_weights(i, p, rope_cos, rope_sin):
    o = IN_OFFSETS
    w_in = p["w_in"][i]
    lw = {}
    lw["g_mix"] = p["g_mix"][i].reshape(1, D_MODEL)
    lw["w_u"] = w_in[:, :o[0]].astype(BF16)
    lw["w_cq"] = w_in[:, o[0]:o[1]].astype(BF16)
    w_kr = w_in[:, o[2]:o[3]]
    w_kr_sw = jnp.concatenate([w_kr[:, HALF_ROPE:], w_kr[:, :HALF_ROPE]], axis=1)
    lw["w_ckv"] = jnp.concatenate([w_in[:, o[1]:o[2]], _pad_lane_block(w_kr, MLA_NOPE),
                                   _pad_lane_block(w_kr_sw, MLA_NOPE)], axis=1).astype(BF16)
    lw["w_qm"] = w_in[:, o[3]:o[4]].astype(BF16)
    lw["w_gate"] = w_in[:, o[4]:].astype(BF16)
    lw["g_q_lora"] = p["g_q_lora"][i].reshape(1, Q_LORA)
    w_uq = p["w_uq"][i].reshape(Q_LORA, MLA_HEADS, MLA_QK)
    lw["w_uq"] = jnp.concatenate(
        [_head_cols(w_uq.reshape(Q_LORA, -1), MLA_QK, MLA_HEADS),
         _head_cols(_swap_rope(w_uq).reshape(Q_LORA, -1), MLA_QK, MLA_HEADS)], axis=1).astype(BF16)
    lw["g_kv_lora"] = p["g_kv_lora"][i].reshape(1, KV_LORA)
    w_ukv = p["w_ukv"][i].reshape(KV_LORA, MLA_HEADS, MLA_NOPE + MLA_V)
    lw["w_kn"] = _head_cols(w_ukv[:, :, :MLA_NOPE].reshape(KV_LORA, -1), MLA_NOPE, MLA_HEADS).astype(BF16)
    w_v = w_ukv[:, :, MLA_NOPE:].reshape(KV_LORA, MLA_HEADS // 2, 2, MLA_V)
    w_v = jnp.stack([jnp.pad(w_v[:, :, 0], ((0, 0), (0, 0), (0, HEAD_PAD - MLA_V))),
                     jnp.pad(w_v[:, :, 1], ((0, 0), (0, 0), (HEAD_PAD - MLA_V, 0)))], axis=2)
    lw["w_v"] = w_v.reshape(KV_LORA, MLA_WIDTH_PAD).astype(BF16)
    one_even = jnp.zeros((HEAD_PAD,), F32).at[MLA_V].set(1.0)
    one_odd = jnp.zeros((HEAD_PAD,), F32).at[0].set(1.0)
    lw["v_one"] = jnp.tile(jnp.concatenate([one_even, one_odd]), MLA_HEADS // 2).reshape(1, MLA_WIDTH_PAD)
    for name, g in (("q", p["g_mla_q"][i]), ("k", p["g_mla_k"][i])):
        lw["rope_a" + name] = rope_cos * _pad_lane_block(g)[None, :]
        lw["rope_b" + name] = rope_sin * _pad_lane_block(_swap_rope(g))[None, :]
    lw["g_mem_q"] = jnp.tile(p["g_mem_q"][i], MEM_HEADS).reshape(1, MEM_WIDTH) * (MEM_HEAD_DIM ** -0.5)
    lw["g_mem"] = p["g_mem"][i].reshape(1, D_MODEL)
    lw["w_mem_kv"] = p["w_mem_kv"][i].astype(BF16)
    lw["g_mem_k"] = p["g_mem_k"][i].reshape(1, MEM_HEAD_DIM)
    lw["w_glu"] = p["ssm_w_glu"][i].astype(BF16)
    lw["b_glu"] = p["ssm_b_glu"][i].reshape(1, 2 * SSM_WIDTH)
    lw["w_branch"] = p["w_branch"][i].astype(BF16)
    lw["w_out"] = p["w_out"][i].astype(BF16)
    lw["g_ffn"] = p["g_ffn"][i].reshape(1, D_MODEL)
    lw["ssm"] = _ssm_tables(p["ssm_a_re"][i], p["ssm_a_im"][i], p["ssm_log_dt"][i], p["ssm_b_re"][i],
                            p["ssm_b_im"][i], p["ssm_c_re"][i], p["ssm_c_im"][i], p["ssm_d"][i])
    lw["score_bound"] = (1.02 * MLA_QK ** 0.5 * LOG2E
                         * jnp.max(jnp.abs(p["g_mla_q"][i])) * jnp.max(jnp.abs(p["g_mla_k"][i])))
    if i % 2 == 0:
        w13 = p["ffn_w13"][i // 2]
        lw["ffn_w1"] = w13[:, :D_FF].astype(BF16)
        lw["ffn_w3"] = w13[:, D_FF:].astype(BF16)
        lw["ffn_w2"] = p["ffn_w2"][i // 2].astype(BF16)
    else:
        w13 = p["moe_w13"][i // 2]
        lw["moe_w1"] = w13[:, :, :D_FF_EXPERT].astype(BF16)
        lw["moe_w3"] = w13[:, :, D_FF_EXPERT:].astype(BF16)
        lw["moe_w2"] = p["moe_w2"][i // 2].astype(BF16)
        wr = jnp.pad(p["moe_router"][i // 2], ((0, 0), (0, LANES - N_EXPERTS)))
        lw["w_router_hi"], lw["w_router_lo"] = _split_bf16(wr)
    return lw


def _rope_tables(max_len):
    inv = 1.0 / (ROPE_BASE ** (jnp.arange(0, MLA_ROPE, 2, dtype=F32) / MLA_ROPE))
    ang = jnp.arange(max_len, dtype=F32)[:, None] * inv[None, :]
    cos, sin = jnp.cos(ang), jnp.sin(ang)
    ones = jnp.ones((max_len, MLA_NOPE), F32)
    zn = jnp.zeros((max_len, MLA_NOPE), F32)
    zp = jnp.zeros((max_len, HEAD_PAD - MLA_QK), F32)
    return (jnp.concatenate([ones, cos, cos, zp], axis=1),
            jnp.concatenate([zn, -sin, sin, zp], axis=1))


def _gcd_tile(limit, *sizes):
    t = limit
    while any(s % t for s in sizes):
        t //= 2
    return t


def kernel(x_prompt, x_sample, mem_prompt, mem_sample, g_mix, w_in, ssm_a_re, ssm_a_im, ssm_log_dt, ssm_b_re, ssm_b_im, ssm_c_re, ssm_c_im, ssm_d, ssm_w_glu, ssm_b_glu, g_q_lora, w_uq, g_kv_lora, w_ukv, g_mla_q, g_mla_k, g_mem, w_mem_kv, g_mem_q, g_mem_k, w_branch, w_out, g_ffn, ffn_w13, ffn_w2, moe_router, moe_w13, moe_w2):
    params = dict(g_mix=g_mix, w_in=w_in, ssm_a_re=ssm_a_re, ssm_a_im=ssm_a_im, ssm_log_dt=ssm_log_dt,
                  ssm_b_re=ssm_b_re, ssm_b_im=ssm_b_im, ssm_c_re=ssm_c_re, ssm_c_im=ssm_c_im, ssm_d=ssm_d,
                  ssm_w_glu=ssm_w_glu, ssm_b_glu=ssm_b_glu, g_q_lora=g_q_lora, w_uq=w_uq,
                  g_kv_lora=g_kv_lora, w_ukv=w_ukv, g_mla_q=g_mla_q, g_mla_k=g_mla_k, g_mem=g_mem,
                  w_mem_kv=w_mem_kv, g_mem_q=g_mem_q, g_mem_k=g_mem_k, w_branch=w_branch, w_out=w_out,
                  g_ffn=g_ffn, ffn_w13=ffn_w13, ffn_w2=ffn_w2, moe_router=moe_router, moe_w13=moe_w13,
                  moe_w2=moe_w2)
    depth = g_mix.shape[0]
    bp, lp, _ = x_prompt.shape
    bs, ls, _ = x_sample.shape
    tp, ts = bp * lp, bs * ls
    t_tok = tp + ts

    tm = _gcd_tile(TOKEN_TILE, lp, ls)
    attn_tile = _gcd_tile(ATTN_TILE, lp, ls)
    npt = tp // tm
    lp_t, ls_t = lp // tm, ls // tm

    def pos_map(i):
        return jnp.where(i < npt, i % lp_t, (i - npt) % ls_t)

    def seq_map(i):
        return jnp.where(i < npt, i // lp_t, bp + (i - npt) // ls_t)

    rope_cos, rope_sin = _rope_tables(max(lp, ls))
    head_ind = (jnp.arange(MLA_WIDTH_PAD)[:, None] // HEAD_PAD == jnp.arange(LANES)[None, :])
    mem_bd = (jnp.arange(MEM_WIDTH)[:, None] // MEM_HEAD_DIM == jnp.arange(MEM_WIDTH)[None, :] // MEM_HEAD_DIM)
    tiles = dict(tm=tm, pos_map=pos_map, seq_map=seq_map,
                 ones=jnp.ones((D_MODEL, LANES), BF16),
                 head_ind=head_ind.astype(BF16), head_ind_t=head_ind.T.astype(BF16),
                 mem_bd=mem_bd.astype(BF16),
                 moe_block=_gcd_tile(MOE_BLOCK, 2 * t_tok))

    x = jnp.concatenate([x_prompt.reshape(tp, D_MODEL), x_sample.reshape(ts, D_MODEL)], axis=0)
    mem = jnp.concatenate([mem_prompt.reshape(bp * MEM_TOKENS, D_MODEL),
                           mem_sample.reshape(bs * MEM_TOKENS, D_MODEL)], axis=0)
    seqs = [(bp, lp), (bs, ls)]

    for i in range(depth):
        lw = _layer_weights(i, params, rope_cos, rope_sin)
        u, q, k, v, qm, sg = _in_projection(x, lw, tiles)

        y_ssm_raw = _ssm_call(u, lw["ssm"], seqs)

        y_mla = lax.cond(
            lw["score_bound"] <= MAX_SAFE_LOG2_SCORE,
            lambda q_, k_, v_: _flash_attention(q_, k_, v_, seqs, attn_tile, running_max=False),
            lambda q_, k_, v_: _flash_attention(q_, k_, v_, seqs, attn_tile, running_max=True),
            q, k, v)
        km, vm = _memory_kv(mem, lw)

        with_router = i % 2 == 1
        outs = _merge(x, y_ssm_raw, y_mla, qm, km, vm, sg, lw, tiles, with_router)
        if with_router:
            x1, h2t, route = outs
            x = _moe_ffn(h2t, x1, route, lw, tiles)
        else:
            x1, h2 = outs
            x = _dense_ffn(h2, x1, lw, tiles)

    return (x[:tp].reshape(bp, lp, D_MODEL), x[tp:].reshape(bs, ls, D_MODEL))
```

```python
import functools
import math

import numpy as np
import jax
import jax.numpy as jnp
from jax import lax
from jax.experimental import pallas as pl
from jax.experimental.pallas import tpu as pltpu

F32 = jnp.float32
BF16 = jnp.bfloat16
EPS = 1e-6
LOG2E = math.log2(math.e)

D_MODEL = 1024
SSM_WIDTH = 512
SSM_GROUPS = 32
SSM_GROUP = 16
SSM_STATE = 64
MLA_HEADS = 8
MLA_NOPE = 64
MLA_ROPE = 32
MLA_V = 64
MLA_QK = MLA_NOPE + MLA_ROPE
Q_LORA = 256
KV_LORA = 128
ROPE_BASE = 10000.0
MEM_TOKENS = 256
MEM_HEADS = 4
MEM_HEAD_DIM = 128
MEM_WIDTH = 512
N_BRANCH = 3
D_FF = 2816
N_EXPERTS = 8
D_FF_EXPERT = 3584
IN_SIZES = (SSM_WIDTH, Q_LORA, KV_LORA, MLA_ROPE, MEM_WIDTH, N_BRANCH * D_MODEL)
IN_OFFSETS = tuple(int(v) for v in np.cumsum(IN_SIZES)[:-1])

LANES = 128
SUBLANES = 8
HEAD_PAD = 128
HALF_ROPE = MLA_ROPE // 2
MLA_WIDTH_PAD = MLA_HEADS * HEAD_PAD
ROW_TILES = D_MODEL // LANES
VMEM_LIMIT = 56 * 1024 * 1024
MAX_SAFE_LOG2_SCORE = 100.0

TOKEN_TILE = 512
ATTN_TILE = 1024
SSM_CHUNK = 16
MOE_BLOCK = 512
FFN_CHUNK = 1408
MOE_CHUNK = 512
DMA_UNROLL = 8


def _cparams(sem):
    return pltpu.CompilerParams(dimension_semantics=sem, vmem_limit_bytes=VMEM_LIMIT)


def _const_spec(shape):
    nd = len(shape)
    return pl.BlockSpec(shape, lambda *_: (0,) * nd)


def _sigmoid(x):
    return 1.0 / (1.0 + jnp.exp(-x))


def _rms_scale(x, n):
    return lax.rsqrt(jnp.sum(x * x, axis=-1, keepdims=True) * (1.0 / n) + EPS)


def _split_bf16(x):
    hi = x.astype(BF16)
    lo = (x - hi.astype(F32)).astype(BF16)
    return hi, lo


def _dot(a, b):
    return jnp.dot(a, b, preferred_element_type=F32)


def _stream_specs(parts, tm, width):
    if len(parts) == 1:
        return [pl.BlockSpec((tm, width), lambda i: (i, 0))]
    n0 = parts[0].shape[0] // tm
    return [pl.BlockSpec((tm, width), lambda i: (jnp.minimum(i, n0 - 1), 0)),
            pl.BlockSpec((tm, width), lambda i: (jnp.maximum(i - n0, 0), 0))]


def _stream_read(refs, n_first):
    if len(refs) == 1:
        return refs[0][...]
    return jnp.where(pl.program_id(0) < n_first, refs[0][...], refs[1][...])


def _inproj_kernel(*refs, n_x, n_first):
    x_refs = refs[:n_x]
    (gmix_ref, wu_ref, wg_ref, wqm_ref, wcq_ref, wckv_ref,
     gql_ref, wuq_ref, gkvl_ref, wkn_ref, wv_ref, vone_ref, gmq_ref,
     ones_ref, ind_ref, indt_ref, bd_ref,
     aq_ref, bq_ref, ak_ref, bk_ref,
     u_ref, q_ref, k_ref, v_ref, qm_ref, sg_ref) = refs[n_x:]

    def row_ssq(val):
        n = val.shape[1]
        return _dot((val * val).astype(BF16), ones_ref[:n, :])

    def head_scale(val, scale):
        ss = _dot((val * val).astype(BF16), ind_ref[...])
        r = lax.rsqrt(ss * (1.0 / MLA_QK) + EPS) * scale
        r_hi, r_lo = _split_bf16(r)
        return _dot(r_hi, indt_ref[...]) + _dot(r_lo, indt_ref[...])

    x = _stream_read(x_refs, n_first)
    rs = lax.rsqrt(row_ssq(x) * (1.0 / D_MODEL) + EPS)
    h = (x * jnp.tile(rs, (1, D_MODEL // LANES)) * gmix_ref[...]).astype(BF16)

    u_ref[...] = _dot(h, wu_ref[...])

    for b in range(N_BRANCH):
        sl = slice(b * D_MODEL, (b + 1) * D_MODEL)
        sg_ref[:, sl] = _sigmoid(_dot(h, wg_ref[:, sl])).astype(BF16)

    zq = _dot(h, wqm_ref[...])
    zr = lax.rsqrt(_dot((zq * zq).astype(BF16), bd_ref[...]) * (1.0 / MEM_HEAD_DIM) + EPS)
    qm_ref[...] = (zq * zr * gmq_ref[...]).astype(BF16)

    cq = _dot(h, wcq_ref[...])
    cq_rs = lax.rsqrt(row_ssq(cq) * (1.0 / Q_LORA) + EPS)
    cqn = (cq * jnp.tile(cq_rs, (1, Q_LORA // LANES)) * gql_ref[...]).astype(BF16)
    qq = _dot(cqn, wuq_ref[...])
    qf = qq[:, :MLA_WIDTH_PAD]
    qsw = qq[:, MLA_WIDTH_PAD:]

    z3 = _dot(h, wckv_ref[...])
    ckv = z3[:, :LANES]
    kr = z3[:, LANES:2 * LANES]
    kr_sw = z3[:, 2 * LANES:]
    ckv_rs = lax.rsqrt(row_ssq(ckv) * (1.0 / KV_LORA) + EPS)
    ckvn = (ckv * ckv_rs * gkvl_ref[...]).astype(BF16)
    kf = _dot(ckvn, wkn_ref[...]) + jnp.tile(kr, (1, MLA_HEADS))
    v_ref[...] = (_dot(ckvn, wv_ref[...]) + vone_ref[...]).astype(BF16)

    q_rs = head_scale(qf, MLA_QK ** -0.5 * LOG2E)
    k_rs = head_scale(kf, 1.0)
    aq = aq_ref[...]
    bq = bq_ref[...]
    ak = ak_ref[...]
    k_rot = kr_sw * bk_ref[...]
    for hh in range(MLA_HEADS):
        sl = slice(hh * HEAD_PAD, (hh + 1) * HEAD_PAD)
        q_ref[:, sl] = ((qf[:, sl] * aq + qsw[:, sl] * bq) * q_rs[:, sl]).astype(BF16)
        k_ref[:, sl] = ((kf[:, sl] * ak + k_rot) * k_rs[:, sl]).astype(BF16)


def _in_projection(x_parts, lw, tiles):
    t_tok = sum(p.shape[0] for p in x_parts)
    tm = tiles["tm"]
    pos_map = tiles["pos_map"]

    def row(i):
        return (i, 0)

    weights = [lw["g_mix"], lw["w_u"], lw["w_gate"], lw["w_qm"], lw["w_cq"], lw["w_ckv"],
               lw["g_q_lora"], lw["w_uq"], lw["g_kv_lora"], lw["w_kn"], lw["w_v"], lw["v_one"],
               lw["g_mem_q"], tiles["ones"], tiles["head_ind"], tiles["head_ind_t"], tiles["mem_bd"]]
    rope = [lw["rope_aq"], lw["rope_bq"], lw["rope_ak"], lw["rope_bk"]]
    in_specs = _stream_specs(x_parts, tm, D_MODEL) + [_const_spec(w.shape) for w in weights]
    in_specs += [pl.BlockSpec((tm, HEAD_PAD), lambda i: (pos_map(i), 0))] * len(rope)
    kern = functools.partial(_inproj_kernel, n_x=len(x_parts), n_first=x_parts[0].shape[0] // tm)
    out_widths = (SSM_WIDTH, MLA_WIDTH_PAD, MLA_WIDTH_PAD, MLA_WIDTH_PAD, MEM_WIDTH, N_BRANCH * D_MODEL)
    return pl.pallas_call(
        kern,
        grid=(t_tok // tm,),
        in_specs=in_specs,
        out_specs=[pl.BlockSpec((tm, w), row) for w in out_widths],
        out_shape=[jax.ShapeDtypeStruct((t_tok, w), F32 if j == 0 else BF16) for j, w in enumerate(out_widths)],
        compiler_params=_cparams(("parallel",)),
        name="in_projection",
    )(*x_parts, *weights, *rope)


SSM_LANE_GROUPS = LANES // SSM_GROUP
SSM_TAB_DIR = 11
(_TAB_STEP, _TAB_POW, _TAB_A8, _TAB_MASK) = (0, 6, 8, 10)


def _granule_transpose(vs, lane):
    for dist in (4, 2, 1):
        shift = dist * SSM_GROUP
        low = (lane // shift) % 2 == 0
        new = list(vs)
        for a in range(SSM_LANE_GROUPS):
            if a & dist == 0:
                b = a | dist
                new[a] = jnp.where(low, vs[a], pltpu.roll(vs[b], shift, 1))
                new[b] = jnp.where(low, pltpu.roll(vs[a], LANES - shift, 1), vs[b])
        vs = new
    return vs


def _chunk_scan(x, xs, tab_ref, base, keep, backward):
    nv, _, nl = x.shape
    for j, k in enumerate((1, 2, 4)):
        shift = SUBLANES - k if backward else k
        rx = pltpu.roll(x, shift, 1)
        rxs = pltpu.roll(xs, shift, 1)
        a1 = tab_ref[base + _TAB_STEP + 2 * j][None]
        a2 = tab_ref[base + _TAB_STEP + 2 * j + 1][None]
        x, xs = x + a1 * rx + a2 * rxs, xs + a1 * rxs - a2 * rx
    a1 = tab_ref[base + _TAB_A8]
    a2 = tab_ref[base + _TAB_A8 + 1]
    edge = 0 if backward else SUBLANES - 1
    c = jnp.zeros((SUBLANES, nl), F32)
    cs = c
    cb = [None] * nv
    csb = [None] * nv
    for v in (range(nv - 1, -1, -1) if backward else range(nv)):
        kp = keep(v)
        c = c * kp
        cs = cs * kp
        cb[v] = c
        csb[v] = cs
        ex = jnp.broadcast_to(x[v, edge:edge + 1, :], (SUBLANES, nl))
        exs = jnp.broadcast_to(xs[v, edge:edge + 1, :], (SUBLANES, nl))
        c, cs = ex + a1 * c + a2 * cs, exs + a1 * cs - a2 * c
    xe = pltpu.roll(x, SUBLANES - 1 if backward else 1, 1) * tab_ref[base + _TAB_MASK][None]
    return (xe + tab_ref[base + _TAB_POW][None] * jnp.stack(cb)
            + tab_ref[base + _TAB_POW + 1][None] * jnp.stack(csb))


def _ssm_kernel(keepf_ref, keepb_ref, u_ref, t_ref, win_ref, wout_ref, tab_ref, y_ref):
    blk = pl.program_id(1)
    rows = u_ref.shape[0] // SSM_CHUNK
    nv = rows // SUBLANES
    ng = SSM_LANE_GROUPS
    lane = lax.broadcasted_iota(jnp.int32, (rows, LANES), 1)

    pieces = [u_ref[pl.ds(t, rows, stride=SSM_CHUNK), :] for t in range(SSM_CHUNK)]
    halves = [_granule_transpose(pieces[ng * h:ng * (h + 1)], lane) for h in range(SSM_CHUNK // ng)]

    ys = []
    parts = [[], [], [], []]
    for g in range(ng):
        xg = jnp.concatenate([h[g] for h in halves], axis=1).astype(BF16)
        ys.append(_dot(xg, t_ref[g]))
        s4 = _dot(xg, win_ref[g])
        for j in range(4):
            parts[j].append(s4[:, j * LANES:(j + 1) * LANES])
    f, fs, b, bs = [jnp.concatenate(p, axis=1).reshape(nv, SUBLANES, ng * LANES) for p in parts]

    xin_f = _chunk_scan(f, fs, tab_ref, 0, lambda v: keepf_ref[blk * nv + v].astype(F32), False)
    xin_b = _chunk_scan(b, bs, tab_ref, SSM_TAB_DIR, lambda v: keepb_ref[blk * nv + v].astype(F32), True)
    xin_f = xin_f.reshape(rows, ng * LANES)
    xin_b = xin_b.reshape(rows, ng * LANES)

    outs = []
    for g in range(ng):
        sl = slice(g * LANES, (g + 1) * LANES)
        xs = jnp.concatenate([xin_f[:, sl], xin_b[:, sl]], axis=1).astype(BF16)
        outs.append(ys[g] + _dot(xs, wout_ref[g]))
    for h in range(SSM_CHUNK // ng):
        back = _granule_transpose([o[:, h * LANES:(h + 1) * LANES] for o in outs], lane)
        for tt in range(ng):
            y_ref[pl.ds(ng * h + tt, rows, stride=SSM_CHUNK), :] = back[tt]


def _ssm_call(u, tabs, seqs):
    t_tok = u.shape[0]
    block = int(np.lcm.reduce([ln for _, ln in seqs]))
    vreg_tokens = SUBLANES * SSM_CHUNK
    assert all((n * ln) % block == 0 and ln % vreg_tokens == 0 for n, ln in seqs), seqs
    nv = block // vreg_tokens
    keep_f, keep_b = [], []
    for n_seq, seq_len in seqs:
        for _ in range(n_seq * seq_len // block):
            for v in range(nv):
                keep_f.append(int((v * vreg_tokens) % seq_len != 0))
                keep_b.append(int(((v + 1) * vreg_tokens) % seq_len != 0))
    keep = [jnp.asarray(np.asarray(a, np.int32)) for a in (keep_f, keep_b)]
    ng = SSM_LANE_GROUPS
    width = SSM_CHUNK * SSM_GROUP
    n_tab = 2 * SSM_TAB_DIR
    grid_spec = pltpu.PrefetchScalarGridSpec(
        num_scalar_prefetch=2,
        grid=(SSM_WIDTH // LANES, t_tok // block),
        in_specs=[
            pl.BlockSpec((block, LANES), lambda g, b, kf, kb: (b, g)),
            pl.BlockSpec((ng, width, width), lambda g, b, kf, kb: (g, 0, 0)),
            pl.BlockSpec((ng, width, 4 * LANES), lambda g, b, kf, kb: (g, 0, 0)),
            pl.BlockSpec((ng, 2 * LANES, width), lambda g, b, kf, kb: (g, 0, 0)),
            pl.BlockSpec((n_tab, SUBLANES, ng * LANES), lambda g, b, kf, kb: (0, 0, g)),
        ],
        out_specs=pl.BlockSpec((block, LANES), lambda g, b, kf, kb: (b, g)),
    )
    return pl.pallas_call(
        _ssm_kernel,
        grid_spec=grid_spec,
        out_shape=jax.ShapeDtypeStruct(u.shape, F32),
        compiler_params=_cparams(("parallel", "parallel")),
        name="ssm_chunked",
    )(*keep, u, tabs["t"], tabs["w_in"], tabs["w_out"], tabs["scan"])


def _ssm_tables(a_re, a_im, log_dt, b_re, b_im, c_re, c_im, d_skip):
    q = SSM_CHUNK
    hp = lax.Precision.HIGHEST
    dt = jnp.exp(log_dt)[..., None, None]
    ks = jnp.arange(q + 1, dtype=F32)
    mag = jnp.exp(a_re[..., None] * dt * ks)
    ang = a_im[..., None] * dt * ks
    pw_re = mag * jnp.cos(ang)
    pw_im = mag * jnp.sin(ang)
    lb_re, lb_im = pw_re[..., 1], pw_im[..., 1]
    den = a_re * a_re + a_im * a_im
    f_re = ((lb_re - 1.0) * a_re + lb_im * a_im) / den
    f_im = (lb_im * a_re - (lb_re - 1.0) * a_im) / den
    bb_re = f_re[..., None] * b_re - f_im[..., None] * b_im
    bb_im = f_re[..., None] * b_im + f_im[..., None] * b_re

    cl_re = c_re[..., None] * pw_re[:, :, None] - c_im[..., None] * pw_im[:, :, None]
    cl_im = c_re[..., None] * pw_im[:, :, None] + c_im[..., None] * pw_re[:, :, None]

    kern = (jnp.einsum("dgnpk,dgpm->dgknm", cl_re[..., :q], bb_re, precision=hp)
            - jnp.einsum("dgnpk,dgpm->dgknm", cl_im[..., :q], bb_im, precision=hp))
    s_idx = jnp.arange(q)[:, None]
    t_idx = jnp.arange(q)[None, :]
    lag_f = jnp.clip(t_idx - s_idx, 0, q - 1)
    lag_b = jnp.clip(s_idx - t_idx, 0, q - 1)
    tf = kern[0][:, lag_f] * (t_idx >= s_idx)[None, :, :, None, None]
    tb = kern[1][:, lag_b] * (s_idx >= t_idx)[None, :, :, None, None]
    t_mat = (tf + tb).transpose(0, 1, 4, 2, 3)
    t_mat = t_mat.reshape(SSM_GROUPS, q * SSM_GROUP, q * SSM_GROUP)
    d_rep = jnp.tile(d_skip.reshape(SSM_GROUPS, 1, SSM_GROUP), (1, q, 1)).reshape(SSM_GROUPS, -1)
    t_mat = t_mat + d_rep[:, :, None] * jnp.eye(q * SSM_GROUP, dtype=F32)[None]

    def w_in_dir(d, exps):
        pr = pw_re[d][:, :, exps]
        pi = pw_im[d][:, :, exps]
        re = pr[..., None] * bb_re[d][:, :, None, :] - pi[..., None] * bb_im[d][:, :, None, :]
        im = pr[..., None] * bb_im[d][:, :, None, :] + pi[..., None] * bb_re[d][:, :, None, :]
        re = re.transpose(0, 2, 3, 1).reshape(SSM_GROUPS, q * SSM_GROUP, SSM_STATE)
        im = im.transpose(0, 2, 3, 1).reshape(SSM_GROUPS, q * SSM_GROUP, SSM_STATE)
        return jnp.concatenate([re, im, im, re], axis=-1)

    w_in = jnp.concatenate([w_in_dir(0, q - 1 - jnp.arange(q)), w_in_dir(1, jnp.arange(q))], axis=-1)

    def w_out_dir(d, exps):
        re = cl_re[d][..., exps]
        im = cl_im[d][..., exps]
        re = re.transpose(0, 2, 3, 1).reshape(SSM_GROUPS, SSM_STATE, q * SSM_GROUP)
        im = im.transpose(0, 2, 3, 1).reshape(SSM_GROUPS, SSM_STATE, q * SSM_GROUP)
        return jnp.concatenate([re, -im], axis=1)

    w_out = jnp.concatenate([w_out_dir(0, 1 + jnp.arange(q)), w_out_dir(1, q - jnp.arange(q))], axis=1)

    ms = jnp.arange(SUBLANES + 1, dtype=F32) * q
    cr = jnp.exp(a_re[..., None] * dt * ms) * jnp.cos(a_im[..., None] * dt * ms)
    ci = jnp.exp(a_re[..., None] * dt * ms) * jnp.sin(a_im[..., None] * dt * ms)

    def pat(d, m):
        ar, ai = cr[d][..., m], ci[d][..., m]
        return (jnp.concatenate([ar, ar], axis=-1).reshape(-1), jnp.concatenate([-ai, ai], axis=-1).reshape(-1))

    row = jnp.arange(SUBLANES)

    def dir_tables(d, backward):
        tabs = []
        for k in (1, 2, 4):
            valid = (row <= SUBLANES - 1 - k) if backward else (row >= k)
            tabs += [valid[:, None] * p[None, :] for p in pat(d, k)]
        pows = [pat(d, SUBLANES - 1 - i if backward else i) for i in range(SUBLANES)]
        tabs += [jnp.stack([p[0] for p in pows]), jnp.stack([p[1] for p in pows])]
        tabs += [jnp.tile(p[None, :], (SUBLANES, 1)) for p in pat(d, SUBLANES)]
        excl = (row <= SUBLANES - 2) if backward else (row >= 1)
        tabs.append(jnp.tile(excl[:, None].astype(F32), (1, SSM_GROUPS * LANES)))
        return tabs

    scan_tab = jnp.stack(dir_tables(0, False) + dir_tables(1, True), axis=0).astype(F32)
    return {"t": t_mat.astype(BF16), "w_in": w_in.astype(BF16), "w_out": w_out.astype(BF16),
            "scan": scan_tab}


def _flash_kernel(qb_ref, kb_ref, first_ref, last_ref, q_ref, k_ref, v_ref, o_ref,
                  acc_ref, *m_scratch, running_max):
    del qb_ref, kb_ref
    s_id = pl.program_id(1)

    @pl.when(first_ref[s_id] == 1)
    def _():
        acc_ref[...] = jnp.zeros(acc_ref.shape, F32)
        if running_max:
            m_scratch[0][...] = jnp.full(m_scratch[0].shape, -jnp.inf, F32)

    for hh in range(2):
        sl = slice(hh * HEAD_PAD, (hh + 1) * HEAD_PAD)
        s = lax.dot_general(q_ref[:, sl], k_ref[:, sl], (((1,), (1,)), ((), ())),
                            preferred_element_type=F32)
        if running_max:
            m_ref = m_scratch[0]
            m_prev = m_ref[hh]
            m_new = jnp.maximum(m_prev, jnp.max(s, axis=1, keepdims=True))
            p = jnp.exp2(s - m_new[:, :1]).astype(BF16)
            acc_ref[hh] = jnp.exp2(m_prev - m_new) * acc_ref[hh] + _dot(p, v_ref[:, sl])
            m_ref[hh] = m_new
        else:
            acc_ref[hh] += _dot(jnp.exp2(s).astype(BF16), v_ref[:, sl])

    @pl.when(last_ref[s_id] == 1)
    def _():
        lane = lax.broadcasted_iota(jnp.int32, o_ref.shape, 1)
        a0 = acc_ref[0]
        a1 = acc_ref[1]
        o0 = a0 / a0[:, MLA_V:MLA_V + 1]
        o1 = a1 / a1[:, 0:1]
        o_ref[...] = jnp.where(lane < MLA_V, o0, o1).astype(BF16)


def _flash_attention(q, k, v, seqs, tile, running_max):
    qb, kb, first, last = [], [], [], []
    base = 0
    for n_seq, seq_len in seqs:
        nb = seq_len // tile
        for _ in range(n_seq):
            for qi in range(nb):
                for ki in range(nb):
                    qb.append(base + qi)
                    kb.append(base + ki)
                    first.append(int(ki == 0))
                    last.append(int(ki == nb - 1))
            base += nb
    tabs = [jnp.asarray(np.asarray(a, np.int32)) for a in (qb, kb, first, last)]
    n_steps = len(qb)
    t_tok = q.shape[0]
    pair_w = 2 * HEAD_PAD
    scratch = [pltpu.VMEM((2, tile, LANES), F32)] * (2 if running_max else 1)
    grid_spec = pltpu.PrefetchScalarGridSpec(
        num_scalar_prefetch=4,
        grid=(MLA_HEADS // 2, n_steps),
        in_specs=[
            pl.BlockSpec((tile, pair_w), lambda p, s, qb, kb, fi, la: (qb[s], p)),
            pl.BlockSpec((tile, pair_w), lambda p, s, qb, kb, fi, la: (kb[s], p)),
            pl.BlockSpec((tile, pair_w), lambda p, s, qb, kb, fi, la: (kb[s], p)),
        ],
        out_specs=pl.BlockSpec((tile, 2 * MLA_V), lambda p, s, qb, kb, fi, la: (qb[s], p)),
        scratch_shapes=scratch,
    )
    return pl.pallas_call(
        functools.partial(_flash_kernel, running_max=running_max),
        grid_spec=grid_spec,
        out_shape=jax.ShapeDtypeStruct((t_tok, MLA_HEADS * MLA_V), BF16),
        compiler_params=_cparams(("parallel", "arbitrary")),
        name="mla_flash_max" if running_max else "mla_flash",
    )(*tabs, q, k, v)


def _memkv_kernel(mem_ref, gmem_ref, w_ref, gk_ref, k_ref, v_ref):
    x = mem_ref[...]
    h = (x * _rms_scale(x, D_MODEL) * gmem_ref[...]).astype(BF16)
    kv = _dot(h, w_ref[...])
    for hh in range(MEM_HEADS):
        sl = slice(hh * MEM_HEAD_DIM, (hh + 1) * MEM_HEAD_DIM)
        blk = kv[:, sl]
        k_ref[:, sl] = (blk * _rms_scale(blk, MEM_HEAD_DIM) * gk_ref[...]).astype(BF16)
    v_ref[...] = kv[:, MEM_WIDTH:].astype(BF16)


def _memory_kv(mem, lw):
    rows = mem.shape[0]
    tm = MEM_TOKENS
    return pl.pallas_call(
        _memkv_kernel,
        grid=(rows // tm,),
        in_specs=[pl.BlockSpec((tm, D_MODEL), lambda i: (i, 0)),
                  _const_spec(lw["g_mem"].shape), _const_spec(lw["w_mem_kv"].shape),
                  _const_spec(lw["g_mem_k"].shape)],
        out_specs=[pl.BlockSpec((tm, MEM_WIDTH), lambda i: (i, 0))] * 2,
        out_shape=[jax.ShapeDtypeStruct((rows, MEM_WIDTH), BF16)] * 2,
        compiler_params=_cparams(("parallel",)),
        name="memory_kv",
    )(mem, lw["g_mem"], lw["w_mem_kv"], lw["g_mem_k"])


def _gelu_tanh(x):
    return 0.5 * x * (1.0 + jnp.tanh(0.7978845608028654 * (x + 0.044715 * x * x * x)))


def _merge_kernel(*refs, with_router, n_x, n_first):
    x_refs = refs[:n_x]
    refs = refs[n_x:]
    if with_router:
        (ys_ref, ya_ref, qm_ref, km_ref, vm_ref, sg_ref, wglu_ref, bglu_ref, wb_ref,
         wo_ref, gffn_ref, wr_hi_ref, wr_lo_ref, rep_ref, x1_ref, h2_ref, route_ref) = refs
    else:
        (ys_ref, ya_ref, qm_ref, km_ref, vm_ref, sg_ref, wglu_ref, bglu_ref, wb_ref,
         wo_ref, gffn_ref, x1_ref, h2_ref) = refs

    ya = _gelu_tanh(ys_ref[...]).astype(BF16)
    z = _dot(ya, wglu_ref[...]) + bglu_ref[...]
    y_ssm = (z[:, :SSM_WIDTH] * _sigmoid(z[:, SSM_WIDTH:])).astype(BF16)

    mem_parts = []
    for hh in range(MEM_HEADS):
        sl = slice(hh * MEM_HEAD_DIM, (hh + 1) * MEM_HEAD_DIM)
        s = lax.dot_general(qm_ref[:, sl], km_ref[:, sl], (((1,), (1,)), ((), ())),
                            preferred_element_type=F32)
        p = jnp.exp(s - jnp.max(s, axis=1, keepdims=True))
        p = p / jnp.sum(p, axis=1, keepdims=True)
        mem_parts.append(_dot(p.astype(BF16), vm_ref[:, sl]))
    y_mem = jnp.concatenate(mem_parts, axis=1).astype(BF16)

    merged = None
    for b, yb in enumerate((y_ssm, ya_ref[...], y_mem)):
        gate = sg_ref[:, b * D_MODEL:(b + 1) * D_MODEL].astype(F32)
        term = gate * _dot(yb, wb_ref[b])
        merged = term if merged is None else merged + term

    x1 = _stream_read(x_refs, n_first) + _dot(merged.astype(BF16), wo_ref[...])
    x1_ref[...] = x1
    h2 = x1 * _rms_scale(x1, D_MODEL) * gffn_ref[...]

    if not with_router:
        h2_ref[...] = h2.astype(BF16)
        return

    for j in range(ROW_TILES):
        h2_ref[:, j, :] = h2[:, j * LANES:(j + 1) * LANES]

    h_hi, h_lo = _split_bf16(h2)
    logits = _dot(h_hi, wr_hi_ref[...]) + _dot(h_hi, wr_lo_ref[...]) + _dot(h_lo, wr_hi_ref[...])
    l_hi = logits.astype(BF16)
    l_rest = logits - l_hi.astype(F32)
    l_mid, l_lo = _split_bf16(l_rest)
    rep = _dot(l_hi, rep_ref[...]) + _dot(l_mid, rep_ref[...]) + _dot(l_lo, rep_ref[...])
    lg = [rep[:, e * LANES:(e + 1) * LANES] for e in range(N_EXPERTS)]
    neg = jnp.float32(-jnp.inf)

    def top1(vals):
        best = vals[0]
        for val in vals[1:]:
            best = jnp.maximum(best, val)
        idx = jnp.full(best.shape, float(N_EXPERTS), F32)
        for e in range(N_EXPERTS - 1, -1, -1):
            idx = jnp.where(vals[e] == best, float(e), idx)
        return best, idx

    m1, i1 = top1(lg)
    m2, i2 = top1([jnp.where(i1 == float(e), neg, lg[e]) for e in range(N_EXPERTS)])
    e2 = jnp.exp(m2 - m1)
    w1 = 1.0 / (1.0 + e2)
    w2 = e2 / (1.0 + e2)
    lane = lax.broadcasted_iota(jnp.int32, m1.shape, 1)
    out = jnp.where(lane == 0, i1, jnp.where(lane == 1, i2, jnp.where(lane == 2, w1, w2)))
    route_ref[...] = jnp.where(lane < 4, out, 0.0)


def _merge(x_parts, y_ssm_raw, y_mla, qm, km, vm, sg, lw, tiles, with_router):
    t_tok = y_mla.shape[0]
    tm = tiles["tm"]
    seq_map = tiles["seq_map"]

    def row(i):
        return (i, 0)

    weights = [lw["w_glu"], lw["b_glu"], lw["w_branch"], lw["w_out"], lw["g_ffn"]]
    if with_router:
        weights += [lw["w_router_hi"], lw["w_router_lo"], tiles["expert_rep"]]
    in_specs = _stream_specs(x_parts, tm, D_MODEL) + [
        pl.BlockSpec((tm, SSM_WIDTH), row),
        pl.BlockSpec((tm, MLA_HEADS * MLA_V), row),
        pl.BlockSpec((tm, MEM_WIDTH), row),
        pl.BlockSpec((MEM_TOKENS, MEM_WIDTH), lambda i: (seq_map(i), 0)),
        pl.BlockSpec((MEM_TOKENS, MEM_WIDTH), lambda i: (seq_map(i), 0)),
        pl.BlockSpec((tm, N_BRANCH * D_MODEL), row),
    ] + [_const_spec(w.shape) for w in weights]
    out_specs = [pl.BlockSpec((tm, D_MODEL), row)]
    out_shape = [jax.ShapeDtypeStruct((t_tok, D_MODEL), F32)]
    if with_router:
        out_specs += [pl.BlockSpec((tm, ROW_TILES, LANES), lambda i: (i, 0, 0)),
                      pl.BlockSpec((tm, LANES), row)]
        out_shape += [jax.ShapeDtypeStruct((t_tok, ROW_TILES, LANES), F32),
                      jax.ShapeDtypeStruct((t_tok, LANES), F32)]
    else:
        out_specs.append(pl.BlockSpec((tm, D_MODEL), row))
        out_shape.append(jax.ShapeDtypeStruct((t_tok, D_MODEL), BF16))
    kern = functools.partial(_merge_kernel, with_router=with_router, n_x=len(x_parts),
                             n_first=x_parts[0].shape[0] // tm)
    return pl.pallas_call(
        kern,
        grid=(t_tok // tm,),
        in_specs=in_specs,
        out_specs=out_specs,
        out_shape=out_shape,
        compiler_params=_cparams(("parallel",)),
        name="merge_router" if with_router else "merge",
    )(*x_parts, y_ssm_raw, y_mla, qm, km, vm, sg, *weights)


def _swiglu_acc(x, w1_ref, w3_ref, w2_ref, acc, chunk):
    d_ff = w1_ref.shape[1]
    for c in range(d_ff // chunk):
        sl = slice(c * chunk, (c + 1) * chunk)
        g = _dot(x, w1_ref[:, sl])
        u = _dot(x, w3_ref[:, sl])
        a = (g * _sigmoid(g) * u).astype(BF16)
        acc = acc + _dot(a, w2_ref[sl, :])
    return acc


def _ffn_kernel(h_ref, x_ref, w1_ref, w3_ref, w2_ref, o_ref):
    o_ref[...] = _swiglu_acc(h_ref[...], w1_ref, w3_ref, w2_ref, x_ref[...], FFN_CHUNK)


def _dense_ffn(h2, x1, lw, tiles):
    t_tok = x1.shape[0]
    tm = tiles["tm"]

    def row(i):
        return (i, 0)

    return pl.pallas_call(
        _ffn_kernel,
        grid=(t_tok // tm,),
        in_specs=[pl.BlockSpec((tm, D_MODEL), row), pl.BlockSpec((tm, D_MODEL), row),
                  _const_spec(lw["ffn_w1"].shape), _const_spec(lw["ffn_w3"].shape),
                  _const_spec(lw["ffn_w2"].shape)],
        out_specs=pl.BlockSpec((tm, D_MODEL), row),
        out_shape=jax.ShapeDtypeStruct((t_tok, D_MODEL), F32),
        compiler_params=_cparams(("parallel",)),
        name="dense_ffn",
    )(h2, x1, lw["ffn_w1"], lw["ffn_w3"], lw["ffn_w2"])


def _dispatch_kernel(pos_ref, h_ref, xs_in_ref, xs_ref, sem):
    del xs_in_ref
    tm = h_ref.shape[0]

    def tile_copy(r, dst):
        return pltpu.make_async_copy(h_ref.at[r], xs_ref.at[dst], sem)

    def start(r, c):
        tile_copy(r, pos_ref[0, 0, 2 * r]).start(priority=0)
        tile_copy(r, pos_ref[0, 0, 2 * r + 1]).start(priority=1)
        return c

    def wait(r, c):
        tile_copy(r, 0).wait()
        tile_copy(r, 0).wait()
        return c

    lax.fori_loop(0, tm, start, 0, unroll=DMA_UNROLL)
    lax.fori_loop(0, tm, wait, 0, unroll=DMA_UNROLL)


def _dispatch(h2t, pos3, n_rows, tm):
    t_tok = h2t.shape[0]
    xs0 = jnp.zeros((n_rows, ROW_TILES, LANES), F32)
    return pl.pallas_call(
        _dispatch_kernel,
        grid=(t_tok // tm,),
        in_specs=[pl.BlockSpec((1, 1, 2 * tm), lambda i: (i, 0, 0), memory_space=pltpu.SMEM),
                  pl.BlockSpec((tm, ROW_TILES, LANES), lambda i: (i, 0, 0)),
                  pl.BlockSpec(memory_space=pl.ANY)],
        out_specs=pl.BlockSpec(memory_space=pl.ANY),
        out_shape=jax.ShapeDtypeStruct(xs0.shape, F32),
        input_output_aliases={2: 0},
        scratch_shapes=[pltpu.SemaphoreType.DMA(())],
        compiler_params=_cparams(("arbitrary",)),
        name="moe_dispatch",
    )(pos3, h2t, xs0)


def _expert_kernel(be_ref, nu_ref, x_ref, w1_ref, w3_ref, w2_ref, y_ref):
    del be_ref
    i = pl.program_id(0)

    @pl.when(i < nu_ref[0])
    def _():
        x = jnp.concatenate([x_ref[:, j, :] for j in range(ROW_TILES)], axis=1).astype(BF16)
        acc = _swiglu_acc(x, w1_ref, w3_ref, w2_ref, jnp.zeros((x.shape[0], D_MODEL), F32), MOE_CHUNK)
        for j in range(ROW_TILES):
            y_ref[:, j, :] = acc[:, j * LANES:(j + 1) * LANES]

    @pl.when(i >= nu_ref[0])
    def _():
        y_ref[...] = jnp.zeros(y_ref.shape, F32)


def _expert_ffn(xs, blk_expert, n_used, lw, bm):
    n_rows = xs.shape[0]

    def w_spec(shape):
        return pl.BlockSpec((None,) + shape, lambda i, be, nu: (be[i], 0, 0),
                            pipeline_mode=pl.Buffered(1))

    grid_spec = pltpu.PrefetchScalarGridSpec(
        num_scalar_prefetch=2,
        grid=(n_rows // bm,),
        in_specs=[
            pl.BlockSpec((bm, ROW_TILES, LANES), lambda i, be, nu: (i, 0, 0)),
            w_spec((D_MODEL, D_FF_EXPERT)), w_spec((D_MODEL, D_FF_EXPERT)),
            w_spec((D_FF_EXPERT, D_MODEL)),
        ],
        out_specs=pl.BlockSpec((bm, ROW_TILES, LANES), lambda i, be, nu: (i, 0, 0)),
    )
    return pl.pallas_call(
        _expert_kernel,
        grid_spec=grid_spec,
        out_shape=jax.ShapeDtypeStruct(xs.shape, F32),
        compiler_params=_cparams(("arbitrary",)),
        name="moe_experts",
    )(blk_expert, n_used, xs, lw["moe_w1"], lw["moe_w3"], lw["moe_w2"])


def _combine_kernel(pos_ref, pos_next_ref, ys_hbm, x_ref, route_ref, *refs, n_first):
    out_refs, (buf_ref, sem) = refs[:-2], refs[-2:]
    tm = x_ref.shape[0]
    i = pl.program_id(0)
    n = pl.num_programs(0)
    slot = i % 2

    def tile_copy(s, k, r, src):
        return pltpu.make_async_copy(ys_hbm.at[src], buf_ref.at[s, k, r], sem.at[s])

    def issue(p_ref, s):
        def start(r, c):
            tile_copy(s, 0, r, p_ref[0, 0, 2 * r]).start(priority=0)
            tile_copy(s, 1, r, p_ref[0, 0, 2 * r + 1]).start(priority=1)
            return c
        lax.fori_loop(0, tm, start, 0, unroll=DMA_UNROLL)

    @pl.when(i == 0)
    def _():
        issue(pos_ref, slot)

    @pl.when(i + 1 < n)
    def _():
        issue(pos_next_ref, 1 - slot)

    def wait(r, c):
        tile_copy(slot, 0, r, 0).wait()
        tile_copy(slot, 1, r, 0).wait()
        return c

    lax.fori_loop(0, tm, wait, 0, unroll=DMA_UNROLL)
    w1 = route_ref[:, 2:3]
    w2 = route_ref[:, 3:4]

    def write(o_ref):
        for j in range(ROW_TILES):
            sl = slice(j * LANES, (j + 1) * LANES)
            o_ref[:, sl] = x_ref[:, sl] + w1 * buf_ref[slot, 0, :, j, :] + w2 * buf_ref[slot, 1, :, j, :]

    if len(out_refs) == 1:
        write(out_refs[0])
    else:
        pl.when(i < n_first)(lambda: write(out_refs[0]))
        pl.when(i >= n_first)(lambda: write(out_refs[1]))


def _combine(ys, pos3, x1, route, tm, out_rows):
    t_tok = x1.shape[0]
    n_tiles = t_tok // tm
    n_first = out_rows[0] // tm
    if len(out_rows) == 1:
        out_specs = [pl.BlockSpec((tm, D_MODEL), lambda i: (i, 0))]
    else:
        out_specs = [pl.BlockSpec((tm, D_MODEL), lambda i: (jnp.minimum(i, n_first - 1), 0)),
                     pl.BlockSpec((tm, D_MODEL), lambda i: (jnp.maximum(i - n_first, 0), 0))]
    pos_spec = functools.partial(pl.BlockSpec, (1, 1, 2 * tm), memory_space=pltpu.SMEM)
    return pl.pallas_call(
        functools.partial(_combine_kernel, n_first=n_first),
        grid=(n_tiles,),
        in_specs=[pos_spec(lambda i: (i, 0, 0)),
                  pos_spec(lambda i: (jnp.minimum(i + 1, n_tiles - 1), 0, 0)),
                  pl.BlockSpec(memory_space=pl.ANY),
                  pl.BlockSpec((tm, D_MODEL), lambda i: (i, 0)),
                  pl.BlockSpec((tm, LANES), lambda i: (i, 0))],
        out_specs=out_specs,
        out_shape=[jax.ShapeDtypeStruct((r, D_MODEL), F32) for r in out_rows],
        scratch_shapes=[pltpu.VMEM((2, 2, tm, ROW_TILES, LANES), F32), pltpu.SemaphoreType.DMA((2,))],
        compiler_params=_cparams(("arbitrary",)),
        name="moe_combine",
    )(pos3, pos3, ys, x1, route)


def _moe_ffn(h2t, x1, route, lw, tiles, out_rows):
    t_tok = x1.shape[0]
    tm = tiles["tm"]
    bm = tiles["moe_block"]
    n_assign = 2 * t_tok
    experts = route[:, :2].astype(jnp.int32).reshape(n_assign)
    onehot = (experts[:, None] == jnp.arange(N_EXPERTS, dtype=jnp.int32)[None, :]).astype(jnp.int32)
    csum = jnp.cumsum(onehot, axis=0)
    rank = jnp.sum((csum - 1) * onehot, axis=1)
    counts = csum[-1]
    padded = ((counts + bm - 1) // bm) * bm
    ends = jnp.cumsum(padded)
    starts = ends - padded
    pos = jnp.sum(onehot * starts[None, :], axis=1) + rank
    n_blocks = n_assign // bm + N_EXPERTS
    blk_start = jnp.arange(n_blocks, dtype=jnp.int32) * bm
    blk_expert = jnp.sum((blk_start[:, None] >= ends[None, :]).astype(jnp.int32), axis=1)
    blk_expert = jnp.minimum(blk_expert, N_EXPERTS - 1)
    n_used = (ends[-1:] // bm).astype(jnp.int32)
    pos3 = pos.astype(jnp.int32).reshape(t_tok // tm, 1, 2 * tm)

    xs = _dispatch(h2t, pos3, n_blocks * bm, tm)
    ys = _expert_ffn(xs, blk_expert, n_used, lw, bm)
    return _combine(ys, pos3, x1, route, tm, out_rows)


def _head_cols(w, per_head, n_heads, offset=0):
    k = w.shape[0]
    w = w.reshape(k, n_heads, per_head)
    w = jnp.pad(w, ((0, 0), (0, 0), (offset, HEAD_PAD - per_head - offset)))
    return w.reshape(k, n_heads * HEAD_PAD)


def _swap_rope(w):
    z = jnp.zeros_like(w[..., :MLA_NOPE])
    return jnp.concatenate([z, w[..., MLA_NOPE + HALF_ROPE:], w[..., MLA_NOPE:MLA_NOPE + HALF_ROPE]], axis=-1)


def _pad_lane_block(v, offset=0):
    return jnp.pad(v, ((0, 0),) * (v.ndim - 1) + ((offset, LANES - v.shape[-1] - offset),))


def _layer_weights(i, p, rope_cos, rope_sin):
    o = IN_OFFSETS
    w_in = p["w_in"][i]
    lw = {}
    lw["g_mix"] = p["g_mix"][i].reshape(1, D_MODEL)
    lw["w_u"] = w_in[:, :o[0]].astype(BF16)
    lw["w_cq"] = w_in[:, o[0]:o[1]].astype(BF16)
    w_kr = w_in[:, o[2]:o[3]]
    w_kr_sw = jnp.concatenate([w_kr[:, HALF_ROPE:], w_kr[:, :HALF_ROPE]], axis=1)
    lw["w_ckv"] = jnp.concatenate([w_in[:, o[1]:o[2]], _pad_lane_block(w_kr, MLA_NOPE),
                                   _pad_lane_block(w_kr_sw, MLA_NOPE)], axis=1).astype(BF16)
    lw["w_qm"] = w_in[:, o[3]:o[4]].astype(BF16)
    lw["w_gate"] = w_in[:, o[4]:].astype(BF16)
    lw["g_q_lora"] = p["g_q_lora"][i].reshape(1, Q_LORA)
    w_uq = p["w_uq"][i].reshape(Q_LORA, MLA_HEADS, MLA_QK)
    lw["w_uq"] = jnp.concatenate(
        [_head_cols(w_uq.reshape(Q_LORA, -1), MLA_QK, MLA_HEADS),
         _head_cols(_swap_rope(w_uq).reshape(Q_LORA, -1), MLA_QK, MLA_HEADS)], axis=1).astype(BF16)
    lw["g_kv_lora"] = p["g_kv_lora"][i].reshape(1, KV_LORA)
    w_ukv = p["w_ukv"][i].reshape(KV_LORA, MLA_HEADS, MLA_NOPE + MLA_V)
    lw["w_kn"] = _head_cols(w_ukv[:, :, :MLA_NOPE].reshape(KV_LORA, -1), MLA_NOPE, MLA_HEADS).astype(BF16)
    w_v = w_ukv[:, :, MLA_NOPE:].reshape(KV_LORA, MLA_HEADS // 2, 2, MLA_V)
    w_v = jnp.stack([jnp.pad(w_v[:, :, 0], ((0, 0), (0, 0), (0, HEAD_PAD - MLA_V))),
                     jnp.pad(w_v[:, :, 1], ((0, 0), (0, 0), (HEAD_PAD - MLA_V, 0)))], axis=2)
    lw["w_v"] = w_v.reshape(KV_LORA, MLA_WIDTH_PAD).astype(BF16)
    one_even = jnp.zeros((HEAD_PAD,), F32).at[MLA_V].set(1.0)
    one_odd = jnp.zeros((HEAD_PAD,), F32).at[0].set(1.0)
    lw["v_one"] = jnp.tile(jnp.concatenate([one_even, one_odd]), MLA_HEADS // 2).reshape(1, MLA_WIDTH_PAD)
    for name, g in (("q", p["g_mla_q"][i]), ("k", p["g_mla_k"][i])):
        lw["rope_a" + name] = rope_cos * _pad_lane_block(g)[None, :]
        lw["rope_b" + name] = rope_sin * _pad_lane_block(_swap_rope(g))[None, :]
    lw["g_mem_q"] = jnp.tile(p["g_mem_q"][i], MEM_HEADS).reshape(1, MEM_WIDTH) * (MEM_HEAD_DIM ** -0.5)
    lw["g_mem"] = p["g_mem"][i].reshape(1, D_MODEL)
    lw["w_mem_kv"] = p["w_mem_kv"][i].astype(BF16)
    lw["g_mem_k"] = p["g_mem_k"][i].reshape(1, MEM_HEAD_DIM)
    lw["w_glu"] = p["ssm_w_glu"][i].astype(BF16)
    lw["b_glu"] = p["ssm_b_glu"][i].reshape(1, 2 * SSM_WIDTH)
    lw["w_branch"] = p["w_branch"][i].astype(BF16)
    lw["w_out"] = p["w_out"][i].astype(BF16)
    lw["g_ffn"] = p["g_ffn"][i].reshape(1, D_MODEL)
    lw["ssm"] = _ssm_tables(p["ssm_a_re"][i], p["ssm_a_im"][i], p["ssm_log_dt"][i], p["ssm_b_re"][i],
                            p["ssm_b_im"][i], p["ssm_c_re"][i], p["ssm_c_im"][i], p["ssm_d"][i])
    lw["score_bound"] = (1.02 * MLA_QK ** 0.5 * LOG2E
                         * jnp.max(jnp.abs(p["g_mla_q"][i])) * jnp.max(jnp.abs(p["g_mla_k"][i])))
    if i % 2 == 0:
        w13 = p["ffn_w13"][i // 2]
        lw["ffn_w1"] = w13[:, :D_FF].astype(BF16)
        lw["ffn_w3"] = w13[:, D_FF:].astype(BF16)
        lw["ffn_w2"] = p["ffn_w2"][i // 2].astype(BF16)
    else:
        w13 = p["moe_w13"][i // 2]
        lw["moe_w1"] = w13[:, :, :D_FF_EXPERT].astype(BF16)
        lw["moe_w3"] = w13[:, :, D_FF_EXPERT:].astype(BF16)
        lw["moe_w2"] = p["moe_w2"][i // 2].astype(BF16)
        wr = jnp.pad(p["moe_router"][i // 2], ((0, 0), (0, LANES - N_EXPERTS)))
        lw["w_router_hi"], lw["w_router_lo"] = _split_bf16(wr)
    return lw


def _rope_tables(max_len):
    inv = 1.0 / (ROPE_BASE ** (jnp.arange(0, MLA_ROPE, 2, dtype=F32) / MLA_ROPE))
    ang = jnp.arange(max_len, dtype=F32)[:, None] * inv[None, :]
    cos, sin = jnp.cos(ang), jnp.sin(ang)
    ones = jnp.ones((max_len, MLA_NOPE), F32)
    zn = jnp.zeros((max_len, MLA_NOPE), F32)
    zp = jnp.zeros((max_len, HEAD_PAD - MLA_QK), F32)
    return (jnp.concatenate([ones, cos, cos, zp], axis=1),
            jnp.concatenate([zn, -sin, sin, zp], axis=1))


def _gcd_tile(limit, *sizes):
    t = limit
    while any(s % t for s in sizes):
        t //= 2
    return t


def kernel(x_prompt, x_sample, mem_prompt, mem_sample, g_mix, w_in, ssm_a_re, ssm_a_im, ssm_log_dt, ssm_b_re, ssm_b_im, ssm_c_re, ssm_c_im, ssm_d, ssm_w_glu, ssm_b_glu, g_q_lora, w_uq, g_kv_lora, w_ukv, g_mla_q, g_mla_k, g_mem, w_mem_kv, g_mem_q, g_mem_k, w_branch, w_out, g_ffn, ffn_w13, ffn_w2, moe_router, moe_w13, moe_w2):
    params = dict(g_mix=g_mix, w_in=w_in, ssm_a_re=ssm_a_re, ssm_a_im=ssm_a_im, ssm_log_dt=ssm_log_dt,
                  ssm_b_re=ssm_b_re, ssm_b_im=ssm_b_im, ssm_c_re=ssm_c_re, ssm_c_im=ssm_c_im, ssm_d=ssm_d,
                  ssm_w_glu=ssm_w_glu, ssm_b_glu=ssm_b_glu, g_q_lora=g_q_lora, w_uq=w_uq,
                  g_kv_lora=g_kv_lora, w_ukv=w_ukv, g_mla_q=g_mla_q, g_mla_k=g_mla_k, g_mem=g_mem,
                  w_mem_kv=w_mem_kv, g_mem_q=g_mem_q, g_mem_k=g_mem_k, w_branch=w_branch, w_out=w_out,
                  g_ffn=g_ffn, ffn_w13=ffn_w13, ffn_w2=ffn_w2, moe_router=moe_router, moe_w13=moe_w13,
                  moe_w2=moe_w2)
    depth = g_mix.shape[0]
    bp, lp, _ = x_prompt.shape
    bs, ls, _ = x_sample.shape
    tp, ts = bp * lp, bs * ls
    t_tok = tp + ts

    tm = _gcd_tile(TOKEN_TILE, lp, ls)
    attn_tile = _gcd_tile(ATTN_TILE, lp, ls)
    npt = tp // tm
    lp_t, ls_t = lp // tm, ls // tm

    def pos_map(i):
        return jnp.where(i < npt, i % lp_t, (i - npt) % ls_t)

    def seq_map(i):
        return jnp.where(i < npt, i // lp_t, bp + (i - npt) // ls_t)

    rope_cos, rope_sin = _rope_tables(max(lp, ls))
    head_ind = (jnp.arange(MLA_WIDTH_PAD)[:, None] // HEAD_PAD == jnp.arange(LANES)[None, :])
    mem_bd = (jnp.arange(MEM_WIDTH)[:, None] // MEM_HEAD_DIM == jnp.arange(MEM_WIDTH)[None, :] // MEM_HEAD_DIM)
    tiles = dict(tm=tm, pos_map=pos_map, seq_map=seq_map,
                 ones=jnp.ones((D_MODEL, LANES), BF16),
                 head_ind=head_ind.astype(BF16), head_ind_t=head_ind.T.astype(BF16),
                 mem_bd=mem_bd.astype(BF16),
                 expert_rep=(jnp.arange(LANES)[:, None] == jnp.arange(N_EXPERTS * LANES)[None, :] // LANES).astype(BF16),
                 moe_block=_gcd_tile(MOE_BLOCK, 2 * t_tok))

    x_parts = [x_prompt.reshape(tp, D_MODEL), x_sample.reshape(ts, D_MODEL)]
    mem = jnp.concatenate([mem_prompt.reshape(bp * MEM_TOKENS, D_MODEL),
                           mem_sample.reshape(bs * MEM_TOKENS, D_MODEL)], axis=0)
    seqs = [(bp, lp), (bs, ls)]

    for i in range(depth):
        lw = _layer_weights(i, params, rope_cos, rope_sin)
        u, q, k, v, qm, sg = _in_projection(x_parts, lw, tiles)

        y_ssm_raw = _ssm_call(u, lw["ssm"], seqs)

        y_mla = lax.cond(
            lw["score_bound"] <= MAX_SAFE_LOG2_SCORE,
            lambda q_, k_, v_: _flash_attention(q_, k_, v_, seqs, attn_tile, running_max=False),
            lambda q_, k_, v_: _flash_attention(q_, k_, v_, seqs, attn_tile, running_max=True),
            q, k, v)
        km, vm = _memory_kv(mem, lw)

        with_router = i % 2 == 1
        outs = _merge(x_parts, y_ssm_raw, y_mla, qm, km, vm, sg, lw, tiles, with_router)
        if with_router:
            x1, h2t, route = outs
            out_rows = (tp, ts) if i == depth - 1 else (t_tok,)
            x_parts = _moe_ffn(h2t, x1, route, lw, tiles, out_rows)
        else:
            x1, h2 = outs
            x_parts = [_dense_ffn(h2, x1, lw, tiles)]

    if len(x_parts) == 1:
        x_parts = [x_parts[0][:tp], x_parts[0][tp:]]
    return (x_parts[0].reshape(bp, lp, D_MODEL), x_parts[1].reshape(bs, ls, D_MODEL))
```

```python
import functools
import math

import numpy as np
import jax
import jax.numpy as jnp
from jax import lax
from jax.experimental import pallas as pl
from jax.experimental.pallas import tpu as pltpu

F32 = jnp.float32
BF16 = jnp.bfloat16
EPS = 1e-6
LOG2E = math.log2(math.e)

D_MODEL = 1024
SSM_WIDTH = 512
SSM_GROUPS = 32
SSM_GROUP = 16
SSM_STATE = 64
MLA_HEADS = 8
MLA_NOPE = 64
MLA_ROPE = 32
MLA_V = 64
MLA_QK = MLA_NOPE + MLA_ROPE
Q_LORA = 256
KV_LORA = 128
ROPE_BASE = 10000.0
MEM_TOKENS = 256
MEM_HEADS = 4
MEM_HEAD_DIM = 128
MEM_WIDTH = 512
N_BRANCH = 3
D_FF = 2816
N_EXPERTS = 8
D_FF_EXPERT = 3584
IN_SIZES = (SSM_WIDTH, Q_LORA, KV_LORA, MLA_ROPE, MEM_WIDTH, N_BRANCH * D_MODEL)
IN_OFFSETS = tuple(int(v) for v in np.cumsum(IN_SIZES)[:-1])

LANES = 128
SUBLANES = 8
HEAD_PAD = 128
HALF_ROPE = MLA_ROPE // 2
MLA_WIDTH_PAD = MLA_HEADS * HEAD_PAD
ROW_TILES = D_MODEL // LANES
VMEM_LIMIT = 56 * 1024 * 1024
MAX_SAFE_LOG2_SCORE = 100.0

TOKEN_TILE = 512
ATTN_Q_TILE = 2048
ATTN_KV_TILE = 1024
FFN_TILE = 1024
SSM_CHUNK = 16
MOE_BLOCK = 1024
FFN_CHUNK = 256
MOE_CHUNK = 512
DMA_UNROLL = 8


def _cparams(sem):
    return pltpu.CompilerParams(dimension_semantics=sem, vmem_limit_bytes=VMEM_LIMIT)


def _const_spec(shape):
    nd = len(shape)
    return pl.BlockSpec(shape, lambda *_: (0,) * nd)


def _sigmoid(x):
    return 1.0 / (1.0 + jnp.exp(-x))


def _rms_scale(x, n):
    return lax.rsqrt(jnp.sum(x * x, axis=-1, keepdims=True) * (1.0 / n) + EPS)


def _split_bf16(x):
    hi = x.astype(BF16)
    lo = (x - hi.astype(F32)).astype(BF16)
    return hi, lo


def _dot(a, b):
    return jnp.dot(a, b, preferred_element_type=F32)


def _stream_specs(parts, tm, width):
    if len(parts) == 1:
        return [pl.BlockSpec((tm, width), lambda i: (i, 0))]
    n0 = parts[0].shape[0] // tm
    return [pl.BlockSpec((tm, width), lambda i: (jnp.minimum(i, n0 - 1), 0)),
            pl.BlockSpec((tm, width), lambda i: (jnp.maximum(i - n0, 0), 0))]


def _stream_read(refs, n_first):
    if len(refs) == 1:
        return refs[0][...]
    return jnp.where(pl.program_id(0) < n_first, refs[0][...], refs[1][...])


def _inproj_kernel(*refs, n_x, n_first):
    x_refs = refs[:n_x]
    (gmix_ref, wu_ref, wg_ref, wqm_ref, wcq_ref, wckv_ref,
     gql_ref, wuq_ref, gkvl_ref, wkn_ref, wv_ref, vone_ref, gmq_ref,
     ones_ref, ind_ref, indt_ref, bd_ref,
     aq_ref, bq_ref, ak_ref, bk_ref,
     u_ref, q_ref, k_ref, v_ref, qm_ref, sg_ref) = refs[n_x:]

    def row_ssq(val):
        n = val.shape[1]
        return _dot((val * val).astype(BF16), ones_ref[:n, :])

    def head_scale(val, scale):
        ss = _dot((val * val).astype(BF16), ind_ref[...])
        r = lax.rsqrt(ss * (1.0 / MLA_QK) + EPS) * scale
        r_hi, r_lo = _split_bf16(r)
        return _dot(r_hi, indt_ref[...]) + _dot(r_lo, indt_ref[...])

    x = _stream_read(x_refs, n_first)
    rs = lax.rsqrt(row_ssq(x) * (1.0 / D_MODEL) + EPS)
    h = (x * jnp.tile(rs, (1, D_MODEL // LANES)) * gmix_ref[...]).astype(BF16)

    u_ref[...] = _dot(h, wu_ref[...])

    for b in range(N_BRANCH):
        sl = slice(b * D_MODEL, (b + 1) * D_MODEL)
        sg_ref[:, sl] = _sigmoid(_dot(h, wg_ref[:, sl]).astype(BF16))

    zq = _dot(h, wqm_ref[...])
    zr = lax.rsqrt(_dot((zq * zq).astype(BF16), bd_ref[...]) * (1.0 / MEM_HEAD_DIM) + EPS)
    qm_ref[...] = (zq * zr * gmq_ref[...]).astype(BF16)

    cq = _dot(h, wcq_ref[...])
    cq_rs = lax.rsqrt(row_ssq(cq) * (1.0 / Q_LORA) + EPS)
    cqn = (cq * jnp.tile(cq_rs, (1, Q_LORA // LANES)) * gql_ref[...]).astype(BF16)
    qq = _dot(cqn, wuq_ref[...])
    qf = qq[:, :MLA_WIDTH_PAD]
    qsw = qq[:, MLA_WIDTH_PAD:]

    z3 = _dot(h, wckv_ref[...])
    ckv = z3[:, :LANES]
    kr = z3[:, LANES:2 * LANES]
    kr_sw = z3[:, 2 * LANES:]
    ckv_rs = lax.rsqrt(row_ssq(ckv) * (1.0 / KV_LORA) + EPS)
    ckvn = (ckv * ckv_rs * gkvl_ref[...]).astype(BF16)
    kf = _dot(ckvn, wkn_ref[...]) + jnp.tile(kr, (1, MLA_HEADS))
    v_ref[...] = (_dot(ckvn, wv_ref[...]) + vone_ref[...]).astype(BF16)

    q_rs = head_scale(qf, MLA_QK ** -0.5 * LOG2E)
    k_rs = head_scale(kf, 1.0)
    aq = aq_ref[...]
    bq = bq_ref[...]
    ak = ak_ref[...]
    k_rot = kr_sw * bk_ref[...]
    for hh in range(MLA_HEADS):
        sl = slice(hh * HEAD_PAD, (hh + 1) * HEAD_PAD)
        q_ref[:, sl] = ((qf[:, sl] * aq + qsw[:, sl] * bq) * q_rs[:, sl]).astype(BF16)
        k_ref[:, sl] = ((kf[:, sl] * ak + k_rot) * k_rs[:, sl]).astype(BF16)


def _in_projection(x_parts, lw, tiles):
    t_tok = sum(p.shape[0] for p in x_parts)
    tm = tiles["tm"]
    pos_map = tiles["pos_map"]

    def row(i):
        return (i, 0)

    weights = [lw["g_mix"], lw["w_u"], lw["w_gate"], lw["w_qm"], lw["w_cq"], lw["w_ckv"],
               lw["g_q_lora"], lw["w_uq"], lw["g_kv_lora"], lw["w_kn"], lw["w_v"], lw["v_one"],
               lw["g_mem_q"], tiles["ones"], tiles["head_ind"], tiles["head_ind_t"], tiles["mem_bd"]]
    rope = [lw["rope_aq"], lw["rope_bq"], lw["rope_ak"], lw["rope_bk"]]
    in_specs = _stream_specs(x_parts, tm, D_MODEL) + [_const_spec(w.shape) for w in weights]
    in_specs += [pl.BlockSpec((tm, HEAD_PAD), lambda i: (pos_map(i), 0))] * len(rope)
    kern = functools.partial(_inproj_kernel, n_x=len(x_parts), n_first=x_parts[0].shape[0] // tm)
    out_widths = (SSM_WIDTH, MLA_WIDTH_PAD, MLA_WIDTH_PAD, MLA_WIDTH_PAD, MEM_WIDTH, N_BRANCH * D_MODEL)
    return pl.pallas_call(
        kern,
        grid=(t_tok // tm,),
        in_specs=in_specs,
        out_specs=[pl.BlockSpec((tm, w), row) for w in out_widths],
        out_shape=[jax.ShapeDtypeStruct((t_tok, w), F32 if j == 0 else BF16) for j, w in enumerate(out_widths)],
        compiler_params=_cparams(("parallel",)),
        name="in_projection",
    )(*x_parts, *weights, *rope)


SSM_LANE_GROUPS = LANES // SSM_GROUP
SSM_TAB_DIR = 11
(_TAB_STEP, _TAB_POW, _TAB_A8, _TAB_MASK) = (0, 6, 8, 10)


def _granule_transpose(vs, lane):
    for dist in (4, 2, 1):
        shift = dist * SSM_GROUP
        low = (lane // shift) % 2 == 0
        new = list(vs)
        for a in range(SSM_LANE_GROUPS):
            if a & dist == 0:
                b = a | dist
                new[a] = jnp.where(low, vs[a], pltpu.roll(vs[b], shift, 1))
                new[b] = jnp.where(low, pltpu.roll(vs[a], LANES - shift, 1), vs[b])
        vs = new
    return vs


def _chunk_scan(x, xs, tab_ref, base, keep, backward):
    nv, _, nl = x.shape
    for j, k in enumerate((1, 2, 4)):
        shift = SUBLANES - k if backward else k
        rx = pltpu.roll(x, shift, 1)
        rxs = pltpu.roll(xs, shift, 1)
        a1 = tab_ref[base + _TAB_STEP + 2 * j][None]
        a2 = tab_ref[base + _TAB_STEP + 2 * j + 1][None]
        x, xs = x + a1 * rx + a2 * rxs, xs + a1 * rxs - a2 * rx
    a1 = tab_ref[base + _TAB_A8]
    a2 = tab_ref[base + _TAB_A8 + 1]
    edge = 0 if backward else SUBLANES - 1
    c = jnp.zeros((SUBLANES, nl), F32)
    cs = c
    cb = [None] * nv
    csb = [None] * nv
    for v in (range(nv - 1, -1, -1) if backward else range(nv)):
        kp = keep(v)
        c = c * kp
        cs = cs * kp
        cb[v] = c
        csb[v] = cs
        ex = jnp.broadcast_to(x[v, edge:edge + 1, :], (SUBLANES, nl))
        exs = jnp.broadcast_to(xs[v, edge:edge + 1, :], (SUBLANES, nl))
        c, cs = ex + a1 * c + a2 * cs, exs + a1 * cs - a2 * c
    xe = pltpu.roll(x, SUBLANES - 1 if backward else 1, 1) * tab_ref[base + _TAB_MASK][None]
    return (xe + tab_ref[base + _TAB_POW][None] * jnp.stack(cb)
            + tab_ref[base + _TAB_POW + 1][None] * jnp.stack(csb))


def _ssm_kernel(keepf_ref, keepb_ref, u_ref, t_ref, win_ref, wout_ref, tab_ref, y_ref):
    blk = pl.program_id(1)
    rows = u_ref.shape[0] // SSM_CHUNK
    nv = rows // SUBLANES
    ng = SSM_LANE_GROUPS
    lane = lax.broadcasted_iota(jnp.int32, (rows, LANES), 1)

    pieces = [u_ref[pl.ds(t, rows, stride=SSM_CHUNK), :] for t in range(SSM_CHUNK)]
    halves = [_granule_transpose(pieces[ng * h:ng * (h + 1)], lane) for h in range(SSM_CHUNK // ng)]

    ys = []
    parts = [[], [], [], []]
    for g in range(ng):
        xg = jnp.concatenate([h[g] for h in halves], axis=1).astype(BF16)
        ys.append(_dot(xg, t_ref[g]))
        s4 = _dot(xg, win_ref[g])
        for j in range(4):
            parts[j].append(s4[:, j * LANES:(j + 1) * LANES])
    f, fs, b, bs = [jnp.concatenate(p, axis=1).reshape(nv, SUBLANES, ng * LANES) for p in parts]

    xin_f = _chunk_scan(f, fs, tab_ref, 0, lambda v: keepf_ref[blk * nv + v].astype(F32), False)
    xin_b = _chunk_scan(b, bs, tab_ref, SSM_TAB_DIR, lambda v: keepb_ref[blk * nv + v].astype(F32), True)
    xin_f = xin_f.reshape(rows, ng * LANES)
    xin_b = xin_b.reshape(rows, ng * LANES)

    outs = []
    for g in range(ng):
        sl = slice(g * LANES, (g + 1) * LANES)
        xs = jnp.concatenate([xin_f[:, sl], xin_b[:, sl]], axis=1).astype(BF16)
        outs.append(ys[g] + _dot(xs, wout_ref[g]))
    for h in range(SSM_CHUNK // ng):
        back = _granule_transpose([o[:, h * LANES:(h + 1) * LANES] for o in outs], lane)
        for tt in range(ng):
            y_ref[pl.ds(ng * h + tt, rows, stride=SSM_CHUNK), :] = back[tt]


def _ssm_call(u, tabs, seqs):
    t_tok = u.shape[0]
    block = int(np.lcm.reduce([ln for _, ln in seqs]))
    vreg_tokens = SUBLANES * SSM_CHUNK
    assert all((n * ln) % block == 0 and ln % vreg_tokens == 0 for n, ln in seqs), seqs
    nv = block // vreg_tokens
    keep_f, keep_b = [], []
    for n_seq, seq_len in seqs:
        for _ in range(n_seq * seq_len // block):
            for v in range(nv):
                keep_f.append(int((v * vreg_tokens) % seq_len != 0))
                keep_b.append(int(((v + 1) * vreg_tokens) % seq_len != 0))
    keep = [jnp.asarray(np.asarray(a, np.int32)) for a in (keep_f, keep_b)]
    ng = SSM_LANE_GROUPS
    width = SSM_CHUNK * SSM_GROUP
    n_tab = 2 * SSM_TAB_DIR
    grid_spec = pltpu.PrefetchScalarGridSpec(
        num_scalar_prefetch=2,
        grid=(SSM_WIDTH // LANES, t_tok // block),
        in_specs=[
            pl.BlockSpec((block, LANES), lambda g, b, kf, kb: (b, g)),
            pl.BlockSpec((ng, width, width), lambda g, b, kf, kb: (g, 0, 0)),
            pl.BlockSpec((ng, width, 4 * LANES), lambda g, b, kf, kb: (g, 0, 0)),
            pl.BlockSpec((ng, 2 * LANES, width), lambda g, b, kf, kb: (g, 0, 0)),
            pl.BlockSpec((n_tab, SUBLANES, ng * LANES), lambda g, b, kf, kb: (0, 0, g)),
        ],
        out_specs=pl.BlockSpec((block, LANES), lambda g, b, kf, kb: (b, g)),
    )
    return pl.pallas_call(
        _ssm_kernel,
        grid_spec=grid_spec,
        out_shape=jax.ShapeDtypeStruct(u.shape, F32),
        compiler_params=_cparams(("parallel", "parallel")),
        name="ssm_chunked",
    )(*keep, u, tabs["t"], tabs["w_in"], tabs["w_out"], tabs["scan"])


def _ssm_tables(a_re, a_im, log_dt, b_re, b_im, c_re, c_im, d_skip):
    q = SSM_CHUNK
    hp = lax.Precision.HIGHEST
    dt = jnp.exp(log_dt)[..., None, None]
    ks = jnp.arange(q + 1, dtype=F32)
    mag = jnp.exp(a_re[..., None] * dt * ks)
    ang = a_im[..., None] * dt * ks
    pw_re = mag * jnp.cos(ang)
    pw_im = mag * jnp.sin(ang)
    lb_re, lb_im = pw_re[..., 1], pw_im[..., 1]
    den = a_re * a_re + a_im * a_im
    f_re = ((lb_re - 1.0) * a_re + lb_im * a_im) / den
    f_im = (lb_im * a_re - (lb_re - 1.0) * a_im) / den
    bb_re = f_re[..., None] * b_re - f_im[..., None] * b_im
    bb_im = f_re[..., None] * b_im + f_im[..., None] * b_re

    cl_re = c_re[..., None] * pw_re[:, :, None] - c_im[..., None] * pw_im[:, :, None]
    cl_im = c_re[..., None] * pw_im[:, :, None] + c_im[..., None] * pw_re[:, :, None]

    kern = (jnp.einsum("dgnpk,dgpm->dgknm", cl_re[..., :q], bb_re, precision=hp)
            - jnp.einsum("dgnpk,dgpm->dgknm", cl_im[..., :q], bb_im, precision=hp))
    s_idx = jnp.arange(q)[:, None]
    t_idx = jnp.arange(q)[None, :]
    lag_f = jnp.clip(t_idx - s_idx, 0, q - 1)
    lag_b = jnp.clip(s_idx - t_idx, 0, q - 1)
    tf = kern[0][:, lag_f] * (t_idx >= s_idx)[None, :, :, None, None]
    tb = kern[1][:, lag_b] * (s_idx >= t_idx)[None, :, :, None, None]
    t_mat = (tf + tb).transpose(0, 1, 4, 2, 3)
    t_mat = t_mat.reshape(SSM_GROUPS, q * SSM_GROUP, q * SSM_GROUP)
    d_rep = jnp.tile(d_skip.reshape(SSM_GROUPS, 1, SSM_GROUP), (1, q, 1)).reshape(SSM_GROUPS, -1)
    t_mat = t_mat + d_rep[:, :, None] * jnp.eye(q * SSM_GROUP, dtype=F32)[None]

    def w_in_dir(d, exps):
        pr = pw_re[d][:, :, exps]
        pi = pw_im[d][:, :, exps]
        re = pr[..., None] * bb_re[d][:, :, None, :] - pi[..., None] * bb_im[d][:, :, None, :]
        im = pr[..., None] * bb_im[d][:, :, None, :] + pi[..., None] * bb_re[d][:, :, None, :]
        re = re.transpose(0, 2, 3, 1).reshape(SSM_GROUPS, q * SSM_GROUP, SSM_STATE)
        im = im.transpose(0, 2, 3, 1).reshape(SSM_GROUPS, q * SSM_GROUP, SSM_STATE)
        return jnp.concatenate([re, im, im, re], axis=-1)

    w_in = jnp.concatenate([w_in_dir(0, q - 1 - jnp.arange(q)), w_in_dir(1, jnp.arange(q))], axis=-1)

    def w_out_dir(d, exps):
        re = cl_re[d][..., exps]
        im = cl_im[d][..., exps]
        re = re.transpose(0, 2, 3, 1).reshape(SSM_GROUPS, SSM_STATE, q * SSM_GROUP)
        im = im.transpose(0, 2, 3, 1).reshape(SSM_GROUPS, SSM_STATE, q * SSM_GROUP)
        return jnp.concatenate([re, -im], axis=1)

    w_out = jnp.concatenate([w_out_dir(0, 1 + jnp.arange(q)), w_out_dir(1, q - jnp.arange(q))], axis=1)

    ms = jnp.arange(SUBLANES + 1, dtype=F32) * q
    cr = jnp.exp(a_re[..., None] * dt * ms) * jnp.cos(a_im[..., None] * dt * ms)
    ci = jnp.exp(a_re[..., None] * dt * ms) * jnp.sin(a_im[..., None] * dt * ms)

    def pat(d, m):
        ar, ai = cr[d][..., m], ci[d][..., m]
        return (jnp.concatenate([ar, ar], axis=-1).reshape(-1), jnp.concatenate([-ai, ai], axis=-1).reshape(-1))

    row = jnp.arange(SUBLANES)

    def dir_tables(d, backward):
        tabs = []
        for k in (1, 2, 4):
            valid = (row <= SUBLANES - 1 - k) if backward else (row >= k)
            tabs += [valid[:, None] * p[None, :] for p in pat(d, k)]
        pows = [pat(d, SUBLANES - 1 - i if backward else i) for i in range(SUBLANES)]
        tabs += [jnp.stack([p[0] for p in pows]), jnp.stack([p[1] for p in pows])]
        tabs += [jnp.tile(p[None, :], (SUBLANES, 1)) for p in pat(d, SUBLANES)]
        excl = (row <= SUBLANES - 2) if backward else (row >= 1)
        tabs.append(jnp.tile(excl[:, None].astype(F32), (1, SSM_GROUPS * LANES)))
        return tabs

    scan_tab = jnp.stack(dir_tables(0, False) + dir_tables(1, True), axis=0).astype(F32)
    return {"t": t_mat.astype(BF16), "w_in": w_in.astype(BF16), "w_out": w_out.astype(BF16),
            "scan": scan_tab}


def _flash_kernel(qb_ref, kb_ref, first_ref, last_ref, q_ref, k_ref, v_ref, o_ref,
                  acc_ref, *m_scratch, running_max):
    del qb_ref, kb_ref
    s_id = pl.program_id(1)

    @pl.when(first_ref[s_id] == 1)
    def _():
        acc_ref[...] = jnp.zeros(acc_ref.shape, F32)
        if running_max:
            m_scratch[0][...] = jnp.full(m_scratch[0].shape, -jnp.inf, F32)

    for hh in range(2):
        sl = slice(hh * HEAD_PAD, (hh + 1) * HEAD_PAD)
        s = lax.dot_general(q_ref[:, sl], k_ref[:, sl], (((1,), (1,)), ((), ())),
                            preferred_element_type=F32)
        if running_max:
            m_ref = m_scratch[0]
            m_prev = m_ref[hh]
            m_new = jnp.maximum(m_prev, jnp.max(s, axis=1, keepdims=True))
            p = jnp.exp2(s - m_new[:, :1]).astype(BF16)
            acc_ref[hh] = jnp.exp2(m_prev - m_new) * acc_ref[hh] + _dot(p, v_ref[:, sl])
            m_ref[hh] = m_new
        else:
            acc_ref[hh] += _dot(jnp.exp2(s).astype(BF16), v_ref[:, sl])

    @pl.when(last_ref[s_id] == 1)
    def _():
        lane = lax.broadcasted_iota(jnp.int32, o_ref.shape, 1)
        a0 = acc_ref[0]
        a1 = acc_ref[1]
        o0 = a0 / a0[:, MLA_V:MLA_V + 1]
        o1 = a1 / a1[:, 0:1]
        o_ref[...] = jnp.where(lane < MLA_V, o0, o1).astype(BF16)


def _flash_attention(q, k, v, seqs, q_tile, kv_tile, running_max):
    qb, kb, first, last = [], [], [], []
    base = 0
    for n_seq, seq_len in seqs:
        nq, nk = seq_len // q_tile, seq_len // kv_tile
        for _ in range(n_seq):
            for qi in range(nq):
                for ki in range(nk):
                    qb.append(base // q_tile + qi)
                    kb.append(base // kv_tile + ki)
                    first.append(int(ki == 0))
                    last.append(int(ki == nk - 1))
            base += seq_len
    tabs = [jnp.asarray(np.asarray(a, np.int32)) for a in (qb, kb, first, last)]
    n_steps = len(qb)
    t_tok = q.shape[0]
    pair_w = 2 * HEAD_PAD
    scratch = [pltpu.VMEM((2, q_tile, LANES), F32)] * (2 if running_max else 1)
    grid_spec = pltpu.PrefetchScalarGridSpec(
        num_scalar_prefetch=4,
        grid=(MLA_HEADS // 2, n_steps),
        in_specs=[
            pl.BlockSpec((q_tile, pair_w), lambda p, s, qb, kb, fi, la: (qb[s], p)),
            pl.BlockSpec((kv_tile, pair_w), lambda p, s, qb, kb, fi, la: (kb[s], p)),
            pl.BlockSpec((kv_tile, pair_w), lambda p, s, qb, kb, fi, la: (kb[s], p)),
        ],
        out_specs=pl.BlockSpec((q_tile, 2 * MLA_V), lambda p, s, qb, kb, fi, la: (qb[s], p)),
        scratch_shapes=scratch,
    )
    return pl.pallas_call(
        functools.partial(_flash_kernel, running_max=running_max),
        grid_spec=grid_spec,
        out_shape=jax.ShapeDtypeStruct((t_tok, MLA_HEADS * MLA_V), BF16),
        compiler_params=_cparams(("parallel", "arbitrary")),
        name="mla_flash_max" if running_max else "mla_flash",
    )(*tabs, q, k, v)


def _memkv_kernel(mem_ref, gmem_ref, w_ref, gk_ref, k_ref, v_ref):
    x = mem_ref[...]
    h = (x * _rms_scale(x, D_MODEL) * gmem_ref[...]).astype(BF16)
    kv = _dot(h, w_ref[...])
    for hh in range(MEM_HEADS):
        sl = slice(hh * MEM_HEAD_DIM, (hh + 1) * MEM_HEAD_DIM)
        blk = kv[:, sl]
        k_ref[:, sl] = (blk * _rms_scale(blk, MEM_HEAD_DIM) * gk_ref[...]).astype(BF16)
    v_ref[...] = kv[:, MEM_WIDTH:].astype(BF16)


def _memory_kv(mem, lw):
    rows = mem.shape[0]
    tm = MEM_TOKENS
    return pl.pallas_call(
        _memkv_kernel,
        grid=(rows // tm,),
        in_specs=[pl.BlockSpec((tm, D_MODEL), lambda i: (i, 0)),
                  _const_spec(lw["g_mem"].shape), _const_spec(lw["w_mem_kv"].shape),
                  _const_spec(lw["g_mem_k"].shape)],
        out_specs=[pl.BlockSpec((tm, MEM_WIDTH), lambda i: (i, 0))] * 2,
        out_shape=[jax.ShapeDtypeStruct((rows, MEM_WIDTH), BF16)] * 2,
        compiler_params=_cparams(("parallel",)),
        name="memory_kv",
    )(mem, lw["g_mem"], lw["w_mem_kv"], lw["g_mem_k"])


def _gelu_tanh(x):
    return 0.5 * x * (1.0 + jnp.tanh(0.7978845608028654 * (x + 0.044715 * x * x * x)))


def _merge_kernel(*refs, with_router, n_x, n_first):
    x_refs = refs[:n_x]
    refs = refs[n_x:]
    if with_router:
        (ys_ref, ya_ref, qm_ref, km_ref, vm_ref, sg_ref, wglu_ref, bglu_ref, wb_ref,
         wo_ref, gffn_ref, wr_hi_ref, wr_lo_ref, x1_ref, h2_ref, route_ref) = refs
    else:
        (ys_ref, ya_ref, qm_ref, km_ref, vm_ref, sg_ref, wglu_ref, bglu_ref, wb_ref,
         wo_ref, gffn_ref, x1_ref, h2_ref) = refs

    ya = _gelu_tanh(ys_ref[...]).astype(BF16)
    z = _dot(ya, wglu_ref[...]) + bglu_ref[...]
    y_ssm = (z[:, :SSM_WIDTH] * _sigmoid(z[:, SSM_WIDTH:])).astype(BF16)

    mem_parts = []
    for hh in range(MEM_HEADS):
        sl = slice(hh * MEM_HEAD_DIM, (hh + 1) * MEM_HEAD_DIM)
        s = lax.dot_general(qm_ref[:, sl], km_ref[:, sl], (((1,), (1,)), ((), ())),
                            preferred_element_type=F32)
        p = jnp.exp(s - jnp.max(s, axis=1, keepdims=True))
        p = p / jnp.sum(p, axis=1, keepdims=True)
        mem_parts.append(_dot(p.astype(BF16), vm_ref[:, sl]))
    y_mem = jnp.concatenate(mem_parts, axis=1).astype(BF16)

    merged = None
    for b, yb in enumerate((y_ssm, ya_ref[...], y_mem)):
        gate = sg_ref[:, b * D_MODEL:(b + 1) * D_MODEL].astype(F32)
        term = gate * _dot(yb, wb_ref[b])
        merged = term if merged is None else merged + term

    x1 = _stream_read(x_refs, n_first) + _dot(merged.astype(BF16), wo_ref[...])
    x1_ref[...] = x1
    h2 = x1 * _rms_scale(x1, D_MODEL) * gffn_ref[...]

    if not with_router:
        h2_ref[...] = h2.astype(BF16)
        return

    for j in range(ROW_TILES):
        h2_ref[:, j, :] = h2[:, j * LANES:(j + 1) * LANES]

    h_hi, h_lo = _split_bf16(h2)
    logits = _dot(h_hi, wr_hi_ref[...]) + _dot(h_hi, wr_lo_ref[...]) + _dot(h_lo, wr_hi_ref[...])
    lane = lax.broadcasted_iota(jnp.int32, logits.shape, 1)
    lane_f = lane.astype(F32)
    neg = jnp.float32(-jnp.inf)
    big = jnp.float32(LANES)
    logits = jnp.where(lane < N_EXPERTS, logits, neg)
    m1 = jnp.max(logits, axis=1, keepdims=True)
    i1 = jnp.min(jnp.where(logits == m1, lane_f, big), axis=1, keepdims=True)
    rest = jnp.where(lane_f == i1, neg, logits)
    m2 = jnp.max(rest, axis=1, keepdims=True)
    i2 = jnp.min(jnp.where(rest == m2, lane_f, big), axis=1, keepdims=True)
    e2 = jnp.exp(m2 - m1)
    w1 = 1.0 / (1.0 + e2)
    w2 = e2 / (1.0 + e2)
    out = jnp.where(lane == 0, i1, jnp.where(lane == 1, i2, jnp.where(lane == 2, w1, w2)))
    route_ref[...] = jnp.where(lane < 4, out, 0.0)


def _merge(x_parts, y_ssm_raw, y_mla, qm, km, vm, sg, lw, tiles, with_router):
    t_tok = y_mla.shape[0]
    tm = tiles["tm"]
    seq_map = tiles["seq_map"]

    def row(i):
        return (i, 0)

    weights = [lw["w_glu"], lw["b_glu"], lw["w_branch"], lw["w_out"], lw["g_ffn"]]
    if with_router:
        weights += [lw["w_router_hi"], lw["w_router_lo"]]
    in_specs = _stream_specs(x_parts, tm, D_MODEL) + [
        pl.BlockSpec((tm, SSM_WIDTH), row),
        pl.BlockSpec((tm, MLA_HEADS * MLA_V), row),
        pl.BlockSpec((tm, MEM_WIDTH), row),
        pl.BlockSpec((MEM_TOKENS, MEM_WIDTH), lambda i: (seq_map(i), 0)),
        pl.BlockSpec((MEM_TOKENS, MEM_WIDTH), lambda i: (seq_map(i), 0)),
        pl.BlockSpec((tm, N_BRANCH * D_MODEL), row),
    ] + [_const_spec(w.shape) for w in weights]
    out_specs = [pl.BlockSpec((tm, D_MODEL), row)]
    out_shape = [jax.ShapeDtypeStruct((t_tok, D_MODEL), F32)]
    if with_router:
        out_specs += [pl.BlockSpec((tm, ROW_TILES, LANES), lambda i: (i, 0, 0)),
                      pl.BlockSpec((tm, LANES), row)]
        out_shape += [jax.ShapeDtypeStruct((t_tok, ROW_TILES, LANES), F32),
                      jax.ShapeDtypeStruct((t_tok, LANES), F32)]
    else:
        out_specs.append(pl.BlockSpec((tm, D_MODEL), row))
        out_shape.append(jax.ShapeDtypeStruct((t_tok, D_MODEL), BF16))
    kern = functools.partial(_merge_kernel, with_router=with_router, n_x=len(x_parts),
                             n_first=x_parts[0].shape[0] // tm)
    return pl.pallas_call(
        kern,
        grid=(t_tok // tm,),
        in_specs=in_specs,
        out_specs=out_specs,
        out_shape=out_shape,
        compiler_params=_cparams(("parallel",)),
        name="merge_router" if with_router else "merge",
    )(*x_parts, y_ssm_raw, y_mla, qm, km, vm, sg, *weights)


def _swiglu_acc(x, w1_ref, w3_ref, w2_ref, acc, chunk):
    d_ff = w1_ref.shape[1]
    for c in range(d_ff // chunk):
        sl = slice(c * chunk, (c + 1) * chunk)
        g = _dot(x, w1_ref[:, sl])
        u = _dot(x, w3_ref[:, sl])
        a = (g * _sigmoid(g) * u).astype(BF16)
        acc = acc + _dot(a, w2_ref[sl, :])
    return acc


def _ffn_kernel(h_ref, x_ref, w1_ref, w3_ref, w2_ref, o_ref):
    o_ref[...] = _swiglu_acc(h_ref[...], w1_ref, w3_ref, w2_ref, x_ref[...], FFN_CHUNK)


def _dense_ffn(h2, x1, lw, tiles):
    t_tok = x1.shape[0]
    tm = tiles["ffn_tile"]

    def row(i):
        return (i, 0)

    def resident(shape):
        return pl.BlockSpec(shape, lambda i: (0, 0), pipeline_mode=pl.Buffered(1))

    return pl.pallas_call(
        _ffn_kernel,
        grid=(t_tok // tm,),
        in_specs=[pl.BlockSpec((tm, D_MODEL), row), pl.BlockSpec((tm, D_MODEL), row),
                  resident(lw["ffn_w1"].shape), resident(lw["ffn_w3"].shape),
                  resident(lw["ffn_w2"].shape)],
        out_specs=pl.BlockSpec((tm, D_MODEL), row),
        out_shape=jax.ShapeDtypeStruct((t_tok, D_MODEL), F32),
        compiler_params=_cparams(("parallel",)),
        name="dense_ffn",
    )(h2, x1, lw["ffn_w1"], lw["ffn_w3"], lw["ffn_w2"])


def _dispatch_kernel(pos_ref, h_ref, xs_in_ref, xs_ref, sem):
    del xs_in_ref
    tm = h_ref.shape[0]

    def tile_copy(r, dst):
        return pltpu.make_async_copy(h_ref.at[r], xs_ref.at[dst], sem)

    def start(r, c):
        tile_copy(r, pos_ref[0, 0, 2 * r]).start(priority=0)
        tile_copy(r, pos_ref[0, 0, 2 * r + 1]).start(priority=1)
        return c

    def wait(r, c):
        tile_copy(r, 0).wait()
        tile_copy(r, 0).wait()
        return c

    lax.fori_loop(0, tm, start, 0, unroll=DMA_UNROLL)
    lax.fori_loop(0, tm, wait, 0, unroll=DMA_UNROLL)


def _dispatch(h2t, pos3, n_rows, tm):
    t_tok = h2t.shape[0]
    xs0 = jnp.zeros((n_rows, ROW_TILES, LANES), F32)
    return pl.pallas_call(
        _dispatch_kernel,
        grid=(t_tok // tm,),
        in_specs=[pl.BlockSpec((1, 1, 2 * tm), lambda i: (i, 0, 0), memory_space=pltpu.SMEM),
                  pl.BlockSpec((tm, ROW_TILES, LANES), lambda i: (i, 0, 0)),
                  pl.BlockSpec(memory_space=pl.ANY)],
        out_specs=pl.BlockSpec(memory_space=pl.ANY),
        out_shape=jax.ShapeDtypeStruct(xs0.shape, F32),
        input_output_aliases={2: 0},
        scratch_shapes=[pltpu.SemaphoreType.DMA(())],
        compiler_params=_cparams(("arbitrary",)),
        name="moe_dispatch",
    )(pos3, h2t, xs0)


def _expert_kernel(be_ref, nu_ref, x_ref, w1_ref, w3_ref, w2_ref, y_ref):
    del be_ref
    i = pl.program_id(0)

    @pl.when(i < nu_ref[0])
    def _():
        x = jnp.concatenate([x_ref[:, j, :] for j in range(ROW_TILES)], axis=1).astype(BF16)
        acc = _swiglu_acc(x, w1_ref, w3_ref, w2_ref, jnp.zeros((x.shape[0], D_MODEL), F32), MOE_CHUNK)
        for j in range(ROW_TILES):
            y_ref[:, j, :] = acc[:, j * LANES:(j + 1) * LANES]

    @pl.when(i >= nu_ref[0])
    def _():
        y_ref[...] = jnp.zeros(y_ref.shape, F32)


def _expert_ffn(xs, blk_expert, n_used, lw, bm):
    n_rows = xs.shape[0]

    def w_spec(shape):
        return pl.BlockSpec((None,) + shape, lambda i, be, nu: (be[i], 0, 0),
                            pipeline_mode=pl.Buffered(1))

    grid_spec = pltpu.PrefetchScalarGridSpec(
        num_scalar_prefetch=2,
        grid=(n_rows // bm,),
        in_specs=[
            pl.BlockSpec((bm, ROW_TILES, LANES), lambda i, be, nu: (i, 0, 0)),
            w_spec((D_MODEL, D_FF_EXPERT)), w_spec((D_MODEL, D_FF_EXPERT)),
            w_spec((D_FF_EXPERT, D_MODEL)),
        ],
        out_specs=pl.BlockSpec((bm, ROW_TILES, LANES), lambda i, be, nu: (i, 0, 0)),
    )
    return pl.pallas_call(
        _expert_kernel,
        grid_spec=grid_spec,
        out_shape=jax.ShapeDtypeStruct(xs.shape, F32),
        compiler_params=_cparams(("arbitrary",)),
        name="moe_experts",
    )(blk_expert, n_used, xs, lw["moe_w1"], lw["moe_w3"], lw["moe_w2"])


def _combine_kernel(pos_ref, pos_next_ref, ys_hbm, x_ref, route_ref, *refs, n_first):
    out_refs, (buf_ref, sem) = refs[:-2], refs[-2:]
    tm = x_ref.shape[0]
    i = pl.program_id(0)
    n = pl.num_programs(0)
    slot = i % 2

    def tile_copy(s, k, r, src):
        return pltpu.make_async_copy(ys_hbm.at[src], buf_ref.at[s, k, r], sem.at[s])

    def issue(p_ref, s):
        def start(r, c):
            tile_copy(s, 0, r, p_ref[0, 0, 2 * r]).start(priority=0)
            tile_copy(s, 1, r, p_ref[0, 0, 2 * r + 1]).start(priority=1)
            return c
        lax.fori_loop(0, tm, start, 0, unroll=DMA_UNROLL)

    @pl.when(i == 0)
    def _():
        issue(pos_ref, slot)

    @pl.when(i + 1 < n)
    def _():
        issue(pos_next_ref, 1 - slot)

    def wait(r, c):
        tile_copy(slot, 0, r, 0).wait()
        tile_copy(slot, 1, r, 0).wait()
        return c

    lax.fori_loop(0, tm, wait, 0, unroll=DMA_UNROLL)
    w1 = route_ref[:, 2:3]
    w2 = route_ref[:, 3:4]

    def write(o_ref):
        for j in range(ROW_TILES):
            sl = slice(j * LANES, (j + 1) * LANES)
            o_ref[:, sl] = x_ref[:, sl] + w1 * buf_ref[slot, 0, :, j, :] + w2 * buf_ref[slot, 1, :, j, :]

    if len(out_refs) == 1:
        write(out_refs[0])
    else:
        pl.when(i < n_first)(lambda: write(out_refs[0]))
        pl.when(i >= n_first)(lambda: write(out_refs[1]))


def _combine(ys, pos3, x1, route, tm, out_rows):
    t_tok = x1.shape[0]
    n_tiles = t_tok // tm
    n_first = out_rows[0] // tm
    if len(out_rows) == 1:
        out_specs = [pl.BlockSpec((tm, D_MODEL), lambda i: (i, 0))]
    else:
        out_specs = [pl.BlockSpec((tm, D_MODEL), lambda i: (jnp.minimum(i, n_first - 1), 0)),
                     pl.BlockSpec((tm, D_MODEL), lambda i: (jnp.maximum(i - n_first, 0), 0))]
    pos_spec = functools.partial(pl.BlockSpec, (1, 1, 2 * tm), memory_space=pltpu.SMEM)
    return pl.pallas_call(
        functools.partial(_combine_kernel, n_first=n_first),
        grid=(n_tiles,),
        in_specs=[pos_spec(lambda i: (i, 0, 0)),
                  pos_spec(lambda i: (jnp.minimum(i + 1, n_tiles - 1), 0, 0)),
                  pl.BlockSpec(memory_space=pl.ANY),
                  pl.BlockSpec((tm, D_MODEL), lambda i: (i, 0)),
                  pl.BlockSpec((tm, LANES), lambda i: (i, 0))],
        out_specs=out_specs,
        out_shape=[jax.ShapeDtypeStruct((r, D_MODEL), F32) for r in out_rows],
        scratch_shapes=[pltpu.VMEM((2, 2, tm, ROW_TILES, LANES), F32), pltpu.SemaphoreType.DMA((2,))],
        compiler_params=_cparams(("arbitrary",)),
        name="moe_combine",
    )(pos3, pos3, ys, x1, route)


def _moe_ffn(h2t, x1, route, lw, tiles, out_rows):
    t_tok = x1.shape[0]
    tm = tiles["tm"]
    bm = tiles["moe_block"]
    n_assign = 2 * t_tok
    experts = route[:, :2].astype(jnp.int32).reshape(n_assign)
    onehot = (experts[:, None] == jnp.arange(N_EXPERTS, dtype=jnp.int32)[None, :]).astype(jnp.int32)
    csum = jnp.cumsum(onehot, axis=0)
    rank = jnp.sum((csum - 1) * onehot, axis=1)
    counts = csum[-1]
    padded = ((counts + bm - 1) // bm) * bm
    ends = jnp.cumsum(padded)
    starts = ends - padded
    pos = jnp.sum(onehot * starts[None, :], axis=1) + rank
    n_blocks = n_assign // bm + N_EXPERTS
    blk_start = jnp.arange(n_blocks, dtype=jnp.int32) * bm
    blk_expert = jnp.sum((blk_start[:, None] >= ends[None, :]).astype(jnp.int32), axis=1)
    blk_expert = jnp.minimum(blk_expert, N_EXPERTS - 1)
    n_used = (ends[-1:] // bm).astype(jnp.int32)
    pos3 = pos.astype(jnp.int32).reshape(t_tok // tm, 1, 2 * tm)

    xs = _dispatch(h2t, pos3, n_blocks * bm, tm)
    ys = _expert_ffn(xs, blk_expert, n_used, lw, bm)
    return _combine(ys, pos3, x1, route, tm, out_rows)


def _head_cols(w, per_head, n_heads, offset=0):
    k = w.shape[0]
    w = w.reshape(k, n_heads, per_head)
    w = jnp.pad(w, ((0, 0), (0, 0), (offset, HEAD_PAD - per_head - offset)))
    return w.reshape(k, n_heads * HEAD_PAD)


def _swap_rope(w):
    z = jnp.zeros_like(w[..., :MLA_NOPE])
    return jnp.concatenate([z, w[..., MLA_NOPE + HALF_ROPE:], w[..., MLA_NOPE:MLA_NOPE + HALF_ROPE]], axis=-1)


def _pad_lane_block(v, offset=0):
    return jnp.pad(v, ((0, 0),) * (v.ndim - 1) + ((offset, LANES - v.shape[-1] - offset),))


def _layer_weights(i, p, rope_cos, rope_sin):
    o = IN_OFFSETS
    w_in = p["w_in"][i]
    lw = {}
    lw["g_mix"] = p["g_mix"][i].reshape(1, D_MODEL)
    lw["w_u"] = w_in[:, :o[0]].astype(BF16)
    lw["w_cq"] = w_in[:, o[0]:o[1]].astype(BF16)
    w_kr = w_in[:, o[2]:o[3]]
    w_kr_sw = jnp.concatenate([w_kr[:, HALF_ROPE:], w_kr[:, :HALF_ROPE]], axis=1)
    lw["w_ckv"] = jnp.concatenate([w_in[:, o[1]:o[2]], _pad_lane_block(w_kr, MLA_NOPE),
                                   _pad_lane_block(w_kr_sw, MLA_NOPE)], axis=1).astype(BF16)
    lw["w_qm"] = w_in[:, o[3]:o[4]].astype(BF16)
    lw["w_gate"] = w_in[:, o[4]:].astype(BF16)
    lw["g_q_lora"] = p["g_q_lora"][i].reshape(1, Q_LORA)
    w_uq = p["w_uq"][i].reshape(Q_LORA, MLA_HEADS, MLA_QK)
    lw["w_uq"] = jnp.concatenate(
        [_head_cols(w_uq.reshape(Q_LORA, -1), MLA_QK, MLA_HEADS),
         _head_cols(_swap_rope(w_uq).reshape(Q_LORA, -1), MLA_QK, MLA_HEADS)], axis=1).astype(BF16)
    lw["g_kv_lora"] = p["g_kv_lora"][i].reshape(1, KV_LORA)
    w_ukv = p["w_ukv"][i].reshape(KV_LORA, MLA_HEADS, MLA_NOPE + MLA_V)
    lw["w_kn"] = _head_cols(w_ukv[:, :, :MLA_NOPE].reshape(KV_LORA, -1), MLA_NOPE, MLA_HEADS).astype(BF16)
    w_v = w_ukv[:, :, MLA_NOPE:].reshape(KV_LORA, MLA_HEADS // 2, 2, MLA_V)
    w_v = jnp.stack([jnp.pad(w_v[:, :, 0], ((0, 0), (0, 0), (0, HEAD_PAD - MLA_V))),
                     jnp.pad(w_v[:, :, 1], ((0, 0), (0, 0), (HEAD_PAD - MLA_V, 0)))], axis=2)
    lw["w_v"] = w_v.reshape(KV_LORA, MLA_WIDTH_PAD).astype(BF16)
    one_even = jnp.zeros((HEAD_PAD,), F32).at[MLA_V].set(1.0)
    one_odd = jnp.zeros((HEAD_PAD,), F32).at[0].set(1.0)
    lw["v_one"] = jnp.tile(jnp.concatenate([one_even, one_odd]), MLA_HEADS // 2).reshape(1, MLA_WIDTH_PAD)
    for name, g in (("q", p["g_mla_q"][i]), ("k", p["g_mla_k"][i])):
        lw["rope_a" + name] = rope_cos * _pad_lane_block(g)[None, :]
        lw["rope_b" + name] = rope_sin * _pad_lane_block(_swap_rope(g))[None, :]
    lw["g_mem_q"] = jnp.tile(p["g_mem_q"][i], MEM_HEADS).reshape(1, MEM_WIDTH) * (MEM_HEAD_DIM ** -0.5)
    lw["g_mem"] = p["g_mem"][i].reshape(1, D_MODEL)
    lw["w_mem_kv"] = p["w_mem_kv"][i].astype(BF16)
    lw["g_mem_k"] = p["g_mem_k"][i].reshape(1, MEM_HEAD_DIM)
    lw["w_glu"] = p["ssm_w_glu"][i].astype(BF16)
    lw["b_glu"] = p["ssm_b_glu"][i].reshape(1, 2 * SSM_WIDTH)
    lw["w_branch"] = p["w_branch"][i].astype(BF16)
    lw["w_out"] = p["w_out"][i].astype(BF16)
    lw["g_ffn"] = p["g_ffn"][i].reshape(1, D_MODEL)
    lw["ssm"] = _ssm_tables(p["ssm_a_re"][i], p["ssm_a_im"][i], p["ssm_log_dt"][i], p["ssm_b_re"][i],
                            p["ssm_b_im"][i], p["ssm_c_re"][i], p["ssm_c_im"][i], p["ssm_d"][i])
    lw["score_bound"] = (1.02 * MLA_QK ** 0.5 * LOG2E
                         * jnp.max(jnp.abs(p["g_mla_q"][i])) * jnp.max(jnp.abs(p["g_mla_k"][i])))
    if i % 2 == 0:
        w13 = p["ffn_w13"][i // 2]
        lw["ffn_w1"] = w13[:, :D_FF].astype(BF16)
        lw["ffn_w3"] = w13[:, D_FF:].astype(BF16)
        lw["ffn_w2"] = p["ffn_w2"][i // 2].astype(BF16)
    else:
        w13 = p["moe_w13"][i // 2]
        lw["moe_w1"] = w13[:, :, :D_FF_EXPERT].astype(BF16)
        lw["moe_w3"] = w13[:, :, D_FF_EXPERT:].astype(BF16)
        lw["moe_w2"] = p["moe_w2"][i // 2].astype(BF16)
        wr = jnp.pad(p["moe_router"][i // 2], ((0, 0), (0, LANES - N_EXPERTS)))
        lw["w_router_hi"], lw["w_router_lo"] = _split_bf16(wr)
    return lw


def _rope_tables(max_len):
    inv = 1.0 / (ROPE_BASE ** (jnp.arange(0, MLA_ROPE, 2, dtype=F32) / MLA_ROPE))
    ang = jnp.arange(max_len, dtype=F32)[:, None] * inv[None, :]
    cos, sin = jnp.cos(ang), jnp.sin(ang)
    ones = jnp.ones((max_len, MLA_NOPE), F32)
    zn = jnp.zeros((max_len, MLA_NOPE), F32)
    zp = jnp.zeros((max_len, HEAD_PAD - MLA_QK), F32)
    return (jnp.concatenate([ones, cos, cos, zp], axis=1),
            jnp.concatenate([zn, -sin, sin, zp], axis=1))


def _gcd_tile(limit, *sizes):
    t = limit
    while any(s % t for s in sizes):
        t //= 2
    return t


def kernel(x_prompt, x_sample, mem_prompt, mem_sample, g_mix, w_in, ssm_a_re, ssm_a_im, ssm_log_dt, ssm_b_re, ssm_b_im, ssm_c_re, ssm_c_im, ssm_d, ssm_w_glu, ssm_b_glu, g_q_lora, w_uq, g_kv_lora, w_ukv, g_mla_q, g_mla_k, g_mem, w_mem_kv, g_mem_q, g_mem_k, w_branch, w_out, g_ffn, ffn_w13, ffn_w2, moe_router, moe_w13, moe_w2):
    params = dict(g_mix=g_mix, w_in=w_in, ssm_a_re=ssm_a_re, ssm_a_im=ssm_a_im, ssm_log_dt=ssm_log_dt,
                  ssm_b_re=ssm_b_re, ssm_b_im=ssm_b_im, ssm_c_re=ssm_c_re, ssm_c_im=ssm_c_im, ssm_d=ssm_d,
                  ssm_w_glu=ssm_w_glu, ssm_b_glu=ssm_b_glu, g_q_lora=g_q_lora, w_uq=w_uq,
                  g_kv_lora=g_kv_lora, w_ukv=w_ukv, g_mla_q=g_mla_q, g_mla_k=g_mla_k, g_mem=g_mem,
                  w_mem_kv=w_mem_kv, g_mem_q=g_mem_q, g_mem_k=g_mem_k, w_branch=w_branch, w_out=w_out,
                  g_ffn=g_ffn, ffn_w13=ffn_w13, ffn_w2=ffn_w2, moe_router=moe_router, moe_w13=moe_w13,
                  moe_w2=moe_w2)
    depth = g_mix.shape[0]
    bp, lp, _ = x_prompt.shape
    bs, ls, _ = x_sample.shape
    tp, ts = bp * lp, bs * ls
    t_tok = tp + ts

    tm = _gcd_tile(TOKEN_TILE, lp, ls)
    attn_q_tile = _gcd_tile(ATTN_Q_TILE, lp, ls)
    attn_kv_tile = _gcd_tile(ATTN_KV_TILE, lp, ls)
    ffn_tile = _gcd_tile(FFN_TILE, lp, ls)
    npt = tp // tm
    lp_t, ls_t = lp // tm, ls // tm

    def pos_map(i):
        return jnp.where(i < npt, i % lp_t, (i - npt) % ls_t)

    def seq_map(i):
        return jnp.where(i < npt, i // lp_t, bp + (i - npt) // ls_t)

    rope_cos, rope_sin = _rope_tables(max(lp, ls))
    head_ind = (jnp.arange(MLA_WIDTH_PAD)[:, None] // HEAD_PAD == jnp.arange(LANES)[None, :])
    mem_bd = (jnp.arange(MEM_WIDTH)[:, None] // MEM_HEAD_DIM == jnp.arange(MEM_WIDTH)[None, :] // MEM_HEAD_DIM)
    tiles = dict(tm=tm, ffn_tile=ffn_tile, pos_map=pos_map, seq_map=seq_map,
                 ones=jnp.ones((D_MODEL, LANES), BF16),
                 head_ind=head_ind.astype(BF16), head_ind_t=head_ind.T.astype(BF16),
                 mem_bd=mem_bd.astype(BF16),
                 moe_block=_gcd_tile(MOE_BLOCK, 2 * t_tok))

    x_parts = [x_prompt.reshape(tp, D_MODEL), x_sample.reshape(ts, D_MODEL)]
    mem = jnp.concatenate([mem_prompt.reshape(bp * MEM_TOKENS, D_MODEL),
                           mem_sample.reshape(bs * MEM_TOKENS, D_MODEL)], axis=0)
    seqs = [(bp, lp), (bs, ls)]

    for i in range(depth):
        lw = _layer_weights(i, params, rope_cos, rope_sin)
        u, q, k, v, qm, sg = _in_projection(x_parts, lw, tiles)

        y_ssm_raw = _ssm_call(u, lw["ssm"], seqs)

        y_mla = lax.cond(
            lw["score_bound"] <= MAX_SAFE_LOG2_SCORE,
            lambda q_, k_, v_: _flash_attention(q_, k_, v_, seqs, attn_q_tile, attn_kv_tile, running_max=False),
            lambda q_, k_, v_: _flash_attention(q_, k_, v_, seqs, attn_q_tile, attn_kv_tile, running_max=True),
            q, k, v)
        km, vm = _memory_kv(mem, lw)

        with_router = i % 2 == 1
        outs = _merge(x_parts, y_ssm_raw, y_mla, qm, km, vm, sg, lw, tiles, with_router)
        if with_router:
            x1, h2t, route = outs
            out_rows = (tp, ts) if i == depth - 1 else (t_tok,)
            x_parts = _moe_ffn(h2t, x1, route, lw, tiles, out_rows)
        else:
            x1, h2 = outs
            x_parts = [_dense_ffn(h2, x1, lw, tiles)]

    if len(x_parts) == 1:
        x_parts = [x_parts[0][:tp], x_parts[0][tp:]]
    return (x_parts[0].reshape(bp, lp, D_MODEL), x_parts[1].reshape(bs, ls, D_MODEL))
```

```python
import functools
import math

import numpy as np
import jax
import jax.numpy as jnp
from jax import lax
from jax.experimental import pallas as pl
from jax.experimental.pallas import tpu as pltpu

F32 = jnp.float32
BF16 = jnp.bfloat16
EPS = 1e-6
LOG2E = math.log2(math.e)

D_MODEL = 1024
SSM_WIDTH = 512
SSM_GROUPS = 32
SSM_GROUP = 16
SSM_STATE = 64
MLA_HEADS = 8
MLA_NOPE = 64
MLA_ROPE = 32
MLA_V = 64
MLA_QK = MLA_NOPE + MLA_ROPE
Q_LORA = 256
KV_LORA = 128
ROPE_BASE = 10000.0
MEM_TOKENS = 256
MEM_HEADS = 4
MEM_HEAD_DIM = 128
MEM_WIDTH = 512
N_BRANCH = 3
D_FF = 2816
N_EXPERTS = 8
D_FF_EXPERT = 3584
IN_SIZES = (SSM_WIDTH, Q_LORA, KV_LORA, MLA_ROPE, MEM_WIDTH, N_BRANCH * D_MODEL)
IN_OFFSETS = tuple(int(v) for v in np.cumsum(IN_SIZES)[:-1])

LANES = 128
SUBLANES = 8
HEAD_PAD = 128
HALF_ROPE = MLA_ROPE // 2
MLA_WIDTH_PAD = MLA_HEADS * HEAD_PAD
ROW_TILES = D_MODEL // LANES
VMEM_LIMIT = 56 * 1024 * 1024
MAX_SAFE_LOG2_SCORE = 100.0

TOKEN_TILE = 512
ATTN_Q_TILE = 2048
ATTN_KV_TILE = 1024
FFN_TILE = 1024
SSM_CHUNK = 16
MOE_BLOCK = 512
FFN_CHUNK = 256
MOE_CHUNK = 512
DMA_UNROLL = 8


def _cparams(sem):
    return pltpu.CompilerParams(dimension_semantics=sem, vmem_limit_bytes=VMEM_LIMIT)


def _const_spec(shape):
    nd = len(shape)
    return pl.BlockSpec(shape, lambda *_: (0,) * nd)


def _sigmoid(x):
    return 1.0 / (1.0 + jnp.exp(-x))


def _rms_scale(x, n):
    return lax.rsqrt(jnp.sum(x * x, axis=-1, keepdims=True) * (1.0 / n) + EPS)


def _split_bf16(x):
    hi = x.astype(BF16)
    lo = (x - hi.astype(F32)).astype(BF16)
    return hi, lo


def _dot(a, b):
    return jnp.dot(a, b, preferred_element_type=F32)


def _tile_transpose(vs):
    sub = lax.broadcasted_iota(jnp.int32, vs[0].shape, 1)
    for dist in (4, 2, 1):
        low = (sub // dist) % 2 == 0
        new = list(vs)
        for a in range(SUBLANES):
            if a & dist == 0:
                b = a | dist
                new[a] = jnp.where(low, vs[a], pltpu.roll(vs[b], dist, 1))
                new[b] = jnp.where(low, pltpu.roll(vs[a], SUBLANES - dist, 1), vs[b])
        vs = new
    return vs


def _rows_to_token_tiles(x):
    rows = x.shape[0]
    cols = [x[:, j * LANES:(j + 1) * LANES].reshape(rows // SUBLANES, SUBLANES, LANES) for j in range(ROW_TILES)]
    return jnp.stack(_tile_transpose(cols), axis=1).reshape(rows, ROW_TILES, LANES)


def _token_tiles_to_rows(t):
    rows = t.shape[0]
    t4 = t.reshape(rows // SUBLANES, SUBLANES, ROW_TILES, LANES)
    cols = _tile_transpose([t4[:, k] for k in range(SUBLANES)])
    return [c.reshape(rows, LANES) for c in cols]


def _stream_specs(parts, tm, width):
    if len(parts) == 1:
        return [pl.BlockSpec((tm, width), lambda i: (i, 0))]
    n0 = parts[0].shape[0] // tm
    return [pl.BlockSpec((tm, width), lambda i: (jnp.minimum(i, n0 - 1), 0)),
            pl.BlockSpec((tm, width), lambda i: (jnp.maximum(i - n0, 0), 0))]


def _stream_read(refs, n_first):
    if len(refs) == 1:
        return refs[0][...]
    return jnp.where(pl.program_id(0) < n_first, refs[0][...], refs[1][...])


def _inproj_kernel(*refs, n_x, n_first):
    x_refs = refs[:n_x]
    (gmix_ref, wu_ref, wg_ref, wqm_ref, wcq_ref, wckv_ref,
     gql_ref, wuq_ref, gkvl_ref, wkn_ref, wv_ref, vone_ref, gmq_ref,
     ones_ref, ind_ref, indt_ref, bd_ref,
     aq_ref, bq_ref, ak_ref, bk_ref,
     u_ref, q_ref, k_ref, v_ref, qm_ref, sg_ref) = refs[n_x:]

    def row_ssq(val):
        n = val.shape[1]
        return _dot((val * val).astype(BF16), ones_ref[:n, :])

    def head_scale(val, scale):
        ss = _dot((val * val).astype(BF16), ind_ref[...])
        r = lax.rsqrt(ss * (1.0 / MLA_QK) + EPS) * scale
        r_hi, r_lo = _split_bf16(r)
        return _dot(r_hi, indt_ref[...]) + _dot(r_lo, indt_ref[...])

    x = _stream_read(x_refs, n_first)
    rs = lax.rsqrt(row_ssq(x) * (1.0 / D_MODEL) + EPS)
    h = (x * jnp.tile(rs, (1, D_MODEL // LANES)) * gmix_ref[...]).astype(BF16)

    u_ref[...] = _dot(h, wu_ref[...])

    for b in range(N_BRANCH):
        sl = slice(b * D_MODEL, (b + 1) * D_MODEL)
        sg_ref[:, sl] = _sigmoid(_dot(h, wg_ref[:, sl])).astype(BF16)

    zq = _dot(h, wqm_ref[...])
    zr = lax.rsqrt(_dot((zq * zq).astype(BF16), bd_ref[...]) * (1.0 / MEM_HEAD_DIM) + EPS)
    qm_ref[...] = (zq * zr * gmq_ref[...]).astype(BF16)

    cq = _dot(h, wcq_ref[...])
    cq_rs = lax.rsqrt(row_ssq(cq) * (1.0 / Q_LORA) + EPS)
    cqn = (cq * jnp.tile(cq_rs, (1, Q_LORA // LANES)) * gql_ref[...]).astype(BF16)
    qq = _dot(cqn, wuq_ref[...])
    qf = qq[:, :MLA_WIDTH_PAD]
    qsw = qq[:, MLA_WIDTH_PAD:]

    z3 = _dot(h, wckv_ref[...])
    ckv = z3[:, :LANES]
    kr = z3[:, LANES:2 * LANES]
    kr_sw = z3[:, 2 * LANES:]
    ckv_rs = lax.rsqrt(row_ssq(ckv) * (1.0 / KV_LORA) + EPS)
    ckvn = (ckv * ckv_rs * gkvl_ref[...]).astype(BF16)
    kf = _dot(ckvn, wkn_ref[...]) + jnp.tile(kr, (1, MLA_HEADS))
    v_ref[...] = (_dot(ckvn, wv_ref[...]) + vone_ref[...]).astype(BF16)

    q_rs = head_scale(qf, MLA_QK ** -0.5 * LOG2E)
    k_rs = head_scale(kf, 1.0)
    aq = aq_ref[...]
    bq = bq_ref[...]
    ak = ak_ref[...]
    k_rot = kr_sw * bk_ref[...]
    for hh in range(MLA_HEADS):
        sl = slice(hh * HEAD_PAD, (hh + 1) * HEAD_PAD)
        q_ref[:, sl] = ((qf[:, sl] * aq + qsw[:, sl] * bq) * q_rs[:, sl]).astype(BF16)
        k_ref[:, sl] = ((kf[:, sl] * ak + k_rot) * k_rs[:, sl]).astype(BF16)


def _in_projection(x_parts, lw, tiles):
    t_tok = sum(p.shape[0] for p in x_parts)
    tm = tiles["tm"]
    pos_map = tiles["pos_map"]

    def row(i):
        return (i, 0)

    weights = [lw["g_mix"], lw["w_u"], lw["w_gate"], lw["w_qm"], lw["w_cq"], lw["w_ckv"],
               lw["g_q_lora"], lw["w_uq"], lw["g_kv_lora"], lw["w_kn"], lw["w_v"], lw["v_one"],
               lw["g_mem_q"], tiles["ones"], tiles["head_ind"], tiles["head_ind_t"], tiles["mem_bd"]]
    rope = [lw["rope_aq"], lw["rope_bq"], lw["rope_ak"], lw["rope_bk"]]
    in_specs = _stream_specs(x_parts, tm, D_MODEL) + [_const_spec(w.shape) for w in weights]
    in_specs += [pl.BlockSpec((tm, HEAD_PAD), lambda i: (pos_map(i), 0))] * len(rope)
    kern = functools.partial(_inproj_kernel, n_x=len(x_parts), n_first=x_parts[0].shape[0] // tm)
    out_widths = (SSM_WIDTH, MLA_WIDTH_PAD, MLA_WIDTH_PAD, MLA_WIDTH_PAD, MEM_WIDTH, N_BRANCH * D_MODEL)
    return pl.pallas_call(
        kern,
        grid=(t_tok // tm,),
        in_specs=in_specs,
        out_specs=[pl.BlockSpec((tm, w), row) for w in out_widths],
        out_shape=[jax.ShapeDtypeStruct((t_tok, w), F32 if j == 0 else BF16) for j, w in enumerate(out_widths)],
        compiler_params=_cparams(("parallel",)),
        name="in_projection",
    )(*x_parts, *weights, *rope)


SSM_LANE_GROUPS = LANES // SSM_GROUP
SSM_TAB_DIR = 11
(_TAB_STEP, _TAB_POW, _TAB_A8, _TAB_MASK) = (0, 6, 8, 10)


def _granule_transpose(vs, lane):
    for dist in (4, 2, 1):
        shift = dist * SSM_GROUP
        low = (lane // shift) % 2 == 0
        new = list(vs)
        for a in range(SSM_LANE_GROUPS):
            if a & dist == 0:
                b = a | dist
                new[a] = jnp.where(low, vs[a], pltpu.roll(vs[b], shift, 1))
                new[b] = jnp.where(low, pltpu.roll(vs[a], LANES - shift, 1), vs[b])
        vs = new
    return vs


def _chunk_scan(x, xs, tab_ref, base, keep, backward):
    nv, _, nl = x.shape
    for j, k in enumerate((1, 2, 4)):
        shift = SUBLANES - k if backward else k
        rx = pltpu.roll(x, shift, 1)
        rxs = pltpu.roll(xs, shift, 1)
        a1 = tab_ref[base + _TAB_STEP + 2 * j][None]
        a2 = tab_ref[base + _TAB_STEP + 2 * j + 1][None]
        x, xs = x + a1 * rx + a2 * rxs, xs + a1 * rxs - a2 * rx
    a1 = tab_ref[base + _TAB_A8]
    a2 = tab_ref[base + _TAB_A8 + 1]
    edge = 0 if backward else SUBLANES - 1
    c = jnp.zeros((SUBLANES, nl), F32)
    cs = c
    cb = [None] * nv
    csb = [None] * nv
    for v in (range(nv - 1, -1, -1) if backward else range(nv)):
        kp = keep(v)
        c = c * kp
        cs = cs * kp
        cb[v] = c
        csb[v] = cs
        ex = jnp.broadcast_to(x[v, edge:edge + 1, :], (SUBLANES, nl))
        exs = jnp.broadcast_to(xs[v, edge:edge + 1, :], (SUBLANES, nl))
        c, cs = ex + a1 * c + a2 * cs, exs + a1 * cs - a2 * c
    xe = pltpu.roll(x, SUBLANES - 1 if backward else 1, 1) * tab_ref[base + _TAB_MASK][None]
    return (xe + tab_ref[base + _TAB_POW][None] * jnp.stack(cb)
            + tab_ref[base + _TAB_POW + 1][None] * jnp.stack(csb))


def _ssm_kernel(keepf_ref, keepb_ref, u_ref, t_ref, win_ref, wout_ref, tab_ref, y_ref):
    blk = pl.program_id(1)
    rows = u_ref.shape[0] // SSM_CHUNK
    nv = rows // SUBLANES
    ng = SSM_LANE_GROUPS
    lane = lax.broadcasted_iota(jnp.int32, (rows, LANES), 1)

    pieces = [u_ref[pl.ds(t, rows, stride=SSM_CHUNK), :] for t in range(SSM_CHUNK)]
    halves = [_granule_transpose(pieces[ng * h:ng * (h + 1)], lane) for h in range(SSM_CHUNK // ng)]

    ys = []
    parts = [[], [], [], []]
    for g in range(ng):
        xg = jnp.concatenate([h[g] for h in halves], axis=1).astype(BF16)
        ys.append(_dot(xg, t_ref[g]))
        s4 = _dot(xg, win_ref[g])
        for j in range(4):
            parts[j].append(s4[:, j * LANES:(j + 1) * LANES])
    f, fs, b, bs = [jnp.concatenate(p, axis=1).reshape(nv, SUBLANES, ng * LANES) for p in parts]

    xin_f = _chunk_scan(f, fs, tab_ref, 0, lambda v: keepf_ref[blk * nv + v].astype(F32), False)
    xin_b = _chunk_scan(b, bs, tab_ref, SSM_TAB_DIR, lambda v: keepb_ref[blk * nv + v].astype(F32), True)
    xin_f = xin_f.reshape(rows, ng * LANES)
    xin_b = xin_b.reshape(rows, ng * LANES)

    outs = []
    for g in range(ng):
        sl = slice(g * LANES, (g + 1) * LANES)
        xs = jnp.concatenate([xin_f[:, sl], xin_b[:, sl]], axis=1).astype(BF16)
        outs.append(ys[g] + _dot(xs, wout_ref[g]))
    for h in range(SSM_CHUNK // ng):
        back = _granule_transpose([o[:, h * LANES:(h + 1) * LANES] for o in outs], lane)
        for tt in range(ng):
            y_ref[pl.ds(ng * h + tt, rows, stride=SSM_CHUNK), :] = back[tt]


def _ssm_call(u, tabs, seqs):
    t_tok = u.shape[0]
    block = int(np.lcm.reduce([ln for _, ln in seqs]))
    vreg_tokens = SUBLANES * SSM_CHUNK
    assert all((n * ln) % block == 0 and ln % vreg_tokens == 0 for n, ln in seqs), seqs
    nv = block // vreg_tokens
    keep_f, keep_b = [], []
    for n_seq, seq_len in seqs:
        for _ in range(n_seq * seq_len // block):
            for v in range(nv):
                keep_f.append(int((v * vreg_tokens) % seq_len != 0))
                keep_b.append(int(((v + 1) * vreg_tokens) % seq_len != 0))
    keep = [jnp.asarray(np.asarray(a, np.int32)) for a in (keep_f, keep_b)]
    ng = SSM_LANE_GROUPS
    width = SSM_CHUNK * SSM_GROUP
    n_tab = 2 * SSM_TAB_DIR
    grid_spec = pltpu.PrefetchScalarGridSpec(
        num_scalar_prefetch=2,
        grid=(SSM_WIDTH // LANES, t_tok // block),
        in_specs=[
            pl.BlockSpec((block, LANES), lambda g, b, kf, kb: (b, g)),
            pl.BlockSpec((ng, width, width), lambda g, b, kf, kb: (g, 0, 0)),
            pl.BlockSpec((ng, width, 4 * LANES), lambda g, b, kf, kb: (g, 0, 0)),
            pl.BlockSpec((ng, 2 * LANES, width), lambda g, b, kf, kb: (g, 0, 0)),
            pl.BlockSpec((n_tab, SUBLANES, ng * LANES), lambda g, b, kf, kb: (0, 0, g)),
        ],
        out_specs=pl.BlockSpec((block, LANES), lambda g, b, kf, kb: (b, g)),
    )
    return pl.pallas_call(
        _ssm_kernel,
        grid_spec=grid_spec,
        out_shape=jax.ShapeDtypeStruct(u.shape, F32),
        compiler_params=_cparams(("parallel", "parallel")),
        name="ssm_chunked",
    )(*keep, u, tabs["t"], tabs["w_in"], tabs["w_out"], tabs["scan"])


def _ssm_tables(a_re, a_im, log_dt, b_re, b_im, c_re, c_im, d_skip):
    q = SSM_CHUNK
    hp = lax.Precision.HIGHEST
    dt = jnp.exp(log_dt)[..., None, None]
    ks = jnp.arange(q + 1, dtype=F32)
    mag = jnp.exp(a_re[..., None] * dt * ks)
    ang = a_im[..., None] * dt * ks
    pw_re = mag * jnp.cos(ang)
    pw_im = mag * jnp.sin(ang)
    lb_re, lb_im = pw_re[..., 1], pw_im[..., 1]
    den = a_re * a_re + a_im * a_im
    f_re = ((lb_re - 1.0) * a_re + lb_im * a_im) / den
    f_im = (lb_im * a_re - (lb_re - 1.0) * a_im) / den
    bb_re = f_re[..., None] * b_re - f_im[..., None] * b_im
    bb_im = f_re[..., None] * b_im + f_im[..., None] * b_re

    cl_re = c_re[..., None] * pw_re[:, :, None] - c_im[..., None] * pw_im[:, :, None]
    cl_im = c_re[..., None] * pw_im[:, :, None] + c_im[..., None] * pw_re[:, :, None]

    kern = (jnp.einsum("dgnpk,dgpm->dgknm", cl_re[..., :q], bb_re, precision=hp)
            - jnp.einsum("dgnpk,dgpm->dgknm", cl_im[..., :q], bb_im, precision=hp))
    s_idx = jnp.arange(q)[:, None]
    t_idx = jnp.arange(q)[None, :]
    lag_f = jnp.clip(t_idx - s_idx, 0, q - 1)
    lag_b = jnp.clip(s_idx - t_idx, 0, q - 1)
    tf = kern[0][:, lag_f] * (t_idx >= s_idx)[None, :, :, None, None]
    tb = kern[1][:, lag_b] * (s_idx >= t_idx)[None, :, :, None, None]
    t_mat = (tf + tb).transpose(0, 1, 4, 2, 3)
    t_mat = t_mat.reshape(SSM_GROUPS, q * SSM_GROUP, q * SSM_GROUP)
    d_rep = jnp.tile(d_skip.reshape(SSM_GROUPS, 1, SSM_GROUP), (1, q, 1)).reshape(SSM_GROUPS, -1)
    t_mat = t_mat + d_rep[:, :, None] * jnp.eye(q * SSM_GROUP, dtype=F32)[None]

    def w_in_dir(d, exps):
        pr = pw_re[d][:, :, exps]
        pi = pw_im[d][:, :, exps]
        re = pr[..., None] * bb_re[d][:, :, None, :] - pi[..., None] * bb_im[d][:, :, None, :]
        im = pr[..., None] * bb_im[d][:, :, None, :] + pi[..., None] * bb_re[d][:, :, None, :]
        re = re.transpose(0, 2, 3, 1).reshape(SSM_GROUPS, q * SSM_GROUP, SSM_STATE)
        im = im.transpose(0, 2, 3, 1).reshape(SSM_GROUPS, q * SSM_GROUP, SSM_STATE)
        return jnp.concatenate([re, im, im, re], axis=-1)

    w_in = jnp.concatenate([w_in_dir(0, q - 1 - jnp.arange(q)), w_in_dir(1, jnp.arange(q))], axis=-1)

    def w_out_dir(d, exps):
        re = cl_re[d][..., exps]
        im = cl_im[d][..., exps]
        re = re.transpose(0, 2, 3, 1).reshape(SSM_GROUPS, SSM_STATE, q * SSM_GROUP)
        im = im.transpose(0, 2, 3, 1).reshape(SSM_GROUPS, SSM_STATE, q * SSM_GROUP)
        return jnp.concatenate([re, -im], axis=1)

    w_out = jnp.concatenate([w_out_dir(0, 1 + jnp.arange(q)), w_out_dir(1, q - jnp.arange(q))], axis=1)

    ms = jnp.arange(SUBLANES + 1, dtype=F32) * q
    cr = jnp.exp(a_re[..., None] * dt * ms) * jnp.cos(a_im[..., None] * dt * ms)
    ci = jnp.exp(a_re[..., None] * dt * ms) * jnp.sin(a_im[..., None] * dt * ms)

    def pat(d, m):
        ar, ai = cr[d][..., m], ci[d][..., m]
        return (jnp.concatenate([ar, ar], axis=-1).reshape(-1), jnp.concatenate([-ai, ai], axis=-1).reshape(-1))

    row = jnp.arange(SUBLANES)

    def dir_tables(d, backward):
        tabs = []
        for k in (1, 2, 4):
            valid = (row <= SUBLANES - 1 - k) if backward else (row >= k)
            tabs += [valid[:, None] * p[None, :] for p in pat(d, k)]
        pows = [pat(d, SUBLANES - 1 - i if backward else i) for i in range(SUBLANES)]
        tabs += [jnp.stack([p[0] for p in pows]), jnp.stack([p[1] for p in pows])]
        tabs += [jnp.tile(p[None, :], (SUBLANES, 1)) for p in pat(d, SUBLANES)]
        excl = (row <= SUBLANES - 2) if backward else (row >= 1)
        tabs.append(jnp.tile(excl[:, None].astype(F32), (1, SSM_GROUPS * LANES)))
        return tabs

    scan_tab = jnp.stack(dir_tables(0, False) + dir_tables(1, True), axis=0).astype(F32)
    return {"t": t_mat.astype(BF16), "w_in": w_in.astype(BF16), "w_out": w_out.astype(BF16),
            "scan": scan_tab}


def _flash_kernel(qb_ref, kb_ref, first_ref, last_ref, q_ref, k_ref, v_ref, o_ref,
                  acc_ref, *m_scratch, running_max):
    del qb_ref, kb_ref
    s_id = pl.program_id(1)

    @pl.when(first_ref[s_id] == 1)
    def _():
        acc_ref[...] = jnp.zeros(acc_ref.shape, F32)
        if running_max:
            m_scratch[0][...] = jnp.full(m_scratch[0].shape, -jnp.inf, F32)

    for hh in range(2):
        sl = slice(hh * HEAD_PAD, (hh + 1) * HEAD_PAD)
        s = lax.dot_general(q_ref[:, sl], k_ref[:, sl], (((1,), (1,)), ((), ())),
                            preferred_element_type=F32)
        if running_max:
            m_ref = m_scratch[0]
            m_prev = m_ref[hh]
            m_new = jnp.maximum(m_prev, jnp.max(s, axis=1, keepdims=True))
            p = jnp.exp2(s - m_new[:, :1]).astype(BF16)
            acc_ref[hh] = jnp.exp2(m_prev - m_new) * acc_ref[hh] + _dot(p, v_ref[:, sl])
            m_ref[hh] = m_new
        else:
            acc_ref[hh] += _dot(jnp.exp2(s).astype(BF16), v_ref[:, sl])

    @pl.when(last_ref[s_id] == 1)
    def _():
        lane = lax.broadcasted_iota(jnp.int32, o_ref.shape, 1)
        a0 = acc_ref[0]
        a1 = acc_ref[1]
        o0 = a0 / a0[:, MLA_V:MLA_V + 1]
        o1 = a1 / a1[:, 0:1]
        o_ref[...] = jnp.where(lane < MLA_V, o0, o1).astype(BF16)


def _flash_attention(q, k, v, seqs, q_tile, kv_tile, running_max):
    qb, kb, first, last = [], [], [], []
    base = 0
    for n_seq, seq_len in seqs:
        nq, nk = seq_len // q_tile, seq_len // kv_tile
        for _ in range(n_seq):
            for qi in range(nq):
                for ki in range(nk):
                    qb.append(base // q_tile + qi)
                    kb.append(base // kv_tile + ki)
                    first.append(int(ki == 0))
                    last.append(int(ki == nk - 1))
            base += seq_len
    tabs = [jnp.asarray(np.asarray(a, np.int32)) for a in (qb, kb, first, last)]
    n_steps = len(qb)
    t_tok = q.shape[0]
    pair_w = 2 * HEAD_PAD
    scratch = [pltpu.VMEM((2, q_tile, LANES), F32)] * (2 if running_max else 1)
    grid_spec = pltpu.PrefetchScalarGridSpec(
        num_scalar_prefetch=4,
        grid=(MLA_HEADS // 2, n_steps),
        in_specs=[
            pl.BlockSpec((q_tile, pair_w), lambda p, s, qb, kb, fi, la: (qb[s], p)),
            pl.BlockSpec((kv_tile, pair_w), lambda p, s, qb, kb, fi, la: (kb[s], p)),
            pl.BlockSpec((kv_tile, pair_w), lambda p, s, qb, kb, fi, la: (kb[s], p)),
        ],
        out_specs=pl.BlockSpec((q_tile, 2 * MLA_V), lambda p, s, qb, kb, fi, la: (qb[s], p)),
        scratch_shapes=scratch,
    )
    return pl.pallas_call(
        functools.partial(_flash_kernel, running_max=running_max),
        grid_spec=grid_spec,
        out_shape=jax.ShapeDtypeStruct((t_tok, MLA_HEADS * MLA_V), BF16),
        compiler_params=_cparams(("parallel", "arbitrary")),
        name="mla_flash_max" if running_max else "mla_flash",
    )(*tabs, q, k, v)


def _memkv_kernel(mem_ref, gmem_ref, w_ref, gk_ref, k_ref, v_ref):
    x = mem_ref[...]
    h = (x * _rms_scale(x, D_MODEL) * gmem_ref[...]).astype(BF16)
    kv = _dot(h, w_ref[...])
    for hh in range(MEM_HEADS):
        sl = slice(hh * MEM_HEAD_DIM, (hh + 1) * MEM_HEAD_DIM)
        blk = kv[:, sl]
        k_ref[:, sl] = (blk * _rms_scale(blk, MEM_HEAD_DIM) * gk_ref[...]).astype(BF16)
    v_ref[...] = kv[:, MEM_WIDTH:].astype(BF16)


def _memory_kv(mem, lw):
    rows = mem.shape[0]
    tm = MEM_TOKENS
    return pl.pallas_call(
        _memkv_kernel,
        grid=(rows // tm,),
        in_specs=[pl.BlockSpec((tm, D_MODEL), lambda i: (i, 0)),
                  _const_spec(lw["g_mem"].shape), _const_spec(lw["w_mem_kv"].shape),
                  _const_spec(lw["g_mem_k"].shape)],
        out_specs=[pl.BlockSpec((tm, MEM_WIDTH), lambda i: (i, 0))] * 2,
        out_shape=[jax.ShapeDtypeStruct((rows, MEM_WIDTH), BF16)] * 2,
        compiler_params=_cparams(("parallel",)),
        name="memory_kv",
    )(mem, lw["g_mem"], lw["w_mem_kv"], lw["g_mem_k"])


def _gelu_tanh(x):
    return 0.5 * x * (1.0 + jnp.tanh(0.7978845608028654 * (x + 0.044715 * x * x * x)))


def _merge_kernel(*refs, with_router, n_x, n_first):
    x_refs = refs[:n_x]
    refs = refs[n_x:]
    if with_router:
        (ys_ref, ya_ref, qm_ref, km_ref, vm_ref, sg_ref, wglu_ref, bglu_ref, wb_ref,
         wo_ref, gffn_ref, wr_hi_ref, wr_lo_ref, x1_ref, h2_ref, route_ref) = refs
    else:
        (ys_ref, ya_ref, qm_ref, km_ref, vm_ref, sg_ref, wglu_ref, bglu_ref, wb_ref,
         wo_ref, gffn_ref, x1_ref, h2_ref) = refs

    ya = _gelu_tanh(ys_ref[...]).astype(BF16)
    z = _dot(ya, wglu_ref[...]) + bglu_ref[...]
    y_ssm = (z[:, :SSM_WIDTH] * _sigmoid(z[:, SSM_WIDTH:])).astype(BF16)

    mem_parts = []
    for hh in range(MEM_HEADS):
        sl = slice(hh * MEM_HEAD_DIM, (hh + 1) * MEM_HEAD_DIM)
        s = lax.dot_general(qm_ref[:, sl], km_ref[:, sl], (((1,), (1,)), ((), ())),
                            preferred_element_type=F32)
        p = jnp.exp(s - jnp.max(s, axis=1, keepdims=True))
        p = p / jnp.sum(p, axis=1, keepdims=True)
        mem_parts.append(_dot(p.astype(BF16), vm_ref[:, sl]))
    y_mem = jnp.concatenate(mem_parts, axis=1).astype(BF16)

    merged = None
    for b, yb in enumerate((y_ssm, ya_ref[...], y_mem)):
        gate = sg_ref[:, b * D_MODEL:(b + 1) * D_MODEL].astype(F32)
        term = gate * _dot(yb, wb_ref[b])
        merged = term if merged is None else merged + term

    x1 = _stream_read(x_refs, n_first) + _dot(merged.astype(BF16), wo_ref[...])
    x1_ref[...] = x1
    h2 = x1 * _rms_scale(x1, D_MODEL) * gffn_ref[...]

    if not with_router:
        h2_ref[...] = h2.astype(BF16)
        return

    h2_ref[...] = _rows_to_token_tiles(h2)

    h_hi, h_lo = _split_bf16(h2)
    logits = _dot(h_hi, wr_hi_ref[...]) + _dot(h_hi, wr_lo_ref[...]) + _dot(h_lo, wr_hi_ref[...])
    lane = lax.broadcasted_iota(jnp.int32, logits.shape, 1)
    lane_f = lane.astype(F32)
    neg = jnp.float32(-jnp.inf)
    big = jnp.float32(LANES)
    logits = jnp.where(lane < N_EXPERTS, logits, neg)
    m1 = jnp.max(logits, axis=1, keepdims=True)
    i1 = jnp.min(jnp.where(logits == m1, lane_f, big), axis=1, keepdims=True)
    rest = jnp.where(lane_f == i1, neg, logits)
    m2 = jnp.max(rest, axis=1, keepdims=True)
    i2 = jnp.min(jnp.where(rest == m2, lane_f, big), axis=1, keepdims=True)
    e2 = jnp.exp(m2 - m1)
    w1 = 1.0 / (1.0 + e2)
    w2 = e2 / (1.0 + e2)
    out = jnp.where(lane == 0, i1, jnp.where(lane == 1, i2, jnp.where(lane == 2, w1, w2)))
    route_ref[...] = jnp.where(lane < 4, out, 0.0)


def _merge(x_parts, y_ssm_raw, y_mla, qm, km, vm, sg, lw, tiles, with_router):
    t_tok = y_mla.shape[0]
    tm = tiles["tm"]
    seq_map = tiles["seq_map"]

    def row(i):
        return (i, 0)

    weights = [lw["w_glu"], lw["b_glu"], lw["w_branch"], lw["w_out"], lw["g_ffn"]]
    if with_router:
        weights += [lw["w_router_hi"], lw["w_router_lo"]]
    in_specs = _stream_specs(x_parts, tm, D_MODEL) + [
        pl.BlockSpec((tm, SSM_WIDTH), row),
        pl.BlockSpec((tm, MLA_HEADS * MLA_V), row),
        pl.BlockSpec((tm, MEM_WIDTH), row),
        pl.BlockSpec((MEM_TOKENS, MEM_WIDTH), lambda i: (seq_map(i), 0)),
        pl.BlockSpec((MEM_TOKENS, MEM_WIDTH), lambda i: (seq_map(i), 0)),
        pl.BlockSpec((tm, N_BRANCH * D_MODEL), row),
    ] + [_const_spec(w.shape) for w in weights]
    out_specs = [pl.BlockSpec((tm, D_MODEL), row)]
    out_shape = [jax.ShapeDtypeStruct((t_tok, D_MODEL), F32)]
    if with_router:
        out_specs += [pl.BlockSpec((tm, ROW_TILES, LANES), lambda i: (i, 0, 0)),
                      pl.BlockSpec((tm, LANES), row)]
        out_shape += [jax.ShapeDtypeStruct((t_tok, ROW_TILES, LANES), F32),
                      jax.ShapeDtypeStruct((t_tok, LANES), F32)]
    else:
        out_specs.append(pl.BlockSpec((tm, D_MODEL), row))
        out_shape.append(jax.ShapeDtypeStruct((t_tok, D_MODEL), BF16))
    kern = functools.partial(_merge_kernel, with_router=with_router, n_x=len(x_parts),
                             n_first=x_parts[0].shape[0] // tm)
    return pl.pallas_call(
        kern,
        grid=(t_tok // tm,),
        in_specs=in_specs,
        out_specs=out_specs,
        out_shape=out_shape,
        compiler_params=_cparams(("parallel",)),
        name="merge_router" if with_router else "merge",
    )(*x_parts, y_ssm_raw, y_mla, qm, km, vm, sg, *weights)


def _swiglu_acc(x, w1_ref, w3_ref, w2_ref, acc, chunk):
    d_ff = w1_ref.shape[1]
    for c in range(d_ff // chunk):
        sl = slice(c * chunk, (c + 1) * chunk)
        g = _dot(x, w1_ref[:, sl])
        u = _dot(x, w3_ref[:, sl])
        a = (g * _sigmoid(g) * u).astype(BF16)
        acc = acc + _dot(a, w2_ref[sl, :])
    return acc


def _ffn_kernel(h_ref, x_ref, w1_ref, w3_ref, w2_ref, o_ref):
    o_ref[...] = _swiglu_acc(h_ref[...], w1_ref, w3_ref, w2_ref, x_ref[...], FFN_CHUNK)


def _dense_ffn(h2, x1, lw, tiles):
    t_tok = x1.shape[0]
    tm = tiles["ffn_tile"]

    def row(i):
        return (i, 0)

    def resident(shape):
        return pl.BlockSpec(shape, lambda i: (0, 0), pipeline_mode=pl.Buffered(1))

    return pl.pallas_call(
        _ffn_kernel,
        grid=(t_tok // tm,),
        in_specs=[pl.BlockSpec((tm, D_MODEL), row), pl.BlockSpec((tm, D_MODEL), row),
                  resident(lw["ffn_w1"].shape), resident(lw["ffn_w3"].shape),
                  resident(lw["ffn_w2"].shape)],
        out_specs=pl.BlockSpec((tm, D_MODEL), row),
        out_shape=jax.ShapeDtypeStruct((t_tok, D_MODEL), F32),
        compiler_params=_cparams(("parallel",)),
        name="dense_ffn",
    )(h2, x1, lw["ffn_w1"], lw["ffn_w3"], lw["ffn_w2"])


def _dispatch_kernel(pos_ref, h_ref, xs_in_ref, xs_ref, sem):
    del xs_in_ref
    tm = h_ref.shape[0]

    def tile_copy(r, dst):
        return pltpu.make_async_copy(h_ref.at[r], xs_ref.at[dst], sem)

    def start(r, c):
        tile_copy(r, pos_ref[0, 0, 2 * r]).start(priority=0)
        tile_copy(r, pos_ref[0, 0, 2 * r + 1]).start(priority=1)
        return c

    def wait(r, c):
        tile_copy(r, 0).wait()
        tile_copy(r, 0).wait()
        return c

    lax.fori_loop(0, tm, start, 0, unroll=DMA_UNROLL)
    lax.fori_loop(0, tm, wait, 0, unroll=DMA_UNROLL)


def _dispatch(h2t, pos3, n_rows, tm):
    t_tok = h2t.shape[0]
    xs0 = jnp.zeros((n_rows, ROW_TILES, LANES), F32)
    return pl.pallas_call(
        _dispatch_kernel,
        grid=(t_tok // tm,),
        in_specs=[pl.BlockSpec((1, 1, 2 * tm), lambda i: (i, 0, 0), memory_space=pltpu.SMEM),
                  pl.BlockSpec((tm, ROW_TILES, LANES), lambda i: (i, 0, 0)),
                  pl.BlockSpec(memory_space=pl.ANY)],
        out_specs=pl.BlockSpec(memory_space=pl.ANY),
        out_shape=jax.ShapeDtypeStruct(xs0.shape, F32),
        input_output_aliases={2: 0},
        scratch_shapes=[pltpu.SemaphoreType.DMA(())],
        compiler_params=_cparams(("arbitrary",)),
        name="moe_dispatch",
    )(pos3, h2t, xs0)


def _expert_kernel(be_ref, nu_ref, x_ref, w1_ref, w3_ref, w2_ref, y_ref):
    del be_ref
    i = pl.program_id(0)

    @pl.when(i < nu_ref[0])
    def _():
        x = jnp.concatenate(_token_tiles_to_rows(x_ref[...]), axis=1).astype(BF16)
        acc = _swiglu_acc(x, w1_ref, w3_ref, w2_ref, jnp.zeros((x.shape[0], D_MODEL), F32), MOE_CHUNK)
        y_ref[...] = _rows_to_token_tiles(acc)

    @pl.when(i >= nu_ref[0])
    def _():
        y_ref[...] = jnp.zeros(y_ref.shape, F32)


def _expert_ffn(xs, blk_expert, n_used, lw, bm):
    n_rows = xs.shape[0]

    def w_spec(shape):
        return pl.BlockSpec((None,) + shape, lambda i, be, nu: (be[i], 0, 0),
                            pipeline_mode=pl.Buffered(1))

    grid_spec = pltpu.PrefetchScalarGridSpec(
        num_scalar_prefetch=2,
        grid=(n_rows // bm,),
        in_specs=[
            pl.BlockSpec((bm, ROW_TILES, LANES), lambda i, be, nu: (i, 0, 0)),
            w_spec((D_MODEL, D_FF_EXPERT)), w_spec((D_MODEL, D_FF_EXPERT)),
            w_spec((D_FF_EXPERT, D_MODEL)),
        ],
        out_specs=pl.BlockSpec((bm, ROW_TILES, LANES), lambda i, be, nu: (i, 0, 0)),
    )
    return pl.pallas_call(
        _expert_kernel,
        grid_spec=grid_spec,
        out_shape=jax.ShapeDtypeStruct(xs.shape, F32),
        compiler_params=_cparams(("arbitrary",)),
        name="moe_experts",
    )(blk_expert, n_used, xs, lw["moe_w1"], lw["moe_w3"], lw["moe_w2"])


def _combine_kernel(pos_ref, pos_next_ref, ys_hbm, x_ref, route_ref, *refs, n_first):
    out_refs, (buf_ref, sem) = refs[:-2], refs[-2:]
    tm = x_ref.shape[0]
    i = pl.program_id(0)
    n = pl.num_programs(0)
    slot = i % 2

    def tile_copy(s, k, r, src):
        return pltpu.make_async_copy(ys_hbm.at[src], buf_ref.at[s, k, r], sem.at[s])

    def issue(p_ref, s):
        def start(r, c):
            tile_copy(s, 0, r, p_ref[0, 0, 2 * r]).start(priority=0)
            tile_copy(s, 1, r, p_ref[0, 0, 2 * r + 1]).start(priority=1)
            return c
        lax.fori_loop(0, tm, start, 0, unroll=DMA_UNROLL)

    @pl.when(i == 0)
    def _():
        issue(pos_ref, slot)

    @pl.when(i + 1 < n)
    def _():
        issue(pos_next_ref, 1 - slot)

    def wait(r, c):
        tile_copy(slot, 0, r, 0).wait()
        tile_copy(slot, 1, r, 0).wait()
        return c

    lax.fori_loop(0, tm, wait, 0, unroll=DMA_UNROLL)
    w1 = route_ref[:, 2:3]
    w2 = route_ref[:, 3:4]

    def write(o_ref):
        y1 = _token_tiles_to_rows(buf_ref[slot, 0])
        y2 = _token_tiles_to_rows(buf_ref[slot, 1])
        for j in range(ROW_TILES):
            sl = slice(j * LANES, (j + 1) * LANES)
            o_ref[:, sl] = x_ref[:, sl] + w1 * y1[j] + w2 * y2[j]

    if len(out_refs) == 1:
        write(out_refs[0])
    else:
        pl.when(i < n_first)(lambda: write(out_refs[0]))
        pl.when(i >= n_first)(lambda: write(out_refs[1]))


def _combine(ys, pos3, x1, route, tm, out_rows):
    t_tok = x1.shape[0]
    n_tiles = t_tok // tm
    n_first = out_rows[0] // tm
    if len(out_rows) == 1:
        out_specs = [pl.BlockSpec((tm, D_MODEL), lambda i: (i, 0))]
    else:
        out_specs = [pl.BlockSpec((tm, D_MODEL), lambda i: (jnp.minimum(i, n_first - 1), 0)),
                     pl.BlockSpec((tm, D_MODEL), lambda i: (jnp.maximum(i - n_first, 0), 0))]
    pos_spec = functools.partial(pl.BlockSpec, (1, 1, 2 * tm), memory_space=pltpu.SMEM)
    return pl.pallas_call(
        functools.partial(_combine_kernel, n_first=n_first),
        grid=(n_tiles,),
        in_specs=[pos_spec(lambda i: (i, 0, 0)),
                  pos_spec(lambda i: (jnp.minimum(i + 1, n_tiles - 1), 0, 0)),
                  pl.BlockSpec(memory_space=pl.ANY),
                  pl.BlockSpec((tm, D_MODEL), lambda i: (i, 0)),
                  pl.BlockSpec((tm, LANES), lambda i: (i, 0))],
        out_specs=out_specs,
        out_shape=[jax.ShapeDtypeStruct((r, D_MODEL), F32) for r in out_rows],
        scratch_shapes=[pltpu.VMEM((2, 2, tm, ROW_TILES, LANES), F32), pltpu.SemaphoreType.DMA((2,))],
        compiler_params=_cparams(("arbitrary",)),
        name="moe_combine",
    )(pos3, pos3, ys, x1, route)


def _moe_ffn(h2t, x1, route, lw, tiles, out_rows):
    t_tok = x1.shape[0]
    tm = tiles["tm"]
    bm = tiles["moe_block"]
    n_assign = 2 * t_tok
    experts = route[:, :2].astype(jnp.int32).reshape(n_assign)
    onehot = (experts[:, None] == jnp.arange(N_EXPERTS, dtype=jnp.int32)[None, :]).astype(jnp.int32)
    csum = jnp.cumsum(onehot, axis=0)
    rank = jnp.sum((csum - 1) * onehot, axis=1)
    counts = csum[-1]
    padded = ((counts + bm - 1) // bm) * bm
    ends = jnp.cumsum(padded)
    starts = ends - padded
    pos = jnp.sum(onehot * starts[None, :], axis=1) + rank
    n_blocks = n_assign // bm + N_EXPERTS
    blk_start = jnp.arange(n_blocks, dtype=jnp.int32) * bm
    blk_expert = jnp.sum((blk_start[:, None] >= ends[None, :]).astype(jnp.int32), axis=1)
    blk_expert = jnp.minimum(blk_expert, N_EXPERTS - 1)
    n_used = (ends[-1:] // bm).astype(jnp.int32)
    pos3 = pos.astype(jnp.int32).reshape(t_tok // tm, 1, 2 * tm)

    xs = _dispatch(h2t, pos3, n_blocks * bm, tm)
    ys = _expert_ffn(xs, blk_expert, n_used, lw, bm)
    return _combine(ys, pos3, x1, route, tm, out_rows)


def _head_cols(w, per_head, n_heads, offset=0):
    k = w.shape[0]
    w = w.reshape(k, n_heads, per_head)
    w = jnp.pad(w, ((0, 0), (0, 0), (offset, HEAD_PAD - per_head - offset)))
    return w.reshape(k, n_heads * HEAD_PAD)


def _swap_rope(w):
    z = jnp.zeros_like(w[..., :MLA_NOPE])
    return jnp.concatenate([z, w[..., MLA_NOPE + HALF_ROPE:], w[..., MLA_NOPE:MLA_NOPE + HALF_ROPE]], axis=-1)


def _pad_lane_block(v, offset=0):
    return jnp.pad(v, ((0, 0),) * (v.ndim - 1) + ((offset, LANES - v.shape[-1] - offset),))


def _layer_weights(i, p, rope_cos, rope_sin):
    o = IN_OFFSETS
    w_in = p["w_in"][i]
    lw = {}
    lw["g_mix"] = p["g_mix"][i].reshape(1, D_MODEL)
    lw["w_u"] = w_in[:, :o[0]].astype(BF16)
    lw["w_cq"] = w_in[:, o[0]:o[1]].astype(BF16)
    w_kr = w_in[:, o[2]:o[3]]
    w_kr_sw = jnp.concatenate([w_kr[:, HALF_ROPE:], w_kr[:, :HALF_ROPE]], axis=1)
    lw["w_ckv"] = jnp.concatenate([w_in[:, o[1]:o[2]], _pad_lane_block(w_kr, MLA_NOPE),
                                   _pad_lane_block(w_kr_sw, MLA_NOPE)], axis=1).astype(BF16)
    lw["w_qm"] = w_in[:, o[3]:o[4]].astype(BF16)
    lw["w_gate"] = w_in[:, o[4]:].astype(BF16)
    lw["g_q_lora"] = p["g_q_lora"][i].reshape(1, Q_LORA)
    w_uq = p["w_uq"][i].reshape(Q_LORA, MLA_HEADS, MLA_QK)
    lw["w_uq"] = jnp.concatenate(
        [_head_cols(w_uq.reshape(Q_LORA, -1), MLA_QK, MLA_HEADS),
         _head_cols(_swap_rope(w_uq).reshape(Q_LORA, -1), MLA_QK, MLA_HEADS)], axis=1).astype(BF16)
    lw["g_kv_lora"] = p["g_kv_lora"][i].reshape(1, KV_LORA)
    w_ukv = p["w_ukv"][i].reshape(KV_LORA, MLA_HEADS, MLA_NOPE + MLA_V)
    lw["w_kn"] = _head_cols(w_ukv[:, :, :MLA_NOPE].reshape(KV_LORA, -1), MLA_NOPE, MLA_HEADS).astype(BF16)
    w_v = w_ukv[:, :, MLA_NOPE:].reshape(KV_LORA, MLA_HEADS // 2, 2, MLA_V)
    w_v = jnp.stack([jnp.pad(w_v[:, :, 0], ((0, 0), (0, 0), (0, HEAD_PAD - MLA_V))),
                     jnp.pad(w_v[:, :, 1], ((0, 0), (0, 0), (HEAD_PAD - MLA_V, 0)))], axis=2)
    lw["w_v"] = w_v.reshape(KV_LORA, MLA_WIDTH_PAD).astype(BF16)
    one_even = jnp.zeros((HEAD_PAD,), F32).at[MLA_V].set(1.0)
    one_odd = jnp.zeros((HEAD_PAD,), F32).at[0].set(1.0)
    lw["v_one"] = jnp.tile(jnp.concatenate([one_even, one_odd]), MLA_HEADS // 2).reshape(1, MLA_WIDTH_PAD)
    for name, g in (("q", p["g_mla_q"][i]), ("k", p["g_mla_k"][i])):
        lw["rope_a" + name] = rope_cos * _pad_lane_block(g)[None, :]
        lw["rope_b" + name] = rope_sin * _pad_lane_block(_swap_rope(g))[None, :]
    lw["g_mem_q"] = jnp.tile(p["g_mem_q"][i], MEM_HEADS).reshape(1, MEM_WIDTH) * (MEM_HEAD_DIM ** -0.5)
    lw["g_mem"] = p["g_mem"][i].reshape(1, D_MODEL)
    lw["w_mem_kv"] = p["w_mem_kv"][i].astype(BF16)
    lw["g_mem_k"] = p["g_mem_k"][i].reshape(1, MEM_HEAD_DIM)
    lw["w_glu"] = p["ssm_w_glu"][i].astype(BF16)
    lw["b_glu"] = p["ssm_b_glu"][i].reshape(1, 2 * SSM_WIDTH)
    lw["w_branch"] = p["w_branch"][i].astype(BF16)
    lw["w_out"] = p["w_out"][i].astype(BF16)
    lw["g_ffn"] = p["g_ffn"][i].reshape(1, D_MODEL)
    lw["ssm"] = _ssm_tables(p["ssm_a_re"][i], p["ssm_a_im"][i], p["ssm_log_dt"][i], p["ssm_b_re"][i],
                            p["ssm_b_im"][i], p["ssm_c_re"][i], p["ssm_c_im"][i], p["ssm_d"][i])
    lw["score_bound"] = (1.02 * MLA_QK ** 0.5 * LOG2E
                         * jnp.max(jnp.abs(p["g_mla_q"][i])) * jnp.max(jnp.abs(p["g_mla_k"][i])))
    if i % 2 == 0:
        w13 = p["ffn_w13"][i // 2]
        lw["ffn_w1"] = w13[:, :D_FF].astype(BF16)
        lw["ffn_w3"] = w13[:, D_FF:].astype(BF16)
        lw["ffn_w2"] = p["ffn_w2"][i // 2].astype(BF16)
    else:
        w13 = p["moe_w13"][i // 2]
        lw["moe_w1"] = w13[:, :, :D_FF_EXPERT].astype(BF16)
        lw["moe_w3"] = w13[:, :, D_FF_EXPERT:].astype(BF16)
        lw["moe_w2"] = p["moe_w2"][i // 2].astype(BF16)
        wr = jnp.pad(p["moe_router"][i // 2], ((0, 0), (0, LANES - N_EXPERTS)))
        lw["w_router_hi"], lw["w_router_lo"] = _split_bf16(wr)
    return lw


def _rope_tables(max_len):
    inv = 1.0 / (ROPE_BASE ** (jnp.arange(0, MLA_ROPE, 2, dtype=F32) / MLA_ROPE))
    ang = jnp.arange(max_len, dtype=F32)[:, None] * inv[None, :]
    cos, sin = jnp.cos(ang), jnp.sin(ang)
    ones = jnp.ones((max_len, MLA_NOPE), F32)
    zn = jnp.zeros((max_len, MLA_NOPE), F32)
    zp = jnp.zeros((max_len, HEAD_PAD - MLA_QK), F32)
    return (jnp.concatenate([ones, cos, cos, zp], axis=1),
            jnp.concatenate([zn, -sin, sin, zp], axis=1))


def _gcd_tile(limit, *sizes):
    t = limit
    while any(s % t for s in sizes):
        t //= 2
    return t


def kernel(x_prompt, x_sample, mem_prompt, mem_sample, g_mix, w_in, ssm_a_re, ssm_a_im, ssm_log_dt, ssm_b_re, ssm_b_im, ssm_c_re, ssm_c_im, ssm_d, ssm_w_glu, ssm_b_glu, g_q_lora, w_uq, g_kv_lora, w_ukv, g_mla_q, g_mla_k, g_mem, w_mem_kv, g_mem_q, g_mem_k, w_branch, w_out, g_ffn, ffn_w13, ffn_w2, moe_router, moe_w13, moe_w2):
    params = dict(g_mix=g_mix, w_in=w_in, ssm_a_re=ssm_a_re, ssm_a_im=ssm_a_im, ssm_log_dt=ssm_log_dt,
                  ssm_b_re=ssm_b_re, ssm_b_im=ssm_b_im, ssm_c_re=ssm_c_re, ssm_c_im=ssm_c_im, ssm_d=ssm_d,
                  ssm_w_glu=ssm_w_glu, ssm_b_glu=ssm_b_glu, g_q_lora=g_q_lora, w_uq=w_uq,
                  g_kv_lora=g_kv_lora, w_ukv=w_ukv, g_mla_q=g_mla_q, g_mla_k=g_mla_k, g_mem=g_mem,
                  w_mem_kv=w_mem_kv, g_mem_q=g_mem_q, g_mem_k=g_mem_k, w_branch=w_branch, w_out=w_out,
                  g_ffn=g_ffn, ffn_w13=ffn_w13, ffn_w2=ffn_w2, moe_router=moe_router, moe_w13=moe_w13,
                  moe_w2=moe_w2)
    depth = g_mix.shape[0]
    bp, lp, _ = x_prompt.shape
    bs, ls, _ = x_sample.shape
    tp, ts = bp * lp, bs * ls
    t_tok = tp + ts

    tm = _gcd_tile(TOKEN_TILE, lp, ls)
    attn_q_tile = _gcd_tile(ATTN_Q_TILE, lp, ls)
    attn_kv_tile = _gcd_tile(ATTN_KV_TILE, lp, ls)
    ffn_tile = _gcd_tile(FFN_TILE, lp, ls)
    npt = tp // tm
    lp_t, ls_t = lp // tm, ls // tm

    def pos_map(i):
        return jnp.where(i < npt, i % lp_t, (i - npt) % ls_t)

    def seq_map(i):
        return jnp.where(i < npt, i // lp_t, bp + (i - npt) // ls_t)

    rope_cos, rope_sin = _rope_tables(max(lp, ls))
    head_ind = (jnp.arange(MLA_WIDTH_PAD)[:, None] // HEAD_PAD == jnp.arange(LANES)[None, :])
    mem_bd = (jnp.arange(MEM_WIDTH)[:, None] // MEM_HEAD_DIM == jnp.arange(MEM_WIDTH)[None, :] // MEM_HEAD_DIM)
    tiles = dict(tm=tm, ffn_tile=ffn_tile, pos_map=pos_map, seq_map=seq_map,
                 ones=jnp.ones((D_MODEL, LANES), BF16),
                 head_ind=head_ind.astype(BF16), head_ind_t=head_ind.T.astype(BF16),
                 mem_bd=mem_bd.astype(BF16),
                 moe_block=_gcd_tile(MOE_BLOCK, 2 * t_tok))

    x_parts = [x_prompt.reshape(tp, D_MODEL), x_sample.reshape(ts, D_MODEL)]
    mem = jnp.concatenate([mem_prompt.reshape(bp * MEM_TOKENS, D_MODEL),
                           mem_sample.reshape(bs * MEM_TOKENS, D_MODEL)], axis=0)
    seqs = [(bp, lp), (bs, ls)]

    for i in range(depth):
        lw = _layer_weights(i, params, rope_cos, rope_sin)
        u, q, k, v, qm, sg = _in_projection(x_parts, lw, tiles)

        y_ssm_raw = _ssm_call(u, lw["ssm"], seqs)

        y_mla = lax.cond(
            lw["score_bound"] <= MAX_SAFE_LOG2_SCORE,
            lambda q_, k_, v_: _flash_attention(q_, k_, v_, seqs, attn_q_tile, attn_kv_tile, running_max=False),
            lambda q_, k_, v_: _flash_attention(q_, k_, v_, seqs, attn_q_tile, attn_kv_tile, running_max=True),
            q, k, v)
        km, vm = _memory_kv(mem, lw)

        with_router = i % 2 == 1
        outs = _merge(x_parts, y_ssm_raw, y_mla, qm, km, vm, sg, lw, tiles, with_router)
        if with_router:
            x1, h2t, route = outs
            out_rows = (tp, ts) if i == depth - 1 else (t_tok,)
            x_parts = _moe_ffn(h2t, x1, route, lw, tiles, out_rows)
        else:
            x1, h2 = outs
            x_parts = [_dense_ffn(h2, x1, lw, tiles)]

    if len(x_parts) == 1:
        x_parts = [x_parts[0][:tp], x_parts[0][tp:]]
    return (x_parts[0].reshape(bp, lp, D_MODEL), x_parts[1].reshape(bs, ls, D_MODEL))
```

```python
import functools
import math

import numpy as np
import jax
import jax.numpy as jnp
from jax import lax
from jax.experimental import pallas as pl
from jax.experimental.pallas import tpu as pltpu

F32 = jnp.float32
BF16 = jnp.bfloat16
EPS = 1e-6
LOG2E = math.log2(math.e)

D_MODEL = 1024
SSM_WIDTH = 512
SSM_GROUPS = 32
SSM_GROUP = 16
SSM_STATE = 64
MLA_HEADS = 8
MLA_NOPE = 64
MLA_ROPE = 32
MLA_V = 64
MLA_QK = MLA_NOPE + MLA_ROPE
Q_LORA = 256
KV_LORA = 128
ROPE_BASE = 10000.0
MEM_TOKENS = 256
MEM_HEADS = 4
MEM_HEAD_DIM = 128
MEM_WIDTH = 512
N_BRANCH = 3
D_FF = 2816
N_EXPERTS = 8
D_FF_EXPERT = 3584
IN_SIZES = (SSM_WIDTH, Q_LORA, KV_LORA, MLA_ROPE, MEM_WIDTH, N_BRANCH * D_MODEL)
IN_OFFSETS = tuple(int(v) for v in np.cumsum(IN_SIZES)[:-1])

LANES = 128
SUBLANES = 8
HEAD_PAD = 128
HALF_ROPE = MLA_ROPE // 2
MLA_WIDTH_PAD = MLA_HEADS * HEAD_PAD
ROW_TILES = D_MODEL // LANES
VMEM_LIMIT = 56 * 1024 * 1024
MAX_SAFE_LOG2_SCORE = 100.0
MAX_FP8_LOG2_SCORE = 20.0

TOKEN_TILE = 512
ATTN_Q_TILE = 2048
ATTN_KV_TILE = 1024
FFN_TILE = 1024
SSM_CHUNK = 16
MOE_BLOCK = 512
FFN_CHUNK = 256
MOE_CHUNK = 512
DMA_UNROLL = 8


def _cparams(sem):
    return pltpu.CompilerParams(dimension_semantics=sem, vmem_limit_bytes=VMEM_LIMIT)


def _const_spec(shape):
    nd = len(shape)
    return pl.BlockSpec(shape, lambda *_: (0,) * nd)


def _sigmoid(x):
    return 1.0 / (1.0 + jnp.exp(-x))


def _rms_scale(x, n):
    return lax.rsqrt(jnp.sum(x * x, axis=-1, keepdims=True) * (1.0 / n) + EPS)


def _split_bf16(x):
    hi = x.astype(BF16)
    lo = (x - hi.astype(F32)).astype(BF16)
    return hi, lo


def _dot(a, b):
    return jnp.dot(a, b, preferred_element_type=F32)


def _tile_transpose(vs):
    sub = lax.broadcasted_iota(jnp.int32, vs[0].shape, 1)
    for dist in (4, 2, 1):
        low = (sub // dist) % 2 == 0
        new = list(vs)
        for a in range(SUBLANES):
            if a & dist == 0:
                b = a | dist
                new[a] = jnp.where(low, vs[a], pltpu.roll(vs[b], dist, 1))
                new[b] = jnp.where(low, pltpu.roll(vs[a], SUBLANES - dist, 1), vs[b])
        vs = new
    return vs


def _rows_to_token_tiles(x):
    rows = x.shape[0]
    cols = [x[:, j * LANES:(j + 1) * LANES].reshape(rows // SUBLANES, SUBLANES, LANES) for j in range(ROW_TILES)]
    return jnp.stack(_tile_transpose(cols), axis=1).reshape(rows, ROW_TILES, LANES)


def _token_tiles_to_rows(t):
    rows = t.shape[0]
    t4 = t.reshape(rows // SUBLANES, SUBLANES, ROW_TILES, LANES)
    cols = _tile_transpose([t4[:, k] for k in range(SUBLANES)])
    return [c.reshape(rows, LANES) for c in cols]


def _stream_specs(parts, tm, width):
    if len(parts) == 1:
        return [pl.BlockSpec((tm, width), lambda i: (i, 0))]
    n0 = parts[0].shape[0] // tm
    return [pl.BlockSpec((tm, width), lambda i: (jnp.minimum(i, n0 - 1), 0)),
            pl.BlockSpec((tm, width), lambda i: (jnp.maximum(i - n0, 0), 0))]


def _stream_read(refs, n_first):
    if len(refs) == 1:
        return refs[0][...]
    return jnp.where(pl.program_id(0) < n_first, refs[0][...], refs[1][...])


def _inproj_kernel(*refs, n_x, n_first):
    x_refs = refs[:n_x]
    (gmix_ref, wu_ref, wg_ref, wqm_ref, wcq_ref, wckv_ref,
     gql_ref, wuq_ref, gkvl_ref, wkn_ref, wv_ref, vone_ref, gmq_ref,
     ones_ref, ind_ref, indt_ref, bd_ref,
     aq_ref, bq_ref, ak_ref, bk_ref,
     u_ref, q_ref, k_ref, v_ref, qm_ref, sg_ref) = refs[n_x:]

    def row_ssq(val):
        sq = val * val
        part = sq[:, :LANES]
        for j in range(1, val.shape[1] // LANES):
            part = part + sq[:, j * LANES:(j + 1) * LANES]
        return _dot(part.astype(BF16), ones_ref[...])

    def head_scale(val, scale):
        ss = _dot((val * val).astype(BF16), ind_ref[...])
        r = lax.rsqrt(ss * (1.0 / MLA_QK) + EPS) * scale
        return _dot(jnp.concatenate(_split_bf16(r), axis=1), indt_ref[...])

    x = _stream_read(x_refs, n_first)
    rs = lax.rsqrt(row_ssq(x) * (1.0 / D_MODEL) + EPS)
    h = (x * jnp.tile(rs, (1, D_MODEL // LANES)) * gmix_ref[...]).astype(BF16)

    u_ref[...] = _dot(h, wu_ref[...])

    for b in range(N_BRANCH):
        sl = slice(b * D_MODEL, (b + 1) * D_MODEL)
        sg_ref[:, sl] = _sigmoid(_dot(h, wg_ref[:, sl])).astype(BF16)

    zq = _dot(h, wqm_ref[...])
    zr = lax.rsqrt(_dot((zq * zq).astype(BF16), bd_ref[...]) * (1.0 / MEM_HEAD_DIM) + EPS)
    qm_ref[...] = (zq * zr * gmq_ref[...]).astype(BF16)

    cq = _dot(h, wcq_ref[...])
    cq_rs = lax.rsqrt(row_ssq(cq) * (1.0 / Q_LORA) + EPS)
    cqn = (cq * jnp.tile(cq_rs, (1, Q_LORA // LANES)) * gql_ref[...]).astype(BF16)
    qq = _dot(cqn, wuq_ref[...])
    qf = qq[:, :MLA_WIDTH_PAD]
    qsw = qq[:, MLA_WIDTH_PAD:]

    z3 = _dot(h, wckv_ref[...])
    ckv = z3[:, :LANES]
    kr = z3[:, LANES:2 * LANES]
    kr_sw = z3[:, 2 * LANES:]
    ckv_rs = lax.rsqrt(row_ssq(ckv) * (1.0 / KV_LORA) + EPS)
    ckvn = (ckv * ckv_rs * gkvl_ref[...]).astype(BF16)
    kf = _dot(ckvn, wkn_ref[...]) + jnp.tile(kr, (1, MLA_HEADS))
    v_ref[...] = (_dot(ckvn, wv_ref[...]) + vone_ref[...]).astype(BF16)

    q_rs = head_scale(qf, MLA_QK ** -0.5 * LOG2E)
    k_rs = head_scale(kf, 1.0)
    aq = aq_ref[...]
    bq = bq_ref[...]
    ak = ak_ref[...]
    k_rot = kr_sw * bk_ref[...]
    for hh in range(MLA_HEADS):
        sl = slice(hh * HEAD_PAD, (hh + 1) * HEAD_PAD)
        q_ref[:, sl] = ((qf[:, sl] * aq + qsw[:, sl] * bq) * q_rs[:, sl]).astype(BF16)
        k_ref[:, sl] = ((kf[:, sl] * ak + k_rot) * k_rs[:, sl]).astype(BF16)


def _in_projection(x_parts, lw, tiles):
    t_tok = sum(p.shape[0] for p in x_parts)
    tm = tiles["tm"]
    pos_map = tiles["pos_map"]

    def row(i):
        return (i, 0)

    weights = [lw["g_mix"], lw["w_u"], lw["w_gate"], lw["w_qm"], lw["w_cq"], lw["w_ckv"],
               lw["g_q_lora"], lw["w_uq"], lw["g_kv_lora"], lw["w_kn"], lw["w_v"], lw["v_one"],
               lw["g_mem_q"], tiles["ones"], tiles["head_ind"], tiles["head_ind_t"], tiles["mem_bd"]]
    rope = [lw["rope_aq"], lw["rope_bq"], lw["rope_ak"], lw["rope_bk"]]
    in_specs = _stream_specs(x_parts, tm, D_MODEL) + [_const_spec(w.shape) for w in weights]
    in_specs += [pl.BlockSpec((tm, HEAD_PAD), lambda i: (pos_map(i), 0))] * len(rope)
    kern = functools.partial(_inproj_kernel, n_x=len(x_parts), n_first=x_parts[0].shape[0] // tm)
    out_widths = (SSM_WIDTH, MLA_WIDTH_PAD, MLA_WIDTH_PAD, MLA_WIDTH_PAD, MEM_WIDTH, N_BRANCH * D_MODEL)
    return pl.pallas_call(
        kern,
        grid=(t_tok // tm,),
        in_specs=in_specs,
        out_specs=[pl.BlockSpec((tm, w), row) for w in out_widths],
        out_shape=[jax.ShapeDtypeStruct((t_tok, w), F32 if j == 0 else BF16) for j, w in enumerate(out_widths)],
        compiler_params=_cparams(("parallel",)),
        name="in_projection",
    )(*x_parts, *weights, *rope)


SSM_LANE_GROUPS = LANES // SSM_GROUP
SSM_TAB_DIR = 11
(_TAB_STEP, _TAB_POW, _TAB_A8, _TAB_MASK) = (0, 6, 8, 10)


def _granule_transpose(vs, lane):
    for dist in (4, 2, 1):
        shift = dist * SSM_GROUP
        low = (lane // shift) % 2 == 0
        new = list(vs)
        for a in range(SSM_LANE_GROUPS):
            if a & dist == 0:
                b = a | dist
                new[a] = jnp.where(low, vs[a], pltpu.roll(vs[b], shift, 1))
                new[b] = jnp.where(low, pltpu.roll(vs[a], LANES - shift, 1), vs[b])
        vs = new
    return vs


def _chunk_scan(x, xs, tab_ref, base, keep, backward):
    nv, _, nl = x.shape
    for j, k in enumerate((1, 2, 4)):
        shift = SUBLANES - k if backward else k
        rx = pltpu.roll(x, shift, 1)
        rxs = pltpu.roll(xs, shift, 1)
        a1 = tab_ref[base + _TAB_STEP + 2 * j][None]
        a2 = tab_ref[base + _TAB_STEP + 2 * j + 1][None]
        x, xs = x + a1 * rx + a2 * rxs, xs + a1 * rxs - a2 * rx
    a1 = tab_ref[base + _TAB_A8]
    a2 = tab_ref[base + _TAB_A8 + 1]
    edge = 0 if backward else SUBLANES - 1
    c = jnp.zeros((SUBLANES, nl), F32)
    cs = c
    cb = [None] * nv
    csb = [None] * nv
    for v in (range(nv - 1, -1, -1) if backward else range(nv)):
        kp = keep(v)
        c = c * kp
        cs = cs * kp
        cb[v] = c
        csb[v] = cs
        ex = jnp.broadcast_to(x[v, edge:edge + 1, :], (SUBLANES, nl))
        exs = jnp.broadcast_to(xs[v, edge:edge + 1, :], (SUBLANES, nl))
        c, cs = ex + a1 * c + a2 * cs, exs + a1 * cs - a2 * c
    xe = pltpu.roll(x, SUBLANES - 1 if backward else 1, 1) * tab_ref[base + _TAB_MASK][None]
    return (xe + tab_ref[base + _TAB_POW][None] * jnp.stack(cb)
            + tab_ref[base + _TAB_POW + 1][None] * jnp.stack(csb))


def _ssm_kernel(keepf_ref, keepb_ref, u_ref, t_ref, win_ref, wout_ref, tab_ref, y_ref):
    blk = pl.program_id(1)
    rows = u_ref.shape[0] // SSM_CHUNK
    nv = rows // SUBLANES
    ng = SSM_LANE_GROUPS
    lane = lax.broadcasted_iota(jnp.int32, (rows, LANES), 1)

    pieces = [u_ref[pl.ds(t, rows, stride=SSM_CHUNK), :] for t in range(SSM_CHUNK)]
    halves = [_granule_transpose(pieces[ng * h:ng * (h + 1)], lane) for h in range(SSM_CHUNK // ng)]

    ys = []
    parts = [[], [], [], []]
    for g in range(ng):
        xg = jnp.concatenate([h[g] for h in halves], axis=1).astype(BF16)
        ys.append(_dot(xg, t_ref[g]))
        s4 = _dot(xg, win_ref[g])
        for j in range(4):
            parts[j].append(s4[:, j * LANES:(j + 1) * LANES])
    f, fs, b, bs = [jnp.concatenate(p, axis=1).reshape(nv, SUBLANES, ng * LANES) for p in parts]

    xin_f = _chunk_scan(f, fs, tab_ref, 0, lambda v: keepf_ref[blk * nv + v].astype(F32), False)
    xin_b = _chunk_scan(b, bs, tab_ref, SSM_TAB_DIR, lambda v: keepb_ref[blk * nv + v].astype(F32), True)
    xin_f = xin_f.reshape(rows, ng * LANES)
    xin_b = xin_b.reshape(rows, ng * LANES)

    outs = []
    for g in range(ng):
        sl = slice(g * LANES, (g + 1) * LANES)
        xs = jnp.concatenate([xin_f[:, sl], xin_b[:, sl]], axis=1).astype(BF16)
        outs.append(ys[g] + _dot(xs, wout_ref[g]))
    for h in range(SSM_CHUNK // ng):
        back = _granule_transpose([o[:, h * LANES:(h + 1) * LANES] for o in outs], lane)
        for tt in range(ng):
            y_ref[pl.ds(ng * h + tt, rows, stride=SSM_CHUNK), :] = back[tt]


def _ssm_call(u, tabs, seqs):
    t_tok = u.shape[0]
    block = int(np.lcm.reduce([ln for _, ln in seqs]))
    vreg_tokens = SUBLANES * SSM_CHUNK
    assert all((n * ln) % block == 0 and ln % vreg_tokens == 0 for n, ln in seqs), seqs
    nv = block // vreg_tokens
    keep_f, keep_b = [], []
    for n_seq, seq_len in seqs:
        for _ in range(n_seq * seq_len // block):
            for v in range(nv):
                keep_f.append(int((v * vreg_tokens) % seq_len != 0))
                keep_b.append(int(((v + 1) * vreg_tokens) % seq_len != 0))
    keep = [jnp.asarray(np.asarray(a, np.int32)) for a in (keep_f, keep_b)]
    ng = SSM_LANE_GROUPS
    width = SSM_CHUNK * SSM_GROUP
    n_tab = 2 * SSM_TAB_DIR
    grid_spec = pltpu.PrefetchScalarGridSpec(
        num_scalar_prefetch=2,
        grid=(SSM_WIDTH // LANES, t_tok // block),
        in_specs=[
            pl.BlockSpec((block, LANES), lambda g, b, kf, kb: (b, g)),
            pl.BlockSpec((ng, width, width), lambda g, b, kf, kb: (g, 0, 0)),
            pl.BlockSpec((ng, width, 4 * LANES), lambda g, b, kf, kb: (g, 0, 0)),
            pl.BlockSpec((ng, 2 * LANES, width), lambda g, b, kf, kb: (g, 0, 0)),
            pl.BlockSpec((n_tab, SUBLANES, ng * LANES), lambda g, b, kf, kb: (0, 0, g)),
        ],
        out_specs=pl.BlockSpec((block, LANES), lambda g, b, kf, kb: (b, g)),
    )
    return pl.pallas_call(
        _ssm_kernel,
        grid_spec=grid_spec,
        out_shape=jax.ShapeDtypeStruct(u.shape, F32),
        compiler_params=_cparams(("parallel", "parallel")),
        name="ssm_chunked",
    )(*keep, u, tabs["t"], tabs["w_in"], tabs["w_out"], tabs["scan"])


def _ssm_tables(a_re, a_im, log_dt, b_re, b_im, c_re, c_im, d_skip):
    q = SSM_CHUNK
    hp = lax.Precision.HIGHEST
    dt = jnp.exp(log_dt)[..., None, None]
    ks = jnp.arange(q + 1, dtype=F32)
    mag = jnp.exp(a_re[..., None] * dt * ks)
    ang = a_im[..., None] * dt * ks
    pw_re = mag * jnp.cos(ang)
    pw_im = mag * jnp.sin(ang)
    lb_re, lb_im = pw_re[..., 1], pw_im[..., 1]
    den = a_re * a_re + a_im * a_im
    f_re = ((lb_re - 1.0) * a_re + lb_im * a_im) / den
    f_im = (lb_im * a_re - (lb_re - 1.0) * a_im) / den
    bb_re = f_re[..., None] * b_re - f_im[..., None] * b_im
    bb_im = f_re[..., None] * b_im + f_im[..., None] * b_re

    cl_re = c_re[..., None] * pw_re[:, :, None] - c_im[..., None] * pw_im[:, :, None]
    cl_im = c_re[..., None] * pw_im[:, :, None] + c_im[..., None] * pw_re[:, :, None]

    kern = (jnp.einsum("dgnpk,dgpm->dgknm", cl_re[..., :q], bb_re, precision=hp)
            - jnp.einsum("dgnpk,dgpm->dgknm", cl_im[..., :q], bb_im, precision=hp))
    s_idx = jnp.arange(q)[:, None]
    t_idx = jnp.arange(q)[None, :]
    lag_f = jnp.clip(t_idx - s_idx, 0, q - 1)
    lag_b = jnp.clip(s_idx - t_idx, 0, q - 1)
    tf = kern[0][:, lag_f] * (t_idx >= s_idx)[None, :, :, None, None]
    tb = kern[1][:, lag_b] * (s_idx >= t_idx)[None, :, :, None, None]
    t_mat = (tf + tb).transpose(0, 1, 4, 2, 3)
    t_mat = t_mat.reshape(SSM_GROUPS, q * SSM_GROUP, q * SSM_GROUP)
    d_rep = jnp.tile(d_skip.reshape(SSM_GROUPS, 1, SSM_GROUP), (1, q, 1)).reshape(SSM_GROUPS, -1)
    t_mat = t_mat + d_rep[:, :, None] * jnp.eye(q * SSM_GROUP, dtype=F32)[None]

    def w_in_dir(d, exps):
        pr = pw_re[d][:, :, exps]
        pi = pw_im[d][:, :, exps]
        re = pr[..., None] * bb_re[d][:, :, None, :] - pi[..., None] * bb_im[d][:, :, None, :]
        im = pr[..., None] * bb_im[d][:, :, None, :] + pi[..., None] * bb_re[d][:, :, None, :]
        re = re.transpose(0, 2, 3, 1).reshape(SSM_GROUPS, q * SSM_GROUP, SSM_STATE)
        im = im.transpose(0, 2, 3, 1).reshape(SSM_GROUPS, q * SSM_GROUP, SSM_STATE)
        return jnp.concatenate([re, im, im, re], axis=-1)

    w_in = jnp.concatenate([w_in_dir(0, q - 1 - jnp.arange(q)), w_in_dir(1, jnp.arange(q))], axis=-1)

    def w_out_dir(d, exps):
        re = cl_re[d][..., exps]
        im = cl_im[d][..., exps]
        re = re.transpose(0, 2, 3, 1).reshape(SSM_GROUPS, SSM_STATE, q * SSM_GROUP)
        im = im.transpose(0, 2, 3, 1).reshape(SSM_GROUPS, SSM_STATE, q * SSM_GROUP)
        return jnp.concatenate([re, -im], axis=1)

    w_out = jnp.concatenate([w_out_dir(0, 1 + jnp.arange(q)), w_out_dir(1, q - jnp.arange(q))], axis=1)

    ms = jnp.arange(SUBLANES + 1, dtype=F32) * q
    cr = jnp.exp(a_re[..., None] * dt * ms) * jnp.cos(a_im[..., None] * dt * ms)
    ci = jnp.exp(a_re[..., None] * dt * ms) * jnp.sin(a_im[..., None] * dt * ms)

    def pat(d, m):
        ar, ai = cr[d][..., m], ci[d][..., m]
        return (jnp.concatenate([ar, ar], axis=-1).reshape(-1), jnp.concatenate([-ai, ai], axis=-1).reshape(-1))

    row = jnp.arange(SUBLANES)

    def dir_tables(d, backward):
        tabs = []
        for k in (1, 2, 4):
            valid = (row <= SUBLANES - 1 - k) if backward else (row >= k)
            tabs += [valid[:, None] * p[None, :] for p in pat(d, k)]
        pows = [pat(d, SUBLANES - 1 - i if backward else i) for i in range(SUBLANES)]
        tabs += [jnp.stack([p[0] for p in pows]), jnp.stack([p[1] for p in pows])]
        tabs += [jnp.tile(p[None, :], (SUBLANES, 1)) for p in pat(d, SUBLANES)]
        excl = (row <= SUBLANES - 2) if backward else (row >= 1)
        tabs.append(jnp.tile(excl[:, None].astype(F32), (1, SSM_GROUPS * LANES)))
        return tabs

    scan_tab = jnp.stack(dir_tables(0, False) + dir_tables(1, True), axis=0).astype(F32)
    return {"t": t_mat.astype(BF16), "w_in": w_in.astype(BF16), "w_out": w_out.astype(BF16),
            "scan": scan_tab}


def _flash_kernel(qb_ref, kb_ref, first_ref, last_ref, q_ref, k_ref, v_ref, o_ref,
                  acc_ref, *m_scratch, running_max, score_dtype):
    del qb_ref, kb_ref
    s_id = pl.program_id(1)

    @pl.when(first_ref[s_id] == 1)
    def _():
        acc_ref[...] = jnp.zeros(acc_ref.shape, F32)
        if running_max:
            m_scratch[0][...] = jnp.full(m_scratch[0].shape, -jnp.inf, F32)

    for hh in range(2):
        sl = slice(hh * HEAD_PAD, (hh + 1) * HEAD_PAD)
        s = lax.dot_general(q_ref[:, sl].astype(score_dtype), k_ref[:, sl].astype(score_dtype),
                            (((1,), (1,)), ((), ())), preferred_element_type=F32)
        if running_max:
            m_ref = m_scratch[0]
            m_prev = m_ref[hh]
            m_new = jnp.maximum(m_prev, jnp.max(s, axis=1, keepdims=True))
            p = jnp.exp2(s - m_new[:, :1]).astype(BF16)
            acc_ref[hh] = jnp.exp2(m_prev - m_new) * acc_ref[hh] + _dot(p, v_ref[:, sl])
            m_ref[hh] = m_new
        else:
            acc_ref[hh] += _dot(jnp.exp2(s.astype(BF16)), v_ref[:, sl])

    @pl.when(last_ref[s_id] == 1)
    def _():
        lane = lax.broadcasted_iota(jnp.int32, o_ref.shape, 1)
        a0 = acc_ref[0]
        a1 = acc_ref[1]
        o0 = a0 / a0[:, MLA_V:MLA_V + 1]
        o1 = a1 / a1[:, 0:1]
        o_ref[...] = jnp.where(lane < MLA_V, o0, o1).astype(BF16)


def _flash_attention(q, k, v, seqs, q_tile, kv_tile, running_max, score_dtype=BF16):
    qb, kb, first, last = [], [], [], []
    base = 0
    for n_seq, seq_len in seqs:
        nq, nk = seq_len // q_tile, seq_len // kv_tile
        for _ in range(n_seq):
            for qi in range(nq):
                for ki in range(nk):
                    qb.append(base // q_tile + qi)
                    kb.append(base // kv_tile + ki)
                    first.append(int(ki == 0))
                    last.append(int(ki == nk - 1))
            base += seq_len
    tabs = [jnp.asarray(np.asarray(a, np.int32)) for a in (qb, kb, first, last)]
    n_steps = len(qb)
    t_tok = q.shape[0]
    pair_w = 2 * HEAD_PAD
    scratch = [pltpu.VMEM((2, q_tile, LANES), F32)] * (2 if running_max else 1)
    grid_spec = pltpu.PrefetchScalarGridSpec(
        num_scalar_prefetch=4,
        grid=(MLA_HEADS // 2, n_steps),
        in_specs=[
            pl.BlockSpec((q_tile, pair_w), lambda p, s, qb, kb, fi, la: (qb[s], p)),
            pl.BlockSpec((kv_tile, pair_w), lambda p, s, qb, kb, fi, la: (kb[s], p)),
            pl.BlockSpec((kv_tile, pair_w), lambda p, s, qb, kb, fi, la: (kb[s], p)),
        ],
        out_specs=pl.BlockSpec((q_tile, 2 * MLA_V), lambda p, s, qb, kb, fi, la: (qb[s], p)),
        scratch_shapes=scratch,
    )
    return pl.pallas_call(
        functools.partial(_flash_kernel, running_max=running_max, score_dtype=score_dtype),
        grid_spec=grid_spec,
        out_shape=jax.ShapeDtypeStruct((t_tok, MLA_HEADS * MLA_V), BF16),
        compiler_params=_cparams(("parallel", "arbitrary")),
        name="mla_flash_max" if running_max else ("mla_flash" if score_dtype == BF16 else "mla_flash_f8"),
    )(*tabs, q, k, v)


def _memkv_kernel(mem_ref, gmem_ref, w_ref, gk_ref, k_ref, v_ref):
    x = mem_ref[...]
    h = (x * _rms_scale(x, D_MODEL) * gmem_ref[...]).astype(BF16)
    kv = _dot(h, w_ref[...])
    for hh in range(MEM_HEADS):
        sl = slice(hh * MEM_HEAD_DIM, (hh + 1) * MEM_HEAD_DIM)
        blk = kv[:, sl]
        k_ref[:, sl] = (blk * _rms_scale(blk, MEM_HEAD_DIM) * gk_ref[...]).astype(BF16)
    v_ref[...] = kv[:, MEM_WIDTH:].astype(BF16)


def _memory_kv(mem, lw):
    rows = mem.shape[0]
    tm = MEM_TOKENS
    return pl.pallas_call(
        _memkv_kernel,
        grid=(rows // tm,),
        in_specs=[pl.BlockSpec((tm, D_MODEL), lambda i: (i, 0)),
                  _const_spec(lw["g_mem"].shape), _const_spec(lw["w_mem_kv"].shape),
                  _const_spec(lw["g_mem_k"].shape)],
        out_specs=[pl.BlockSpec((tm, MEM_WIDTH), lambda i: (i, 0))] * 2,
        out_shape=[jax.ShapeDtypeStruct((rows, MEM_WIDTH), BF16)] * 2,
        compiler_params=_cparams(("parallel",)),
        name="memory_kv",
    )(mem, lw["g_mem"], lw["w_mem_kv"], lw["g_mem_k"])


def _gelu_tanh(x):
    return 0.5 * x * (1.0 + jnp.tanh(0.7978845608028654 * (x + 0.044715 * x * x * x)))


def _merge_kernel(*refs, with_router, n_x, n_first):
    x_refs = refs[:n_x]
    refs = refs[n_x:]
    if with_router:
        (ys_ref, ya_ref, qm_ref, km_ref, vm_ref, sg_ref, wglu_ref, bglu_ref, wb_ref,
         wo_ref, gffn_ref, wr_hi_ref, wr_lo_ref, x1_ref, h2_ref, route_ref) = refs
    else:
        (ys_ref, ya_ref, qm_ref, km_ref, vm_ref, sg_ref, wglu_ref, bglu_ref, wb_ref,
         wo_ref, gffn_ref, x1_ref, h2_ref) = refs

    ya = _gelu_tanh(ys_ref[...]).astype(BF16)
    z = _dot(ya, wglu_ref[...]) + bglu_ref[...]
    y_ssm = (z[:, :SSM_WIDTH] * _sigmoid(z[:, SSM_WIDTH:])).astype(BF16)

    mem_parts = []
    for hh in range(MEM_HEADS):
        sl = slice(hh * MEM_HEAD_DIM, (hh + 1) * MEM_HEAD_DIM)
        s = lax.dot_general(qm_ref[:, sl], km_ref[:, sl], (((1,), (1,)), ((), ())),
                            preferred_element_type=F32)
        p = jnp.exp(s - jnp.max(s, axis=1, keepdims=True))
        p = p / jnp.sum(p, axis=1, keepdims=True)
        mem_parts.append(_dot(p.astype(BF16), vm_ref[:, sl]))
    y_mem = jnp.concatenate(mem_parts, axis=1).astype(BF16)

    merged = None
    for b, yb in enumerate((y_ssm, ya_ref[...], y_mem)):
        gate = sg_ref[:, b * D_MODEL:(b + 1) * D_MODEL].astype(F32)
        term = gate * _dot(yb, wb_ref[b])
        merged = term if merged is None else merged + term

    x1 = _stream_read(x_refs, n_first) + _dot(merged.astype(BF16), wo_ref[...])
    x1_ref[...] = x1
    h2 = x1 * _rms_scale(x1, D_MODEL) * gffn_ref[...]

    if not with_router:
        h2_ref[...] = h2.astype(BF16)
        return

    h2_ref[...] = _rows_to_token_tiles(h2)

    h_hi, h_lo = _split_bf16(h2)
    logits = _dot(h_hi, wr_hi_ref[...]) + _dot(h_hi, wr_lo_ref[...]) + _dot(h_lo, wr_hi_ref[...])
    lane = lax.broadcasted_iota(jnp.int32, logits.shape, 1)
    lane_f = lane.astype(F32)
    neg = jnp.float32(-jnp.inf)
    big = jnp.float32(LANES)
    logits = jnp.where(lane < N_EXPERTS, logits, neg)
    m1 = jnp.max(logits, axis=1, keepdims=True)
    i1 = jnp.min(jnp.where(logits == m1, lane_f, big), axis=1, keepdims=True)
    rest = jnp.where(lane_f == i1, neg, logits)
    m2 = jnp.max(rest, axis=1, keepdims=True)
    i2 = jnp.min(jnp.where(rest == m2, lane_f, big), axis=1, keepdims=True)
    e2 = jnp.exp(m2 - m1)
    w1 = 1.0 / (1.0 + e2)
    w2 = e2 / (1.0 + e2)
    out = jnp.where(lane == 0, i1, jnp.where(lane == 1, i2, jnp.where(lane == 2, w1, w2)))
    route_ref[...] = jnp.where(lane < 4, out, 0.0)


def _merge(x_parts, y_ssm_raw, y_mla, qm, km, vm, sg, lw, tiles, with_router):
    t_tok = y_mla.shape[0]
    tm = tiles["tm"]
    seq_map = tiles["seq_map"]

    def row(i):
        return (i, 0)

    weights = [lw["w_glu"], lw["b_glu"], lw["w_branch"], lw["w_out"], lw["g_ffn"]]
    if with_router:
        weights += [lw["w_router_hi"], lw["w_router_lo"]]
    in_specs = _stream_specs(x_parts, tm, D_MODEL) + [
        pl.BlockSpec((tm, SSM_WIDTH), row),
        pl.BlockSpec((tm, MLA_HEADS * MLA_V), row),
        pl.BlockSpec((tm, MEM_WIDTH), row),
        pl.BlockSpec((MEM_TOKENS, MEM_WIDTH), lambda i: (seq_map(i), 0)),
        pl.BlockSpec((MEM_TOKENS, MEM_WIDTH), lambda i: (seq_map(i), 0)),
        pl.BlockSpec((tm, N_BRANCH * D_MODEL), row),
    ] + [_const_spec(w.shape) for w in weights]
    out_specs = [pl.BlockSpec((tm, D_MODEL), row)]
    out_shape = [jax.ShapeDtypeStruct((t_tok, D_MODEL), F32)]
    if with_router:
        out_specs += [pl.BlockSpec((tm, ROW_TILES, LANES), lambda i: (i, 0, 0)),
                      pl.BlockSpec((tm, LANES), row)]
        out_shape += [jax.ShapeDtypeStruct((t_tok, ROW_TILES, LANES), F32),
                      jax.ShapeDtypeStruct((t_tok, LANES), F32)]
    else:
        out_specs.append(pl.BlockSpec((tm, D_MODEL), row))
        out_shape.append(jax.ShapeDtypeStruct((t_tok, D_MODEL), BF16))
    kern = functools.partial(_merge_kernel, with_router=with_router, n_x=len(x_parts),
                             n_first=x_parts[0].shape[0] // tm)
    return pl.pallas_call(
        kern,
        grid=(t_tok // tm,),
        in_specs=in_specs,
        out_specs=out_specs,
        out_shape=out_shape,
        compiler_params=_cparams(("parallel",)),
        name="merge_router" if with_router else "merge",
    )(*x_parts, y_ssm_raw, y_mla, qm, km, vm, sg, *weights)


def _swiglu_acc(x, w1_ref, w3_ref, w2_ref, acc, chunk):
    d_ff = w1_ref.shape[1]
    for c in range(d_ff // chunk):
        sl = slice(c * chunk, (c + 1) * chunk)
        g = _dot(x, w1_ref[:, sl])
        u = _dot(x, w3_ref[:, sl])
        a = (g * _sigmoid(g) * u).astype(BF16)
        acc = acc + _dot(a, w2_ref[sl, :])
    return acc


def _ffn_kernel(h_ref, x_ref, w1_ref, w3_ref, w2_ref, o_ref):
    o_ref[...] = _swiglu_acc(h_ref[...], w1_ref, w3_ref, w2_ref, x_ref[...], FFN_CHUNK)


def _dense_ffn(h2, x1, lw, tiles):
    t_tok = x1.shape[0]
    tm = tiles["ffn_tile"]

    def row(i):
        return (i, 0)

    def resident(shape):
        return pl.BlockSpec(shape, lambda i: (0, 0), pipeline_mode=pl.Buffered(1))

    return pl.pallas_call(
        _ffn_kernel,
        grid=(t_tok // tm,),
        in_specs=[pl.BlockSpec((tm, D_MODEL), row), pl.BlockSpec((tm, D_MODEL), row),
                  resident(lw["ffn_w1"].shape), resident(lw["ffn_w3"].shape),
                  resident(lw["ffn_w2"].shape)],
        out_specs=pl.BlockSpec((tm, D_MODEL), row),
        out_shape=jax.ShapeDtypeStruct((t_tok, D_MODEL), F32),
        compiler_params=_cparams(("parallel",)),
        name="dense_ffn",
    )(h2, x1, lw["ffn_w1"], lw["ffn_w3"], lw["ffn_w2"])


def _dispatch_kernel(pos_ref, h_ref, xs_in_ref, xs_ref, sem):
    del xs_in_ref
    tm = h_ref.shape[0]

    def tile_copy(r, dst):
        return pltpu.make_async_copy(h_ref.at[r], xs_ref.at[dst], sem)

    def start(r, c):
        tile_copy(r, pos_ref[0, 0, 2 * r]).start(priority=0)
        tile_copy(r, pos_ref[0, 0, 2 * r + 1]).start(priority=1)
        return c

    def wait(r, c):
        tile_copy(r, 0).wait()
        tile_copy(r, 0).wait()
        return c

    lax.fori_loop(0, tm, start, 0, unroll=DMA_UNROLL)
    lax.fori_loop(0, tm, wait, 0, unroll=DMA_UNROLL)


def _dispatch(h2t, pos3, n_rows, tm):
    t_tok = h2t.shape[0]
    xs0 = jnp.zeros((n_rows, ROW_TILES, LANES), F32)
    return pl.pallas_call(
        _dispatch_kernel,
        grid=(t_tok // tm,),
        in_specs=[pl.BlockSpec((1, 1, 2 * tm), lambda i: (i, 0, 0), memory_space=pltpu.SMEM),
                  pl.BlockSpec((tm, ROW_TILES, LANES), lambda i: (i, 0, 0)),
                  pl.BlockSpec(memory_space=pl.ANY)],
        out_specs=pl.BlockSpec(memory_space=pl.ANY),
        out_shape=jax.ShapeDtypeStruct(xs0.shape, F32),
        input_output_aliases={2: 0},
        scratch_shapes=[pltpu.SemaphoreType.DMA(())],
        compiler_params=_cparams(("arbitrary",)),
        name="moe_dispatch",
    )(pos3, h2t, xs0)


def _expert_kernel(be_ref, nu_ref, x_ref, w1_ref, w3_ref, w2_ref, y_ref):
    del be_ref
    i = pl.program_id(0)

    @pl.when(i < nu_ref[0])
    def _():
        x = jnp.concatenate(_token_tiles_to_rows(x_ref[...]), axis=1).astype(BF16)
        acc = _swiglu_acc(x, w1_ref, w3_ref, w2_ref, jnp.zeros((x.shape[0], D_MODEL), F32), MOE_CHUNK)
        y_ref[...] = _rows_to_token_tiles(acc)

    @pl.when(i >= nu_ref[0])
    def _():
        y_ref[...] = jnp.zeros(y_ref.shape, F32)


def _expert_ffn(xs, blk_expert, n_used, lw, bm):
    n_rows = xs.shape[0]

    def w_spec(shape):
        return pl.BlockSpec((None,) + shape, lambda i, be, nu: (be[i], 0, 0),
                            pipeline_mode=pl.Buffered(1))

    grid_spec = pltpu.PrefetchScalarGridSpec(
        num_scalar_prefetch=2,
        grid=(n_rows // bm,),
        in_specs=[
            pl.BlockSpec((bm, ROW_TILES, LANES), lambda i, be, nu: (i, 0, 0)),
            w_spec((D_MODEL, D_FF_EXPERT)), w_spec((D_MODEL, D_FF_EXPERT)),
            w_spec((D_FF_EXPERT, D_MODEL)),
        ],
        out_specs=pl.BlockSpec((bm, ROW_TILES, LANES), lambda i, be, nu: (i, 0, 0)),
    )
    return pl.pallas_call(
        _expert_kernel,
        grid_spec=grid_spec,
        out_shape=jax.ShapeDtypeStruct(xs.shape, F32),
        compiler_params=_cparams(("arbitrary",)),
        name="moe_experts",
    )(blk_expert, n_used, xs, lw["moe_w1"], lw["moe_w3"], lw["moe_w2"])


def _combine_kernel(pos_ref, pos_next_ref, ys_hbm, x_ref, route_ref, *refs, n_first):
    out_refs, (buf_ref, sem) = refs[:-2], refs[-2:]
    tm = x_ref.shape[0]
    i = pl.program_id(0)
    n = pl.num_programs(0)
    slot = i % 2

    def tile_copy(s, k, r, src):
        return pltpu.make_async_copy(ys_hbm.at[src], buf_ref.at[s, k, r], sem.at[s])

    def issue(p_ref, s):
        def start(r, c):
            tile_copy(s, 0, r, p_ref[0, 0, 2 * r]).start(priority=0)
            tile_copy(s, 1, r, p_ref[0, 0, 2 * r + 1]).start(priority=1)
            return c
        lax.fori_loop(0, tm, start, 0, unroll=DMA_UNROLL)

    @pl.when(i == 0)
    def _():
        issue(pos_ref, slot)

    @pl.when(i + 1 < n)
    def _():
        issue(pos_next_ref, 1 - slot)

    def wait(r, c):
        tile_copy(slot, 0, r, 0).wait()
        tile_copy(slot, 1, r, 0).wait()
        return c

    lax.fori_loop(0, tm, wait, 0, unroll=DMA_UNROLL)
    w1 = route_ref[:, 2:3]
    w2 = route_ref[:, 3:4]

    def write(o_ref):
        y1 = _token_tiles_to_rows(buf_ref[slot, 0])
        y2 = _token_tiles_to_rows(buf_ref[slot, 1])
        for j in range(ROW_TILES):
            sl = slice(j * LANES, (j + 1) * LANES)
            o_ref[:, sl] = x_ref[:, sl] + w1 * y1[j] + w2 * y2[j]

    if len(out_refs) == 1:
        write(out_refs[0])
    else:
        pl.when(i < n_first)(lambda: write(out_refs[0]))
        pl.when(i >= n_first)(lambda: write(out_refs[1]))


def _combine(ys, pos3, x1, route, tm, out_rows):
    t_tok = x1.shape[0]
    n_tiles = t_tok // tm
    n_first = out_rows[0] // tm
    if len(out_rows) == 1:
        out_specs = [pl.BlockSpec((tm, D_MODEL), lambda i: (i, 0))]
    else:
        out_specs = [pl.BlockSpec((tm, D_MODEL), lambda i: (jnp.minimum(i, n_first - 1), 0)),
                     pl.BlockSpec((tm, D_MODEL), lambda i: (jnp.maximum(i - n_first, 0), 0))]
    pos_spec = functools.partial(pl.BlockSpec, (1, 1, 2 * tm), memory_space=pltpu.SMEM)
    return pl.pallas_call(
        functools.partial(_combine_kernel, n_first=n_first),
        grid=(n_tiles,),
        in_specs=[pos_spec(lambda i: (i, 0, 0)),
                  pos_spec(lambda i: (jnp.minimum(i + 1, n_tiles - 1), 0, 0)),
                  pl.BlockSpec(memory_space=pl.ANY),
                  pl.BlockSpec((tm, D_MODEL), lambda i: (i, 0)),
                  pl.BlockSpec((tm, LANES), lambda i: (i, 0))],
        out_specs=out_specs,
        out_shape=[jax.ShapeDtypeStruct((r, D_MODEL), F32) for r in out_rows],
        scratch_shapes=[pltpu.VMEM((2, 2, tm, ROW_TILES, LANES), F32), pltpu.SemaphoreType.DMA((2,))],
        compiler_params=_cparams(("arbitrary",)),
        name="moe_combine",
    )(pos3, pos3, ys, x1, route)


def _moe_ffn(h2t, x1, route, lw, tiles, out_rows):
    t_tok = x1.shape[0]
    tm = tiles["tm"]
    bm = tiles["moe_block"]
    n_assign = 2 * t_tok
    experts = route[:, :2].astype(jnp.int32).reshape(n_assign)
    onehot = (experts[:, None] == jnp.arange(N_EXPERTS, dtype=jnp.int32)[None, :]).astype(jnp.int32)
    csum = jnp.cumsum(onehot, axis=0)
    rank = jnp.sum((csum - 1) * onehot, axis=1)
    counts = csum[-1]
    padded = ((counts + bm - 1) // bm) * bm
    ends = jnp.cumsum(padded)
    starts = ends - padded
    pos = jnp.sum(onehot * starts[None, :], axis=1) + rank
    n_blocks = n_assign // bm + N_EXPERTS
    blk_start = jnp.arange(n_blocks, dtype=jnp.int32) * bm
    blk_expert = jnp.sum((blk_start[:, None] >= ends[None, :]).astype(jnp.int32), axis=1)
    blk_expert = jnp.minimum(blk_expert, N_EXPERTS - 1)
    n_used = (ends[-1:] // bm).astype(jnp.int32)
    pos3 = pos.astype(jnp.int32).reshape(t_tok // tm, 1, 2 * tm)

    xs = _dispatch(h2t, pos3, n_blocks * bm, tm)
    ys = _expert_ffn(xs, blk_expert, n_used, lw, bm)
    return _combine(ys, pos3, x1, route, tm, out_rows)


def _head_cols(w, per_head, n_heads, offset=0):
    k = w.shape[0]
    w = w.reshape(k, n_heads, per_head)
    w = jnp.pad(w, ((0, 0), (0, 0), (offset, HEAD_PAD - per_head - offset)))
    return w.reshape(k, n_heads * HEAD_PAD)


def _swap_rope(w):
    z = jnp.zeros_like(w[..., :MLA_NOPE])
    return jnp.concatenate([z, w[..., MLA_NOPE + HALF_ROPE:], w[..., MLA_NOPE:MLA_NOPE + HALF_ROPE]], axis=-1)


def _pad_lane_block(v, offset=0):
    return jnp.pad(v, ((0, 0),) * (v.ndim - 1) + ((offset, LANES - v.shape[-1] - offset),))


def _layer_weights(i, p, rope_cos, rope_sin):
    o = IN_OFFSETS
    w_in = p["w_in"][i]
    lw = {}
    lw["g_mix"] = p["g_mix"][i].reshape(1, D_MODEL)
    lw["w_u"] = w_in[:, :o[0]].astype(BF16)
    lw["w_cq"] = w_in[:, o[0]:o[1]].astype(BF16)
    w_kr = w_in[:, o[2]:o[3]]
    w_kr_sw = jnp.concatenate([w_kr[:, HALF_ROPE:], w_kr[:, :HALF_ROPE]], axis=1)
    lw["w_ckv"] = jnp.concatenate([w_in[:, o[1]:o[2]], _pad_lane_block(w_kr, MLA_NOPE),
                                   _pad_lane_block(w_kr_sw, MLA_NOPE)], axis=1).astype(BF16)
    lw["w_qm"] = w_in[:, o[3]:o[4]].astype(BF16)
    lw["w_gate"] = w_in[:, o[4]:].astype(BF16)
    lw["g_q_lora"] = p["g_q_lora"][i].reshape(1, Q_LORA)
    w_uq = p["w_uq"][i].reshape(Q_LORA, MLA_HEADS, MLA_QK)
    lw["w_uq"] = jnp.concatenate(
        [_head_cols(w_uq.reshape(Q_LORA, -1), MLA_QK, MLA_HEADS),
         _head_cols(_swap_rope(w_uq).reshape(Q_LORA, -1), MLA_QK, MLA_HEADS)], axis=1).astype(BF16)
    lw["g_kv_lora"] = p["g_kv_lora"][i].reshape(1, KV_LORA)
    w_ukv = p["w_ukv"][i].reshape(KV_LORA, MLA_HEADS, MLA_NOPE + MLA_V)
    lw["w_kn"] = _head_cols(w_ukv[:, :, :MLA_NOPE].reshape(KV_LORA, -1), MLA_NOPE, MLA_HEADS).astype(BF16)
    w_v = w_ukv[:, :, MLA_NOPE:].reshape(KV_LORA, MLA_HEADS // 2, 2, MLA_V)
    w_v = jnp.stack([jnp.pad(w_v[:, :, 0], ((0, 0), (0, 0), (0, HEAD_PAD - MLA_V))),
                     jnp.pad(w_v[:, :, 1], ((0, 0), (0, 0), (HEAD_PAD - MLA_V, 0)))], axis=2)
    lw["w_v"] = w_v.reshape(KV_LORA, MLA_WIDTH_PAD).astype(BF16)
    one_even = jnp.zeros((HEAD_PAD,), F32).at[MLA_V].set(1.0)
    one_odd = jnp.zeros((HEAD_PAD,), F32).at[0].set(1.0)
    lw["v_one"] = jnp.tile(jnp.concatenate([one_even, one_odd]), MLA_HEADS // 2).reshape(1, MLA_WIDTH_PAD)
    for name, g in (("q", p["g_mla_q"][i]), ("k", p["g_mla_k"][i])):
        lw["rope_a" + name] = rope_cos * _pad_lane_block(g)[None, :]
        lw["rope_b" + name] = rope_sin * _pad_lane_block(_swap_rope(g))[None, :]
    lw["g_mem_q"] = jnp.tile(p["g_mem_q"][i], MEM_HEADS).reshape(1, MEM_WIDTH) * (MEM_HEAD_DIM ** -0.5)
    lw["g_mem"] = p["g_mem"][i].reshape(1, D_MODEL)
    lw["w_mem_kv"] = p["w_mem_kv"][i].astype(BF16)
    lw["g_mem_k"] = p["g_mem_k"][i].reshape(1, MEM_HEAD_DIM)
    lw["w_glu"] = p["ssm_w_glu"][i].astype(BF16)
    lw["b_glu"] = p["ssm_b_glu"][i].reshape(1, 2 * SSM_WIDTH)
    lw["w_branch"] = p["w_branch"][i].astype(BF16)
    lw["w_out"] = p["w_out"][i].astype(BF16)
    lw["g_ffn"] = p["g_ffn"][i].reshape(1, D_MODEL)
    lw["ssm"] = _ssm_tables(p["ssm_a_re"][i], p["ssm_a_im"][i], p["ssm_log_dt"][i], p["ssm_b_re"][i],
                            p["ssm_b_im"][i], p["ssm_c_re"][i], p["ssm_c_im"][i], p["ssm_d"][i])
    lw["score_bound"] = (1.02 * MLA_QK ** 0.5 * LOG2E
                         * jnp.max(jnp.abs(p["g_mla_q"][i])) * jnp.max(jnp.abs(p["g_mla_k"][i])))
    if i % 2 == 0:
        w13 = p["ffn_w13"][i // 2]
        lw["ffn_w1"] = w13[:, :D_FF].astype(BF16)
        lw["ffn_w3"] = w13[:, D_FF:].astype(BF16)
        lw["ffn_w2"] = p["ffn_w2"][i // 2].astype(BF16)
    else:
        w13 = p["moe_w13"][i // 2]
        lw["moe_w1"] = w13[:, :, :D_FF_EXPERT].astype(BF16)
        lw["moe_w3"] = w13[:, :, D_FF_EXPERT:].astype(BF16)
        lw["moe_w2"] = p["moe_w2"][i // 2].astype(BF16)
        wr = jnp.pad(p["moe_router"][i // 2], ((0, 0), (0, LANES - N_EXPERTS)))
        lw["w_router_hi"], lw["w_router_lo"] = _split_bf16(wr)
    return lw


def _rope_tables(max_len):
    inv = 1.0 / (ROPE_BASE ** (jnp.arange(0, MLA_ROPE, 2, dtype=F32) / MLA_ROPE))
    ang = jnp.arange(max_len, dtype=F32)[:, None] * inv[None, :]
    cos, sin = jnp.cos(ang), jnp.sin(ang)
    ones = jnp.ones((max_len, MLA_NOPE), F32)
    zn = jnp.zeros((max_len, MLA_NOPE), F32)
    zp = jnp.zeros((max_len, HEAD_PAD - MLA_QK), F32)
    return (jnp.concatenate([ones, cos, cos, zp], axis=1),
            jnp.concatenate([zn, -sin, sin, zp], axis=1))


def _gcd_tile(limit, *sizes):
    t = limit
    while any(s % t for s in sizes):
        t //= 2
    return t


def kernel(x_prompt, x_sample, mem_prompt, mem_sample, g_mix, w_in, ssm_a_re, ssm_a_im, ssm_log_dt, ssm_b_re, ssm_b_im, ssm_c_re, ssm_c_im, ssm_d, ssm_w_glu, ssm_b_glu, g_q_lora, w_uq, g_kv_lora, w_ukv, g_mla_q, g_mla_k, g_mem, w_mem_kv, g_mem_q, g_mem_k, w_branch, w_out, g_ffn, ffn_w13, ffn_w2, moe_router, moe_w13, moe_w2):
    params = dict(g_mix=g_mix, w_in=w_in, ssm_a_re=ssm_a_re, ssm_a_im=ssm_a_im, ssm_log_dt=ssm_log_dt,
                  ssm_b_re=ssm_b_re, ssm_b_im=ssm_b_im, ssm_c_re=ssm_c_re, ssm_c_im=ssm_c_im, ssm_d=ssm_d,
                  ssm_w_glu=ssm_w_glu, ssm_b_glu=ssm_b_glu, g_q_lora=g_q_lora, w_uq=w_uq,
                  g_kv_lora=g_kv_lora, w_ukv=w_ukv, g_mla_q=g_mla_q, g_mla_k=g_mla_k, g_mem=g_mem,
                  w_mem_kv=w_mem_kv, g_mem_q=g_mem_q, g_mem_k=g_mem_k, w_branch=w_branch, w_out=w_out,
                  g_ffn=g_ffn, ffn_w13=ffn_w13, ffn_w2=ffn_w2, moe_router=moe_router, moe_w13=moe_w13,
                  moe_w2=moe_w2)
    depth = g_mix.shape[0]
    bp, lp, _ = x_prompt.shape
    bs, ls, _ = x_sample.shape
    tp, ts = bp * lp, bs * ls
    t_tok = tp + ts

    tm = _gcd_tile(TOKEN_TILE, lp, ls)
    attn_q_tile = _gcd_tile(ATTN_Q_TILE, lp, ls)
    attn_kv_tile = _gcd_tile(ATTN_KV_TILE, lp, ls)
    ffn_tile = _gcd_tile(FFN_TILE, lp, ls)
    npt = tp // tm
    lp_t, ls_t = lp // tm, ls // tm

    def pos_map(i):
        return jnp.where(i < npt, i % lp_t, (i - npt) % ls_t)

    def seq_map(i):
        return jnp.where(i < npt, i // lp_t, bp + (i - npt) // ls_t)

    rope_cos, rope_sin = _rope_tables(max(lp, ls))
    head_ind = (jnp.arange(MLA_WIDTH_PAD)[:, None] // HEAD_PAD == jnp.arange(LANES)[None, :])
    mem_bd = (jnp.arange(MEM_WIDTH)[:, None] // MEM_HEAD_DIM == jnp.arange(MEM_WIDTH)[None, :] // MEM_HEAD_DIM)
    tiles = dict(tm=tm, ffn_tile=ffn_tile, pos_map=pos_map, seq_map=seq_map,
                 ones=jnp.ones((LANES, LANES), BF16),
                 head_ind=head_ind.astype(BF16),
                 head_ind_t=jnp.concatenate([head_ind.T, head_ind.T], axis=0).astype(BF16),
                 mem_bd=mem_bd.astype(BF16),
                 moe_block=_gcd_tile(MOE_BLOCK, 2 * t_tok))

    x_parts = [x_prompt.reshape(tp, D_MODEL), x_sample.reshape(ts, D_MODEL)]
    mem = jnp.concatenate([mem_prompt.reshape(bp * MEM_TOKENS, D_MODEL),
                           mem_sample.reshape(bs * MEM_TOKENS, D_MODEL)], axis=0)
    seqs = [(bp, lp), (bs, ls)]

    for i in range(depth):
        lw = _layer_weights(i, params, rope_cos, rope_sin)
        u, q, k, v, qm, sg = _in_projection(x_parts, lw, tiles)

        y_ssm_raw = _ssm_call(u, lw["ssm"], seqs)

        def attend(running_max, score_dtype):
            return lambda q_, k_, v_: _flash_attention(q_, k_, v_, seqs, attn_q_tile, attn_kv_tile,
                                                       running_max, score_dtype)

        y_mla = lax.cond(
            lw["score_bound"] <= MAX_FP8_LOG2_SCORE,
            attend(False, jnp.float8_e4m3fn),
            lambda q_, k_, v_: lax.cond(lw["score_bound"] <= MAX_SAFE_LOG2_SCORE,
                                        attend(False, BF16), attend(True, BF16), q_, k_, v_),
            q, k, v)
        km, vm = _memory_kv(mem, lw)

        with_router = i % 2 == 1
        outs = _merge(x_parts, y_ssm_raw, y_mla, qm, km, vm, sg, lw, tiles, with_router)
        if with_router:
            x1, h2t, route = outs
            out_rows = (tp, ts) if i == depth - 1 else (t_tok,)
            x_parts = _moe_ffn(h2t, x1, route, lw, tiles, out_rows)
        else:
            x1, h2 = outs
            x_parts = [_dense_ffn(h2, x1, lw, tiles)]

    if len(x_parts) == 1:
        x_parts = [x_parts[0][:tp], x_parts[0][tp:]]
    return (x_parts[0].reshape(bp, lp, D_MODEL), x_parts[1].reshape(bs, ls, D_MODEL))
```

```python
import functools
import math

import numpy as np
import jax
import jax.numpy as jnp
from jax import lax
from jax.experimental import pallas as pl
from jax.experimental.pallas import tpu as pltpu

F32 = jnp.float32
BF16 = jnp.bfloat16
EPS = 1e-6
LOG2E = math.log2(math.e)

D_MODEL = 1024
SSM_WIDTH = 512
SSM_GROUPS = 32
SSM_GROUP = 16
SSM_STATE = 64
MLA_HEADS = 8
MLA_NOPE = 64
MLA_ROPE = 32
MLA_V = 64
MLA_QK = MLA_NOPE + MLA_ROPE
Q_LORA = 256
KV_LORA = 128
ROPE_BASE = 10000.0
MEM_TOKENS = 256
MEM_HEADS = 4
MEM_HEAD_DIM = 128
MEM_WIDTH = 512
N_BRANCH = 3
D_FF = 2816
N_EXPERTS = 8
D_FF_EXPERT = 3584
IN_SIZES = (SSM_WIDTH, Q_LORA, KV_LORA, MLA_ROPE, MEM_WIDTH, N_BRANCH * D_MODEL)
IN_OFFSETS = tuple(int(v) for v in np.cumsum(IN_SIZES)[:-1])

LANES = 128
SUBLANES = 8
HEAD_PAD = 128
HALF_ROPE = MLA_ROPE // 2
MLA_WIDTH_PAD = MLA_HEADS * HEAD_PAD
ROW_TILES = D_MODEL // LANES
VMEM_LIMIT = 56 * 1024 * 1024
MAX_SAFE_LOG2_SCORE = 100.0
MAX_FP8_LOG2_SCORE = 20.0

TOKEN_TILE = 512
ATTN_Q_TILE = 2048
ATTN_KV_TILE = 1024
FFN_TILE = 1024
SSM_CHUNK = 16
MOE_BLOCK = 512
FFN_CHUNK = 256
MOE_CHUNK = 512
DMA_UNROLL = 8


def _cparams(sem):
    return pltpu.CompilerParams(dimension_semantics=sem, vmem_limit_bytes=VMEM_LIMIT)


def _const_spec(shape):
    nd = len(shape)
    return pl.BlockSpec(shape, lambda *_: (0,) * nd)


def _sigmoid(x):
    return 1.0 / (1.0 + jnp.exp(-x))


def _rms_scale(x, n):
    return lax.rsqrt(jnp.sum(x * x, axis=-1, keepdims=True) * (1.0 / n) + EPS)


def _split_bf16(x):
    hi = x.astype(BF16)
    lo = (x - hi.astype(F32)).astype(BF16)
    return hi, lo


def _dot(a, b):
    return jnp.dot(a, b, preferred_element_type=F32)


def _tile_transpose(vs):
    sub = lax.broadcasted_iota(jnp.int32, vs[0].shape, 1)
    for dist in (4, 2, 1):
        low = (sub // dist) % 2 == 0
        new = list(vs)
        for a in range(SUBLANES):
            if a & dist == 0:
                b = a | dist
                new[a] = jnp.where(low, vs[a], pltpu.roll(vs[b], dist, 1))
                new[b] = jnp.where(low, pltpu.roll(vs[a], SUBLANES - dist, 1), vs[b])
        vs = new
    return vs


def _rows_to_token_tiles(x):
    rows = x.shape[0]
    cols = [x[:, j * LANES:(j + 1) * LANES].reshape(rows // SUBLANES, SUBLANES, LANES) for j in range(ROW_TILES)]
    return jnp.stack(_tile_transpose(cols), axis=1).reshape(rows, ROW_TILES, LANES)


def _token_tiles_to_rows(t):
    rows = t.shape[0]
    t4 = t.reshape(rows // SUBLANES, SUBLANES, ROW_TILES, LANES)
    cols = _tile_transpose([t4[:, k] for k in range(SUBLANES)])
    return [c.reshape(rows, LANES) for c in cols]


def _stream_specs(parts, tm, width):
    if len(parts) == 1:
        return [pl.BlockSpec((tm, width), lambda i: (i, 0))]
    n0 = parts[0].shape[0] // tm
    return [pl.BlockSpec((tm, width), lambda i: (jnp.minimum(i, n0 - 1), 0)),
            pl.BlockSpec((tm, width), lambda i: (jnp.maximum(i - n0, 0), 0))]


def _stream_read(refs, n_first):
    if len(refs) == 1:
        return refs[0][...]
    return jnp.where(pl.program_id(0) < n_first, refs[0][...], refs[1][...])


def _inproj_kernel(*refs, n_x, n_first):
    x_refs = refs[:n_x]
    (gmix_ref, wu_ref, wg_ref, wqm_ref, wcq_ref, wckv_ref,
     gql_ref, wuq_ref, gkvl_ref, wkn_ref, wv_ref, vone_ref, gmq_ref,
     ones_ref, ind_ref, indt_ref, bd_ref,
     aq_ref, bq_ref, ak_ref, bk_ref,
     u_ref, q_ref, k_ref, v_ref, qm_ref, sg_ref) = refs[n_x:]

    def row_ssq(val):
        sq = val * val
        part = sq[:, :LANES]
        for j in range(1, val.shape[1] // LANES):
            part = part + sq[:, j * LANES:(j + 1) * LANES]
        return _dot(part.astype(BF16), ones_ref[...])

    def head_scale(val, scale):
        ss = _dot((val * val).astype(BF16), ind_ref[...])
        r = lax.rsqrt(ss * (1.0 / MLA_QK) + EPS) * scale
        return _dot(jnp.concatenate(_split_bf16(r), axis=1), indt_ref[...])

    x = _stream_read(x_refs, n_first)
    rs = lax.rsqrt(row_ssq(x) * (1.0 / D_MODEL) + EPS)
    h = (x * jnp.tile(rs, (1, D_MODEL // LANES)) * gmix_ref[...]).astype(BF16)

    u_ref[...] = _dot(h, wu_ref[...])

    for b in range(N_BRANCH):
        sl = slice(b * D_MODEL, (b + 1) * D_MODEL)
        sg_ref[:, sl] = _sigmoid(_dot(h, wg_ref[:, sl])).astype(BF16)

    zq = _dot(h, wqm_ref[...])
    zr = lax.rsqrt(_dot((zq * zq).astype(BF16), bd_ref[...]) * (1.0 / MEM_HEAD_DIM) + EPS)
    qm_ref[...] = (zq * zr * gmq_ref[...]).astype(BF16)

    cq = _dot(h, wcq_ref[...])
    cq_rs = lax.rsqrt(row_ssq(cq) * (1.0 / Q_LORA) + EPS)
    cqn = (cq * jnp.tile(cq_rs, (1, Q_LORA // LANES)) * gql_ref[...]).astype(BF16)
    qq = _dot(cqn, wuq_ref[...])
    qf = qq[:, :MLA_WIDTH_PAD]
    qsw = qq[:, MLA_WIDTH_PAD:]

    z3 = _dot(h, wckv_ref[...])
    ckv = z3[:, :LANES]
    kr = z3[:, LANES:2 * LANES]
    kr_sw = z3[:, 2 * LANES:]
    ckv_rs = lax.rsqrt(row_ssq(ckv) * (1.0 / KV_LORA) + EPS)
    ckvn = (ckv * ckv_rs * gkvl_ref[...]).astype(BF16)
    kf = _dot(ckvn, wkn_ref[...]) + jnp.tile(kr, (1, MLA_HEADS))
    v_ref[...] = (_dot(ckvn, wv_ref[...]) + vone_ref[...]).astype(BF16)

    q_rs = head_scale(qf, MLA_QK ** -0.5 * LOG2E)
    k_rs = head_scale(kf, 1.0)
    aq = aq_ref[...]
    bq = bq_ref[...]
    ak = ak_ref[...]
    k_rot = kr_sw * bk_ref[...]
    for hh in range(MLA_HEADS):
        sl = slice(hh * HEAD_PAD, (hh + 1) * HEAD_PAD)
        q_ref[:, sl] = ((qf[:, sl] * aq + qsw[:, sl] * bq) * q_rs[:, sl]).astype(BF16)
        k_ref[:, sl] = ((kf[:, sl] * ak + k_rot) * k_rs[:, sl]).astype(BF16)


def _in_projection(x_parts, lw, tiles):
    t_tok = sum(p.shape[0] for p in x_parts)
    tm = tiles["tm"]
    pos_map = tiles["pos_map"]

    def row(i):
        return (i, 0)

    weights = [lw["g_mix"], lw["w_u"], lw["w_gate"], lw["w_qm"], lw["w_cq"], lw["w_ckv"],
               lw["g_q_lora"], lw["w_uq"], lw["g_kv_lora"], lw["w_kn"], lw["w_v"], lw["v_one"],
               lw["g_mem_q"], tiles["ones"], tiles["head_ind"], tiles["head_ind_t"], tiles["mem_bd"]]
    rope = [lw["rope_aq"], lw["rope_bq"], lw["rope_ak"], lw["rope_bk"]]
    in_specs = _stream_specs(x_parts, tm, D_MODEL) + [_const_spec(w.shape) for w in weights]
    in_specs += [pl.BlockSpec((tm, HEAD_PAD), lambda i: (pos_map(i), 0))] * len(rope)
    kern = functools.partial(_inproj_kernel, n_x=len(x_parts), n_first=x_parts[0].shape[0] // tm)
    out_widths = (SSM_WIDTH, MLA_WIDTH_PAD, MLA_WIDTH_PAD, MLA_WIDTH_PAD, MEM_WIDTH, N_BRANCH * D_MODEL)
    return pl.pallas_call(
        kern,
        grid=(t_tok // tm,),
        in_specs=in_specs,
        out_specs=[pl.BlockSpec((tm, w), row) for w in out_widths],
        out_shape=[jax.ShapeDtypeStruct((t_tok, w), F32 if j == 0 else BF16) for j, w in enumerate(out_widths)],
        compiler_params=_cparams(("parallel",)),
        name="in_projection",
    )(*x_parts, *weights, *rope)


SSM_LANE_GROUPS = LANES // SSM_GROUP
SSM_TAB_DIR = 11
(_TAB_STEP, _TAB_POW, _TAB_A8, _TAB_MASK) = (0, 6, 8, 10)


def _granule_transpose(vs, lane):
    for dist in (4, 2, 1):
        shift = dist * SSM_GROUP
        low = (lane // shift) % 2 == 0
        new = list(vs)
        for a in range(SSM_LANE_GROUPS):
            if a & dist == 0:
                b = a | dist
                new[a] = jnp.where(low, vs[a], pltpu.roll(vs[b], shift, 1))
                new[b] = jnp.where(low, pltpu.roll(vs[a], LANES - shift, 1), vs[b])
        vs = new
    return vs


def _chunk_scan(x, xs, tab_ref, base, keep, backward):
    nv, _, nl = x.shape
    for j, k in enumerate((1, 2, 4)):
        shift = SUBLANES - k if backward else k
        rx = pltpu.roll(x, shift, 1)
        rxs = pltpu.roll(xs, shift, 1)
        a1 = tab_ref[base + _TAB_STEP + 2 * j][None]
        a2 = tab_ref[base + _TAB_STEP + 2 * j + 1][None]
        x, xs = x + a1 * rx + a2 * rxs, xs + a1 * rxs - a2 * rx
    a1 = tab_ref[base + _TAB_A8]
    a2 = tab_ref[base + _TAB_A8 + 1]
    edge = 0 if backward else SUBLANES - 1
    c = jnp.zeros((SUBLANES, nl), F32)
    cs = c
    cb = [None] * nv
    csb = [None] * nv
    for v in (range(nv - 1, -1, -1) if backward else range(nv)):
        kp = keep(v)
        c = c * kp
        cs = cs * kp
        cb[v] = c
        csb[v] = cs
        ex = jnp.broadcast_to(x[v, edge:edge + 1, :], (SUBLANES, nl))
        exs = jnp.broadcast_to(xs[v, edge:edge + 1, :], (SUBLANES, nl))
        c, cs = ex + a1 * c + a2 * cs, exs + a1 * cs - a2 * c
    xe = pltpu.roll(x, SUBLANES - 1 if backward else 1, 1) * tab_ref[base + _TAB_MASK][None]
    return (xe + tab_ref[base + _TAB_POW][None] * jnp.stack(cb)
            + tab_ref[base + _TAB_POW + 1][None] * jnp.stack(csb))


def _ssm_kernel(keepf_ref, keepb_ref, u_ref, t_ref, win_ref, wout_ref, tab_ref, y_ref):
    blk = pl.program_id(1)
    rows = u_ref.shape[0] // SSM_CHUNK
    nv = rows // SUBLANES
    ng = SSM_LANE_GROUPS
    lane = lax.broadcasted_iota(jnp.int32, (rows, LANES), 1)

    pieces = [u_ref[pl.ds(t, rows, stride=SSM_CHUNK), :] for t in range(SSM_CHUNK)]
    halves = [_granule_transpose(pieces[ng * h:ng * (h + 1)], lane) for h in range(SSM_CHUNK // ng)]

    ys = []
    parts = [[], [], [], []]
    for g in range(ng):
        xg = jnp.concatenate([h[g] for h in halves], axis=1).astype(BF16)
        ys.append(_dot(xg, t_ref[g]))
        s4 = _dot(xg, win_ref[g])
        for j in range(4):
            parts[j].append(s4[:, j * LANES:(j + 1) * LANES])
    f, fs, b, bs = [jnp.concatenate(p, axis=1).reshape(nv, SUBLANES, ng * LANES) for p in parts]

    xin_f = _chunk_scan(f, fs, tab_ref, 0, lambda v: keepf_ref[blk * nv + v].astype(F32), False)
    xin_b = _chunk_scan(b, bs, tab_ref, SSM_TAB_DIR, lambda v: keepb_ref[blk * nv + v].astype(F32), True)
    xin_f = xin_f.reshape(rows, ng * LANES)
    xin_b = xin_b.reshape(rows, ng * LANES)

    outs = []
    for g in range(ng):
        sl = slice(g * LANES, (g + 1) * LANES)
        xs = jnp.concatenate([xin_f[:, sl], xin_b[:, sl]], axis=1).astype(BF16)
        outs.append(ys[g] + _dot(xs, wout_ref[g]))
    for h in range(SSM_CHUNK // ng):
        back = _granule_transpose([o[:, h * LANES:(h + 1) * LANES] for o in outs], lane)
        for tt in range(ng):
            y_ref[pl.ds(ng * h + tt, rows, stride=SSM_CHUNK), :] = back[tt]


def _ssm_call(u, tabs, seqs):
    t_tok = u.shape[0]
    block = int(np.lcm.reduce([ln for _, ln in seqs]))
    vreg_tokens = SUBLANES * SSM_CHUNK
    assert all((n * ln) % block == 0 and ln % vreg_tokens == 0 for n, ln in seqs), seqs
    nv = block // vreg_tokens
    keep_f, keep_b = [], []
    for n_seq, seq_len in seqs:
        for _ in range(n_seq * seq_len // block):
            for v in range(nv):
                keep_f.append(int((v * vreg_tokens) % seq_len != 0))
                keep_b.append(int(((v + 1) * vreg_tokens) % seq_len != 0))
    keep = [jnp.asarray(np.asarray(a, np.int32)) for a in (keep_f, keep_b)]
    ng = SSM_LANE_GROUPS
    width = SSM_CHUNK * SSM_GROUP
    n_tab = 2 * SSM_TAB_DIR
    grid_spec = pltpu.PrefetchScalarGridSpec(
        num_scalar_prefetch=2,
        grid=(SSM_WIDTH // LANES, t_tok // block),
        in_specs=[
            pl.BlockSpec((block, LANES), lambda g, b, kf, kb: (b, g)),
            pl.BlockSpec((ng, width, width), lambda g, b, kf, kb: (g, 0, 0)),
            pl.BlockSpec((ng, width, 4 * LANES), lambda g, b, kf, kb: (g, 0, 0)),
            pl.BlockSpec((ng, 2 * LANES, width), lambda g, b, kf, kb: (g, 0, 0)),
            pl.BlockSpec((n_tab, SUBLANES, ng * LANES), lambda g, b, kf, kb: (0, 0, g)),
        ],
        out_specs=pl.BlockSpec((block, LANES), lambda g, b, kf, kb: (b, g)),
    )
    return pl.pallas_call(
        _ssm_kernel,
        grid_spec=grid_spec,
        out_shape=jax.ShapeDtypeStruct(u.shape, F32),
        compiler_params=_cparams(("parallel", "parallel")),
        name="ssm_chunked",
    )(*keep, u, tabs["t"], tabs["w_in"], tabs["w_out"], tabs["scan"])


def _ssm_tables(a_re, a_im, log_dt, b_re, b_im, c_re, c_im, d_skip):
    q = SSM_CHUNK
    hp = lax.Precision.HIGHEST
    dt = jnp.exp(log_dt)[..., None, None]
    ks = jnp.arange(q + 1, dtype=F32)
    mag = jnp.exp(a_re[..., None] * dt * ks)
    ang = a_im[..., None] * dt * ks
    pw_re = mag * jnp.cos(ang)
    pw_im = mag * jnp.sin(ang)
    lb_re, lb_im = pw_re[..., 1], pw_im[..., 1]
    den = a_re * a_re + a_im * a_im
    f_re = ((lb_re - 1.0) * a_re + lb_im * a_im) / den
    f_im = (lb_im * a_re - (lb_re - 1.0) * a_im) / den
    bb_re = f_re[..., None] * b_re - f_im[..., None] * b_im
    bb_im = f_re[..., None] * b_im + f_im[..., None] * b_re

    cl_re = c_re[..., None] * pw_re[:, :, None] - c_im[..., None] * pw_im[:, :, None]
    cl_im = c_re[..., None] * pw_im[:, :, None] + c_im[..., None] * pw_re[:, :, None]

    kern = (jnp.einsum("dgnpk,dgpm->dgknm", cl_re[..., :q], bb_re, precision=hp)
            - jnp.einsum("dgnpk,dgpm->dgknm", cl_im[..., :q], bb_im, precision=hp))
    s_idx = jnp.arange(q)[:, None]
    t_idx = jnp.arange(q)[None, :]
    lag_f = jnp.clip(t_idx - s_idx, 0, q - 1)
    lag_b = jnp.clip(s_idx - t_idx, 0, q - 1)
    tf = kern[0][:, lag_f] * (t_idx >= s_idx)[None, :, :, None, None]
    tb = kern[1][:, lag_b] * (s_idx >= t_idx)[None, :, :, None, None]
    t_mat = (tf + tb).transpose(0, 1, 4, 2, 3)
    t_mat = t_mat.reshape(SSM_GROUPS, q * SSM_GROUP, q * SSM_GROUP)
    d_rep = jnp.tile(d_skip.reshape(SSM_GROUPS, 1, SSM_GROUP), (1, q, 1)).reshape(SSM_GROUPS, -1)
    t_mat = t_mat + d_rep[:, :, None] * jnp.eye(q * SSM_GROUP, dtype=F32)[None]

    def w_in_dir(d, exps):
        pr = pw_re[d][:, :, exps]
        pi = pw_im[d][:, :, exps]
        re = pr[..., None] * bb_re[d][:, :, None, :] - pi[..., None] * bb_im[d][:, :, None, :]
        im = pr[..., None] * bb_im[d][:, :, None, :] + pi[..., None] * bb_re[d][:, :, None, :]
        re = re.transpose(0, 2, 3, 1).reshape(SSM_GROUPS, q * SSM_GROUP, SSM_STATE)
        im = im.transpose(0, 2, 3, 1).reshape(SSM_GROUPS, q * SSM_GROUP, SSM_STATE)
        return jnp.concatenate([re, im, im, re], axis=-1)

    w_in = jnp.concatenate([w_in_dir(0, q - 1 - jnp.arange(q)), w_in_dir(1, jnp.arange(q))], axis=-1)

    def w_out_dir(d, exps):
        re = cl_re[d][..., exps]
        im = cl_im[d][..., exps]
        re = re.transpose(0, 2, 3, 1).reshape(SSM_GROUPS, SSM_STATE, q * SSM_GROUP)
        im = im.transpose(0, 2, 3, 1).reshape(SSM_GROUPS, SSM_STATE, q * SSM_GROUP)
        return jnp.concatenate([re, -im], axis=1)

    w_out = jnp.concatenate([w_out_dir(0, 1 + jnp.arange(q)), w_out_dir(1, q - jnp.arange(q))], axis=1)

    ms = jnp.arange(SUBLANES + 1, dtype=F32) * q
    cr = jnp.exp(a_re[..., None] * dt * ms) * jnp.cos(a_im[..., None] * dt * ms)
    ci = jnp.exp(a_re[..., None] * dt * ms) * jnp.sin(a_im[..., None] * dt * ms)

    def pat(d, m):
        ar, ai = cr[d][..., m], ci[d][..., m]
        return (jnp.concatenate([ar, ar], axis=-1).reshape(-1), jnp.concatenate([-ai, ai], axis=-1).reshape(-1))

    row = jnp.arange(SUBLANES)

    def dir_tables(d, backward):
        tabs = []
        for k in (1, 2, 4):
            valid = (row <= SUBLANES - 1 - k) if backward else (row >= k)
            tabs += [valid[:, None] * p[None, :] for p in pat(d, k)]
        pows = [pat(d, SUBLANES - 1 - i if backward else i) for i in range(SUBLANES)]
        tabs += [jnp.stack([p[0] for p in pows]), jnp.stack([p[1] for p in pows])]
        tabs += [jnp.tile(p[None, :], (SUBLANES, 1)) for p in pat(d, SUBLANES)]
        excl = (row <= SUBLANES - 2) if backward else (row >= 1)
        tabs.append(jnp.tile(excl[:, None].astype(F32), (1, SSM_GROUPS * LANES)))
        return tabs

    scan_tab = jnp.stack(dir_tables(0, False) + dir_tables(1, True), axis=0).astype(F32)
    return {"t": t_mat.astype(BF16), "w_in": w_in.astype(BF16), "w_out": w_out.astype(BF16),
            "scan": scan_tab}


def _flash_kernel(qb_ref, kb_ref, first_ref, last_ref, q_ref, k_ref, v_ref, o_ref,
                  acc_ref, *m_scratch, running_max, score_dtype):
    del qb_ref, kb_ref
    s_id = pl.program_id(1)

    @pl.when(first_ref[s_id] == 1)
    def _():
        acc_ref[...] = jnp.zeros(acc_ref.shape, F32)
        if running_max:
            m_scratch[0][...] = jnp.full(m_scratch[0].shape, -jnp.inf, F32)

    for hh in range(2):
        sl = slice(hh * HEAD_PAD, (hh + 1) * HEAD_PAD)
        s = lax.dot_general(q_ref[:, sl].astype(score_dtype), k_ref[:, sl].astype(score_dtype),
                            (((1,), (1,)), ((), ())), preferred_element_type=F32)
        if running_max:
            m_ref = m_scratch[0]
            m_prev = m_ref[hh]
            m_new = jnp.maximum(m_prev, jnp.max(s, axis=1, keepdims=True))
            p = jnp.exp2(s - m_new[:, :1]).astype(BF16)
            acc_ref[hh] = jnp.exp2(m_prev - m_new) * acc_ref[hh] + _dot(p, v_ref[:, sl])
            m_ref[hh] = m_new
        else:
            acc_ref[hh] += _dot(jnp.exp2(s.astype(BF16)), v_ref[:, sl])

    @pl.when(last_ref[s_id] == 1)
    def _():
        lane = lax.broadcasted_iota(jnp.int32, o_ref.shape, 1)
        a0 = acc_ref[0]
        a1 = acc_ref[1]
        o0 = a0 / a0[:, MLA_V:MLA_V + 1]
        o1 = a1 / a1[:, 0:1]
        o_ref[...] = jnp.where(lane < MLA_V, o0, o1).astype(BF16)


def _flash_attention(q, k, v, seqs, q_tile, kv_tile, running_max, score_dtype=BF16):
    qb, kb, first, last = [], [], [], []
    base = 0
    for n_seq, seq_len in seqs:
        nq, nk = seq_len // q_tile, seq_len // kv_tile
        for _ in range(n_seq):
            for qi in range(nq):
                for ki in range(nk):
                    qb.append(base // q_tile + qi)
                    kb.append(base // kv_tile + ki)
                    first.append(int(ki == 0))
                    last.append(int(ki == nk - 1))
            base += seq_len
    tabs = [jnp.asarray(np.asarray(a, np.int32)) for a in (qb, kb, first, last)]
    n_steps = len(qb)
    t_tok = q.shape[0]
    pair_w = 2 * HEAD_PAD
    scratch = [pltpu.VMEM((2, q_tile, LANES), F32)] * (2 if running_max else 1)
    grid_spec = pltpu.PrefetchScalarGridSpec(
        num_scalar_prefetch=4,
        grid=(MLA_HEADS // 2, n_steps),
        in_specs=[
            pl.BlockSpec((q_tile, pair_w), lambda p, s, qb, kb, fi, la: (qb[s], p)),
            pl.BlockSpec((kv_tile, pair_w), lambda p, s, qb, kb, fi, la: (kb[s], p)),
            pl.BlockSpec((kv_tile, pair_w), lambda p, s, qb, kb, fi, la: (kb[s], p)),
        ],
        out_specs=pl.BlockSpec((q_tile, 2 * MLA_V), lambda p, s, qb, kb, fi, la: (qb[s], p)),
        scratch_shapes=scratch,
    )
    return pl.pallas_call(
        functools.partial(_flash_kernel, running_max=running_max, score_dtype=score_dtype),
        grid_spec=grid_spec,
        out_shape=jax.ShapeDtypeStruct((t_tok, MLA_HEADS * MLA_V), BF16),
        compiler_params=_cparams(("parallel", "arbitrary")),
        name="mla_flash_max" if running_max else ("mla_flash" if score_dtype == BF16 else "mla_flash_f8"),
    )(*tabs, q, k, v)


def _memkv_kernel(mem_ref, gmem_ref, w_ref, gk_ref, k_ref, v_ref):
    x = mem_ref[...]
    h = (x * _rms_scale(x, D_MODEL) * gmem_ref[...]).astype(BF16)
    kv = _dot(h, w_ref[...])
    for hh in range(MEM_HEADS):
        sl = slice(hh * MEM_HEAD_DIM, (hh + 1) * MEM_HEAD_DIM)
        blk = kv[:, sl]
        k_ref[:, sl] = (blk * _rms_scale(blk, MEM_HEAD_DIM) * gk_ref[...]).astype(BF16)
    v_ref[...] = kv[:, MEM_WIDTH:].astype(BF16)


def _memory_kv(mem, lw):
    rows = mem.shape[0]
    tm = MEM_TOKENS
    return pl.pallas_call(
        _memkv_kernel,
        grid=(rows // tm,),
        in_specs=[pl.BlockSpec((tm, D_MODEL), lambda i: (i, 0)),
                  _const_spec(lw["g_mem"].shape), _const_spec(lw["w_mem_kv"].shape),
                  _const_spec(lw["g_mem_k"].shape)],
        out_specs=[pl.BlockSpec((tm, MEM_WIDTH), lambda i: (i, 0))] * 2,
        out_shape=[jax.ShapeDtypeStruct((rows, MEM_WIDTH), BF16)] * 2,
        compiler_params=_cparams(("parallel",)),
        name="memory_kv",
    )(mem, lw["g_mem"], lw["w_mem_kv"], lw["g_mem_k"])


def _gelu_tanh(x):
    return 0.5 * x * (1.0 + jnp.tanh(0.7978845608028654 * (x + 0.044715 * x * x * x)))


def _merge_kernel(*refs, with_router, n_x, n_first):
    x_refs = refs[:n_x]
    refs = refs[n_x:]
    if with_router:
        (ys_ref, ya_ref, qm_ref, km_ref, vm_ref, sg_ref, wglu_ref, bglu_ref, wb_ref,
         wo_ref, gffn_ref, wr_hi_ref, wr_lo_ref, tri_ref, x1_ref, h2_ref, route_ref, counts_ref,
         cnt_ref) = refs
    else:
        (ys_ref, ya_ref, qm_ref, km_ref, vm_ref, sg_ref, wglu_ref, bglu_ref, wb_ref,
         wo_ref, gffn_ref, x1_ref, h2_ref) = refs

    ya = _gelu_tanh(ys_ref[...]).astype(BF16)
    z = _dot(ya, wglu_ref[...]) + bglu_ref[...]
    y_ssm = (z[:, :SSM_WIDTH] * _sigmoid(z[:, SSM_WIDTH:])).astype(BF16)

    mem_parts = []
    for hh in range(MEM_HEADS):
        sl = slice(hh * MEM_HEAD_DIM, (hh + 1) * MEM_HEAD_DIM)
        s = lax.dot_general(qm_ref[:, sl], km_ref[:, sl], (((1,), (1,)), ((), ())),
                            preferred_element_type=F32)
        p = jnp.exp(s - jnp.max(s, axis=1, keepdims=True))
        p = p / jnp.sum(p, axis=1, keepdims=True)
        mem_parts.append(_dot(p.astype(BF16), vm_ref[:, sl]))
    y_mem = jnp.concatenate(mem_parts, axis=1).astype(BF16)

    merged = None
    for b, yb in enumerate((y_ssm, ya_ref[...], y_mem)):
        gate = sg_ref[:, b * D_MODEL:(b + 1) * D_MODEL].astype(F32)
        term = gate * _dot(yb, wb_ref[b])
        merged = term if merged is None else merged + term

    x1 = _stream_read(x_refs, n_first) + _dot(merged.astype(BF16), wo_ref[...])
    x1_ref[...] = x1
    h2 = x1 * _rms_scale(x1, D_MODEL) * gffn_ref[...]

    if not with_router:
        h2_ref[...] = h2.astype(BF16)
        return

    h2_ref[...] = _rows_to_token_tiles(h2)

    h_hi, h_lo = _split_bf16(h2)
    logits = _dot(h_hi, wr_hi_ref[...]) + _dot(h_hi, wr_lo_ref[...]) + _dot(h_lo, wr_hi_ref[...])
    lane = lax.broadcasted_iota(jnp.int32, logits.shape, 1)
    lane_f = lane.astype(F32)
    neg = jnp.float32(-jnp.inf)
    big = jnp.float32(LANES)
    logits = jnp.where(lane < N_EXPERTS, logits, neg)
    m1 = jnp.max(logits, axis=1, keepdims=True)
    i1 = jnp.min(jnp.where(logits == m1, lane_f, big), axis=1, keepdims=True)
    rest = jnp.where(lane_f == i1, neg, logits)
    m2 = jnp.max(rest, axis=1, keepdims=True)
    i2 = jnp.min(jnp.where(rest == m2, lane_f, big), axis=1, keepdims=True)
    e2 = jnp.exp(m2 - m1)
    w1 = 1.0 / (1.0 + e2)
    w2 = e2 / (1.0 + e2)
    @pl.when(pl.program_id(0) == 0)
    def _():
        cnt_ref[...] = jnp.zeros(cnt_ref.shape, F32)

    o1 = jnp.where(lane_f == i1, 1.0, 0.0)
    o2 = jnp.where(lane_f == i2, 1.0, 0.0)
    pre1 = _dot(tri_ref[...], o1.astype(BF16))
    pre2 = _dot(tri_ref[...], o2.astype(BF16))
    tot1 = jnp.sum(o1, axis=0, keepdims=True)
    tot2 = jnp.sum(o2, axis=0, keepdims=True)
    carry = cnt_ref[0:1, :]
    r1 = jnp.sum(o1 * (carry + pre1), axis=1, keepdims=True)
    r2 = jnp.sum(o2 * (carry + tot1 + pre2), axis=1, keepdims=True)
    new_cnt = jnp.broadcast_to(carry + tot1 + tot2, cnt_ref.shape)
    cnt_ref[...] = new_cnt
    counts_ref[...] = new_cnt

    out = jnp.where(lane == 0, i1, jnp.where(lane == 1, i2, jnp.where(lane == 2, w1, w2)))
    out = jnp.where(lane == 4, r1, jnp.where(lane == 5, r2, out))
    route_ref[...] = jnp.where(lane < 6, out, 0.0)


def _merge(x_parts, y_ssm_raw, y_mla, qm, km, vm, sg, lw, tiles, with_router):
    t_tok = y_mla.shape[0]
    tm = tiles["tm"]
    seq_map = tiles["seq_map"]

    def row(i):
        return (i, 0)

    weights = [lw["w_glu"], lw["b_glu"], lw["w_branch"], lw["w_out"], lw["g_ffn"]]
    if with_router:
        tri = (jnp.arange(tm)[:, None] > jnp.arange(tm)[None, :]).astype(BF16)
        weights += [lw["w_router_hi"], lw["w_router_lo"], tri]
    in_specs = _stream_specs(x_parts, tm, D_MODEL) + [
        pl.BlockSpec((tm, SSM_WIDTH), row),
        pl.BlockSpec((tm, MLA_HEADS * MLA_V), row),
        pl.BlockSpec((tm, MEM_WIDTH), row),
        pl.BlockSpec((MEM_TOKENS, MEM_WIDTH), lambda i: (seq_map(i), 0)),
        pl.BlockSpec((MEM_TOKENS, MEM_WIDTH), lambda i: (seq_map(i), 0)),
        pl.BlockSpec((tm, N_BRANCH * D_MODEL), row),
    ] + [_const_spec(w.shape) for w in weights]
    out_specs = [pl.BlockSpec((tm, D_MODEL), row)]
    out_shape = [jax.ShapeDtypeStruct((t_tok, D_MODEL), F32)]
    if with_router:
        out_specs += [pl.BlockSpec((tm, ROW_TILES, LANES), lambda i: (i, 0, 0)),
                      pl.BlockSpec((tm, LANES), row),
                      pl.BlockSpec((SUBLANES, LANES), lambda i: (0, 0))]
        out_shape += [jax.ShapeDtypeStruct((t_tok, ROW_TILES, LANES), F32),
                      jax.ShapeDtypeStruct((t_tok, LANES), F32),
                      jax.ShapeDtypeStruct((SUBLANES, LANES), F32)]
    else:
        out_specs.append(pl.BlockSpec((tm, D_MODEL), row))
        out_shape.append(jax.ShapeDtypeStruct((t_tok, D_MODEL), BF16))
    kern = functools.partial(_merge_kernel, with_router=with_router, n_x=len(x_parts),
                             n_first=x_parts[0].shape[0] // tm)
    return pl.pallas_call(
        kern,
        grid=(t_tok // tm,),
        in_specs=in_specs,
        out_specs=out_specs,
        out_shape=out_shape,
        scratch_shapes=[pltpu.VMEM((SUBLANES, LANES), F32)] if with_router else [],
        compiler_params=_cparams(("arbitrary",) if with_router else ("parallel",)),
        name="merge_router" if with_router else "merge",
    )(*x_parts, y_ssm_raw, y_mla, qm, km, vm, sg, *weights)


def _swiglu_acc(x, w1_ref, w3_ref, w2_ref, acc, chunk):
    d_ff = w1_ref.shape[1]
    for c in range(d_ff // chunk):
        sl = slice(c * chunk, (c + 1) * chunk)
        g = _dot(x, w1_ref[:, sl])
        u = _dot(x, w3_ref[:, sl])
        a = (g * _sigmoid(g) * u).astype(BF16)
        acc = acc + _dot(a, w2_ref[sl, :])
    return acc


def _ffn_kernel(h_ref, x_ref, w1_ref, w3_ref, w2_ref, o_ref):
    o_ref[...] = _swiglu_acc(h_ref[...], w1_ref, w3_ref, w2_ref, x_ref[...], FFN_CHUNK)


def _dense_ffn(h2, x1, lw, tiles):
    t_tok = x1.shape[0]
    tm = tiles["ffn_tile"]

    def row(i):
        return (i, 0)

    def resident(shape):
        return pl.BlockSpec(shape, lambda i: (0, 0), pipeline_mode=pl.Buffered(1))

    return pl.pallas_call(
        _ffn_kernel,
        grid=(t_tok // tm,),
        in_specs=[pl.BlockSpec((tm, D_MODEL), row), pl.BlockSpec((tm, D_MODEL), row),
                  resident(lw["ffn_w1"].shape), resident(lw["ffn_w3"].shape),
                  resident(lw["ffn_w2"].shape)],
        out_specs=pl.BlockSpec((tm, D_MODEL), row),
        out_shape=jax.ShapeDtypeStruct((t_tok, D_MODEL), F32),
        compiler_params=_cparams(("parallel",)),
        name="dense_ffn",
    )(h2, x1, lw["ffn_w1"], lw["ffn_w3"], lw["ffn_w2"])


def _dispatch_kernel(pos_ref, h_ref, xs_in_ref, xs_ref, sem):
    del xs_in_ref
    tm = h_ref.shape[0]

    def tile_copy(r, dst):
        return pltpu.make_async_copy(h_ref.at[r], xs_ref.at[dst], sem)

    def start(r, c):
        tile_copy(r, pos_ref[0, 0, 2 * r]).start(priority=0)
        tile_copy(r, pos_ref[0, 0, 2 * r + 1]).start(priority=1)
        return c

    def wait(r, c):
        tile_copy(r, 0).wait()
        tile_copy(r, 0).wait()
        return c

    lax.fori_loop(0, tm, start, 0, unroll=DMA_UNROLL)
    lax.fori_loop(0, tm, wait, 0, unroll=DMA_UNROLL)


def _dispatch(h2t, pos3, n_rows, tm):
    t_tok = h2t.shape[0]
    xs0 = jnp.zeros((n_rows, ROW_TILES, LANES), F32)
    return pl.pallas_call(
        _dispatch_kernel,
        grid=(t_tok // tm,),
        in_specs=[pl.BlockSpec((1, 1, 2 * tm), lambda i: (i, 0, 0), memory_space=pltpu.SMEM),
                  pl.BlockSpec((tm, ROW_TILES, LANES), lambda i: (i, 0, 0)),
                  pl.BlockSpec(memory_space=pl.ANY)],
        out_specs=pl.BlockSpec(memory_space=pl.ANY),
        out_shape=jax.ShapeDtypeStruct(xs0.shape, F32),
        input_output_aliases={2: 0},
        scratch_shapes=[pltpu.SemaphoreType.DMA(())],
        compiler_params=_cparams(("arbitrary",)),
        name="moe_dispatch",
    )(pos3, h2t, xs0)


def _expert_kernel(be_ref, nu_ref, x_ref, w1_ref, w3_ref, w2_ref, y_ref):
    del be_ref
    i = pl.program_id(0)

    @pl.when(i < nu_ref[0])
    def _():
        x = jnp.concatenate(_token_tiles_to_rows(x_ref[...]), axis=1).astype(BF16)
        acc = _swiglu_acc(x, w1_ref, w3_ref, w2_ref, jnp.zeros((x.shape[0], D_MODEL), F32), MOE_CHUNK)
        y_ref[...] = _rows_to_token_tiles(acc)

    @pl.when(i >= nu_ref[0])
    def _():
        y_ref[...] = jnp.zeros(y_ref.shape, F32)


def _expert_ffn(xs, blk_expert, n_used, lw, bm):
    n_rows = xs.shape[0]

    def w_spec(shape):
        return pl.BlockSpec((None,) + shape, lambda i, be, nu: (be[i], 0, 0),
                            pipeline_mode=pl.Buffered(1))

    grid_spec = pltpu.PrefetchScalarGridSpec(
        num_scalar_prefetch=2,
        grid=(n_rows // bm,),
        in_specs=[
            pl.BlockSpec((bm, ROW_TILES, LANES), lambda i, be, nu: (i, 0, 0)),
            w_spec((D_MODEL, D_FF_EXPERT)), w_spec((D_MODEL, D_FF_EXPERT)),
            w_spec((D_FF_EXPERT, D_MODEL)),
        ],
        out_specs=pl.BlockSpec((bm, ROW_TILES, LANES), lambda i, be, nu: (i, 0, 0)),
    )
    return pl.pallas_call(
        _expert_kernel,
        grid_spec=grid_spec,
        out_shape=jax.ShapeDtypeStruct(xs.shape, F32),
        compiler_params=_cparams(("arbitrary",)),
        name="moe_experts",
    )(blk_expert, n_used, xs, lw["moe_w1"], lw["moe_w3"], lw["moe_w2"])


def _combine_kernel(pos_ref, pos_next_ref, ys_hbm, x_ref, route_ref, *refs, n_first):
    out_refs, (buf_ref, sem) = refs[:-2], refs[-2:]
    tm = x_ref.shape[0]
    i = pl.program_id(0)
    n = pl.num_programs(0)
    slot = i % 2

    def tile_copy(s, k, r, src):
        return pltpu.make_async_copy(ys_hbm.at[src], buf_ref.at[s, k, r], sem.at[s])

    def issue(p_ref, s):
        def start(r, c):
            tile_copy(s, 0, r, p_ref[0, 0, 2 * r]).start(priority=0)
            tile_copy(s, 1, r, p_ref[0, 0, 2 * r + 1]).start(priority=1)
            return c
        lax.fori_loop(0, tm, start, 0, unroll=DMA_UNROLL)

    @pl.when(i == 0)
    def _():
        issue(pos_ref, slot)

    @pl.when(i + 1 < n)
    def _():
        issue(pos_next_ref, 1 - slot)

    def wait(r, c):
        tile_copy(slot, 0, r, 0).wait()
        tile_copy(slot, 1, r, 0).wait()
        return c

    lax.fori_loop(0, tm, wait, 0, unroll=DMA_UNROLL)
    w1 = route_ref[:, 2:3]
    w2 = route_ref[:, 3:4]

    def write(o_ref):
        y1 = _token_tiles_to_rows(buf_ref[slot, 0])
        y2 = _token_tiles_to_rows(buf_ref[slot, 1])
        for j in range(ROW_TILES):
            sl = slice(j * LANES, (j + 1) * LANES)
            o_ref[:, sl] = x_ref[:, sl] + w1 * y1[j] + w2 * y2[j]

    if len(out_refs) == 1:
        write(out_refs[0])
    else:
        pl.when(i < n_first)(lambda: write(out_refs[0]))
        pl.when(i >= n_first)(lambda: write(out_refs[1]))


def _combine(ys, pos3, x1, route, tm, out_rows):
    t_tok = x1.shape[0]
    n_tiles = t_tok // tm
    n_first = out_rows[0] // tm
    if len(out_rows) == 1:
        out_specs = [pl.BlockSpec((tm, D_MODEL), lambda i: (i, 0))]
    else:
        out_specs = [pl.BlockSpec((tm, D_MODEL), lambda i: (jnp.minimum(i, n_first - 1), 0)),
                     pl.BlockSpec((tm, D_MODEL), lambda i: (jnp.maximum(i - n_first, 0), 0))]
    pos_spec = functools.partial(pl.BlockSpec, (1, 1, 2 * tm), memory_space=pltpu.SMEM)
    return pl.pallas_call(
        functools.partial(_combine_kernel, n_first=n_first),
        grid=(n_tiles,),
        in_specs=[pos_spec(lambda i: (i, 0, 0)),
                  pos_spec(lambda i: (jnp.minimum(i + 1, n_tiles - 1), 0, 0)),
                  pl.BlockSpec(memory_space=pl.ANY),
                  pl.BlockSpec((tm, D_MODEL), lambda i: (i, 0)),
                  pl.BlockSpec((tm, LANES), lambda i: (i, 0))],
        out_specs=out_specs,
        out_shape=[jax.ShapeDtypeStruct((r, D_MODEL), F32) for r in out_rows],
        scratch_shapes=[pltpu.VMEM((2, 2, tm, ROW_TILES, LANES), F32), pltpu.SemaphoreType.DMA((2,))],
        compiler_params=_cparams(("arbitrary",)),
        name="moe_combine",
    )(pos3, pos3, ys, x1, route)


def _moe_ffn(h2t, x1, route, counts, lw, tiles, out_rows):
    t_tok = x1.shape[0]
    tm = tiles["tm"]
    bm = tiles["moe_block"]
    n_assign = 2 * t_tok
    experts = route[:, :2].astype(jnp.int32).reshape(n_assign)
    rank = route[:, 4:6].astype(jnp.int32).reshape(n_assign)
    counts = counts[0, :N_EXPERTS].astype(jnp.int32)
    padded = ((counts + bm - 1) // bm) * bm
    ends = jnp.cumsum(padded)
    starts = ends - padded
    onehot = (experts[:, None] == jnp.arange(N_EXPERTS, dtype=jnp.int32)[None, :]).astype(jnp.int32)
    pos = jnp.sum(onehot * starts[None, :], axis=1) + rank
    n_blocks = n_assign // bm + N_EXPERTS
    blk_start = jnp.arange(n_blocks, dtype=jnp.int32) * bm
    blk_expert = jnp.sum((blk_start[:, None] >= ends[None, :]).astype(jnp.int32), axis=1)
    blk_expert = jnp.minimum(blk_expert, N_EXPERTS - 1)
    n_used = (ends[-1:] // bm).astype(jnp.int32)
    pos3 = pos.astype(jnp.int32).reshape(t_tok // tm, 1, 2 * tm)

    xs = _dispatch(h2t, pos3, n_blocks * bm, tm)
    ys = _expert_ffn(xs, blk_expert, n_used, lw, bm)
    return _combine(ys, pos3, x1, route, tm, out_rows)


def _head_cols(w, per_head, n_heads, offset=0):
    k = w.shape[0]
    w = w.reshape(k, n_heads, per_head)
    w = jnp.pad(w, ((0, 0), (0, 0), (offset, HEAD_PAD - per_head - offset)))
    return w.reshape(k, n_heads * HEAD_PAD)


def _swap_rope(w):
    z = jnp.zeros_like(w[..., :MLA_NOPE])
    return jnp.concatenate([z, w[..., MLA_NOPE + HALF_ROPE:], w[..., MLA_NOPE:MLA_NOPE + HALF_ROPE]], axis=-1)


def _pad_lane_block(v, offset=0):
    return jnp.pad(v, ((0, 0),) * (v.ndim - 1) + ((offset, LANES - v.shape[-1] - offset),))


def _layer_weights(i, p, rope_cos, rope_sin):
    o = IN_OFFSETS
    w_in = p["w_in"][i]
    lw = {}
    lw["g_mix"] = p["g_mix"][i].reshape(1, D_MODEL)
    lw["w_u"] = w_in[:, :o[0]].astype(BF16)
    lw["w_cq"] = w_in[:, o[0]:o[1]].astype(BF16)
    w_kr = w_in[:, o[2]:o[3]]
    w_kr_sw = jnp.concatenate([w_kr[:, HALF_ROPE:], w_kr[:, :HALF_ROPE]], axis=1)
    lw["w_ckv"] = jnp.concatenate([w_in[:, o[1]:o[2]], _pad_lane_block(w_kr, MLA_NOPE),
                                   _pad_lane_block(w_kr_sw, MLA_NOPE)], axis=1).astype(BF16)
    lw["w_qm"] = w_in[:, o[3]:o[4]].astype(BF16)
    lw["w_gate"] = w_in[:, o[4]:].astype(BF16)
    lw["g_q_lora"] = p["g_q_lora"][i].reshape(1, Q_LORA)
    w_uq = p["w_uq"][i].reshape(Q_LORA, MLA_HEADS, MLA_QK)
    lw["w_uq"] = jnp.concatenate(
        [_head_cols(w_uq.reshape(Q_LORA, -1), MLA_QK, MLA_HEADS),
         _head_cols(_swap_rope(w_uq).reshape(Q_LORA, -1), MLA_QK, MLA_HEADS)], axis=1).astype(BF16)
    lw["g_kv_lora"] = p["g_kv_lora"][i].reshape(1, KV_LORA)
    w_ukv = p["w_ukv"][i].reshape(KV_LORA, MLA_HEADS, MLA_NOPE + MLA_V)
    lw["w_kn"] = _head_cols(w_ukv[:, :, :MLA_NOPE].reshape(KV_LORA, -1), MLA_NOPE, MLA_HEADS).astype(BF16)
    w_v = w_ukv[:, :, MLA_NOPE:].reshape(KV_LORA, MLA_HEADS // 2, 2, MLA_V)
    w_v = jnp.stack([jnp.pad(w_v[:, :, 0], ((0, 0), (0, 0), (0, HEAD_PAD - MLA_V))),
                     jnp.pad(w_v[:, :, 1], ((0, 0), (0, 0), (HEAD_PAD - MLA_V, 0)))], axis=2)
    lw["w_v"] = w_v.reshape(KV_LORA, MLA_WIDTH_PAD).astype(BF16)
    one_even = jnp.zeros((HEAD_PAD,), F32).at[MLA_V].set(1.0)
    one_odd = jnp.zeros((HEAD_PAD,), F32).at[0].set(1.0)
    lw["v_one"] = jnp.tile(jnp.concatenate([one_even, one_odd]), MLA_HEADS // 2).reshape(1, MLA_WIDTH_PAD)
    for name, g in (("q", p["g_mla_q"][i]), ("k", p["g_mla_k"][i])):
        lw["rope_a" + name] = rope_cos * _pad_lane_block(g)[None, :]
        lw["rope_b" + name] = rope_sin * _pad_lane_block(_swap_rope(g))[None, :]
    lw["g_mem_q"] = jnp.tile(p["g_mem_q"][i], MEM_HEADS).reshape(1, MEM_WIDTH) * (MEM_HEAD_DIM ** -0.5)
    lw["g_mem"] = p["g_mem"][i].reshape(1, D_MODEL)
    lw["w_mem_kv"] = p["w_mem_kv"][i].astype(BF16)
    lw["g_mem_k"] = p["g_mem_k"][i].reshape(1, MEM_HEAD_DIM)
    lw["w_glu"] = p["ssm_w_glu"][i].astype(BF16)
    lw["b_glu"] = p["ssm_b_glu"][i].reshape(1, 2 * SSM_WIDTH)
    lw["w_branch"] = p["w_branch"][i].astype(BF16)
    lw["w_out"] = p["w_out"][i].astype(BF16)
    lw["g_ffn"] = p["g_ffn"][i].reshape(1, D_MODEL)
    lw["ssm"] = _ssm_tables(p["ssm_a_re"][i], p["ssm_a_im"][i], p["ssm_log_dt"][i], p["ssm_b_re"][i],
                            p["ssm_b_im"][i], p["ssm_c_re"][i], p["ssm_c_im"][i], p["ssm_d"][i])
    lw["score_bound"] = (1.02 * MLA_QK ** 0.5 * LOG2E
                         * jnp.max(jnp.abs(p["g_mla_q"][i])) * jnp.max(jnp.abs(p["g_mla_k"][i])))
    if i % 2 == 0:
        w13 = p["ffn_w13"][i // 2]
        lw["ffn_w1"] = w13[:, :D_FF].astype(BF16)
        lw["ffn_w3"] = w13[:, D_FF:].astype(BF16)
        lw["ffn_w2"] = p["ffn_w2"][i // 2].astype(BF16)
    else:
        w13 = p["moe_w13"][i // 2]
        lw["moe_w1"] = w13[:, :, :D_FF_EXPERT].astype(BF16)
        lw["moe_w3"] = w13[:, :, D_FF_EXPERT:].astype(BF16)
        lw["moe_w2"] = p["moe_w2"][i // 2].astype(BF16)
        wr = jnp.pad(p["moe_router"][i // 2], ((0, 0), (0, LANES - N_EXPERTS)))
        lw["w_router_hi"], lw["w_router_lo"] = _split_bf16(wr)
    return lw


def _rope_tables(max_len):
    inv = 1.0 / (ROPE_BASE ** (jnp.arange(0, MLA_ROPE, 2, dtype=F32) / MLA_ROPE))
    ang = jnp.arange(max_len, dtype=F32)[:, None] * inv[None, :]
    cos, sin = jnp.cos(ang), jnp.sin(ang)
    ones = jnp.ones((max_len, MLA_NOPE), F32)
    zn = jnp.zeros((max_len, MLA_NOPE), F32)
    zp = jnp.zeros((max_len, HEAD_PAD - MLA_QK), F32)
    return (jnp.concatenate([ones, cos, cos, zp], axis=1),
            jnp.concatenate([zn, -sin, sin, zp], axis=1))


def _gcd_tile(limit, *sizes):
    t = limit
    while any(s % t for s in sizes):
        t //= 2
    return t


def kernel(x_prompt, x_sample, mem_prompt, mem_sample, g_mix, w_in, ssm_a_re, ssm_a_im, ssm_log_dt, ssm_b_re, ssm_b_im, ssm_c_re, ssm_c_im, ssm_d, ssm_w_glu, ssm_b_glu, g_q_lora, w_uq, g_kv_lora, w_ukv, g_mla_q, g_mla_k, g_mem, w_mem_kv, g_mem_q, g_mem_k, w_branch, w_out, g_ffn, ffn_w13, ffn_w2, moe_router, moe_w13, moe_w2):
    params = dict(g_mix=g_mix, w_in=w_in, ssm_a_re=ssm_a_re, ssm_a_im=ssm_a_im, ssm_log_dt=ssm_log_dt,
                  ssm_b_re=ssm_b_re, ssm_b_im=ssm_b_im, ssm_c_re=ssm_c_re, ssm_c_im=ssm_c_im, ssm_d=ssm_d,
                  ssm_w_glu=ssm_w_glu, ssm_b_glu=ssm_b_glu, g_q_lora=g_q_lora, w_uq=w_uq,
                  g_kv_lora=g_kv_lora, w_ukv=w_ukv, g_mla_q=g_mla_q, g_mla_k=g_mla_k, g_mem=g_mem,
                  w_mem_kv=w_mem_kv, g_mem_q=g_mem_q, g_mem_k=g_mem_k, w_branch=w_branch, w_out=w_out,
                  g_ffn=g_ffn, ffn_w13=ffn_w13, ffn_w2=ffn_w2, moe_router=moe_router, moe_w13=moe_w13,
                  moe_w2=moe_w2)
    depth = g_mix.shape[0]
    bp, lp, _ = x_prompt.shape
    bs, ls, _ = x_sample.shape
    tp, ts = bp * lp, bs * ls
    t_tok = tp + ts

    tm = _gcd_tile(TOKEN_TILE, lp, ls)
    attn_q_tile = _gcd_tile(ATTN_Q_TILE, lp, ls)
    attn_kv_tile = _gcd_tile(ATTN_KV_TILE, lp, ls)
    ffn_tile = _gcd_tile(FFN_TILE, lp, ls)
    npt = tp // tm
    lp_t, ls_t = lp // tm, ls // tm

    def pos_map(i):
        return jnp.where(i < npt, i % lp_t, (i - npt) % ls_t)

    def seq_map(i):
        return jnp.where(i < npt, i // lp_t, bp + (i - npt) // ls_t)

    rope_cos, rope_sin = _rope_tables(max(lp, ls))
    head_ind = (jnp.arange(MLA_WIDTH_PAD)[:, None] // HEAD_PAD == jnp.arange(LANES)[None, :])
    mem_bd = (jnp.arange(MEM_WIDTH)[:, None] // MEM_HEAD_DIM == jnp.arange(MEM_WIDTH)[None, :] // MEM_HEAD_DIM)
    tiles = dict(tm=tm, ffn_tile=ffn_tile, pos_map=pos_map, seq_map=seq_map,
                 ones=jnp.ones((LANES, LANES), BF16),
                 head_ind=head_ind.astype(BF16),
                 head_ind_t=jnp.concatenate([head_ind.T, head_ind.T], axis=0).astype(BF16),
                 mem_bd=mem_bd.astype(BF16),
                 moe_block=_gcd_tile(MOE_BLOCK, 2 * t_tok))

    x_parts = [x_prompt.reshape(tp, D_MODEL), x_sample.reshape(ts, D_MODEL)]
    mem = jnp.concatenate([mem_prompt.reshape(bp * MEM_TOKENS, D_MODEL),
                           mem_sample.reshape(bs * MEM_TOKENS, D_MODEL)], axis=0)
    seqs = [(bp, lp), (bs, ls)]

    for i in range(depth):
        lw = _layer_weights(i, params, rope_cos, rope_sin)
        u, q, k, v, qm, sg = _in_projection(x_parts, lw, tiles)

        y_ssm_raw = _ssm_call(u, lw["ssm"], seqs)

        def attend(running_max, score_dtype):
            return lambda q_, k_, v_: _flash_attention(q_, k_, v_, seqs, attn_q_tile, attn_kv_tile,
                                                       running_max, score_dtype)

        y_mla = lax.cond(
            lw["score_bound"] <= MAX_FP8_LOG2_SCORE,
            attend(False, jnp.float8_e4m3fn),
            lambda q_, k_, v_: lax.cond(lw["score_bound"] <= MAX_SAFE_LOG2_SCORE,
                                        attend(False, BF16), attend(True, BF16), q_, k_, v_),
            q, k, v)
        km, vm = _memory_kv(mem, lw)

        with_router = i % 2 == 1
        outs = _merge(x_parts, y_ssm_raw, y_mla, qm, km, vm, sg, lw, tiles, with_router)
        if with_router:
            x1, h2t, route, counts = outs
            out_rows = (tp, ts) if i == depth - 1 else (t_tok,)
            x_parts = _moe_ffn(h2t, x1, route, counts, lw, tiles, out_rows)
        else:
            x1, h2 = outs
            x_parts = [_dense_ffn(h2, x1, lw, tiles)]

    if len(x_parts) == 1:
        x_parts = [x_parts[0][:tp], x_parts[0][tp:]]
    return (x_parts[0].reshape(bp, lp, D_MODEL), x_parts[1].reshape(bs, ls, D_MODEL))
```

```python
import functools
import math

import numpy as np
import jax
import jax.numpy as jnp
from jax import lax
from jax.experimental import pallas as pl
from jax.experimental.pallas import tpu as pltpu

F32 = jnp.float32
BF16 = jnp.bfloat16
EPS = 1e-6
LOG2E = math.log2(math.e)

D_MODEL = 1024
SSM_WIDTH = 512
SSM_GROUPS = 32
SSM_GROUP = 16
SSM_STATE = 64
MLA_HEADS = 8
MLA_NOPE = 64
MLA_ROPE = 32
MLA_V = 64
MLA_QK = MLA_NOPE + MLA_ROPE
Q_LORA = 256
KV_LORA = 128
ROPE_BASE = 10000.0
MEM_TOKENS = 256
MEM_HEADS = 4
MEM_HEAD_DIM = 128
MEM_WIDTH = 512
N_BRANCH = 3
D_FF = 2816
N_EXPERTS = 8
D_FF_EXPERT = 3584
IN_SIZES = (SSM_WIDTH, Q_LORA, KV_LORA, MLA_ROPE, MEM_WIDTH, N_BRANCH * D_MODEL)
IN_OFFSETS = tuple(int(v) for v in np.cumsum(IN_SIZES)[:-1])

LANES = 128
SUBLANES = 8
HEAD_PAD = 128
HALF_ROPE = MLA_ROPE // 2
MLA_WIDTH_PAD = MLA_HEADS * HEAD_PAD
ROW_TILES = D_MODEL // LANES
VMEM_LIMIT = 56 * 1024 * 1024
MAX_SAFE_LOG2_SCORE = 100.0
MAX_FP8_LOG2_SCORE = 20.0

TOKEN_TILE = 512
ATTN_Q_TILE = 2048
ATTN_KV_TILE = 2048
FFN_TILE = 1024
SSM_CHUNK = 16
MOE_BLOCK = 512
FFN_CHUNK = 256
MOE_CHUNK = 512
DMA_UNROLL = 8


def _cparams(sem):
    return pltpu.CompilerParams(dimension_semantics=sem, vmem_limit_bytes=VMEM_LIMIT)


def _const_spec(shape):
    nd = len(shape)
    return pl.BlockSpec(shape, lambda *_: (0,) * nd)


def _sigmoid(x):
    return 1.0 / (1.0 + jnp.exp(-x))


def _rms_scale(x, n):
    return lax.rsqrt(jnp.sum(x * x, axis=-1, keepdims=True) * (1.0 / n) + EPS)


def _split_bf16(x):
    hi = x.astype(BF16)
    lo = (x - hi.astype(F32)).astype(BF16)
    return hi, lo


def _dot(a, b):
    return jnp.dot(a, b, preferred_element_type=F32)


def _tile_transpose(vs):
    sub = lax.broadcasted_iota(jnp.int32, vs[0].shape, 1)
    for dist in (4, 2, 1):
        low = (sub // dist) % 2 == 0
        new = list(vs)
        for a in range(SUBLANES):
            if a & dist == 0:
                b = a | dist
                new[a] = jnp.where(low, vs[a], pltpu.roll(vs[b], dist, 1))
                new[b] = jnp.where(low, pltpu.roll(vs[a], SUBLANES - dist, 1), vs[b])
        vs = new
    return vs


def _rows_to_token_tiles(x):
    rows = x.shape[0]
    cols = [x[:, j * LANES:(j + 1) * LANES].reshape(rows // SUBLANES, SUBLANES, LANES) for j in range(ROW_TILES)]
    return jnp.stack(_tile_transpose(cols), axis=1).reshape(rows, ROW_TILES, LANES)


def _token_tiles_to_rows(t):
    rows = t.shape[0]
    t4 = t.reshape(rows // SUBLANES, SUBLANES, ROW_TILES, LANES)
    cols = _tile_transpose([t4[:, k] for k in range(SUBLANES)])
    return [c.reshape(rows, LANES) for c in cols]


def _stream_specs(parts, tm, width):
    if len(parts) == 1:
        return [pl.BlockSpec((tm, width), lambda i: (i, 0))]
    n0 = parts[0].shape[0] // tm
    return [pl.BlockSpec((tm, width), lambda i: (jnp.minimum(i, n0 - 1), 0)),
            pl.BlockSpec((tm, width), lambda i: (jnp.maximum(i - n0, 0), 0))]


def _stream_read(refs, n_first):
    if len(refs) == 1:
        return refs[0][...]
    return jnp.where(pl.program_id(0) < n_first, refs[0][...], refs[1][...])


def _inproj_kernel(*refs, n_x, n_first):
    x_refs = refs[:n_x]
    (gmix_ref, wu_ref, wg_ref, wqm_ref, wcq_ref, wckv_ref,
     gql_ref, wuq_ref, gkvl_ref, wkn_ref, wv_ref, vone_ref, gmq_ref,
     ones_ref, ind_ref, indt_ref, bd_ref,
     aq_ref, bq_ref, ak_ref, bk_ref,
     u_ref, q_ref, k_ref, v_ref, qm_ref, sg_ref) = refs[n_x:]

    def row_ssq(val):
        sq = val * val
        part = sq[:, :LANES]
        for j in range(1, val.shape[1] // LANES):
            part = part + sq[:, j * LANES:(j + 1) * LANES]
        return _dot(part.astype(BF16), ones_ref[...])

    def head_scale(val, scale):
        ss = _dot((val * val).astype(BF16), ind_ref[...])
        r = lax.rsqrt(ss * (1.0 / MLA_QK) + EPS) * scale
        return _dot(jnp.concatenate(_split_bf16(r), axis=1), indt_ref[...])

    x = _stream_read(x_refs, n_first)
    rs = lax.rsqrt(row_ssq(x) * (1.0 / D_MODEL) + EPS)
    h = (x * jnp.tile(rs, (1, D_MODEL // LANES)) * gmix_ref[...]).astype(BF16)

    u_ref[...] = _dot(h, wu_ref[...])

    for b in range(N_BRANCH):
        sl = slice(b * D_MODEL, (b + 1) * D_MODEL)
        sg_ref[:, sl] = _sigmoid(_dot(h, wg_ref[:, sl])).astype(BF16)

    zq = _dot(h, wqm_ref[...])
    zr = lax.rsqrt(_dot((zq * zq).astype(BF16), bd_ref[...]) * (1.0 / MEM_HEAD_DIM) + EPS)
    qm_ref[...] = (zq * zr * gmq_ref[...]).astype(BF16)

    cq = _dot(h, wcq_ref[...])
    cq_rs = lax.rsqrt(row_ssq(cq) * (1.0 / Q_LORA) + EPS)
    cqn = (cq * jnp.tile(cq_rs, (1, Q_LORA // LANES)) * gql_ref[...]).astype(BF16)
    qq = _dot(cqn, wuq_ref[...])
    qf = qq[:, :MLA_WIDTH_PAD]
    qsw = qq[:, MLA_WIDTH_PAD:]

    z3 = _dot(h, wckv_ref[...])
    ckv = z3[:, :LANES]
    kr = z3[:, LANES:2 * LANES]
    kr_sw = z3[:, 2 * LANES:]
    ckv_rs = lax.rsqrt(row_ssq(ckv) * (1.0 / KV_LORA) + EPS)
    ckvn = (ckv * ckv_rs * gkvl_ref[...]).astype(BF16)
    kf = _dot(ckvn, wkn_ref[...]) + jnp.tile(kr, (1, MLA_HEADS))
    v_ref[...] = (_dot(ckvn, wv_ref[...]) + vone_ref[...]).astype(BF16)

    q_rs = head_scale(qf, MLA_QK ** -0.5 * LOG2E)
    k_rs = head_scale(kf, 1.0)
    aq = aq_ref[...]
    bq = bq_ref[...]
    ak = ak_ref[...]
    k_rot = kr_sw * bk_ref[...]
    for hh in range(MLA_HEADS):
        sl = slice(hh * HEAD_PAD, (hh + 1) * HEAD_PAD)
        q_ref[:, sl] = ((qf[:, sl] * aq + qsw[:, sl] * bq) * q_rs[:, sl]).astype(BF16)
        k_ref[:, sl] = ((kf[:, sl] * ak + k_rot) * k_rs[:, sl]).astype(BF16)


def _in_projection(x_parts, lw, tiles):
    t_tok = sum(p.shape[0] for p in x_parts)
    tm = tiles["tm"]
    pos_map = tiles["pos_map"]

    def row(i):
        return (i, 0)

    weights = [lw["g_mix"], lw["w_u"], lw["w_gate"], lw["w_qm"], lw["w_cq"], lw["w_ckv"],
               lw["g_q_lora"], lw["w_uq"], lw["g_kv_lora"], lw["w_kn"], lw["w_v"], lw["v_one"],
               lw["g_mem_q"], tiles["ones"], tiles["head_ind"], tiles["head_ind_t"], tiles["mem_bd"]]
    rope = [lw["rope_aq"], lw["rope_bq"], lw["rope_ak"], lw["rope_bk"]]
    in_specs = _stream_specs(x_parts, tm, D_MODEL) + [_const_spec(w.shape) for w in weights]
    in_specs += [pl.BlockSpec((tm, HEAD_PAD), lambda i: (pos_map(i), 0))] * len(rope)
    kern = functools.partial(_inproj_kernel, n_x=len(x_parts), n_first=x_parts[0].shape[0] // tm)
    out_widths = (SSM_WIDTH, MLA_WIDTH_PAD, MLA_WIDTH_PAD, MLA_WIDTH_PAD, MEM_WIDTH, N_BRANCH * D_MODEL)
    return pl.pallas_call(
        kern,
        grid=(t_tok // tm,),
        in_specs=in_specs,
        out_specs=[pl.BlockSpec((tm, w), row) for w in out_widths],
        out_shape=[jax.ShapeDtypeStruct((t_tok, w), F32 if j == 0 else BF16) for j, w in enumerate(out_widths)],
        compiler_params=_cparams(("parallel",)),
        name="in_projection",
    )(*x_parts, *weights, *rope)


SSM_LANE_GROUPS = LANES // SSM_GROUP
SSM_TAB_DIR = 11
(_TAB_STEP, _TAB_POW, _TAB_A8, _TAB_MASK) = (0, 6, 8, 10)


def _granule_transpose(vs, lane):
    for dist in (4, 2, 1):
        shift = dist * SSM_GROUP
        low = (lane // shift) % 2 == 0
        new = list(vs)
        for a in range(SSM_LANE_GROUPS):
            if a & dist == 0:
                b = a | dist
                new[a] = jnp.where(low, vs[a], pltpu.roll(vs[b], shift, 1))
                new[b] = jnp.where(low, pltpu.roll(vs[a], LANES - shift, 1), vs[b])
        vs = new
    return vs


def _chunk_scan(x, xs, tab_ref, base, keep, backward):
    nv, _, nl = x.shape
    for j, k in enumerate((1, 2, 4)):
        shift = SUBLANES - k if backward else k
        rx = pltpu.roll(x, shift, 1)
        rxs = pltpu.roll(xs, shift, 1)
        a1 = tab_ref[base + _TAB_STEP + 2 * j][None]
        a2 = tab_ref[base + _TAB_STEP + 2 * j + 1][None]
        x, xs = x + a1 * rx + a2 * rxs, xs + a1 * rxs - a2 * rx
    a1 = tab_ref[base + _TAB_A8]
    a2 = tab_ref[base + _TAB_A8 + 1]
    edge = 0 if backward else SUBLANES - 1
    c = jnp.zeros((SUBLANES, nl), F32)
    cs = c
    cb = [None] * nv
    csb = [None] * nv
    for v in (range(nv - 1, -1, -1) if backward else range(nv)):
        kp = keep(v)
        c = c * kp
        cs = cs * kp
        cb[v] = c
        csb[v] = cs
        ex = jnp.broadcast_to(x[v, edge:edge + 1, :], (SUBLANES, nl))
        exs = jnp.broadcast_to(xs[v, edge:edge + 1, :], (SUBLANES, nl))
        c, cs = ex + a1 * c + a2 * cs, exs + a1 * cs - a2 * c
    xe = pltpu.roll(x, SUBLANES - 1 if backward else 1, 1) * tab_ref[base + _TAB_MASK][None]
    return (xe + tab_ref[base + _TAB_POW][None] * jnp.stack(cb)
            + tab_ref[base + _TAB_POW + 1][None] * jnp.stack(csb))


def _ssm_kernel(keepf_ref, keepb_ref, u_ref, t_ref, win_ref, wout_ref, tab_ref, y_ref):
    blk = pl.program_id(1)
    rows = u_ref.shape[0] // SSM_CHUNK
    nv = rows // SUBLANES
    ng = SSM_LANE_GROUPS
    lane = lax.broadcasted_iota(jnp.int32, (rows, LANES), 1)

    pieces = [u_ref[pl.ds(t, rows, stride=SSM_CHUNK), :] for t in range(SSM_CHUNK)]
    halves = [_granule_transpose(pieces[ng * h:ng * (h + 1)], lane) for h in range(SSM_CHUNK // ng)]

    ys = []
    parts = [[], [], [], []]
    for g in range(ng):
        xg = jnp.concatenate([h[g] for h in halves], axis=1).astype(BF16)
        ys.append(_dot(xg, t_ref[g]))
        s4 = _dot(xg, win_ref[g])
        for j in range(4):
            parts[j].append(s4[:, j * LANES:(j + 1) * LANES])
    f, fs, b, bs = [jnp.concatenate(p, axis=1).reshape(nv, SUBLANES, ng * LANES) for p in parts]

    xin_f = _chunk_scan(f, fs, tab_ref, 0, lambda v: keepf_ref[blk * nv + v].astype(F32), False)
    xin_b = _chunk_scan(b, bs, tab_ref, SSM_TAB_DIR, lambda v: keepb_ref[blk * nv + v].astype(F32), True)
    xin_f = xin_f.reshape(rows, ng * LANES)
    xin_b = xin_b.reshape(rows, ng * LANES)

    outs = []
    for g in range(ng):
        sl = slice(g * LANES, (g + 1) * LANES)
        xs = jnp.concatenate([xin_f[:, sl], xin_b[:, sl]], axis=1).astype(BF16)
        outs.append(ys[g] + _dot(xs, wout_ref[g]))
    for h in range(SSM_CHUNK // ng):
        back = _granule_transpose([o[:, h * LANES:(h + 1) * LANES] for o in outs], lane)
        for tt in range(ng):
            y_ref[pl.ds(ng * h + tt, rows, stride=SSM_CHUNK), :] = back[tt]


def _ssm_call(u, tabs, seqs):
    t_tok = u.shape[0]
    block = int(np.lcm.reduce([ln for _, ln in seqs]))
    vreg_tokens = SUBLANES * SSM_CHUNK
    assert all((n * ln) % block == 0 and ln % vreg_tokens == 0 for n, ln in seqs), seqs
    nv = block // vreg_tokens
    keep_f, keep_b = [], []
    for n_seq, seq_len in seqs:
        for _ in range(n_seq * seq_len // block):
            for v in range(nv):
                keep_f.append(int((v * vreg_tokens) % seq_len != 0))
                keep_b.append(int(((v + 1) * vreg_tokens) % seq_len != 0))
    keep = [jnp.asarray(np.asarray(a, np.int32)) for a in (keep_f, keep_b)]
    ng = SSM_LANE_GROUPS
    width = SSM_CHUNK * SSM_GROUP
    n_tab = 2 * SSM_TAB_DIR
    grid_spec = pltpu.PrefetchScalarGridSpec(
        num_scalar_prefetch=2,
        grid=(SSM_WIDTH // LANES, t_tok // block),
        in_specs=[
            pl.BlockSpec((block, LANES), lambda g, b, kf, kb: (b, g)),
            pl.BlockSpec((ng, width, width), lambda g, b, kf, kb: (g, 0, 0)),
            pl.BlockSpec((ng, width, 4 * LANES), lambda g, b, kf, kb: (g, 0, 0)),
            pl.BlockSpec((ng, 2 * LANES, width), lambda g, b, kf, kb: (g, 0, 0)),
            pl.BlockSpec((n_tab, SUBLANES, ng * LANES), lambda g, b, kf, kb: (0, 0, g)),
        ],
        out_specs=pl.BlockSpec((block, LANES), lambda g, b, kf, kb: (b, g)),
    )
    return pl.pallas_call(
        _ssm_kernel,
        grid_spec=grid_spec,
        out_shape=jax.ShapeDtypeStruct(u.shape, F32),
        compiler_params=_cparams(("parallel", "parallel")),
        name="ssm_chunked",
    )(*keep, u, tabs["t"], tabs["w_in"], tabs["w_out"], tabs["scan"])


def _ssm_tables(a_re, a_im, log_dt, b_re, b_im, c_re, c_im, d_skip):
    q = SSM_CHUNK
    hp = lax.Precision.HIGHEST
    dt = jnp.exp(log_dt)[..., None, None]
    ks = jnp.arange(q + 1, dtype=F32)
    mag = jnp.exp(a_re[..., None] * dt * ks)
    ang = a_im[..., None] * dt * ks
    pw_re = mag * jnp.cos(ang)
    pw_im = mag * jnp.sin(ang)
    lb_re, lb_im = pw_re[..., 1], pw_im[..., 1]
    den = a_re * a_re + a_im * a_im
    f_re = ((lb_re - 1.0) * a_re + lb_im * a_im) / den
    f_im = (lb_im * a_re - (lb_re - 1.0) * a_im) / den
    bb_re = f_re[..., None] * b_re - f_im[..., None] * b_im
    bb_im = f_re[..., None] * b_im + f_im[..., None] * b_re

    cl_re = c_re[..., None] * pw_re[:, :, None] - c_im[..., None] * pw_im[:, :, None]
    cl_im = c_re[..., None] * pw_im[:, :, None] + c_im[..., None] * pw_re[:, :, None]

    kern = (jnp.einsum("dgnpk,dgpm->dgknm", cl_re[..., :q], bb_re, precision=hp)
            - jnp.einsum("dgnpk,dgpm->dgknm", cl_im[..., :q], bb_im, precision=hp))
    s_idx = jnp.arange(q)[:, None]
    t_idx = jnp.arange(q)[None, :]
    lag_f = jnp.clip(t_idx - s_idx, 0, q - 1)
    lag_b = jnp.clip(s_idx - t_idx, 0, q - 1)
    tf = kern[0][:, lag_f] * (t_idx >= s_idx)[None, :, :, None, None]
    tb = kern[1][:, lag_b] * (s_idx >= t_idx)[None, :, :, None, None]
    t_mat = (tf + tb).transpose(0, 1, 4, 2, 3)
    t_mat = t_mat.reshape(SSM_GROUPS, q * SSM_GROUP, q * SSM_GROUP)
    d_rep = jnp.tile(d_skip.reshape(SSM_GROUPS, 1, SSM_GROUP), (1, q, 1)).reshape(SSM_GROUPS, -1)
    t_mat = t_mat + d_rep[:, :, None] * jnp.eye(q * SSM_GROUP, dtype=F32)[None]

    def w_in_dir(d, exps):
        pr = pw_re[d][:, :, exps]
        pi = pw_im[d][:, :, exps]
        re = pr[..., None] * bb_re[d][:, :, None, :] - pi[..., None] * bb_im[d][:, :, None, :]
        im = pr[..., None] * bb_im[d][:, :, None, :] + pi[..., None] * bb_re[d][:, :, None, :]
        re = re.transpose(0, 2, 3, 1).reshape(SSM_GROUPS, q * SSM_GROUP, SSM_STATE)
        im = im.transpose(0, 2, 3, 1).reshape(SSM_GROUPS, q * SSM_GROUP, SSM_STATE)
        return jnp.concatenate([re, im, im, re], axis=-1)

    w_in = jnp.concatenate([w_in_dir(0, q - 1 - jnp.arange(q)), w_in_dir(1, jnp.arange(q))], axis=-1)

    def w_out_dir(d, exps):
        re = cl_re[d][..., exps]
        im = cl_im[d][..., exps]
        re = re.transpose(0, 2, 3, 1).reshape(SSM_GROUPS, SSM_STATE, q * SSM_GROUP)
        im = im.transpose(0, 2, 3, 1).reshape(SSM_GROUPS, SSM_STATE, q * SSM_GROUP)
        return jnp.concatenate([re, -im], axis=1)

    w_out = jnp.concatenate([w_out_dir(0, 1 + jnp.arange(q)), w_out_dir(1, q - jnp.arange(q))], axis=1)

    ms = jnp.arange(SUBLANES + 1, dtype=F32) * q
    cr = jnp.exp(a_re[..., None] * dt * ms) * jnp.cos(a_im[..., None] * dt * ms)
    ci = jnp.exp(a_re[..., None] * dt * ms) * jnp.sin(a_im[..., None] * dt * ms)

    def pat(d, m):
        ar, ai = cr[d][..., m], ci[d][..., m]
        return (jnp.concatenate([ar, ar], axis=-1).reshape(-1), jnp.concatenate([-ai, ai], axis=-1).reshape(-1))

    row = jnp.arange(SUBLANES)

    def dir_tables(d, backward):
        tabs = []
        for k in (1, 2, 4):
            valid = (row <= SUBLANES - 1 - k) if backward else (row >= k)
            tabs += [valid[:, None] * p[None, :] for p in pat(d, k)]
        pows = [pat(d, SUBLANES - 1 - i if backward else i) for i in range(SUBLANES)]
        tabs += [jnp.stack([p[0] for p in pows]), jnp.stack([p[1] for p in pows])]
        tabs += [jnp.tile(p[None, :], (SUBLANES, 1)) for p in pat(d, SUBLANES)]
        excl = (row <= SUBLANES - 2) if backward else (row >= 1)
        tabs.append(jnp.tile(excl[:, None].astype(F32), (1, SSM_GROUPS * LANES)))
        return tabs

    scan_tab = jnp.stack(dir_tables(0, False) + dir_tables(1, True), axis=0).astype(F32)
    return {"t": t_mat.astype(BF16), "w_in": w_in.astype(BF16), "w_out": w_out.astype(BF16),
            "scan": scan_tab}


def _flash_kernel(qb_ref, kb_ref, first_ref, last_ref, q_ref, k_ref, v_ref, o_ref,
                  acc_ref, *extra, running_max, score_dtype):
    del qb_ref, kb_ref
    s_id = pl.program_id(1)
    m_scratch = extra if running_max else ()
    qc_ref = None if (running_max or score_dtype == BF16) else extra[0]

    @pl.when(first_ref[s_id] == 1)
    def _():
        acc_ref[...] = jnp.zeros(acc_ref.shape, F32)
        if running_max:
            m_scratch[0][...] = jnp.full(m_scratch[0].shape, -jnp.inf, F32)
        if qc_ref is not None:
            qc_ref[...] = q_ref[...].astype(score_dtype)

    for hh in range(2):
        sl = slice(hh * HEAD_PAD, (hh + 1) * HEAD_PAD)
        qh = q_ref[:, sl] if qc_ref is None else qc_ref[:, sl]
        s = lax.dot_general(qh, k_ref[:, sl].astype(score_dtype),
                            (((1,), (1,)), ((), ())), preferred_element_type=F32)
        if running_max:
            m_ref = m_scratch[0]
            m_prev = m_ref[hh]
            m_new = jnp.maximum(m_prev, jnp.max(s, axis=1, keepdims=True))
            p = jnp.exp2(s - m_new[:, :1]).astype(BF16)
            acc_ref[hh] = jnp.exp2(m_prev - m_new) * acc_ref[hh] + _dot(p, v_ref[:, sl])
            m_ref[hh] = m_new
        else:
            acc_ref[hh] += _dot(jnp.exp2(s.astype(BF16)), v_ref[:, sl])

    @pl.when(last_ref[s_id] == 1)
    def _():
        lane = lax.broadcasted_iota(jnp.int32, o_ref.shape, 1)
        a0 = acc_ref[0]
        a1 = acc_ref[1]
        o0 = a0 / a0[:, MLA_V:MLA_V + 1]
        o1 = a1 / a1[:, 0:1]
        o_ref[...] = jnp.where(lane < MLA_V, o0, o1).astype(BF16)


def _flash_attention(q, k, v, seqs, q_tile, kv_tile, running_max, score_dtype=BF16):
    qb, kb, first, last = [], [], [], []
    base = 0
    for n_seq, seq_len in seqs:
        nq, nk = seq_len // q_tile, seq_len // kv_tile
        for _ in range(n_seq):
            for qi in range(nq):
                for ki in range(nk):
                    qb.append(base // q_tile + qi)
                    kb.append(base // kv_tile + ki)
                    first.append(int(ki == 0))
                    last.append(int(ki == nk - 1))
            base += seq_len
    tabs = [jnp.asarray(np.asarray(a, np.int32)) for a in (qb, kb, first, last)]
    n_steps = len(qb)
    t_tok = q.shape[0]
    pair_w = 2 * HEAD_PAD
    scratch = [pltpu.VMEM((2, q_tile, LANES), F32)] * (2 if running_max else 1)
    if not running_max and score_dtype != BF16:
        scratch.append(pltpu.VMEM((q_tile, pair_w), score_dtype))
    grid_spec = pltpu.PrefetchScalarGridSpec(
        num_scalar_prefetch=4,
        grid=(MLA_HEADS // 2, n_steps),
        in_specs=[
            pl.BlockSpec((q_tile, pair_w), lambda p, s, qb, kb, fi, la: (qb[s], p)),
            pl.BlockSpec((kv_tile, pair_w), lambda p, s, qb, kb, fi, la: (kb[s], p)),
            pl.BlockSpec((kv_tile, pair_w), lambda p, s, qb, kb, fi, la: (kb[s], p)),
        ],
        out_specs=pl.BlockSpec((q_tile, 2 * MLA_V), lambda p, s, qb, kb, fi, la: (qb[s], p)),
        scratch_shapes=scratch,
    )
    return pl.pallas_call(
        functools.partial(_flash_kernel, running_max=running_max, score_dtype=score_dtype),
        grid_spec=grid_spec,
        out_shape=jax.ShapeDtypeStruct((t_tok, MLA_HEADS * MLA_V), BF16),
        compiler_params=_cparams(("parallel", "arbitrary")),
        name="mla_flash_max" if running_max else ("mla_flash" if score_dtype == BF16 else "mla_flash_f8"),
    )(*tabs, q, k, v)


def _memkv_kernel(mem_ref, gmem_ref, w_ref, gk_ref, k_ref, v_ref):
    x = mem_ref[...]
    h = (x * _rms_scale(x, D_MODEL) * gmem_ref[...]).astype(BF16)
    kv = _dot(h, w_ref[...])
    for hh in range(MEM_HEADS):
        sl = slice(hh * MEM_HEAD_DIM, (hh + 1) * MEM_HEAD_DIM)
        blk = kv[:, sl]
        k_ref[:, sl] = (blk * _rms_scale(blk, MEM_HEAD_DIM) * gk_ref[...]).astype(BF16)
    v_ref[...] = kv[:, MEM_WIDTH:].astype(BF16)


def _memory_kv(mem, lw):
    rows = mem.shape[0]
    tm = MEM_TOKENS
    return pl.pallas_call(
        _memkv_kernel,
        grid=(rows // tm,),
        in_specs=[pl.BlockSpec((tm, D_MODEL), lambda i: (i, 0)),
                  _const_spec(lw["g_mem"].shape), _const_spec(lw["w_mem_kv"].shape),
                  _const_spec(lw["g_mem_k"].shape)],
        out_specs=[pl.BlockSpec((tm, MEM_WIDTH), lambda i: (i, 0))] * 2,
        out_shape=[jax.ShapeDtypeStruct((rows, MEM_WIDTH), BF16)] * 2,
        compiler_params=_cparams(("parallel",)),
        name="memory_kv",
    )(mem, lw["g_mem"], lw["w_mem_kv"], lw["g_mem_k"])


def _gelu_tanh(x):
    return 0.5 * x * (1.0 + jnp.tanh(0.7978845608028654 * (x + 0.044715 * x * x * x)))


def _merge_kernel(*refs, with_router, n_x, n_first):
    x_refs = refs[:n_x]
    refs = refs[n_x:]
    if with_router:
        (ys_ref, ya_ref, qm_ref, km_ref, vm_ref, sg_ref, wglu_ref, bglu_ref, wb_ref,
         wo_ref, gffn_ref, wr_hi_ref, wr_lo_ref, tri_ref, x1_ref, h2_ref, route_ref, counts_ref,
         cnt_ref) = refs
    else:
        (ys_ref, ya_ref, qm_ref, km_ref, vm_ref, sg_ref, wglu_ref, bglu_ref, wb_ref,
         wo_ref, gffn_ref, x1_ref, h2_ref) = refs

    ya = _gelu_tanh(ys_ref[...]).astype(BF16)
    z = _dot(ya, wglu_ref[...]) + bglu_ref[...]
    y_ssm = (z[:, :SSM_WIDTH] * _sigmoid(z[:, SSM_WIDTH:])).astype(BF16)

    mem_parts = []
    for hh in range(MEM_HEADS):
        sl = slice(hh * MEM_HEAD_DIM, (hh + 1) * MEM_HEAD_DIM)
        s = lax.dot_general(qm_ref[:, sl], km_ref[:, sl], (((1,), (1,)), ((), ())),
                            preferred_element_type=F32)
        p = jnp.exp(s - jnp.max(s, axis=1, keepdims=True))
        p = p / jnp.sum(p, axis=1, keepdims=True)
        mem_parts.append(_dot(p.astype(BF16), vm_ref[:, sl]))
    y_mem = jnp.concatenate(mem_parts, axis=1).astype(BF16)

    merged = None
    for b, yb in enumerate((y_ssm, ya_ref[...], y_mem)):
        gate = sg_ref[:, b * D_MODEL:(b + 1) * D_MODEL].astype(F32)
        term = gate * _dot(yb, wb_ref[b])
        merged = term if merged is None else merged + term

    x1 = _stream_read(x_refs, n_first) + _dot(merged.astype(BF16), wo_ref[...])
    x1_ref[...] = x1
    h2 = x1 * _rms_scale(x1, D_MODEL) * gffn_ref[...]

    if not with_router:
        h2_ref[...] = h2.astype(BF16)
        return

    h2_ref[...] = _rows_to_token_tiles(h2)

    h_hi, h_lo = _split_bf16(h2)
    logits = _dot(h_hi, wr_hi_ref[...]) + _dot(h_hi, wr_lo_ref[...]) + _dot(h_lo, wr_hi_ref[...])
    lane = lax.broadcasted_iota(jnp.int32, logits.shape, 1)
    lane_f = lane.astype(F32)
    neg = jnp.float32(-jnp.inf)
    big = jnp.float32(LANES)
    logits = jnp.where(lane < N_EXPERTS, logits, neg)
    m1 = jnp.max(logits, axis=1, keepdims=True)
    i1 = jnp.min(jnp.where(logits == m1, lane_f, big), axis=1, keepdims=True)
    rest = jnp.where(lane_f == i1, neg, logits)
    m2 = jnp.max(rest, axis=1, keepdims=True)
    i2 = jnp.min(jnp.where(rest == m2, lane_f, big), axis=1, keepdims=True)
    e2 = jnp.exp(m2 - m1)
    w1 = 1.0 / (1.0 + e2)
    w2 = e2 / (1.0 + e2)
    @pl.when(pl.program_id(0) == 0)
    def _():
        cnt_ref[...] = jnp.zeros(cnt_ref.shape, F32)

    o1 = jnp.where(lane_f == i1, 1.0, 0.0)
    o2 = jnp.where(lane_f == i2, 1.0, 0.0)
    pre1 = _dot(tri_ref[...], o1.astype(BF16))
    pre2 = _dot(tri_ref[...], o2.astype(BF16))
    tot1 = jnp.sum(o1, axis=0, keepdims=True)
    tot2 = jnp.sum(o2, axis=0, keepdims=True)
    carry = cnt_ref[0:1, :]
    r1 = jnp.sum(o1 * (carry + pre1), axis=1, keepdims=True)
    r2 = jnp.sum(o2 * (carry + tot1 + pre2), axis=1, keepdims=True)
    new_cnt = jnp.broadcast_to(carry + tot1 + tot2, cnt_ref.shape)
    cnt_ref[...] = new_cnt
    counts_ref[...] = new_cnt

    out = jnp.where(lane == 0, i1, jnp.where(lane == 1, i2, jnp.where(lane == 2, w1, w2)))
    out = jnp.where(lane == 4, r1, jnp.where(lane == 5, r2, out))
    route_ref[...] = jnp.where(lane < 6, out, 0.0)


def _merge(x_parts, y_ssm_raw, y_mla, qm, km, vm, sg, lw, tiles, with_router):
    t_tok = y_mla.shape[0]
    tm = tiles["tm"]
    seq_map = tiles["seq_map"]

    def row(i):
        return (i, 0)

    weights = [lw["w_glu"], lw["b_glu"], lw["w_branch"], lw["w_out"], lw["g_ffn"]]
    if with_router:
        tri = (jnp.arange(tm)[:, None] > jnp.arange(tm)[None, :]).astype(BF16)
        weights += [lw["w_router_hi"], lw["w_router_lo"], tri]
    in_specs = _stream_specs(x_parts, tm, D_MODEL) + [
        pl.BlockSpec((tm, SSM_WIDTH), row),
        pl.BlockSpec((tm, MLA_HEADS * MLA_V), row),
        pl.BlockSpec((tm, MEM_WIDTH), row),
        pl.BlockSpec((MEM_TOKENS, MEM_WIDTH), lambda i: (seq_map(i), 0)),
        pl.BlockSpec((MEM_TOKENS, MEM_WIDTH), lambda i: (seq_map(i), 0)),
        pl.BlockSpec((tm, N_BRANCH * D_MODEL), row),
    ] + [_const_spec(w.shape) for w in weights]
    out_specs = [pl.BlockSpec((tm, D_MODEL), row)]
    out_shape = [jax.ShapeDtypeStruct((t_tok, D_MODEL), F32)]
    if with_router:
        out_specs += [pl.BlockSpec((tm, ROW_TILES, LANES), lambda i: (i, 0, 0)),
                      pl.BlockSpec((tm, LANES), row),
                      pl.BlockSpec((SUBLANES, LANES), lambda i: (0, 0))]
        out_shape += [jax.ShapeDtypeStruct((t_tok, ROW_TILES, LANES), F32),
                      jax.ShapeDtypeStruct((t_tok, LANES), F32),
                      jax.ShapeDtypeStruct((SUBLANES, LANES), F32)]
    else:
        out_specs.append(pl.BlockSpec((tm, D_MODEL), row))
        out_shape.append(jax.ShapeDtypeStruct((t_tok, D_MODEL), BF16))
    kern = functools.partial(_merge_kernel, with_router=with_router, n_x=len(x_parts),
                             n_first=x_parts[0].shape[0] // tm)
    return pl.pallas_call(
        kern,
        grid=(t_tok // tm,),
        in_specs=in_specs,
        out_specs=out_specs,
        out_shape=out_shape,
        scratch_shapes=[pltpu.VMEM((SUBLANES, LANES), F32)] if with_router else [],
        compiler_params=_cparams(("arbitrary",) if with_router else ("parallel",)),
        name="merge_router" if with_router else "merge",
    )(*x_parts, y_ssm_raw, y_mla, qm, km, vm, sg, *weights)


def _swiglu_acc(x, w1_ref, w3_ref, w2_ref, acc, chunk):
    d_ff = w1_ref.shape[1]
    for c in range(d_ff // chunk):
        sl = slice(c * chunk, (c + 1) * chunk)
        g = _dot(x, w1_ref[:, sl])
        u = _dot(x, w3_ref[:, sl])
        a = (g * _sigmoid(g) * u).astype(BF16)
        acc = acc + _dot(a, w2_ref[sl, :])
    return acc


def _ffn_kernel(h_ref, x_ref, w1_ref, w3_ref, w2_ref, o_ref):
    o_ref[...] = _swiglu_acc(h_ref[...], w1_ref, w3_ref, w2_ref, x_ref[...], FFN_CHUNK)


def _dense_ffn(h2, x1, lw, tiles):
    t_tok = x1.shape[0]
    tm = tiles["ffn_tile"]

    def row(i):
        return (i, 0)

    def resident(shape):
        return pl.BlockSpec(shape, lambda i: (0, 0), pipeline_mode=pl.Buffered(1))

    return pl.pallas_call(
        _ffn_kernel,
        grid=(t_tok // tm,),
        in_specs=[pl.BlockSpec((tm, D_MODEL), row), pl.BlockSpec((tm, D_MODEL), row),
                  resident(lw["ffn_w1"].shape), resident(lw["ffn_w3"].shape),
                  resident(lw["ffn_w2"].shape)],
        out_specs=pl.BlockSpec((tm, D_MODEL), row),
        out_shape=jax.ShapeDtypeStruct((t_tok, D_MODEL), F32),
        compiler_params=_cparams(("parallel",)),
        name="dense_ffn",
    )(h2, x1, lw["ffn_w1"], lw["ffn_w3"], lw["ffn_w2"])


def _dispatch_kernel(pos_ref, h_ref, xs_in_ref, xs_ref, sem):
    del xs_in_ref
    tm = h_ref.shape[0]

    def tile_copy(r, dst):
        return pltpu.make_async_copy(h_ref.at[r], xs_ref.at[dst], sem)

    def start(r, c):
        tile_copy(r, pos_ref[0, 0, 2 * r]).start(priority=0)
        tile_copy(r, pos_ref[0, 0, 2 * r + 1]).start(priority=1)
        return c

    def wait(r, c):
        tile_copy(r, 0).wait()
        tile_copy(r, 0).wait()
        return c

    lax.fori_loop(0, tm, start, 0, unroll=DMA_UNROLL)
    lax.fori_loop(0, tm, wait, 0, unroll=DMA_UNROLL)


def _dispatch(h2t, pos3, n_rows, tm):
    t_tok = h2t.shape[0]
    xs0 = jnp.zeros((n_rows, ROW_TILES, LANES), F32)
    return pl.pallas_call(
        _dispatch_kernel,
        grid=(t_tok // tm,),
        in_specs=[pl.BlockSpec((1, 1, 2 * tm), lambda i: (i, 0, 0), memory_space=pltpu.SMEM),
                  pl.BlockSpec((tm, ROW_TILES, LANES), lambda i: (i, 0, 0)),
                  pl.BlockSpec(memory_space=pl.ANY)],
        out_specs=pl.BlockSpec(memory_space=pl.ANY),
        out_shape=jax.ShapeDtypeStruct(xs0.shape, F32),
        input_output_aliases={2: 0},
        scratch_shapes=[pltpu.SemaphoreType.DMA(())],
        compiler_params=_cparams(("arbitrary",)),
        name="moe_dispatch",
    )(pos3, h2t, xs0)


def _expert_kernel(be_ref, nu_ref, x_ref, w1_ref, w3_ref, w2_ref, y_ref):
    del be_ref
    i = pl.program_id(0)

    @pl.when(i < nu_ref[0])
    def _():
        x = jnp.concatenate(_token_tiles_to_rows(x_ref[...]), axis=1).astype(BF16)
        acc = _swiglu_acc(x, w1_ref, w3_ref, w2_ref, jnp.zeros((x.shape[0], D_MODEL), F32), MOE_CHUNK)
        y_ref[...] = _rows_to_token_tiles(acc)

    @pl.when(i >= nu_ref[0])
    def _():
        y_ref[...] = jnp.zeros(y_ref.shape, F32)


def _expert_ffn(xs, blk_expert, n_used, lw, bm):
    n_rows = xs.shape[0]

    def w_spec(shape):
        return pl.BlockSpec((None,) + shape, lambda i, be, nu: (be[i], 0, 0),
                            pipeline_mode=pl.Buffered(1))

    grid_spec = pltpu.PrefetchScalarGridSpec(
        num_scalar_prefetch=2,
        grid=(n_rows // bm,),
        in_specs=[
            pl.BlockSpec((bm, ROW_TILES, LANES), lambda i, be, nu: (i, 0, 0)),
            w_spec((D_MODEL, D_FF_EXPERT)), w_spec((D_MODEL, D_FF_EXPERT)),
            w_spec((D_FF_EXPERT, D_MODEL)),
        ],
        out_specs=pl.BlockSpec((bm, ROW_TILES, LANES), lambda i, be, nu: (i, 0, 0)),
    )
    return pl.pallas_call(
        _expert_kernel,
        grid_spec=grid_spec,
        out_shape=jax.ShapeDtypeStruct(xs.shape, F32),
        compiler_params=_cparams(("arbitrary",)),
        name="moe_experts",
    )(blk_expert, n_used, xs, lw["moe_w1"], lw["moe_w3"], lw["moe_w2"])


def _combine_kernel(pos_ref, pos_next_ref, ys_hbm, x_ref, route_ref, *refs, n_first):
    out_refs, (buf_ref, sem) = refs[:-2], refs[-2:]
    tm = x_ref.shape[0]
    i = pl.program_id(0)
    n = pl.num_programs(0)
    slot = i % 2

    def tile_copy(s, k, r, src):
        return pltpu.make_async_copy(ys_hbm.at[src], buf_ref.at[s, k, r], sem.at[s])

    def issue(p_ref, s):
        def start(r, c):
            tile_copy(s, 0, r, p_ref[0, 0, 2 * r]).start(priority=0)
            tile_copy(s, 1, r, p_ref[0, 0, 2 * r + 1]).start(priority=1)
            return c
        lax.fori_loop(0, tm, start, 0, unroll=DMA_UNROLL)

    @pl.when(i == 0)
    def _():
        issue(pos_ref, slot)

    @pl.when(i + 1 < n)
    def _():
        issue(pos_next_ref, 1 - slot)

    def wait(r, c):
        tile_copy(slot, 0, r, 0).wait()
        tile_copy(slot, 1, r, 0).wait()
        return c

    lax.fori_loop(0, tm, wait, 0, unroll=DMA_UNROLL)
    w1 = route_ref[:, 2:3]
    w2 = route_ref[:, 3:4]

    def write(o_ref):
        y1 = _token_tiles_to_rows(buf_ref[slot, 0])
        y2 = _token_tiles_to_rows(buf_ref[slot, 1])
        for j in range(ROW_TILES):
            sl = slice(j * LANES, (j + 1) * LANES)
            o_ref[:, sl] = x_ref[:, sl] + w1 * y1[j] + w2 * y2[j]

    if len(out_refs) == 1:
        write(out_refs[0])
    else:
        pl.when(i < n_first)(lambda: write(out_refs[0]))
        pl.when(i >= n_first)(lambda: write(out_refs[1]))


def _combine(ys, pos3, x1, route, tm, out_rows):
    t_tok = x1.shape[0]
    n_tiles = t_tok // tm
    n_first = out_rows[0] // tm
    if len(out_rows) == 1:
        out_specs = [pl.BlockSpec((tm, D_MODEL), lambda i: (i, 0))]
    else:
        out_specs = [pl.BlockSpec((tm, D_MODEL), lambda i: (jnp.minimum(i, n_first - 1), 0)),
                     pl.BlockSpec((tm, D_MODEL), lambda i: (jnp.maximum(i - n_first, 0), 0))]
    pos_spec = functools.partial(pl.BlockSpec, (1, 1, 2 * tm), memory_space=pltpu.SMEM)
    return pl.pallas_call(
        functools.partial(_combine_kernel, n_first=n_first),
        grid=(n_tiles,),
        in_specs=[pos_spec(lambda i: (i, 0, 0)),
                  pos_spec(lambda i: (jnp.minimum(i + 1, n_tiles - 1), 0, 0)),
                  pl.BlockSpec(memory_space=pl.ANY),
                  pl.BlockSpec((tm, D_MODEL), lambda i: (i, 0)),
                  pl.BlockSpec((tm, LANES), lambda i: (i, 0))],
        out_specs=out_specs,
        out_shape=[jax.ShapeDtypeStruct((r, D_MODEL), F32) for r in out_rows],
        scratch_shapes=[pltpu.VMEM((2, 2, tm, ROW_TILES, LANES), F32), pltpu.SemaphoreType.DMA((2,))],
        compiler_params=_cparams(("arbitrary",)),
        name="moe_combine",
    )(pos3, pos3, ys, x1, route)


def _moe_ffn(h2t, x1, route, counts, lw, tiles, out_rows):
    t_tok = x1.shape[0]
    tm = tiles["tm"]
    bm = tiles["moe_block"]
    n_assign = 2 * t_tok
    experts = route[:, :2].astype(jnp.int32).reshape(n_assign)
    rank = route[:, 4:6].astype(jnp.int32).reshape(n_assign)
    counts = counts[0, :N_EXPERTS].astype(jnp.int32)
    padded = ((counts + bm - 1) // bm) * bm
    ends = jnp.cumsum(padded)
    starts = ends - padded
    onehot = (experts[:, None] == jnp.arange(N_EXPERTS, dtype=jnp.int32)[None, :]).astype(jnp.int32)
    pos = jnp.sum(onehot * starts[None, :], axis=1) + rank
    n_blocks = n_assign // bm + N_EXPERTS
    blk_start = jnp.arange(n_blocks, dtype=jnp.int32) * bm
    blk_expert = jnp.sum((blk_start[:, None] >= ends[None, :]).astype(jnp.int32), axis=1)
    blk_expert = jnp.minimum(blk_expert, N_EXPERTS - 1)
    n_used = (ends[-1:] // bm).astype(jnp.int32)
    pos3 = pos.astype(jnp.int32).reshape(t_tok // tm, 1, 2 * tm)

    xs = _dispatch(h2t, pos3, n_blocks * bm, tm)
    ys = _expert_ffn(xs, blk_expert, n_used, lw, bm)
    return _combine(ys, pos3, x1, route, tm, out_rows)


def _head_cols(w, per_head, n_heads, offset=0):
    k = w.shape[0]
    w = w.reshape(k, n_heads, per_head)
    w = jnp.pad(w, ((0, 0), (0, 0), (offset, HEAD_PAD - per_head - offset)))
    return w.reshape(k, n_heads * HEAD_PAD)


def _swap_rope(w):
    z = jnp.zeros_like(w[..., :MLA_NOPE])
    return jnp.concatenate([z, w[..., MLA_NOPE + HALF_ROPE:], w[..., MLA_NOPE:MLA_NOPE + HALF_ROPE]], axis=-1)


def _pad_lane_block(v, offset=0):
    return jnp.pad(v, ((0, 0),) * (v.ndim - 1) + ((offset, LANES - v.shape[-1] - offset),))


def _layer_weights(i, p, rope_cos, rope_sin):
    o = IN_OFFSETS
    w_in = p["w_in"][i]
    lw = {}
    lw["g_mix"] = p["g_mix"][i].reshape(1, D_MODEL)
    lw["w_u"] = w_in[:, :o[0]].astype(BF16)
    lw["w_cq"] = w_in[:, o[0]:o[1]].astype(BF16)
    w_kr = w_in[:, o[2]:o[3]]
    w_kr_sw = jnp.concatenate([w_kr[:, HALF_ROPE:], w_kr[:, :HALF_ROPE]], axis=1)
    lw["w_ckv"] = jnp.concatenate([w_in[:, o[1]:o[2]], _pad_lane_block(w_kr, MLA_NOPE),
                                   _pad_lane_block(w_kr_sw, MLA_NOPE)], axis=1).astype(BF16)
    lw["w_qm"] = w_in[:, o[3]:o[4]].astype(BF16)
    lw["w_gate"] = w_in[:, o[4]:].astype(BF16)
    lw["g_q_lora"] = p["g_q_lora"][i].reshape(1, Q_LORA)
    w_uq = p["w_uq"][i].reshape(Q_LORA, MLA_HEADS, MLA_QK)
    lw["w_uq"] = jnp.concatenate(
        [_head_cols(w_uq.reshape(Q_LORA, -1), MLA_QK, MLA_HEADS),
         _head_cols(_swap_rope(w_uq).reshape(Q_LORA, -1), MLA_QK, MLA_HEADS)], axis=1).astype(BF16)
    lw["g_kv_lora"] = p["g_kv_lora"][i].reshape(1, KV_LORA)
    w_ukv = p["w_ukv"][i].reshape(KV_LORA, MLA_HEADS, MLA_NOPE + MLA_V)
    lw["w_kn"] = _head_cols(w_ukv[:, :, :MLA_NOPE].reshape(KV_LORA, -1), MLA_NOPE, MLA_HEADS).astype(BF16)
    w_v = w_ukv[:, :, MLA_NOPE:].reshape(KV_LORA, MLA_HEADS // 2, 2, MLA_V)
    w_v = jnp.stack([jnp.pad(w_v[:, :, 0], ((0, 0), (0, 0), (0, HEAD_PAD - MLA_V))),
                     jnp.pad(w_v[:, :, 1], ((0, 0), (0, 0), (HEAD_PAD - MLA_V, 0)))], axis=2)
    lw["w_v"] = w_v.reshape(KV_LORA, MLA_WIDTH_PAD).astype(BF16)
    one_even = jnp.zeros((HEAD_PAD,), F32).at[MLA_V].set(1.0)
    one_odd = jnp.zeros((HEAD_PAD,), F32).at[0].set(1.0)
    lw["v_one"] = jnp.tile(jnp.concatenate([one_even, one_odd]), MLA_HEADS // 2).reshape(1, MLA_WIDTH_PAD)
    for name, g in (("q", p["g_mla_q"][i]), ("k", p["g_mla_k"][i])):
        lw["rope_a" + name] = rope_cos * _pad_lane_block(g)[None, :]
        lw["rope_b" + name] = rope_sin * _pad_lane_block(_swap_rope(g))[None, :]
    lw["g_mem_q"] = jnp.tile(p["g_mem_q"][i], MEM_HEADS).reshape(1, MEM_WIDTH) * (MEM_HEAD_DIM ** -0.5)
    lw["g_mem"] = p["g_mem"][i].reshape(1, D_MODEL)
    lw["w_mem_kv"] = p["w_mem_kv"][i].astype(BF16)
    lw["g_mem_k"] = p["g_mem_k"][i].reshape(1, MEM_HEAD_DIM)
    lw["w_glu"] = p["ssm_w_glu"][i].astype(BF16)
    lw["b_glu"] = p["ssm_b_glu"][i].reshape(1, 2 * SSM_WIDTH)
    lw["w_branch"] = p["w_branch"][i].astype(BF16)
    lw["w_out"] = p["w_out"][i].astype(BF16)
    lw["g_ffn"] = p["g_ffn"][i].reshape(1, D_MODEL)
    lw["ssm"] = _ssm_tables(p["ssm_a_re"][i], p["ssm_a_im"][i], p["ssm_log_dt"][i], p["ssm_b_re"][i],
                            p["ssm_b_im"][i], p["ssm_c_re"][i], p["ssm_c_im"][i], p["ssm_d"][i])
    lw["score_bound"] = (1.02 * MLA_QK ** 0.5 * LOG2E
                         * jnp.max(jnp.abs(p["g_mla_q"][i])) * jnp.max(jnp.abs(p["g_mla_k"][i])))
    if i % 2 == 0:
        w13 = p["ffn_w13"][i // 2]
        lw["ffn_w1"] = w13[:, :D_FF].astype(BF16)
        lw["ffn_w3"] = w13[:, D_FF:].astype(BF16)
        lw["ffn_w2"] = p["ffn_w2"][i // 2].astype(BF16)
    else:
        w13 = p["moe_w13"][i // 2]
        lw["moe_w1"] = w13[:, :, :D_FF_EXPERT].astype(BF16)
        lw["moe_w3"] = w13[:, :, D_FF_EXPERT:].astype(BF16)
        lw["moe_w2"] = p["moe_w2"][i // 2].astype(BF16)
        wr = jnp.pad(p["moe_router"][i // 2], ((0, 0), (0, LANES - N_EXPERTS)))
        lw["w_router_hi"], lw["w_router_lo"] = _split_bf16(wr)
    return lw


def _rope_tables(max_len):
    inv = 1.0 / (ROPE_BASE ** (jnp.arange(0, MLA_ROPE, 2, dtype=F32) / MLA_ROPE))
    ang = jnp.arange(max_len, dtype=F32)[:, None] * inv[None, :]
    cos, sin = jnp.cos(ang), jnp.sin(ang)
    ones = jnp.ones((max_len, MLA_NOPE), F32)
    zn = jnp.zeros((max_len, MLA_NOPE), F32)
    zp = jnp.zeros((max_len, HEAD_PAD - MLA_QK), F32)
    return (jnp.concatenate([ones, cos, cos, zp], axis=1),
            jnp.concatenate([zn, -sin, sin, zp], axis=1))


def _gcd_tile(limit, *sizes):
    t = limit
    while any(s % t for s in sizes):
        t //= 2
    return t


def kernel(x_prompt, x_sample, mem_prompt, mem_sample, g_mix, w_in, ssm_a_re, ssm_a_im, ssm_log_dt, ssm_b_re, ssm_b_im, ssm_c_re, ssm_c_im, ssm_d, ssm_w_glu, ssm_b_glu, g_q_lora, w_uq, g_kv_lora, w_ukv, g_mla_q, g_mla_k, g_mem, w_mem_kv, g_mem_q, g_mem_k, w_branch, w_out, g_ffn, ffn_w13, ffn_w2, moe_router, moe_w13, moe_w2):
    params = dict(g_mix=g_mix, w_in=w_in, ssm_a_re=ssm_a_re, ssm_a_im=ssm_a_im, ssm_log_dt=ssm_log_dt,
                  ssm_b_re=ssm_b_re, ssm_b_im=ssm_b_im, ssm_c_re=ssm_c_re, ssm_c_im=ssm_c_im, ssm_d=ssm_d,
                  ssm_w_glu=ssm_w_glu, ssm_b_glu=ssm_b_glu, g_q_lora=g_q_lora, w_uq=w_uq,
                  g_kv_lora=g_kv_lora, w_ukv=w_ukv, g_mla_q=g_mla_q, g_mla_k=g_mla_k, g_mem=g_mem,
                  w_mem_kv=w_mem_kv, g_mem_q=g_mem_q, g_mem_k=g_mem_k, w_branch=w_branch, w_out=w_out,
                  g_ffn=g_ffn, ffn_w13=ffn_w13, ffn_w2=ffn_w2, moe_router=moe_router, moe_w13=moe_w13,
                  moe_w2=moe_w2)
    depth = g_mix.shape[0]
    bp, lp, _ = x_prompt.shape
    bs, ls, _ = x_sample.shape
    tp, ts = bp * lp, bs * ls
    t_tok = tp + ts

    tm = _gcd_tile(TOKEN_TILE, lp, ls)
    attn_q_tile = _gcd_tile(ATTN_Q_TILE, lp, ls)
    attn_kv_tile = _gcd_tile(ATTN_KV_TILE, lp, ls)
    ffn_tile = _gcd_tile(FFN_TILE, lp, ls)
    npt = tp // tm
    lp_t, ls_t = lp // tm, ls // tm

    def pos_map(i):
        return jnp.where(i < npt, i % lp_t, (i - npt) % ls_t)

    def seq_map(i):
        return jnp.where(i < npt, i // lp_t, bp + (i - npt) // ls_t)

    rope_cos, rope_sin = _rope_tables(max(lp, ls))
    head_ind = (jnp.arange(MLA_WIDTH_PAD)[:, None] // HEAD_PAD == jnp.arange(LANES)[None, :])
    mem_bd = (jnp.arange(MEM_WIDTH)[:, None] // MEM_HEAD_DIM == jnp.arange(MEM_WIDTH)[None, :] // MEM_HEAD_DIM)
    tiles = dict(tm=tm, ffn_tile=ffn_tile, pos_map=pos_map, seq_map=seq_map,
                 ones=jnp.ones((LANES, LANES), BF16),
                 head_ind=head_ind.astype(BF16),
                 head_ind_t=jnp.concatenate([head_ind.T, head_ind.T], axis=0).astype(BF16),
                 mem_bd=mem_bd.astype(BF16),
                 moe_block=_gcd_tile(MOE_BLOCK, 2 * t_tok))

    x_parts = [x_prompt.reshape(tp, D_MODEL), x_sample.reshape(ts, D_MODEL)]
    mem = jnp.concatenate([mem_prompt.reshape(bp * MEM_TOKENS, D_MODEL),
                           mem_sample.reshape(bs * MEM_TOKENS, D_MODEL)], axis=0)
    seqs = [(bp, lp), (bs, ls)]

    for i in range(depth):
        lw = _layer_weights(i, params, rope_cos, rope_sin)
        u, q, k, v, qm, sg = _in_projection(x_parts, lw, tiles)

        y_ssm_raw = _ssm_call(u, lw["ssm"], seqs)

        def attend(running_max, score_dtype):
            return lambda q_, k_, v_: _flash_attention(q_, k_, v_, seqs, attn_q_tile, attn_kv_tile,
                                                       running_max, score_dtype)

        y_mla = lax.cond(
            lw["score_bound"] <= MAX_FP8_LOG2_SCORE,
            attend(False, jnp.float8_e4m3fn),
            lambda q_, k_, v_: lax.cond(lw["score_bound"] <= MAX_SAFE_LOG2_SCORE,
                                        attend(False, BF16), attend(True, BF16), q_, k_, v_),
            q, k, v)
        km, vm = _memory_kv(mem, lw)

        with_router = i % 2 == 1
        outs = _merge(x_parts, y_ssm_raw, y_mla, qm, km, vm, sg, lw, tiles, with_router)
        if with_router:
            x1, h2t, route, counts = outs
            out_rows = (tp, ts) if i == depth - 1 else (t_tok,)
            x_parts = _moe_ffn(h2t, x1, route, counts, lw, tiles, out_rows)
        else:
            x1, h2 = outs
            x_parts = [_dense_ffn(h2, x1, lw, tiles)]

    if len(x_parts) == 1:
        x_parts = [x_parts[0][:tp], x_parts[0][tp:]]
    return (x_parts[0].reshape(bp, lp, D_MODEL), x_parts[1].reshape(bs, ls, D_MODEL))
```

```python
import functools
import math

import numpy as np
import jax
import jax.numpy as jnp
from jax import lax
from jax.experimental import pallas as pl
from jax.experimental.pallas import tpu as pltpu

F32 = jnp.float32
BF16 = jnp.bfloat16
EPS = 1e-6
LOG2E = math.log2(math.e)

D_MODEL = 1024
SSM_WIDTH = 512
SSM_GROUPS = 32
SSM_GROUP = 16
SSM_STATE = 64
MLA_HEADS = 8
MLA_NOPE = 64
MLA_ROPE = 32
MLA_V = 64
MLA_QK = MLA_NOPE + MLA_ROPE
Q_LORA = 256
KV_LORA = 128
ROPE_BASE = 10000.0
MEM_TOKENS = 256
MEM_HEADS = 4
MEM_HEAD_DIM = 128
MEM_WIDTH = 512
N_BRANCH = 3
D_FF = 2816
N_EXPERTS = 8
D_FF_EXPERT = 3584
IN_SIZES = (SSM_WIDTH, Q_LORA, KV_LORA, MLA_ROPE, MEM_WIDTH, N_BRANCH * D_MODEL)
IN_OFFSETS = tuple(int(v) for v in np.cumsum(IN_SIZES)[:-1])

LANES = 128
SUBLANES = 8
HEAD_PAD = 128
HALF_ROPE = MLA_ROPE // 2
MLA_WIDTH_PAD = MLA_HEADS * HEAD_PAD
ROW_TILES = D_MODEL // LANES
VMEM_LIMIT = 56 * 1024 * 1024
MAX_SAFE_LOG2_SCORE = 100.0
MAX_FP8_LOG2_SCORE = 20.0

TOKEN_TILE = 512
ATTN_Q_TILE = 2048
ATTN_KV_TILE = 2048
FFN_TILE = 1024
SSM_CHUNK = 16
MOE_BLOCK = 512
FFN_CHUNK = 256
MOE_CHUNK = 512
DMA_UNROLL = 8
COMBINE_PHASES = 4


def _cparams(sem):
    return pltpu.CompilerParams(dimension_semantics=sem, vmem_limit_bytes=VMEM_LIMIT)


def _const_spec(shape):
    nd = len(shape)
    return pl.BlockSpec(shape, lambda *_: (0,) * nd)


def _sigmoid(x):
    return 1.0 / (1.0 + jnp.exp(-x))


def _rms_scale(x, n):
    return lax.rsqrt(jnp.sum(x * x, axis=-1, keepdims=True) * (1.0 / n) + EPS)


def _split_bf16(x):
    hi = x.astype(BF16)
    lo = (x - hi.astype(F32)).astype(BF16)
    return hi, lo


def _dot(a, b):
    return jnp.dot(a, b, preferred_element_type=F32)


def _tile_transpose(vs):
    sub = lax.broadcasted_iota(jnp.int32, vs[0].shape, 1)
    for dist in (4, 2, 1):
        low = (sub // dist) % 2 == 0
        new = list(vs)
        for a in range(SUBLANES):
            if a & dist == 0:
                b = a | dist
                new[a] = jnp.where(low, vs[a], pltpu.roll(vs[b], dist, 1))
                new[b] = jnp.where(low, pltpu.roll(vs[a], SUBLANES - dist, 1), vs[b])
        vs = new
    return vs


def _rows_to_token_tiles(x):
    rows = x.shape[0]
    cols = [x[:, j * LANES:(j + 1) * LANES].reshape(rows // SUBLANES, SUBLANES, LANES) for j in range(ROW_TILES)]
    return jnp.stack(_tile_transpose(cols), axis=1).reshape(rows, ROW_TILES, LANES)


def _token_tiles_to_rows(t):
    rows = t.shape[0]
    t4 = t.reshape(rows // SUBLANES, SUBLANES, ROW_TILES, LANES)
    cols = _tile_transpose([t4[:, k] for k in range(SUBLANES)])
    return [c.reshape(rows, LANES) for c in cols]


def _stream_specs(parts, tm, width):
    if len(parts) == 1:
        return [pl.BlockSpec((tm, width), lambda i: (i, 0))]
    n0 = parts[0].shape[0] // tm
    return [pl.BlockSpec((tm, width), lambda i: (jnp.minimum(i, n0 - 1), 0)),
            pl.BlockSpec((tm, width), lambda i: (jnp.maximum(i - n0, 0), 0))]


def _stream_read(refs, n_first):
    if len(refs) == 1:
        return refs[0][...]
    return jnp.where(pl.program_id(0) < n_first, refs[0][...], refs[1][...])


def _inproj_kernel(*refs, n_x, n_first):
    x_refs = refs[:n_x]
    (gmix_ref, wu_ref, wg_ref, wqm_ref, wcq_ref, wckv_ref,
     gql_ref, wuq_ref, gkvl_ref, wkn_ref, wv_ref, vone_ref, gmq_ref,
     ones_ref, ind_ref, indt_ref, bd_ref,
     aq_ref, bq_ref, ak_ref, bk_ref,
     u_ref, q_ref, k_ref, v_ref, qm_ref, sg_ref) = refs[n_x:]

    def row_ssq(val):
        sq = val * val
        part = sq[:, :LANES]
        for j in range(1, val.shape[1] // LANES):
            part = part + sq[:, j * LANES:(j + 1) * LANES]
        return _dot(part.astype(BF16), ones_ref[...])

    def head_scale(val, scale):
        ss = _dot((val * val).astype(BF16), ind_ref[...])
        r = lax.rsqrt(ss * (1.0 / MLA_QK) + EPS) * scale
        return _dot(jnp.concatenate(_split_bf16(r), axis=1), indt_ref[...])

    x = _stream_read(x_refs, n_first)
    rs = lax.rsqrt(row_ssq(x) * (1.0 / D_MODEL) + EPS)
    h = (x * jnp.tile(rs, (1, D_MODEL // LANES)) * gmix_ref[...]).astype(BF16)

    u_ref[...] = _dot(h, wu_ref[...])

    for b in range(N_BRANCH):
        sl = slice(b * D_MODEL, (b + 1) * D_MODEL)
        sg_ref[:, sl] = _sigmoid(_dot(h, wg_ref[:, sl])).astype(BF16)

    zq = _dot(h, wqm_ref[...])
    zr = lax.rsqrt(_dot((zq * zq).astype(BF16), bd_ref[...]) * (1.0 / MEM_HEAD_DIM) + EPS)
    qm_ref[...] = (zq * zr * gmq_ref[...]).astype(BF16)

    cq = _dot(h, wcq_ref[...])
    cq_rs = lax.rsqrt(row_ssq(cq) * (1.0 / Q_LORA) + EPS)
    cqn = (cq * jnp.tile(cq_rs, (1, Q_LORA // LANES)) * gql_ref[...]).astype(BF16)
    qq = _dot(cqn, wuq_ref[...])
    qf = qq[:, :MLA_WIDTH_PAD]
    qsw = qq[:, MLA_WIDTH_PAD:]

    z3 = _dot(h, wckv_ref[...])
    ckv = z3[:, :LANES]
    kr = z3[:, LANES:2 * LANES]
    kr_sw = z3[:, 2 * LANES:]
    ckv_rs = lax.rsqrt(row_ssq(ckv) * (1.0 / KV_LORA) + EPS)
    ckvn = (ckv * ckv_rs * gkvl_ref[...]).astype(BF16)
    kf = _dot(ckvn, wkn_ref[...]) + jnp.tile(kr, (1, MLA_HEADS))
    v_ref[...] = (_dot(ckvn, wv_ref[...]) + vone_ref[...]).astype(BF16)

    q_rs = head_scale(qf, MLA_QK ** -0.5 * LOG2E)
    k_rs = head_scale(kf, 1.0)
    aq = aq_ref[...]
    bq = bq_ref[...]
    ak = ak_ref[...]
    k_rot = kr_sw * bk_ref[...]
    for hh in range(MLA_HEADS):
        sl = slice(hh * HEAD_PAD, (hh + 1) * HEAD_PAD)
        q_ref[:, sl] = ((qf[:, sl] * aq + qsw[:, sl] * bq) * q_rs[:, sl]).astype(BF16)
        k_ref[:, sl] = ((kf[:, sl] * ak + k_rot) * k_rs[:, sl]).astype(BF16)


def _in_projection(x_parts, lw, tiles):
    t_tok = sum(p.shape[0] for p in x_parts)
    tm = tiles["tm"]
    pos_map = tiles["pos_map"]

    def row(i):
        return (i, 0)

    weights = [lw["g_mix"], lw["w_u"], lw["w_gate"], lw["w_qm"], lw["w_cq"], lw["w_ckv"],
               lw["g_q_lora"], lw["w_uq"], lw["g_kv_lora"], lw["w_kn"], lw["w_v"], lw["v_one"],
               lw["g_mem_q"], tiles["ones"], tiles["head_ind"], tiles["head_ind_t"], tiles["mem_bd"]]
    rope = [lw["rope_aq"], lw["rope_bq"], lw["rope_ak"], lw["rope_bk"]]
    in_specs = _stream_specs(x_parts, tm, D_MODEL) + [_const_spec(w.shape) for w in weights]
    in_specs += [pl.BlockSpec((tm, HEAD_PAD), lambda i: (pos_map(i), 0))] * len(rope)
    kern = functools.partial(_inproj_kernel, n_x=len(x_parts), n_first=x_parts[0].shape[0] // tm)
    out_widths = (SSM_WIDTH, MLA_WIDTH_PAD, MLA_WIDTH_PAD, MLA_WIDTH_PAD, MEM_WIDTH, N_BRANCH * D_MODEL)
    return pl.pallas_call(
        kern,
        grid=(t_tok // tm,),
        in_specs=in_specs,
        out_specs=[pl.BlockSpec((tm, w), row) for w in out_widths],
        out_shape=[jax.ShapeDtypeStruct((t_tok, w), F32 if j == 0 else BF16) for j, w in enumerate(out_widths)],
        compiler_params=_cparams(("parallel",)),
        name="in_projection",
    )(*x_parts, *weights, *rope)


SSM_LANE_GROUPS = LANES // SSM_GROUP
SSM_TAB_DIR = 11
(_TAB_STEP, _TAB_POW, _TAB_A8, _TAB_MASK) = (0, 6, 8, 10)


def _granule_transpose(vs, lane):
    for dist in (4, 2, 1):
        shift = dist * SSM_GROUP
        low = (lane // shift) % 2 == 0
        new = list(vs)
        for a in range(SSM_LANE_GROUPS):
            if a & dist == 0:
                b = a | dist
                new[a] = jnp.where(low, vs[a], pltpu.roll(vs[b], shift, 1))
                new[b] = jnp.where(low, pltpu.roll(vs[a], LANES - shift, 1), vs[b])
        vs = new
    return vs


def _chunk_scan(x, xs, tab_ref, base, keep, backward):
    nv, _, nl = x.shape
    for j, k in enumerate((1, 2, 4)):
        shift = SUBLANES - k if backward else k
        rx = pltpu.roll(x, shift, 1)
        rxs = pltpu.roll(xs, shift, 1)
        a1 = tab_ref[base + _TAB_STEP + 2 * j][None]
        a2 = tab_ref[base + _TAB_STEP + 2 * j + 1][None]
        x, xs = x + a1 * rx + a2 * rxs, xs + a1 * rxs - a2 * rx
    a1 = tab_ref[base + _TAB_A8]
    a2 = tab_ref[base + _TAB_A8 + 1]
    edge = 0 if backward else SUBLANES - 1
    c = jnp.zeros((SUBLANES, nl), F32)
    cs = c
    cb = [None] * nv
    csb = [None] * nv
    for v in (range(nv - 1, -1, -1) if backward else range(nv)):
        kp = keep(v)
        c = c * kp
        cs = cs * kp
        cb[v] = c
        csb[v] = cs
        ex = jnp.broadcast_to(x[v, edge:edge + 1, :], (SUBLANES, nl))
        exs = jnp.broadcast_to(xs[v, edge:edge + 1, :], (SUBLANES, nl))
        c, cs = ex + a1 * c + a2 * cs, exs + a1 * cs - a2 * c
    xe = pltpu.roll(x, SUBLANES - 1 if backward else 1, 1) * tab_ref[base + _TAB_MASK][None]
    return (xe + tab_ref[base + _TAB_POW][None] * jnp.stack(cb)
            + tab_ref[base + _TAB_POW + 1][None] * jnp.stack(csb))


def _ssm_kernel(keepf_ref, keepb_ref, u_ref, t_ref, win_ref, wout_ref, tab_ref, y_ref):
    blk = pl.program_id(1)
    rows = u_ref.shape[0] // SSM_CHUNK
    nv = rows // SUBLANES
    ng = SSM_LANE_GROUPS
    lane = lax.broadcasted_iota(jnp.int32, (rows, LANES), 1)

    pieces = [u_ref[pl.ds(t, rows, stride=SSM_CHUNK), :] for t in range(SSM_CHUNK)]
    halves = [_granule_transpose(pieces[ng * h:ng * (h + 1)], lane) for h in range(SSM_CHUNK // ng)]

    ys = []
    parts = [[], [], [], []]
    for g in range(ng):
        xg = jnp.concatenate([h[g] for h in halves], axis=1).astype(BF16)
        ys.append(_dot(xg, t_ref[g]))
        s4 = _dot(xg, win_ref[g])
        for j in range(4):
            parts[j].append(s4[:, j * LANES:(j + 1) * LANES])
    f, fs, b, bs = [jnp.concatenate(p, axis=1).reshape(nv, SUBLANES, ng * LANES) for p in parts]

    xin_f = _chunk_scan(f, fs, tab_ref, 0, lambda v: keepf_ref[blk * nv + v].astype(F32), False)
    xin_b = _chunk_scan(b, bs, tab_ref, SSM_TAB_DIR, lambda v: keepb_ref[blk * nv + v].astype(F32), True)
    xin_f = xin_f.reshape(rows, ng * LANES)
    xin_b = xin_b.reshape(rows, ng * LANES)

    outs = []
    for g in range(ng):
        sl = slice(g * LANES, (g + 1) * LANES)
        xs = jnp.concatenate([xin_f[:, sl], xin_b[:, sl]], axis=1).astype(BF16)
        outs.append(ys[g] + _dot(xs, wout_ref[g]))
    for h in range(SSM_CHUNK // ng):
        back = _granule_transpose([o[:, h * LANES:(h + 1) * LANES] for o in outs], lane)
        for tt in range(ng):
            y_ref[pl.ds(ng * h + tt, rows, stride=SSM_CHUNK), :] = back[tt]


def _ssm_call(u, tabs, seqs):
    t_tok = u.shape[0]
    block = int(np.lcm.reduce([ln for _, ln in seqs]))
    vreg_tokens = SUBLANES * SSM_CHUNK
    assert all((n * ln) % block == 0 and ln % vreg_tokens == 0 for n, ln in seqs), seqs
    nv = block // vreg_tokens
    keep_f, keep_b = [], []
    for n_seq, seq_len in seqs:
        for _ in range(n_seq * seq_len // block):
            for v in range(nv):
                keep_f.append(int((v * vreg_tokens) % seq_len != 0))
                keep_b.append(int(((v + 1) * vreg_tokens) % seq_len != 0))
    keep = [jnp.asarray(np.asarray(a, np.int32)) for a in (keep_f, keep_b)]
    ng = SSM_LANE_GROUPS
    width = SSM_CHUNK * SSM_GROUP
    n_tab = 2 * SSM_TAB_DIR
    grid_spec = pltpu.PrefetchScalarGridSpec(
        num_scalar_prefetch=2,
        grid=(SSM_WIDTH // LANES, t_tok // block),
        in_specs=[
            pl.BlockSpec((block, LANES), lambda g, b, kf, kb: (b, g)),
            pl.BlockSpec((ng, width, width), lambda g, b, kf, kb: (g, 0, 0)),
            pl.BlockSpec((ng, width, 4 * LANES), lambda g, b, kf, kb: (g, 0, 0)),
            pl.BlockSpec((ng, 2 * LANES, width), lambda g, b, kf, kb: (g, 0, 0)),
            pl.BlockSpec((n_tab, SUBLANES, ng * LANES), lambda g, b, kf, kb: (0, 0, g)),
        ],
        out_specs=pl.BlockSpec((block, LANES), lambda g, b, kf, kb: (b, g)),
    )
    return pl.pallas_call(
        _ssm_kernel,
        grid_spec=grid_spec,
        out_shape=jax.ShapeDtypeStruct(u.shape, F32),
        compiler_params=_cparams(("parallel", "parallel")),
        name="ssm_chunked",
    )(*keep, u, tabs["t"], tabs["w_in"], tabs["w_out"], tabs["scan"])


def _ssm_tables(a_re, a_im, log_dt, b_re, b_im, c_re, c_im, d_skip):
    q = SSM_CHUNK
    hp = lax.Precision.HIGHEST
    dt = jnp.exp(log_dt)[..., None, None]
    ks = jnp.arange(q + 1, dtype=F32)
    mag = jnp.exp(a_re[..., None] * dt * ks)
    ang = a_im[..., None] * dt * ks
    pw_re = mag * jnp.cos(ang)
    pw_im = mag * jnp.sin(ang)
    lb_re, lb_im = pw_re[..., 1], pw_im[..., 1]
    den = a_re * a_re + a_im * a_im
    f_re = ((lb_re - 1.0) * a_re + lb_im * a_im) / den
    f_im = (lb_im * a_re - (lb_re - 1.0) * a_im) / den
    bb_re = f_re[..., None] * b_re - f_im[..., None] * b_im
    bb_im = f_re[..., None] * b_im + f_im[..., None] * b_re

    cl_re = c_re[..., None] * pw_re[:, :, None] - c_im[..., None] * pw_im[:, :, None]
    cl_im = c_re[..., None] * pw_im[:, :, None] + c_im[..., None] * pw_re[:, :, None]

    kern = (jnp.einsum("dgnpk,dgpm->dgknm", cl_re[..., :q], bb_re, precision=hp)
            - jnp.einsum("dgnpk,dgpm->dgknm", cl_im[..., :q], bb_im, precision=hp))
    s_idx = jnp.arange(q)[:, None]
    t_idx = jnp.arange(q)[None, :]
    lag_f = jnp.clip(t_idx - s_idx, 0, q - 1)
    lag_b = jnp.clip(s_idx - t_idx, 0, q - 1)
    tf = kern[0][:, lag_f] * (t_idx >= s_idx)[None, :, :, None, None]
    tb = kern[1][:, lag_b] * (s_idx >= t_idx)[None, :, :, None, None]
    t_mat = (tf + tb).transpose(0, 1, 4, 2, 3)
    t_mat = t_mat.reshape(SSM_GROUPS, q * SSM_GROUP, q * SSM_GROUP)
    d_rep = jnp.tile(d_skip.reshape(SSM_GROUPS, 1, SSM_GROUP), (1, q, 1)).reshape(SSM_GROUPS, -1)
    t_mat = t_mat + d_rep[:, :, None] * jnp.eye(q * SSM_GROUP, dtype=F32)[None]

    def w_in_dir(d, exps):
        pr = pw_re[d][:, :, exps]
        pi = pw_im[d][:, :, exps]
        re = pr[..., None] * bb_re[d][:, :, None, :] - pi[..., None] * bb_im[d][:, :, None, :]
        im = pr[..., None] * bb_im[d][:, :, None, :] + pi[..., None] * bb_re[d][:, :, None, :]
        re = re.transpose(0, 2, 3, 1).reshape(SSM_GROUPS, q * SSM_GROUP, SSM_STATE)
        im = im.transpose(0, 2, 3, 1).reshape(SSM_GROUPS, q * SSM_GROUP, SSM_STATE)
        return jnp.concatenate([re, im, im, re], axis=-1)

    w_in = jnp.concatenate([w_in_dir(0, q - 1 - jnp.arange(q)), w_in_dir(1, jnp.arange(q))], axis=-1)

    def w_out_dir(d, exps):
        re = cl_re[d][..., exps]
        im = cl_im[d][..., exps]
        re = re.transpose(0, 2, 3, 1).reshape(SSM_GROUPS, SSM_STATE, q * SSM_GROUP)
        im = im.transpose(0, 2, 3, 1).reshape(SSM_GROUPS, SSM_STATE, q * SSM_GROUP)
        return jnp.concatenate([re, -im], axis=1)

    w_out = jnp.concatenate([w_out_dir(0, 1 + jnp.arange(q)), w_out_dir(1, q - jnp.arange(q))], axis=1)

    ms = jnp.arange(SUBLANES + 1, dtype=F32) * q
    cr = jnp.exp(a_re[..., None] * dt * ms) * jnp.cos(a_im[..., None] * dt * ms)
    ci = jnp.exp(a_re[..., None] * dt * ms) * jnp.sin(a_im[..., None] * dt * ms)

    def pat(d, m):
        ar, ai = cr[d][..., m], ci[d][..., m]
        return (jnp.concatenate([ar, ar], axis=-1).reshape(-1), jnp.concatenate([-ai, ai], axis=-1).reshape(-1))

    row = jnp.arange(SUBLANES)

    def dir_tables(d, backward):
        tabs = []
        for k in (1, 2, 4):
            valid = (row <= SUBLANES - 1 - k) if backward else (row >= k)
            tabs += [valid[:, None] * p[None, :] for p in pat(d, k)]
        pows = [pat(d, SUBLANES - 1 - i if backward else i) for i in range(SUBLANES)]
        tabs += [jnp.stack([p[0] for p in pows]), jnp.stack([p[1] for p in pows])]
        tabs += [jnp.tile(p[None, :], (SUBLANES, 1)) for p in pat(d, SUBLANES)]
        excl = (row <= SUBLANES - 2) if backward else (row >= 1)
        tabs.append(jnp.tile(excl[:, None].astype(F32), (1, SSM_GROUPS * LANES)))
        return tabs

    scan_tab = jnp.stack(dir_tables(0, False) + dir_tables(1, True), axis=0).astype(F32)
    return {"t": t_mat.astype(BF16), "w_in": w_in.astype(BF16), "w_out": w_out.astype(BF16),
            "scan": scan_tab}


def _flash_kernel(qb_ref, kb_ref, first_ref, last_ref, q_ref, k_ref, v_ref, o_ref,
                  acc_ref, *extra, running_max, score_dtype):
    del qb_ref, kb_ref
    s_id = pl.program_id(1)
    m_scratch = extra if running_max else ()
    qc_ref = None if (running_max or score_dtype == BF16) else extra[0]

    @pl.when(first_ref[s_id] == 1)
    def _():
        acc_ref[...] = jnp.zeros(acc_ref.shape, F32)
        if running_max:
            m_scratch[0][...] = jnp.full(m_scratch[0].shape, -jnp.inf, F32)
        if qc_ref is not None:
            qc_ref[...] = q_ref[...].astype(score_dtype)

    for hh in range(2):
        sl = slice(hh * HEAD_PAD, (hh + 1) * HEAD_PAD)
        qh = q_ref[:, sl] if qc_ref is None else qc_ref[:, sl]
        s = lax.dot_general(qh, k_ref[:, sl].astype(score_dtype),
                            (((1,), (1,)), ((), ())), preferred_element_type=F32)
        if running_max:
            m_ref = m_scratch[0]
            m_prev = m_ref[hh]
            m_new = jnp.maximum(m_prev, jnp.max(s, axis=1, keepdims=True))
            p = jnp.exp2(s - m_new[:, :1]).astype(BF16)
            acc_ref[hh] = jnp.exp2(m_prev - m_new) * acc_ref[hh] + _dot(p, v_ref[:, sl])
            m_ref[hh] = m_new
        else:
            acc_ref[hh] += _dot(jnp.exp2(s.astype(BF16)), v_ref[:, sl])

    @pl.when(last_ref[s_id] == 1)
    def _():
        lane = lax.broadcasted_iota(jnp.int32, o_ref.shape, 1)
        a0 = acc_ref[0]
        a1 = acc_ref[1]
        o0 = a0 / a0[:, MLA_V:MLA_V + 1]
        o1 = a1 / a1[:, 0:1]
        o_ref[...] = jnp.where(lane < MLA_V, o0, o1).astype(BF16)


def _flash_attention(q, k, v, seqs, q_tile, kv_tile, running_max, score_dtype=BF16):
    qb, kb, first, last = [], [], [], []
    base = 0
    for n_seq, seq_len in seqs:
        nq, nk = seq_len // q_tile, seq_len // kv_tile
        for _ in range(n_seq):
            for qi in range(nq):
                for ki in range(nk):
                    qb.append(base // q_tile + qi)
                    kb.append(base // kv_tile + ki)
                    first.append(int(ki == 0))
                    last.append(int(ki == nk - 1))
            base += seq_len
    tabs = [jnp.asarray(np.asarray(a, np.int32)) for a in (qb, kb, first, last)]
    n_steps = len(qb)
    t_tok = q.shape[0]
    pair_w = 2 * HEAD_PAD
    scratch = [pltpu.VMEM((2, q_tile, LANES), F32)] * (2 if running_max else 1)
    if not running_max and score_dtype != BF16:
        scratch.append(pltpu.VMEM((q_tile, pair_w), score_dtype))
    grid_spec = pltpu.PrefetchScalarGridSpec(
        num_scalar_prefetch=4,
        grid=(MLA_HEADS // 2, n_steps),
        in_specs=[
            pl.BlockSpec((q_tile, pair_w), lambda p, s, qb, kb, fi, la: (qb[s], p)),
            pl.BlockSpec((kv_tile, pair_w), lambda p, s, qb, kb, fi, la: (kb[s], p)),
            pl.BlockSpec((kv_tile, pair_w), lambda p, s, qb, kb, fi, la: (kb[s], p)),
        ],
        out_specs=pl.BlockSpec((q_tile, 2 * MLA_V), lambda p, s, qb, kb, fi, la: (qb[s], p)),
        scratch_shapes=scratch,
    )
    return pl.pallas_call(
        functools.partial(_flash_kernel, running_max=running_max, score_dtype=score_dtype),
        grid_spec=grid_spec,
        out_shape=jax.ShapeDtypeStruct((t_tok, MLA_HEADS * MLA_V), BF16),
        compiler_params=_cparams(("parallel", "arbitrary")),
        name="mla_flash_max" if running_max else ("mla_flash" if score_dtype == BF16 else "mla_flash_f8"),
    )(*tabs, q, k, v)


def _memkv_kernel(mem_ref, gmem_ref, w_ref, gk_ref, k_ref, v_ref):
    x = mem_ref[...]
    h = (x * _rms_scale(x, D_MODEL) * gmem_ref[...]).astype(BF16)
    kv = _dot(h, w_ref[...])
    for hh in range(MEM_HEADS):
        sl = slice(hh * MEM_HEAD_DIM, (hh + 1) * MEM_HEAD_DIM)
        blk = kv[:, sl]
        k_ref[:, sl] = (blk * _rms_scale(blk, MEM_HEAD_DIM) * gk_ref[...]).astype(BF16)
    v_ref[...] = kv[:, MEM_WIDTH:].astype(BF16)


def _memory_kv(mem, lw):
    rows = mem.shape[0]
    tm = MEM_TOKENS
    return pl.pallas_call(
        _memkv_kernel,
        grid=(rows // tm,),
        in_specs=[pl.BlockSpec((tm, D_MODEL), lambda i: (i, 0)),
                  _const_spec(lw["g_mem"].shape), _const_spec(lw["w_mem_kv"].shape),
                  _const_spec(lw["g_mem_k"].shape)],
        out_specs=[pl.BlockSpec((tm, MEM_WIDTH), lambda i: (i, 0))] * 2,
        out_shape=[jax.ShapeDtypeStruct((rows, MEM_WIDTH), BF16)] * 2,
        compiler_params=_cparams(("parallel",)),
        name="memory_kv",
    )(mem, lw["g_mem"], lw["w_mem_kv"], lw["g_mem_k"])


def _gelu_tanh(x):
    return 0.5 * x * (1.0 + jnp.tanh(0.7978845608028654 * (x + 0.044715 * x * x * x)))


def _merge_kernel(*refs, with_router, n_x, n_first):
    x_refs = refs[:n_x]
    refs = refs[n_x:]
    if with_router:
        (ys_ref, ya_ref, qm_ref, km_ref, vm_ref, sg_ref, wglu_ref, bglu_ref, wb_ref,
         wo_ref, gffn_ref, wr_hi_ref, wr_lo_ref, tri_ref, x1_ref, h2_ref, route_ref, counts_ref,
         cnt_ref) = refs
    else:
        (ys_ref, ya_ref, qm_ref, km_ref, vm_ref, sg_ref, wglu_ref, bglu_ref, wb_ref,
         wo_ref, gffn_ref, x1_ref, h2_ref) = refs

    ya = _gelu_tanh(ys_ref[...]).astype(BF16)
    z = _dot(ya, wglu_ref[...]) + bglu_ref[...]
    y_ssm = (z[:, :SSM_WIDTH] * _sigmoid(z[:, SSM_WIDTH:])).astype(BF16)

    mem_parts = []
    for hh in range(MEM_HEADS):
        sl = slice(hh * MEM_HEAD_DIM, (hh + 1) * MEM_HEAD_DIM)
        s = lax.dot_general(qm_ref[:, sl], km_ref[:, sl], (((1,), (1,)), ((), ())),
                            preferred_element_type=F32)
        p = jnp.exp(s - jnp.max(s, axis=1, keepdims=True))
        p = p / jnp.sum(p, axis=1, keepdims=True)
        mem_parts.append(_dot(p.astype(BF16), vm_ref[:, sl]))
    y_mem = jnp.concatenate(mem_parts, axis=1).astype(BF16)

    merged = None
    for b, yb in enumerate((y_ssm, ya_ref[...], y_mem)):
        gate = sg_ref[:, b * D_MODEL:(b + 1) * D_MODEL].astype(F32)
        term = gate * _dot(yb, wb_ref[b])
        merged = term if merged is None else merged + term

    x1 = _stream_read(x_refs, n_first) + _dot(merged.astype(BF16), wo_ref[...])
    x1_ref[...] = x1
    h2 = x1 * _rms_scale(x1, D_MODEL) * gffn_ref[...]

    if not with_router:
        h2_ref[...] = h2.astype(BF16)
        return

    h2_ref[...] = _rows_to_token_tiles(h2)

    h_hi, h_lo = _split_bf16(h2)
    logits = _dot(h_hi, wr_hi_ref[...]) + _dot(h_hi, wr_lo_ref[...]) + _dot(h_lo, wr_hi_ref[...])
    lane = lax.broadcasted_iota(jnp.int32, logits.shape, 1)
    lane_f = lane.astype(F32)
    neg = jnp.float32(-jnp.inf)
    big = jnp.float32(LANES)
    logits = jnp.where(lane < N_EXPERTS, logits, neg)
    m1 = jnp.max(logits, axis=1, keepdims=True)
    i1 = jnp.min(jnp.where(logits == m1, lane_f, big), axis=1, keepdims=True)
    rest = jnp.where(lane_f == i1, neg, logits)
    m2 = jnp.max(rest, axis=1, keepdims=True)
    i2 = jnp.min(jnp.where(rest == m2, lane_f, big), axis=1, keepdims=True)
    e2 = jnp.exp(m2 - m1)
    w1 = 1.0 / (1.0 + e2)
    w2 = e2 / (1.0 + e2)
    @pl.when(pl.program_id(0) == 0)
    def _():
        cnt_ref[...] = jnp.zeros(cnt_ref.shape, F32)

    o1 = jnp.where(lane_f == i1, 1.0, 0.0)
    o2 = jnp.where(lane_f == i2, 1.0, 0.0)
    pre1 = _dot(tri_ref[...], o1.astype(BF16))
    pre2 = _dot(tri_ref[...], o2.astype(BF16))
    tot1 = jnp.sum(o1, axis=0, keepdims=True)
    tot2 = jnp.sum(o2, axis=0, keepdims=True)
    carry = cnt_ref[0:1, :]
    r1 = jnp.sum(o1 * (carry + pre1), axis=1, keepdims=True)
    r2 = jnp.sum(o2 * (carry + tot1 + pre2), axis=1, keepdims=True)
    new_cnt = jnp.broadcast_to(carry + tot1 + tot2, cnt_ref.shape)
    cnt_ref[...] = new_cnt
    counts_ref[...] = new_cnt

    out = jnp.where(lane == 0, i1, jnp.where(lane == 1, i2, jnp.where(lane == 2, w1, w2)))
    out = jnp.where(lane == 4, r1, jnp.where(lane == 5, r2, out))
    route_ref[...] = jnp.where(lane < 6, out, 0.0)


def _merge(x_parts, y_ssm_raw, y_mla, qm, km, vm, sg, lw, tiles, with_router):
    t_tok = y_mla.shape[0]
    tm = tiles["tm"]
    seq_map = tiles["seq_map"]

    def row(i):
        return (i, 0)

    weights = [lw["w_glu"], lw["b_glu"], lw["w_branch"], lw["w_out"], lw["g_ffn"]]
    if with_router:
        tri = (jnp.arange(tm)[:, None] > jnp.arange(tm)[None, :]).astype(BF16)
        weights += [lw["w_router_hi"], lw["w_router_lo"], tri]
    in_specs = _stream_specs(x_parts, tm, D_MODEL) + [
        pl.BlockSpec((tm, SSM_WIDTH), row),
        pl.BlockSpec((tm, MLA_HEADS * MLA_V), row),
        pl.BlockSpec((tm, MEM_WIDTH), row),
        pl.BlockSpec((MEM_TOKENS, MEM_WIDTH), lambda i: (seq_map(i), 0)),
        pl.BlockSpec((MEM_TOKENS, MEM_WIDTH), lambda i: (seq_map(i), 0)),
        pl.BlockSpec((tm, N_BRANCH * D_MODEL), row),
    ] + [_const_spec(w.shape) for w in weights]
    out_specs = [pl.BlockSpec((tm, D_MODEL), row)]
    out_shape = [jax.ShapeDtypeStruct((t_tok, D_MODEL), F32)]
    if with_router:
        out_specs += [pl.BlockSpec((tm, ROW_TILES, LANES), lambda i: (i, 0, 0)),
                      pl.BlockSpec((tm, LANES), row),
                      pl.BlockSpec((SUBLANES, LANES), lambda i: (0, 0))]
        out_shape += [jax.ShapeDtypeStruct((t_tok, ROW_TILES, LANES), F32),
                      jax.ShapeDtypeStruct((t_tok, LANES), F32),
                      jax.ShapeDtypeStruct((SUBLANES, LANES), F32)]
    else:
        out_specs.append(pl.BlockSpec((tm, D_MODEL), row))
        out_shape.append(jax.ShapeDtypeStruct((t_tok, D_MODEL), BF16))
    kern = functools.partial(_merge_kernel, with_router=with_router, n_x=len(x_parts),
                             n_first=x_parts[0].shape[0] // tm)
    return pl.pallas_call(
        kern,
        grid=(t_tok // tm,),
        in_specs=in_specs,
        out_specs=out_specs,
        out_shape=out_shape,
        scratch_shapes=[pltpu.VMEM((SUBLANES, LANES), F32)] if with_router else [],
        compiler_params=_cparams(("arbitrary",) if with_router else ("parallel",)),
        name="merge_router" if with_router else "merge",
    )(*x_parts, y_ssm_raw, y_mla, qm, km, vm, sg, *weights)


def _swiglu_acc(x, w1_ref, w3_ref, w2_ref, acc, chunk):
    d_ff = w1_ref.shape[1]
    for c in range(d_ff // chunk):
        sl = slice(c * chunk, (c + 1) * chunk)
        g = _dot(x, w1_ref[:, sl])
        u = _dot(x, w3_ref[:, sl])
        a = (g * _sigmoid(g) * u).astype(BF16)
        acc = acc + _dot(a, w2_ref[sl, :])
    return acc


def _ffn_kernel(h_ref, x_ref, w1_ref, w3_ref, w2_ref, o_ref):
    o_ref[...] = _swiglu_acc(h_ref[...], w1_ref, w3_ref, w2_ref, x_ref[...], FFN_CHUNK)


def _dense_ffn(h2, x1, lw, tiles):
    t_tok = x1.shape[0]
    tm = tiles["ffn_tile"]

    def row(i):
        return (i, 0)

    def resident(shape):
        return pl.BlockSpec(shape, lambda i: (0, 0), pipeline_mode=pl.Buffered(1))

    return pl.pallas_call(
        _ffn_kernel,
        grid=(t_tok // tm,),
        in_specs=[pl.BlockSpec((tm, D_MODEL), row), pl.BlockSpec((tm, D_MODEL), row),
                  resident(lw["ffn_w1"].shape), resident(lw["ffn_w3"].shape),
                  resident(lw["ffn_w2"].shape)],
        out_specs=pl.BlockSpec((tm, D_MODEL), row),
        out_shape=jax.ShapeDtypeStruct((t_tok, D_MODEL), F32),
        compiler_params=_cparams(("parallel",)),
        name="dense_ffn",
    )(h2, x1, lw["ffn_w1"], lw["ffn_w3"], lw["ffn_w2"])


def _dispatch_kernel(pos_ref, h_ref, xs_in_ref, xs_ref, sem):
    del xs_in_ref
    tm = h_ref.shape[0]

    def tile_copy(r, dst):
        return pltpu.make_async_copy(h_ref.at[r], xs_ref.at[dst], sem)

    def start(r, c):
        tile_copy(r, pos_ref[0, 0, 2 * r]).start(priority=0)
        tile_copy(r, pos_ref[0, 0, 2 * r + 1]).start(priority=1)
        return c

    def wait(r, c):
        tile_copy(r, 0).wait()
        tile_copy(r, 0).wait()
        return c

    lax.fori_loop(0, tm, start, 0, unroll=DMA_UNROLL)
    lax.fori_loop(0, tm, wait, 0, unroll=DMA_UNROLL)


def _dispatch(h2t, pos3, n_rows, tm):
    t_tok = h2t.shape[0]
    xs0 = jnp.zeros((n_rows, ROW_TILES, LANES), F32)
    return pl.pallas_call(
        _dispatch_kernel,
        grid=(t_tok // tm,),
        in_specs=[pl.BlockSpec((1, 1, 2 * tm), lambda i: (i, 0, 0), memory_space=pltpu.SMEM),
                  pl.BlockSpec((tm, ROW_TILES, LANES), lambda i: (i, 0, 0)),
                  pl.BlockSpec(memory_space=pl.ANY)],
        out_specs=pl.BlockSpec(memory_space=pl.ANY),
        out_shape=jax.ShapeDtypeStruct(xs0.shape, F32),
        input_output_aliases={2: 0},
        scratch_shapes=[pltpu.SemaphoreType.DMA(())],
        compiler_params=_cparams(("arbitrary",)),
        name="moe_dispatch",
    )(pos3, h2t, xs0)


def _expert_kernel(be_ref, nu_ref, x_ref, w1_ref, w3_ref, w2_ref, y_ref):
    del be_ref
    i = pl.program_id(0)

    @pl.when(i < nu_ref[0])
    def _():
        x = jnp.concatenate(_token_tiles_to_rows(x_ref[...]), axis=1).astype(BF16)
        acc = _swiglu_acc(x, w1_ref, w3_ref, w2_ref, jnp.zeros((x.shape[0], D_MODEL), F32), MOE_CHUNK)
        y_ref[...] = _rows_to_token_tiles(acc)

    @pl.when(i >= nu_ref[0])
    def _():
        y_ref[...] = jnp.zeros(y_ref.shape, F32)


def _expert_ffn(xs, blk_expert, n_used, lw, bm):
    n_rows = xs.shape[0]

    def w_spec(shape):
        return pl.BlockSpec((None,) + shape, lambda i, be, nu: (be[i], 0, 0),
                            pipeline_mode=pl.Buffered(1))

    grid_spec = pltpu.PrefetchScalarGridSpec(
        num_scalar_prefetch=2,
        grid=(n_rows // bm,),
        in_specs=[
            pl.BlockSpec((bm, ROW_TILES, LANES), lambda i, be, nu: (i, 0, 0)),
            w_spec((D_MODEL, D_FF_EXPERT)), w_spec((D_MODEL, D_FF_EXPERT)),
            w_spec((D_FF_EXPERT, D_MODEL)),
        ],
        out_specs=pl.BlockSpec((bm, ROW_TILES, LANES), lambda i, be, nu: (i, 0, 0)),
    )
    return pl.pallas_call(
        _expert_kernel,
        grid_spec=grid_spec,
        out_shape=jax.ShapeDtypeStruct(xs.shape, F32),
        compiler_params=_cparams(("arbitrary",)),
        name="moe_experts",
    )(blk_expert, n_used, xs, lw["moe_w1"], lw["moe_w3"], lw["moe_w2"])


def _combine_kernel(pos_ref, pos_next_ref, ys_hbm, x_ref, route_ref, *refs, n_first):
    out_refs, (buf_ref, sem) = refs[:-2], refs[-2:]
    tm = x_ref.shape[0]
    i = pl.program_id(0)
    n = pl.num_programs(0)
    slot = i % 2

    def tile_copy(s, k, r, src):
        return pltpu.make_async_copy(ys_hbm.at[src], buf_ref.at[s, k, r], sem.at[s])

    def issue(p_ref, s, lo, hi):
        def start(r, c):
            tile_copy(s, 0, r, p_ref[0, 0, 2 * r]).start(priority=0)
            tile_copy(s, 1, r, p_ref[0, 0, 2 * r + 1]).start(priority=1)
            return c
        lax.fori_loop(lo, hi, start, 0, unroll=DMA_UNROLL)

    @pl.when(i == 0)
    def _():
        issue(pos_ref, slot, 0, tm)

    def wait(r, c):
        tile_copy(slot, 0, r, 0).wait()
        tile_copy(slot, 1, r, 0).wait()
        return c

    lax.fori_loop(0, tm, wait, 0, unroll=DMA_UNROLL)

    def write(o_ref, rows):
        y1 = _token_tiles_to_rows(buf_ref[slot, 0, rows])
        y2 = _token_tiles_to_rows(buf_ref[slot, 1, rows])
        w1 = route_ref[rows, 2:3]
        w2 = route_ref[rows, 3:4]
        for j in range(ROW_TILES):
            sl = slice(j * LANES, (j + 1) * LANES)
            o_ref[rows, sl] = x_ref[rows, sl] + w1 * y1[j] + w2 * y2[j]

    rows_ph = tm // COMBINE_PHASES
    for ph in range(COMBINE_PHASES):
        rows = pl.ds(ph * rows_ph, rows_ph)
        pl.when(i + 1 < n)(functools.partial(issue, pos_next_ref, 1 - slot, ph * rows_ph, (ph + 1) * rows_ph))
        if len(out_refs) == 1:
            write(out_refs[0], rows)
        else:
            pl.when(i < n_first)(functools.partial(write, out_refs[0], rows))
            pl.when(i >= n_first)(functools.partial(write, out_refs[1], rows))


def _combine(ys, pos3, x1, route, tm, out_rows):
    t_tok = x1.shape[0]
    n_tiles = t_tok // tm
    n_first = out_rows[0] // tm
    if len(out_rows) == 1:
        out_specs = [pl.BlockSpec((tm, D_MODEL), lambda i: (i, 0))]
    else:
        out_specs = [pl.BlockSpec((tm, D_MODEL), lambda i: (jnp.minimum(i, n_first - 1), 0)),
                     pl.BlockSpec((tm, D_MODEL), lambda i: (jnp.maximum(i - n_first, 0), 0))]
    pos_spec = functools.partial(pl.BlockSpec, (1, 1, 2 * tm), memory_space=pltpu.SMEM)
    return pl.pallas_call(
        functools.partial(_combine_kernel, n_first=n_first),
        grid=(n_tiles,),
        in_specs=[pos_spec(lambda i: (i, 0, 0)),
                  pos_spec(lambda i: (jnp.minimum(i + 1, n_tiles - 1), 0, 0)),
                  pl.BlockSpec(memory_space=pl.ANY),
                  pl.BlockSpec((tm, D_MODEL), lambda i: (i, 0)),
                  pl.BlockSpec((tm, LANES), lambda i: (i, 0))],
        out_specs=out_specs,
        out_shape=[jax.ShapeDtypeStruct((r, D_MODEL), F32) for r in out_rows],
        scratch_shapes=[pltpu.VMEM((2, 2, tm, ROW_TILES, LANES), F32), pltpu.SemaphoreType.DMA((2,))],
        compiler_params=_cparams(("arbitrary",)),
        name="moe_combine",
    )(pos3, pos3, ys, x1, route)


def _moe_ffn(h2t, x1, route, counts, lw, tiles, out_rows):
    t_tok = x1.shape[0]
    tm = tiles["tm"]
    bm = tiles["moe_block"]
    n_assign = 2 * t_tok
    experts = route[:, :2].astype(jnp.int32).reshape(n_assign)
    rank = route[:, 4:6].astype(jnp.int32).reshape(n_assign)
    counts = counts[0, :N_EXPERTS].astype(jnp.int32)
    padded = ((counts + bm - 1) // bm) * bm
    ends = jnp.cumsum(padded)
    starts = ends - padded
    onehot = (experts[:, None] == jnp.arange(N_EXPERTS, dtype=jnp.int32)[None, :]).astype(jnp.int32)
    pos = jnp.sum(onehot * starts[None, :], axis=1) + rank
    n_blocks = n_assign // bm + N_EXPERTS
    blk_start = jnp.arange(n_blocks, dtype=jnp.int32) * bm
    blk_expert = jnp.sum((blk_start[:, None] >= ends[None, :]).astype(jnp.int32), axis=1)
    blk_expert = jnp.minimum(blk_expert, N_EXPERTS - 1)
    n_used = (ends[-1:] // bm).astype(jnp.int32)
    pos3 = pos.astype(jnp.int32).reshape(t_tok // tm, 1, 2 * tm)

    xs = _dispatch(h2t, pos3, n_blocks * bm, tm)
    ys = _expert_ffn(xs, blk_expert, n_used, lw, bm)
    return _combine(ys, pos3, x1, route, tm, out_rows)


def _head_cols(w, per_head, n_heads, offset=0):
    k = w.shape[0]
    w = w.reshape(k, n_heads, per_head)
    w = jnp.pad(w, ((0, 0), (0, 0), (offset, HEAD_PAD - per_head - offset)))
    return w.reshape(k, n_heads * HEAD_PAD)


def _swap_rope(w):
    z = jnp.zeros_like(w[..., :MLA_NOPE])
    return jnp.concatenate([z, w[..., MLA_NOPE + HALF_ROPE:], w[..., MLA_NOPE:MLA_NOPE + HALF_ROPE]], axis=-1)


def _pad_lane_block(v, offset=0):
    return jnp.pad(v, ((0, 0),) * (v.ndim - 1) + ((offset, LANES - v.shape[-1] - offset),))


def _layer_weights(i, p, rope_cos, rope_sin):
    o = IN_OFFSETS
    w_in = p["w_in"][i]
    lw = {}
    lw["g_mix"] = p["g_mix"][i].reshape(1, D_MODEL)
    lw["w_u"] = w_in[:, :o[0]].astype(BF16)
    lw["w_cq"] = w_in[:, o[0]:o[1]].astype(BF16)
    w_kr = w_in[:, o[2]:o[3]]
    w_kr_sw = jnp.concatenate([w_kr[:, HALF_ROPE:], w_kr[:, :HALF_ROPE]], axis=1)
    lw["w_ckv"] = jnp.concatenate([w_in[:, o[1]:o[2]], _pad_lane_block(w_kr, MLA_NOPE),
                                   _pad_lane_block(w_kr_sw, MLA_NOPE)], axis=1).astype(BF16)
    lw["w_qm"] = w_in[:, o[3]:o[4]].astype(BF16)
    lw["w_gate"] = w_in[:, o[4]:].astype(BF16)
    lw["g_q_lora"] = p["g_q_lora"][i].reshape(1, Q_LORA)
    w_uq = p["w_uq"][i].reshape(Q_LORA, MLA_HEADS, MLA_QK)
    lw["w_uq"] = jnp.concatenate(
        [_head_cols(w_uq.reshape(Q_LORA, -1), MLA_QK, MLA_HEADS),
         _head_cols(_swap_rope(w_uq).reshape(Q_LORA, -1), MLA_QK, MLA_HEADS)], axis=1).astype(BF16)
    lw["g_kv_lora"] = p["g_kv_lora"][i].reshape(1, KV_LORA)
    w_ukv = p["w_ukv"][i].reshape(KV_LORA, MLA_HEADS, MLA_NOPE + MLA_V)
    lw["w_kn"] = _head_cols(w_ukv[:, :, :MLA_NOPE].reshape(KV_LORA, -1), MLA_NOPE, MLA_HEADS).astype(BF16)
    w_v = w_ukv[:, :, MLA_NOPE:].reshape(KV_LORA, MLA_HEADS // 2, 2, MLA_V)
    w_v = jnp.stack([jnp.pad(w_v[:, :, 0], ((0, 0), (0, 0), (0, HEAD_PAD - MLA_V))),
                     jnp.pad(w_v[:, :, 1], ((0, 0), (0, 0), (HEAD_PAD - MLA_V, 0)))], axis=2)
    lw["w_v"] = w_v.reshape(KV_LORA, MLA_WIDTH_PAD).astype(BF16)
    one_even = jnp.zeros((HEAD_PAD,), F32).at[MLA_V].set(1.0)
    one_odd = jnp.zeros((HEAD_PAD,), F32).at[0].set(1.0)
    lw["v_one"] = jnp.tile(jnp.concatenate([one_even, one_odd]), MLA_HEADS // 2).reshape(1, MLA_WIDTH_PAD)
    for name, g in (("q", p["g_mla_q"][i]), ("k", p["g_mla_k"][i])):
        lw["rope_a" + name] = rope_cos * _pad_lane_block(g)[None, :]
        lw["rope_b" + name] = rope_sin * _pad_lane_block(_swap_rope(g))[None, :]
    lw["g_mem_q"] = jnp.tile(p["g_mem_q"][i], MEM_HEADS).reshape(1, MEM_WIDTH) * (MEM_HEAD_DIM ** -0.5)
    lw["g_mem"] = p["g_mem"][i].reshape(1, D_MODEL)
    lw["w_mem_kv"] = p["w_mem_kv"][i].astype(BF16)
    lw["g_mem_k"] = p["g_mem_k"][i].reshape(1, MEM_HEAD_DIM)
    lw["w_glu"] = p["ssm_w_glu"][i].astype(BF16)
    lw["b_glu"] = p["ssm_b_glu"][i].reshape(1, 2 * SSM_WIDTH)
    lw["w_branch"] = p["w_branch"][i].astype(BF16)
    lw["w_out"] = p["w_out"][i].astype(BF16)
    lw["g_ffn"] = p["g_ffn"][i].reshape(1, D_MODEL)
    lw["ssm"] = _ssm_tables(p["ssm_a_re"][i], p["ssm_a_im"][i], p["ssm_log_dt"][i], p["ssm_b_re"][i],
                            p["ssm_b_im"][i], p["ssm_c_re"][i], p["ssm_c_im"][i], p["ssm_d"][i])
    lw["score_bound"] = (1.02 * MLA_QK ** 0.5 * LOG2E
                         * jnp.max(jnp.abs(p["g_mla_q"][i])) * jnp.max(jnp.abs(p["g_mla_k"][i])))
    if i % 2 == 0:
        w13 = p["ffn_w13"][i // 2]
        lw["ffn_w1"] = w13[:, :D_FF].astype(BF16)
        lw["ffn_w3"] = w13[:, D_FF:].astype(BF16)
        lw["ffn_w2"] = p["ffn_w2"][i // 2].astype(BF16)
    else:
        w13 = p["moe_w13"][i // 2]
        lw["moe_w1"] = w13[:, :, :D_FF_EXPERT].astype(BF16)
        lw["moe_w3"] = w13[:, :, D_FF_EXPERT:].astype(BF16)
        lw["moe_w2"] = p["moe_w2"][i // 2].astype(BF16)
        wr = jnp.pad(p["moe_router"][i // 2], ((0, 0), (0, LANES - N_EXPERTS)))
        lw["w_router_hi"], lw["w_router_lo"] = _split_bf16(wr)
    return lw


def _rope_tables(max_len):
    inv = 1.0 / (ROPE_BASE ** (jnp.arange(0, MLA_ROPE, 2, dtype=F32) / MLA_ROPE))
    ang = jnp.arange(max_len, dtype=F32)[:, None] * inv[None, :]
    cos, sin = jnp.cos(ang), jnp.sin(ang)
    ones = jnp.ones((max_len, MLA_NOPE), F32)
    zn = jnp.zeros((max_len, MLA_NOPE), F32)
    zp = jnp.zeros((max_len, HEAD_PAD - MLA_QK), F32)
    return (jnp.concatenate([ones, cos, cos, zp], axis=1),
            jnp.concatenate([zn, -sin, sin, zp], axis=1))


def _gcd_tile(limit, *sizes):
    t = limit
    while any(s % t for s in sizes):
        t //= 2
    return t


def kernel(x_prompt, x_sample, mem_prompt, mem_sample, g_mix, w_in, ssm_a_re, ssm_a_im, ssm_log_dt, ssm_b_re, ssm_b_im, ssm_c_re, ssm_c_im, ssm_d, ssm_w_glu, ssm_b_glu, g_q_lora, w_uq, g_kv_lora, w_ukv, g_mla_q, g_mla_k, g_mem, w_mem_kv, g_mem_q, g_mem_k, w_branch, w_out, g_ffn, ffn_w13, ffn_w2, moe_router, moe_w13, moe_w2):
    params = dict(g_mix=g_mix, w_in=w_in, ssm_a_re=ssm_a_re, ssm_a_im=ssm_a_im, ssm_log_dt=ssm_log_dt,
                  ssm_b_re=ssm_b_re, ssm_b_im=ssm_b_im, ssm_c_re=ssm_c_re, ssm_c_im=ssm_c_im, ssm_d=ssm_d,
                  ssm_w_glu=ssm_w_glu, ssm_b_glu=ssm_b_glu, g_q_lora=g_q_lora, w_uq=w_uq,
                  g_kv_lora=g_kv_lora, w_ukv=w_ukv, g_mla_q=g_mla_q, g_mla_k=g_mla_k, g_mem=g_mem,
                  w_mem_kv=w_mem_kv, g_mem_q=g_mem_q, g_mem_k=g_mem_k, w_branch=w_branch, w_out=w_out,
                  g_ffn=g_ffn, ffn_w13=ffn_w13, ffn_w2=ffn_w2, moe_router=moe_router, moe_w13=moe_w13,
                  moe_w2=moe_w2)
    depth = g_mix.shape[0]
    bp, lp, _ = x_prompt.shape
    bs, ls, _ = x_sample.shape
    tp, ts = bp * lp, bs * ls
    t_tok = tp + ts

    tm = _gcd_tile(TOKEN_TILE, lp, ls)
    attn_q_tile = _gcd_tile(ATTN_Q_TILE, lp, ls)
    attn_kv_tile = _gcd_tile(ATTN_KV_TILE, lp, ls)
    ffn_tile = _gcd_tile(FFN_TILE, lp, ls)
    npt = tp // tm
    lp_t, ls_t = lp // tm, ls // tm

    def pos_map(i):
        return jnp.where(i < npt, i % lp_t, (i - npt) % ls_t)

    def seq_map(i):
        return jnp.where(i < npt, i // lp_t, bp + (i - npt) // ls_t)

    rope_cos, rope_sin = _rope_tables(max(lp, ls))
    head_ind = (jnp.arange(MLA_WIDTH_PAD)[:, None] // HEAD_PAD == jnp.arange(LANES)[None, :])
    mem_bd = (jnp.arange(MEM_WIDTH)[:, None] // MEM_HEAD_DIM == jnp.arange(MEM_WIDTH)[None, :] // MEM_HEAD_DIM)
    tiles = dict(tm=tm, ffn_tile=ffn_tile, pos_map=pos_map, seq_map=seq_map,
                 ones=jnp.ones((LANES, LANES), BF16),
                 head_ind=head_ind.astype(BF16),
                 head_ind_t=jnp.concatenate([head_ind.T, head_ind.T], axis=0).astype(BF16),
                 mem_bd=mem_bd.astype(BF16),
                 moe_block=_gcd_tile(MOE_BLOCK, 2 * t_tok))

    x_parts = [x_prompt.reshape(tp, D_MODEL), x_sample.reshape(ts, D_MODEL)]
    mem = jnp.concatenate([mem_prompt.reshape(bp * MEM_TOKENS, D_MODEL),
                           mem_sample.reshape(bs * MEM_TOKENS, D_MODEL)], axis=0)
    seqs = [(bp, lp), (bs, ls)]

    for i in range(depth):
        lw = _layer_weights(i, params, rope_cos, rope_sin)
        u, q, k, v, qm, sg = _in_projection(x_parts, lw, tiles)

        y_ssm_raw = _ssm_call(u, lw["ssm"], seqs)

        def attend(running_max, score_dtype):
            return lambda q_, k_, v_: _flash_attention(q_, k_, v_, seqs, attn_q_tile, attn_kv_tile,
                                                       running_max, score_dtype)

        y_mla = lax.cond(
            lw["score_bound"] <= MAX_FP8_LOG2_SCORE,
            attend(False, jnp.float8_e4m3fn),
            lambda q_, k_, v_: lax.cond(lw["score_bound"] <= MAX_SAFE_LOG2_SCORE,
                                        attend(False, BF16), attend(True, BF16), q_, k_, v_),
            q, k, v)
        km, vm = _memory_kv(mem, lw)

        with_router = i % 2 == 1
        outs = _merge(x_parts, y_ssm_raw, y_mla, qm, km, vm, sg, lw, tiles, with_router)
        if with_router:
            x1, h2t, route, counts = outs
            out_rows = (tp, ts) if i == depth - 1 else (t_tok,)
            x_parts = _moe_ffn(h2t, x1, route, counts, lw, tiles, out_rows)
        else:
            x1, h2 = outs
            x_parts = [_dense_ffn(h2, x1, lw, tiles)]

    if len(x_parts) == 1:
        x_parts = [x_parts[0][:tp], x_parts[0][tp:]]
    return (x_parts[0].reshape(bp, lp, D_MODEL), x_parts[1].reshape(bs, ls, D_MODEL))
```

```python
import functools
import math

import numpy as np
import jax
import jax.numpy as jnp
from jax import lax
from jax.experimental import pallas as pl
from jax.experimental.pallas import tpu as pltpu

F32 = jnp.float32
BF16 = jnp.bfloat16
EPS = 1e-6
LOG2E = math.log2(math.e)

D_MODEL = 1024
SSM_WIDTH = 512
SSM_GROUPS = 32
SSM_GROUP = 16
SSM_STATE = 64
MLA_HEADS = 8
MLA_NOPE = 64
MLA_ROPE = 32
MLA_V = 64
MLA_QK = MLA_NOPE + MLA_ROPE
Q_LORA = 256
KV_LORA = 128
ROPE_BASE = 10000.0
MEM_TOKENS = 256
MEM_HEADS = 4
MEM_HEAD_DIM = 128
MEM_WIDTH = 512
N_BRANCH = 3
D_FF = 2816
N_EXPERTS = 8
D_FF_EXPERT = 3584
IN_SIZES = (SSM_WIDTH, Q_LORA, KV_LORA, MLA_ROPE, MEM_WIDTH, N_BRANCH * D_MODEL)
IN_OFFSETS = tuple(int(v) for v in np.cumsum(IN_SIZES)[:-1])

LANES = 128
SUBLANES = 8
HEAD_PAD = 128
HALF_ROPE = MLA_ROPE // 2
MLA_WIDTH_PAD = MLA_HEADS * HEAD_PAD
ROW_TILES = D_MODEL // LANES
VMEM_LIMIT = 56 * 1024 * 1024
MAX_SAFE_LOG2_SCORE = 100.0
MAX_FP8_LOG2_SCORE = 20.0

TOKEN_TILE = 512
ATTN_Q_TILE = 2048
ATTN_KV_TILE = 2048
FFN_TILE = 1024
SSM_CHUNK = 16
MOE_BLOCK = 512
FFN_CHUNK = 256
MOE_CHUNK = 512
DMA_UNROLL = 8
COMBINE_PHASES = 4


def _cparams(sem):
    return pltpu.CompilerParams(dimension_semantics=sem, vmem_limit_bytes=VMEM_LIMIT)


def _const_spec(shape):
    nd = len(shape)
    return pl.BlockSpec(shape, lambda *_: (0,) * nd)


def _sigmoid(x):
    return 1.0 / (1.0 + jnp.exp(-x))


def _rms_scale(x, n):
    return lax.rsqrt(jnp.sum(x * x, axis=-1, keepdims=True) * (1.0 / n) + EPS)


def _split_bf16(x):
    hi = x.astype(BF16)
    lo = (x - hi.astype(F32)).astype(BF16)
    return hi, lo


def _dot(a, b):
    return jnp.dot(a, b, preferred_element_type=F32)


def _tile_transpose(vs):
    sub = lax.broadcasted_iota(jnp.int32, vs[0].shape, 1)
    for dist in (4, 2, 1):
        low = (sub // dist) % 2 == 0
        new = list(vs)
        for a in range(SUBLANES):
            if a & dist == 0:
                b = a | dist
                new[a] = jnp.where(low, vs[a], pltpu.roll(vs[b], dist, 1))
                new[b] = jnp.where(low, pltpu.roll(vs[a], SUBLANES - dist, 1), vs[b])
        vs = new
    return vs


def _rows_to_token_tiles(x):
    rows = x.shape[0]
    cols = [x[:, j * LANES:(j + 1) * LANES].reshape(rows // SUBLANES, SUBLANES, LANES) for j in range(ROW_TILES)]
    return jnp.stack(_tile_transpose(cols), axis=1).reshape(rows, ROW_TILES, LANES)


def _token_tiles_to_rows(t):
    rows = t.shape[0]
    t4 = t.reshape(rows // SUBLANES, SUBLANES, ROW_TILES, LANES)
    cols = _tile_transpose([t4[:, k] for k in range(SUBLANES)])
    return [c.reshape(rows, LANES) for c in cols]


def _stream_specs(parts, tm, width):
    if len(parts) == 1:
        return [pl.BlockSpec((tm, width), lambda i: (i, 0))]
    n0 = parts[0].shape[0] // tm
    return [pl.BlockSpec((tm, width), lambda i: (jnp.minimum(i, n0 - 1), 0)),
            pl.BlockSpec((tm, width), lambda i: (jnp.maximum(i - n0, 0), 0))]


def _stream_read(refs, n_first):
    if len(refs) == 1:
        return refs[0][...]
    return jnp.where(pl.program_id(0) < n_first, refs[0][...], refs[1][...])


def _inproj_kernel(*refs, n_x, n_first):
    x_refs = refs[:n_x]
    (gmix_ref, wu_ref, wg_ref, wqm_ref, wcq_ref, wckv_ref,
     gql_ref, wuq_ref, gkvl_ref, wkn_ref, wv_ref, vone_ref, gmq_ref,
     ones_ref, ind_ref, indt_ref, bd_ref,
     aq_ref, bq_ref, ak_ref, bk_ref,
     u_ref, q_ref, k_ref, v_ref, qm_ref, sg_ref) = refs[n_x:]

    def row_ssq(val):
        sq = val * val
        part = sq[:, :LANES]
        for j in range(1, val.shape[1] // LANES):
            part = part + sq[:, j * LANES:(j + 1) * LANES]
        return _dot(part.astype(BF16), ones_ref[...])

    def head_scale(val, scale):
        ss = _dot((val * val).astype(BF16), ind_ref[...])
        r = lax.rsqrt(ss * (1.0 / MLA_QK) + EPS) * scale
        return _dot(jnp.concatenate(_split_bf16(r), axis=1), indt_ref[...])

    x = _stream_read(x_refs, n_first)
    rs = lax.rsqrt(row_ssq(x) * (1.0 / D_MODEL) + EPS)
    h = (x * jnp.tile(rs, (1, D_MODEL // LANES)) * gmix_ref[...]).astype(BF16)

    u_ref[...] = _dot(h, wu_ref[...])

    for b in range(N_BRANCH):
        sl = slice(b * D_MODEL, (b + 1) * D_MODEL)
        sg_ref[:, sl] = _sigmoid(_dot(h, wg_ref[:, sl])).astype(BF16)

    zq = _dot(h, wqm_ref[...])
    zr = lax.rsqrt(_dot((zq * zq).astype(BF16), bd_ref[...]) * (1.0 / MEM_HEAD_DIM) + EPS)
    qm_ref[...] = (zq * zr * gmq_ref[...]).astype(BF16)

    cq = _dot(h, wcq_ref[...])
    cq_rs = lax.rsqrt(row_ssq(cq) * (1.0 / Q_LORA) + EPS)
    cqn = (cq * jnp.tile(cq_rs, (1, Q_LORA // LANES)) * gql_ref[...]).astype(BF16)
    qq = _dot(cqn, wuq_ref[...])
    qf = qq[:, :MLA_WIDTH_PAD]
    qsw = qq[:, MLA_WIDTH_PAD:]

    z3 = _dot(h, wckv_ref[...])
    ckv = z3[:, :LANES]
    kr = z3[:, LANES:2 * LANES]
    kr_sw = z3[:, 2 * LANES:]
    ckv_rs = lax.rsqrt(row_ssq(ckv) * (1.0 / KV_LORA) + EPS)
    ckvn = (ckv * ckv_rs * gkvl_ref[...]).astype(BF16)
    kf = _dot(ckvn, wkn_ref[...]) + jnp.tile(kr, (1, MLA_HEADS))
    v_ref[...] = (_dot(ckvn, wv_ref[...]) + vone_ref[...]).astype(BF16)

    q_rs = head_scale(qf, MLA_QK ** -0.5 * LOG2E)
    k_rs = head_scale(kf, 1.0)
    aq = aq_ref[...]
    bq = bq_ref[...]
    ak = ak_ref[...]
    k_rot = kr_sw * bk_ref[...]
    for hh in range(MLA_HEADS):
        sl = slice(hh * HEAD_PAD, (hh + 1) * HEAD_PAD)
        q_ref[:, sl] = ((qf[:, sl] * aq + qsw[:, sl] * bq) * q_rs[:, sl]).astype(BF16)
        k_ref[:, sl] = ((kf[:, sl] * ak + k_rot) * k_rs[:, sl]).astype(BF16)


def _in_projection(x_parts, lw, tiles):
    t_tok = sum(p.shape[0] for p in x_parts)
    tm = tiles["tm"]
    pos_map = tiles["pos_map"]

    def row(i):
        return (i, 0)

    weights = [lw["g_mix"], lw["w_u"], lw["w_gate"], lw["w_qm"], lw["w_cq"], lw["w_ckv"],
               lw["g_q_lora"], lw["w_uq"], lw["g_kv_lora"], lw["w_kn"], lw["w_v"], lw["v_one"],
               lw["g_mem_q"], tiles["ones"], tiles["head_ind"], tiles["head_ind_t"], tiles["mem_bd"]]
    rope = [lw["rope_aq"], lw["rope_bq"], lw["rope_ak"], lw["rope_bk"]]
    in_specs = _stream_specs(x_parts, tm, D_MODEL) + [_const_spec(w.shape) for w in weights]
    in_specs += [pl.BlockSpec((tm, HEAD_PAD), lambda i: (pos_map(i), 0))] * len(rope)
    kern = functools.partial(_inproj_kernel, n_x=len(x_parts), n_first=x_parts[0].shape[0] // tm)
    out_widths = (SSM_WIDTH, MLA_WIDTH_PAD, MLA_WIDTH_PAD, MLA_WIDTH_PAD, MEM_WIDTH, N_BRANCH * D_MODEL)
    return pl.pallas_call(
        kern,
        grid=(t_tok // tm,),
        in_specs=in_specs,
        out_specs=[pl.BlockSpec((tm, w), row) for w in out_widths],
        out_shape=[jax.ShapeDtypeStruct((t_tok, w), F32 if j == 0 else BF16) for j, w in enumerate(out_widths)],
        compiler_params=_cparams(("parallel",)),
        name="in_projection",
    )(*x_parts, *weights, *rope)


SSM_LANE_GROUPS = LANES // SSM_GROUP
SSM_TAB_DIR = 11
(_TAB_STEP, _TAB_POW, _TAB_A8, _TAB_MASK) = (0, 6, 8, 10)


def _granule_transpose(vs, lane):
    for dist in (4, 2, 1):
        shift = dist * SSM_GROUP
        low = (lane // shift) % 2 == 0
        new = list(vs)
        for a in range(SSM_LANE_GROUPS):
            if a & dist == 0:
                b = a | dist
                new[a] = jnp.where(low, vs[a], pltpu.roll(vs[b], shift, 1))
                new[b] = jnp.where(low, pltpu.roll(vs[a], LANES - shift, 1), vs[b])
        vs = new
    return vs


def _chunk_scan(x, xs, tab_ref, base, keep, backward):
    nv, _, nl = x.shape
    for j, k in enumerate((1, 2, 4)):
        shift = SUBLANES - k if backward else k
        rx = pltpu.roll(x, shift, 1)
        rxs = pltpu.roll(xs, shift, 1)
        a1 = tab_ref[base + _TAB_STEP + 2 * j][None]
        a2 = tab_ref[base + _TAB_STEP + 2 * j + 1][None]
        x, xs = x + a1 * rx + a2 * rxs, xs + a1 * rxs - a2 * rx
    a1 = tab_ref[base + _TAB_A8]
    a2 = tab_ref[base + _TAB_A8 + 1]
    edge = 0 if backward else SUBLANES - 1
    c = jnp.zeros((SUBLANES, nl), F32)
    cs = c
    cb = [None] * nv
    csb = [None] * nv
    for v in (range(nv - 1, -1, -1) if backward else range(nv)):
        kp = keep(v)
        c = c * kp
        cs = cs * kp
        cb[v] = c
        csb[v] = cs
        ex = jnp.broadcast_to(x[v, edge:edge + 1, :], (SUBLANES, nl))
        exs = jnp.broadcast_to(xs[v, edge:edge + 1, :], (SUBLANES, nl))
        c, cs = ex + a1 * c + a2 * cs, exs + a1 * cs - a2 * c
    xe = pltpu.roll(x, SUBLANES - 1 if backward else 1, 1) * tab_ref[base + _TAB_MASK][None]
    return (xe + tab_ref[base + _TAB_POW][None] * jnp.stack(cb)
            + tab_ref[base + _TAB_POW + 1][None] * jnp.stack(csb))


def _ssm_kernel(keepf_ref, keepb_ref, u_ref, t_ref, win_ref, wout_ref, tab_ref, y_ref):
    blk = pl.program_id(1)
    rows = u_ref.shape[0] // SSM_CHUNK
    nv = rows // SUBLANES
    ng = SSM_LANE_GROUPS
    lane = lax.broadcasted_iota(jnp.int32, (rows, LANES), 1)

    pieces = [u_ref[pl.ds(t, rows, stride=SSM_CHUNK), :] for t in range(SSM_CHUNK)]
    halves = [_granule_transpose(pieces[ng * h:ng * (h + 1)], lane) for h in range(SSM_CHUNK // ng)]

    ys = []
    parts = [[], [], [], []]
    for g in range(ng):
        xg = jnp.concatenate([h[g] for h in halves], axis=1).astype(BF16)
        ys.append(_dot(xg, t_ref[g]))
        s4 = _dot(xg, win_ref[g])
        for j in range(4):
            parts[j].append(s4[:, j * LANES:(j + 1) * LANES])
    f, fs, b, bs = [jnp.concatenate(p, axis=1).reshape(nv, SUBLANES, ng * LANES) for p in parts]

    xin_f = _chunk_scan(f, fs, tab_ref, 0, lambda v: keepf_ref[blk * nv + v].astype(F32), False)
    xin_b = _chunk_scan(b, bs, tab_ref, SSM_TAB_DIR, lambda v: keepb_ref[blk * nv + v].astype(F32), True)
    xin_f = xin_f.reshape(rows, ng * LANES)
    xin_b = xin_b.reshape(rows, ng * LANES)

    outs = []
    for g in range(ng):
        sl = slice(g * LANES, (g + 1) * LANES)
        xs = jnp.concatenate([xin_f[:, sl], xin_b[:, sl]], axis=1).astype(BF16)
        outs.append(ys[g] + _dot(xs, wout_ref[g]))
    for h in range(SSM_CHUNK // ng):
        back = _granule_transpose([o[:, h * LANES:(h + 1) * LANES] for o in outs], lane)
        for tt in range(ng):
            y_ref[pl.ds(ng * h + tt, rows, stride=SSM_CHUNK), :] = back[tt]


def _ssm_call(u, tabs, seqs):
    t_tok = u.shape[0]
    block = int(np.lcm.reduce([ln for _, ln in seqs]))
    vreg_tokens = SUBLANES * SSM_CHUNK
    assert all((n * ln) % block == 0 and ln % vreg_tokens == 0 for n, ln in seqs), seqs
    nv = block // vreg_tokens
    keep_f, keep_b = [], []
    for n_seq, seq_len in seqs:
        for _ in range(n_seq * seq_len // block):
            for v in range(nv):
                keep_f.append(int((v * vreg_tokens) % seq_len != 0))
                keep_b.append(int(((v + 1) * vreg_tokens) % seq_len != 0))
    keep = [jnp.asarray(np.asarray(a, np.int32)) for a in (keep_f, keep_b)]
    ng = SSM_LANE_GROUPS
    width = SSM_CHUNK * SSM_GROUP
    n_tab = 2 * SSM_TAB_DIR
    grid_spec = pltpu.PrefetchScalarGridSpec(
        num_scalar_prefetch=2,
        grid=(SSM_WIDTH // LANES, t_tok // block),
        in_specs=[
            pl.BlockSpec((block, LANES), lambda g, b, kf, kb: (b, g)),
            pl.BlockSpec((ng, width, width), lambda g, b, kf, kb: (g, 0, 0)),
            pl.BlockSpec((ng, width, 4 * LANES), lambda g, b, kf, kb: (g, 0, 0)),
            pl.BlockSpec((ng, 2 * LANES, width), lambda g, b, kf, kb: (g, 0, 0)),
            pl.BlockSpec((n_tab, SUBLANES, ng * LANES), lambda g, b, kf, kb: (0, 0, g)),
        ],
        out_specs=pl.BlockSpec((block, LANES), lambda g, b, kf, kb: (b, g)),
    )
    return pl.pallas_call(
        _ssm_kernel,
        grid_spec=grid_spec,
        out_shape=jax.ShapeDtypeStruct(u.shape, F32),
        compiler_params=_cparams(("parallel", "parallel")),
        name="ssm_chunked",
    )(*keep, u, tabs["t"], tabs["w_in"], tabs["w_out"], tabs["scan"])


def _ssm_tables(a_re, a_im, log_dt, b_re, b_im, c_re, c_im, d_skip):
    q = SSM_CHUNK
    hp = lax.Precision.HIGHEST
    dt = jnp.exp(log_dt)[..., None, None]
    ks = jnp.arange(q + 1, dtype=F32)
    mag = jnp.exp(a_re[..., None] * dt * ks)
    ang = a_im[..., None] * dt * ks
    pw_re = mag * jnp.cos(ang)
    pw_im = mag * jnp.sin(ang)
    lb_re, lb_im = pw_re[..., 1], pw_im[..., 1]
    den = a_re * a_re + a_im * a_im
    f_re = ((lb_re - 1.0) * a_re + lb_im * a_im) / den
    f_im = (lb_im * a_re - (lb_re - 1.0) * a_im) / den
    bb_re = f_re[..., None] * b_re - f_im[..., None] * b_im
    bb_im = f_re[..., None] * b_im + f_im[..., None] * b_re

    cl_re = c_re[..., None] * pw_re[:, :, None] - c_im[..., None] * pw_im[:, :, None]
    cl_im = c_re[..., None] * pw_im[:, :, None] + c_im[..., None] * pw_re[:, :, None]

    kern = (jnp.einsum("dgnpk,dgpm->dgknm", cl_re[..., :q], bb_re, precision=hp)
            - jnp.einsum("dgnpk,dgpm->dgknm", cl_im[..., :q], bb_im, precision=hp))
    s_idx = jnp.arange(q)[:, None]
    t_idx = jnp.arange(q)[None, :]
    lag_f = jnp.clip(t_idx - s_idx, 0, q - 1)
    lag_b = jnp.clip(s_idx - t_idx, 0, q - 1)
    tf = kern[0][:, lag_f] * (t_idx >= s_idx)[None, :, :, None, None]
    tb = kern[1][:, lag_b] * (s_idx >= t_idx)[None, :, :, None, None]
    t_mat = (tf + tb).transpose(0, 1, 4, 2, 3)
    t_mat = t_mat.reshape(SSM_GROUPS, q * SSM_GROUP, q * SSM_GROUP)
    d_rep = jnp.tile(d_skip.reshape(SSM_GROUPS, 1, SSM_GROUP), (1, q, 1)).reshape(SSM_GROUPS, -1)
    t_mat = t_mat + d_rep[:, :, None] * jnp.eye(q * SSM_GROUP, dtype=F32)[None]

    def w_in_dir(d, exps):
        pr = pw_re[d][:, :, exps]
        pi = pw_im[d][:, :, exps]
        re = pr[..., None] * bb_re[d][:, :, None, :] - pi[..., None] * bb_im[d][:, :, None, :]
        im = pr[..., None] * bb_im[d][:, :, None, :] + pi[..., None] * bb_re[d][:, :, None, :]
        re = re.transpose(0, 2, 3, 1).reshape(SSM_GROUPS, q * SSM_GROUP, SSM_STATE)
        im = im.transpose(0, 2, 3, 1).reshape(SSM_GROUPS, q * SSM_GROUP, SSM_STATE)
        return jnp.concatenate([re, im, im, re], axis=-1)

    w_in = jnp.concatenate([w_in_dir(0, q - 1 - jnp.arange(q)), w_in_dir(1, jnp.arange(q))], axis=-1)

    def w_out_dir(d, exps):
        re = cl_re[d][..., exps]
        im = cl_im[d][..., exps]
        re = re.transpose(0, 2, 3, 1).reshape(SSM_GROUPS, SSM_STATE, q * SSM_GROUP)
        im = im.transpose(0, 2, 3, 1).reshape(SSM_GROUPS, SSM_STATE, q * SSM_GROUP)
        return jnp.concatenate([re, -im], axis=1)

    w_out = jnp.concatenate([w_out_dir(0, 1 + jnp.arange(q)), w_out_dir(1, q - jnp.arange(q))], axis=1)

    ms = jnp.arange(SUBLANES + 1, dtype=F32) * q
    cr = jnp.exp(a_re[..., None] * dt * ms) * jnp.cos(a_im[..., None] * dt * ms)
    ci = jnp.exp(a_re[..., None] * dt * ms) * jnp.sin(a_im[..., None] * dt * ms)

    def pat(d, m):
        ar, ai = cr[d][..., m], ci[d][..., m]
        return (jnp.concatenate([ar, ar], axis=-1).reshape(-1), jnp.concatenate([-ai, ai], axis=-1).reshape(-1))

    row = jnp.arange(SUBLANES)

    def dir_tables(d, backward):
        tabs = []
        for k in (1, 2, 4):
            valid = (row <= SUBLANES - 1 - k) if backward else (row >= k)
            tabs += [valid[:, None] * p[None, :] for p in pat(d, k)]
        pows = [pat(d, SUBLANES - 1 - i if backward else i) for i in range(SUBLANES)]
        tabs += [jnp.stack([p[0] for p in pows]), jnp.stack([p[1] for p in pows])]
        tabs += [jnp.tile(p[None, :], (SUBLANES, 1)) for p in pat(d, SUBLANES)]
        excl = (row <= SUBLANES - 2) if backward else (row >= 1)
        tabs.append(jnp.tile(excl[:, None].astype(F32), (1, SSM_GROUPS * LANES)))
        return tabs

    scan_tab = jnp.stack(dir_tables(0, False) + dir_tables(1, True), axis=0).astype(F32)
    return {"t": t_mat.astype(BF16), "w_in": w_in.astype(BF16), "w_out": w_out.astype(BF16),
            "scan": scan_tab}


def _flash_kernel(qb_ref, kb_ref, first_ref, last_ref, q_ref, k_ref, v_ref, o_ref,
                  acc_ref, *extra, running_max, score_dtype):
    del qb_ref, kb_ref
    s_id = pl.program_id(1)
    m_scratch = extra if running_max else ()
    qc_ref = None if (running_max or score_dtype == BF16) else extra[0]

    @pl.when(first_ref[s_id] == 1)
    def _():
        acc_ref[...] = jnp.zeros(acc_ref.shape, F32)
        if running_max:
            m_scratch[0][...] = jnp.full(m_scratch[0].shape, -jnp.inf, F32)
        if qc_ref is not None:
            qc_ref[...] = q_ref[...].astype(score_dtype)

    for hh in range(2):
        sl = slice(hh * HEAD_PAD, (hh + 1) * HEAD_PAD)
        qh = q_ref[:, sl] if qc_ref is None else qc_ref[:, sl]
        s = lax.dot_general(qh, k_ref[:, sl].astype(score_dtype),
                            (((1,), (1,)), ((), ())), preferred_element_type=F32)
        if running_max:
            m_ref = m_scratch[0]
            m_prev = m_ref[hh]
            m_new = jnp.maximum(m_prev, jnp.max(s, axis=1, keepdims=True))
            p = jnp.exp2(s - m_new[:, :1]).astype(BF16)
            acc_ref[hh] = jnp.exp2(m_prev - m_new) * acc_ref[hh] + _dot(p, v_ref[:, sl])
            m_ref[hh] = m_new
        else:
            acc_ref[hh] += _dot(jnp.exp2(s.astype(BF16)), v_ref[:, sl])

    @pl.when(last_ref[s_id] == 1)
    def _():
        lane = lax.broadcasted_iota(jnp.int32, o_ref.shape, 1)
        a0 = acc_ref[0]
        a1 = acc_ref[1]
        o0 = a0 / a0[:, MLA_V:MLA_V + 1]
        o1 = a1 / a1[:, 0:1]
        o_ref[...] = jnp.where(lane < MLA_V, o0, o1).astype(BF16)


def _attend_once_kernel(sb_ref, q_ref, k_ref, v_ref, prev_ref, o_ref, *, score_dtype):
    del sb_ref, prev_ref
    acc = []
    for hh in range(2):
        sl = slice(hh * HEAD_PAD, (hh + 1) * HEAD_PAD)
        s = lax.dot_general(q_ref[:, sl].astype(score_dtype), k_ref[:, sl].astype(score_dtype),
                            (((1,), (1,)), ((), ())), preferred_element_type=F32)
        acc.append(_dot(jnp.exp2(s.astype(BF16)), v_ref[:, sl]))
    lane = lax.broadcasted_iota(jnp.int32, o_ref.shape, 1)
    o0 = acc[0] / acc[0][:, MLA_V:MLA_V + 1]
    o1 = acc[1] / acc[1][:, 0:1]
    o_ref[...] = jnp.where(lane < MLA_V, o0, o1).astype(BF16)


def _attend_once(q, k, v, prev, blocks, seq_len, score_dtype):
    tab = jnp.asarray(np.asarray(blocks, np.int32))
    pair_w = 2 * HEAD_PAD
    grid_spec = pltpu.PrefetchScalarGridSpec(
        num_scalar_prefetch=1,
        grid=(MLA_HEADS // 2, len(blocks)),
        in_specs=[pl.BlockSpec((seq_len, pair_w), lambda p, s, sb: (sb[s], p))] * 3
        + [pl.BlockSpec(memory_space=pl.ANY)],
        out_specs=pl.BlockSpec((seq_len, 2 * MLA_V), lambda p, s, sb: (sb[s], p)),
    )
    return pl.pallas_call(
        functools.partial(_attend_once_kernel, score_dtype=score_dtype),
        grid_spec=grid_spec,
        out_shape=jax.ShapeDtypeStruct(prev.shape, prev.dtype),
        input_output_aliases={4: 0},
        compiler_params=_cparams(("parallel", "parallel")),
        name="mla_once" if score_dtype == BF16 else "mla_once_f8",
    )(tab, q, k, v, prev)


def _flash_attention(q, k, v, seqs, q_tile, kv_tile, running_max, score_dtype=BF16):
    qb, kb, first, last = [], [], [], []
    once_blocks, once_len = [], None
    base = 0
    for n_seq, seq_len in seqs:
        nq, nk = seq_len // q_tile, seq_len // kv_tile
        one_shot = (not running_max) and nq == 1 and nk == 1 and once_len in (None, seq_len)
        for _ in range(n_seq):
            if one_shot:
                once_len = seq_len
                once_blocks.append(base // seq_len)
            else:
                for qi in range(nq):
                    for ki in range(nk):
                        qb.append(base // q_tile + qi)
                        kb.append(base // kv_tile + ki)
                        first.append(int(ki == 0))
                        last.append(int(ki == nk - 1))
            base += seq_len
    if once_blocks and qb:
        out = _flash_steps(q, k, v, (qb, kb, first, last), q_tile, kv_tile, running_max, score_dtype)
        return _attend_once(q, k, v, out, once_blocks, once_len, score_dtype)
    if once_blocks:
        return _flash_attention(q, k, v, seqs, q_tile, kv_tile, True, BF16)
    return _flash_steps(q, k, v, (qb, kb, first, last), q_tile, kv_tile, running_max, score_dtype)


def _flash_steps(q, k, v, tables, q_tile, kv_tile, running_max, score_dtype):
    qb, kb, first, last = tables
    tabs = [jnp.asarray(np.asarray(a, np.int32)) for a in (qb, kb, first, last)]
    n_steps = len(qb)
    t_tok = q.shape[0]
    pair_w = 2 * HEAD_PAD
    scratch = [pltpu.VMEM((2, q_tile, LANES), F32)] * (2 if running_max else 1)
    if not running_max and score_dtype != BF16:
        scratch.append(pltpu.VMEM((q_tile, pair_w), score_dtype))
    grid_spec = pltpu.PrefetchScalarGridSpec(
        num_scalar_prefetch=4,
        grid=(MLA_HEADS // 2, n_steps),
        in_specs=[
            pl.BlockSpec((q_tile, pair_w), lambda p, s, qb, kb, fi, la: (qb[s], p)),
            pl.BlockSpec((kv_tile, pair_w), lambda p, s, qb, kb, fi, la: (kb[s], p)),
            pl.BlockSpec((kv_tile, pair_w), lambda p, s, qb, kb, fi, la: (kb[s], p)),
        ],
        out_specs=pl.BlockSpec((q_tile, 2 * MLA_V), lambda p, s, qb, kb, fi, la: (qb[s], p)),
        scratch_shapes=scratch,
    )
    return pl.pallas_call(
        functools.partial(_flash_kernel, running_max=running_max, score_dtype=score_dtype),
        grid_spec=grid_spec,
        out_shape=jax.ShapeDtypeStruct((t_tok, MLA_HEADS * MLA_V), BF16),
        compiler_params=_cparams(("parallel", "arbitrary")),
        name="mla_flash_max" if running_max else ("mla_flash" if score_dtype == BF16 else "mla_flash_f8"),
    )(*tabs, q, k, v)


def _memkv_kernel(mem_ref, gmem_ref, w_ref, gk_ref, k_ref, v_ref):
    x = mem_ref[...]
    h = (x * _rms_scale(x, D_MODEL) * gmem_ref[...]).astype(BF16)
    kv = _dot(h, w_ref[...])
    for hh in range(MEM_HEADS):
        sl = slice(hh * MEM_HEAD_DIM, (hh + 1) * MEM_HEAD_DIM)
        blk = kv[:, sl]
        k_ref[:, sl] = (blk * _rms_scale(blk, MEM_HEAD_DIM) * gk_ref[...]).astype(BF16)
    v_ref[...] = kv[:, MEM_WIDTH:].astype(BF16)


def _memory_kv(mem, lw):
    rows = mem.shape[0]
    tm = MEM_TOKENS
    return pl.pallas_call(
        _memkv_kernel,
        grid=(rows // tm,),
        in_specs=[pl.BlockSpec((tm, D_MODEL), lambda i: (i, 0)),
                  _const_spec(lw["g_mem"].shape), _const_spec(lw["w_mem_kv"].shape),
                  _const_spec(lw["g_mem_k"].shape)],
        out_specs=[pl.BlockSpec((tm, MEM_WIDTH), lambda i: (i, 0))] * 2,
        out_shape=[jax.ShapeDtypeStruct((rows, MEM_WIDTH), BF16)] * 2,
        compiler_params=_cparams(("parallel",)),
        name="memory_kv",
    )(mem, lw["g_mem"], lw["w_mem_kv"], lw["g_mem_k"])


def _gelu_tanh(x):
    return 0.5 * x * (1.0 + jnp.tanh(0.7978845608028654 * (x + 0.044715 * x * x * x)))


def _merge_kernel(*refs, with_router, n_x, n_first):
    x_refs = refs[:n_x]
    refs = refs[n_x:]
    if with_router:
        (ys_ref, ya_ref, qm_ref, km_ref, vm_ref, sg_ref, wglu_ref, bglu_ref, wb_ref,
         wo_ref, gffn_ref, wr_hi_ref, wr_lo_ref, tri_ref, x1_ref, h2_ref, route_ref, counts_ref,
         cnt_ref) = refs
    else:
        (ys_ref, ya_ref, qm_ref, km_ref, vm_ref, sg_ref, wglu_ref, bglu_ref, wb_ref,
         wo_ref, gffn_ref, x1_ref, h2_ref) = refs

    ya = _gelu_tanh(ys_ref[...]).astype(BF16)
    z = _dot(ya, wglu_ref[...]) + bglu_ref[...]
    y_ssm = (z[:, :SSM_WIDTH] * _sigmoid(z[:, SSM_WIDTH:])).astype(BF16)

    mem_parts = []
    for hh in range(MEM_HEADS):
        sl = slice(hh * MEM_HEAD_DIM, (hh + 1) * MEM_HEAD_DIM)
        s = lax.dot_general(qm_ref[:, sl], km_ref[:, sl], (((1,), (1,)), ((), ())),
                            preferred_element_type=F32)
        p = jnp.exp(s - jnp.max(s, axis=1, keepdims=True))
        p = p / jnp.sum(p, axis=1, keepdims=True)
        mem_parts.append(_dot(p.astype(BF16), vm_ref[:, sl]))
    y_mem = jnp.concatenate(mem_parts, axis=1).astype(BF16)

    merged = None
    for b, yb in enumerate((y_ssm, ya_ref[...], y_mem)):
        gate = sg_ref[:, b * D_MODEL:(b + 1) * D_MODEL].astype(F32)
        term = gate * _dot(yb, wb_ref[b])
        merged = term if merged is None else merged + term

    x1 = _stream_read(x_refs, n_first) + _dot(merged.astype(BF16), wo_ref[...])
    x1_ref[...] = x1
    h2 = x1 * _rms_scale(x1, D_MODEL) * gffn_ref[...]

    if not with_router:
        h2_ref[...] = h2.astype(BF16)
        return

    h2_ref[...] = _rows_to_token_tiles(h2)

    h_hi, h_lo = _split_bf16(h2)
    logits = _dot(h_hi, wr_hi_ref[...]) + _dot(h_hi, wr_lo_ref[...]) + _dot(h_lo, wr_hi_ref[...])
    lane = lax.broadcasted_iota(jnp.int32, logits.shape, 1)
    lane_f = lane.astype(F32)
    neg = jnp.float32(-jnp.inf)
    big = jnp.float32(LANES)
    logits = jnp.where(lane < N_EXPERTS, logits, neg)
    m1 = jnp.max(logits, axis=1, keepdims=True)
    i1 = jnp.min(jnp.where(logits == m1, lane_f, big), axis=1, keepdims=True)
    rest = jnp.where(lane_f == i1, neg, logits)
    m2 = jnp.max(rest, axis=1, keepdims=True)
    i2 = jnp.min(jnp.where(rest == m2, lane_f, big), axis=1, keepdims=True)
    e2 = jnp.exp(m2 - m1)
    w1 = 1.0 / (1.0 + e2)
    w2 = e2 / (1.0 + e2)
    @pl.when(pl.program_id(0) == 0)
    def _():
        cnt_ref[...] = jnp.zeros(cnt_ref.shape, F32)

    o1 = jnp.where(lane_f == i1, 1.0, 0.0)
    o2 = jnp.where(lane_f == i2, 1.0, 0.0)
    pre1 = _dot(tri_ref[...], o1.astype(BF16))
    pre2 = _dot(tri_ref[...], o2.astype(BF16))
    tot1 = jnp.sum(o1, axis=0, keepdims=True)
    tot2 = jnp.sum(o2, axis=0, keepdims=True)
    carry = cnt_ref[0:1, :]
    r1 = jnp.sum(o1 * (carry + pre1), axis=1, keepdims=True)
    r2 = jnp.sum(o2 * (carry + tot1 + pre2), axis=1, keepdims=True)
    new_cnt = jnp.broadcast_to(carry + tot1 + tot2, cnt_ref.shape)
    cnt_ref[...] = new_cnt
    counts_ref[...] = new_cnt

    out = jnp.where(lane == 0, i1, jnp.where(lane == 1, i2, jnp.where(lane == 2, w1, w2)))
    out = jnp.where(lane == 4, r1, jnp.where(lane == 5, r2, out))
    route_ref[...] = jnp.where(lane < 6, out, 0.0)


def _merge(x_parts, y_ssm_raw, y_mla, qm, km, vm, sg, lw, tiles, with_router):
    t_tok = y_mla.shape[0]
    tm = tiles["tm"]
    seq_map = tiles["seq_map"]

    def row(i):
        return (i, 0)

    weights = [lw["w_glu"], lw["b_glu"], lw["w_branch"], lw["w_out"], lw["g_ffn"]]
    if with_router:
        tri = (jnp.arange(tm)[:, None] > jnp.arange(tm)[None, :]).astype(BF16)
        weights += [lw["w_router_hi"], lw["w_router_lo"], tri]
    in_specs = _stream_specs(x_parts, tm, D_MODEL) + [
        pl.BlockSpec((tm, SSM_WIDTH), row),
        pl.BlockSpec((tm, MLA_HEADS * MLA_V), row),
        pl.BlockSpec((tm, MEM_WIDTH), row),
        pl.BlockSpec((MEM_TOKENS, MEM_WIDTH), lambda i: (seq_map(i), 0)),
        pl.BlockSpec((MEM_TOKENS, MEM_WIDTH), lambda i: (seq_map(i), 0)),
        pl.BlockSpec((tm, N_BRANCH * D_MODEL), row),
    ] + [_const_spec(w.shape) for w in weights]
    out_specs = [pl.BlockSpec((tm, D_MODEL), row)]
    out_shape = [jax.ShapeDtypeStruct((t_tok, D_MODEL), F32)]
    if with_router:
        out_specs += [pl.BlockSpec((tm, ROW_TILES, LANES), lambda i: (i, 0, 0)),
                      pl.BlockSpec((tm, LANES), row),
                      pl.BlockSpec((SUBLANES, LANES), lambda i: (0, 0))]
        out_shape += [jax.ShapeDtypeStruct((t_tok, ROW_TILES, LANES), F32),
                      jax.ShapeDtypeStruct((t_tok, LANES), F32),
                      jax.ShapeDtypeStruct((SUBLANES, LANES), F32)]
    else:
        out_specs.append(pl.BlockSpec((tm, D_MODEL), row))
        out_shape.append(jax.ShapeDtypeStruct((t_tok, D_MODEL), BF16))
    kern = functools.partial(_merge_kernel, with_router=with_router, n_x=len(x_parts),
                             n_first=x_parts[0].shape[0] // tm)
    return pl.pallas_call(
        kern,
        grid=(t_tok // tm,),
        in_specs=in_specs,
        out_specs=out_specs,
        out_shape=out_shape,
        scratch_shapes=[pltpu.VMEM((SUBLANES, LANES), F32)] if with_router else [],
        compiler_params=_cparams(("arbitrary",) if with_router else ("parallel",)),
        name="merge_router" if with_router else "merge",
    )(*x_parts, y_ssm_raw, y_mla, qm, km, vm, sg, *weights)


def _swiglu_acc(x, w1_ref, w3_ref, w2_ref, acc, chunk):
    d_ff = w1_ref.shape[1]
    for c in range(d_ff // chunk):
        sl = slice(c * chunk, (c + 1) * chunk)
        g = _dot(x, w1_ref[:, sl])
        u = _dot(x, w3_ref[:, sl])
        a = (g * _sigmoid(g) * u).astype(BF16)
        acc = acc + _dot(a, w2_ref[sl, :])
    return acc


def _ffn_kernel(h_ref, x_ref, w1_ref, w3_ref, w2_ref, o_ref):
    o_ref[...] = _swiglu_acc(h_ref[...], w1_ref, w3_ref, w2_ref, x_ref[...], FFN_CHUNK)


def _dense_ffn(h2, x1, lw, tiles):
    t_tok = x1.shape[0]
    tm = tiles["ffn_tile"]

    def row(i):
        return (i, 0)

    def resident(shape):
        return pl.BlockSpec(shape, lambda i: (0, 0), pipeline_mode=pl.Buffered(1))

    return pl.pallas_call(
        _ffn_kernel,
        grid=(t_tok // tm,),
        in_specs=[pl.BlockSpec((tm, D_MODEL), row), pl.BlockSpec((tm, D_MODEL), row),
                  resident(lw["ffn_w1"].shape), resident(lw["ffn_w3"].shape),
                  resident(lw["ffn_w2"].shape)],
        out_specs=pl.BlockSpec((tm, D_MODEL), row),
        out_shape=jax.ShapeDtypeStruct((t_tok, D_MODEL), F32),
        compiler_params=_cparams(("parallel",)),
        name="dense_ffn",
    )(h2, x1, lw["ffn_w1"], lw["ffn_w3"], lw["ffn_w2"])


def _dispatch_kernel(pos_ref, h_ref, xs_in_ref, xs_ref, sem):
    del xs_in_ref
    tm = h_ref.shape[0]

    def tile_copy(r, dst):
        return pltpu.make_async_copy(h_ref.at[r], xs_ref.at[dst], sem)

    def start(r, c):
        tile_copy(r, pos_ref[0, 0, 2 * r]).start(priority=0)
        tile_copy(r, pos_ref[0, 0, 2 * r + 1]).start(priority=1)
        return c

    def wait(r, c):
        tile_copy(r, 0).wait()
        tile_copy(r, 0).wait()
        return c

    lax.fori_loop(0, tm, start, 0, unroll=DMA_UNROLL)
    lax.fori_loop(0, tm, wait, 0, unroll=DMA_UNROLL)


def _dispatch(h2t, pos3, n_rows, tm):
    t_tok = h2t.shape[0]
    xs0 = jnp.zeros((n_rows, ROW_TILES, LANES), F32)
    return pl.pallas_call(
        _dispatch_kernel,
        grid=(t_tok // tm,),
        in_specs=[pl.BlockSpec((1, 1, 2 * tm), lambda i: (i, 0, 0), memory_space=pltpu.SMEM),
                  pl.BlockSpec((tm, ROW_TILES, LANES), lambda i: (i, 0, 0)),
                  pl.BlockSpec(memory_space=pl.ANY)],
        out_specs=pl.BlockSpec(memory_space=pl.ANY),
        out_shape=jax.ShapeDtypeStruct(xs0.shape, F32),
        input_output_aliases={2: 0},
        scratch_shapes=[pltpu.SemaphoreType.DMA(())],
        compiler_params=_cparams(("arbitrary",)),
        name="moe_dispatch",
    )(pos3, h2t, xs0)


def _expert_kernel(be_ref, nu_ref, x_ref, w1_ref, w3_ref, w2_ref, y_ref):
    del be_ref
    i = pl.program_id(0)

    @pl.when(i < nu_ref[0])
    def _():
        x = jnp.concatenate(_token_tiles_to_rows(x_ref[...]), axis=1).astype(BF16)
        acc = _swiglu_acc(x, w1_ref, w3_ref, w2_ref, jnp.zeros((x.shape[0], D_MODEL), F32), MOE_CHUNK)
        y_ref[...] = _rows_to_token_tiles(acc)

    @pl.when(i >= nu_ref[0])
    def _():
        y_ref[...] = jnp.zeros(y_ref.shape, F32)


def _expert_ffn(xs, blk_expert, n_used, lw, bm):
    n_rows = xs.shape[0]

    def w_spec(shape):
        return pl.BlockSpec((None,) + shape, lambda i, be, nu: (be[i], 0, 0),
                            pipeline_mode=pl.Buffered(1))

    grid_spec = pltpu.PrefetchScalarGridSpec(
        num_scalar_prefetch=2,
        grid=(n_rows // bm,),
        in_specs=[
            pl.BlockSpec((bm, ROW_TILES, LANES), lambda i, be, nu: (i, 0, 0)),
            w_spec((D_MODEL, D_FF_EXPERT)), w_spec((D_MODEL, D_FF_EXPERT)),
            w_spec((D_FF_EXPERT, D_MODEL)),
        ],
        out_specs=pl.BlockSpec((bm, ROW_TILES, LANES), lambda i, be, nu: (i, 0, 0)),
    )
    return pl.pallas_call(
        _expert_kernel,
        grid_spec=grid_spec,
        out_shape=jax.ShapeDtypeStruct(xs.shape, F32),
        compiler_params=_cparams(("arbitrary",)),
        name="moe_experts",
    )(blk_expert, n_used, xs, lw["moe_w1"], lw["moe_w3"], lw["moe_w2"])


def _combine_kernel(pos_ref, pos_next_ref, ys_hbm, x_ref, route_ref, *refs, n_first):
    out_refs, (buf_ref, sem) = refs[:-2], refs[-2:]
    tm = x_ref.shape[0]
    i = pl.program_id(0)
    n = pl.num_programs(0)
    slot = i % 2

    def tile_copy(s, k, r, src):
        return pltpu.make_async_copy(ys_hbm.at[src], buf_ref.at[s, k, r], sem.at[s])

    def issue(p_ref, s, lo, hi):
        def start(r, c):
            tile_copy(s, 0, r, p_ref[0, 0, 2 * r]).start(priority=0)
            tile_copy(s, 1, r, p_ref[0, 0, 2 * r + 1]).start(priority=1)
            return c
        lax.fori_loop(lo, hi, start, 0, unroll=DMA_UNROLL)

    @pl.when(i == 0)
    def _():
        issue(pos_ref, slot, 0, tm)

    def wait(r, c):
        tile_copy(slot, 0, r, 0).wait()
        tile_copy(slot, 1, r, 0).wait()
        return c

    lax.fori_loop(0, tm, wait, 0, unroll=DMA_UNROLL)

    def write(o_ref, rows):
        y1 = _token_tiles_to_rows(buf_ref[slot, 0, rows])
        y2 = _token_tiles_to_rows(buf_ref[slot, 1, rows])
        w1 = route_ref[rows, 2:3]
        w2 = route_ref[rows, 3:4]
        for j in range(ROW_TILES):
            sl = slice(j * LANES, (j + 1) * LANES)
            o_ref[rows, sl] = x_ref[rows, sl] + w1 * y1[j] + w2 * y2[j]

    rows_ph = tm // COMBINE_PHASES
    for ph in range(COMBINE_PHASES):
        rows = pl.ds(ph * rows_ph, rows_ph)
        pl.when(i + 1 < n)(functools.partial(issue, pos_next_ref, 1 - slot, ph * rows_ph, (ph + 1) * rows_ph))
        if len(out_refs) == 1:
            write(out_refs[0], rows)
        else:
            pl.when(i < n_first)(functools.partial(write, out_refs[0], rows))
            pl.when(i >= n_first)(functools.partial(write, out_refs[1], rows))


def _combine(ys, pos3, x1, route, tm, out_rows):
    t_tok = x1.shape[0]
    n_tiles = t_tok // tm
    n_first = out_rows[0] // tm
    if len(out_rows) == 1:
        out_specs = [pl.BlockSpec((tm, D_MODEL), lambda i: (i, 0))]
    else:
        out_specs = [pl.BlockSpec((tm, D_MODEL), lambda i: (jnp.minimum(i, n_first - 1), 0)),
                     pl.BlockSpec((tm, D_MODEL), lambda i: (jnp.maximum(i - n_first, 0), 0))]
    pos_spec = functools.partial(pl.BlockSpec, (1, 1, 2 * tm), memory_space=pltpu.SMEM)
    return pl.pallas_call(
        functools.partial(_combine_kernel, n_first=n_first),
        grid=(n_tiles,),
        in_specs=[pos_spec(lambda i: (i, 0, 0)),
                  pos_spec(lambda i: (jnp.minimum(i + 1, n_tiles - 1), 0, 0)),
                  pl.BlockSpec(memory_space=pl.ANY),
                  pl.BlockSpec((tm, D_MODEL), lambda i: (i, 0)),
                  pl.BlockSpec((tm, LANES), lambda i: (i, 0))],
        out_specs=out_specs,
        out_shape=[jax.ShapeDtypeStruct((r, D_MODEL), F32) for r in out_rows],
        scratch_shapes=[pltpu.VMEM((2, 2, tm, ROW_TILES, LANES), F32), pltpu.SemaphoreType.DMA((2,))],
        compiler_params=_cparams(("arbitrary",)),
        name="moe_combine",
    )(pos3, pos3, ys, x1, route)


def _moe_ffn(h2t, x1, route, counts, lw, tiles, out_rows):
    t_tok = x1.shape[0]
    tm = tiles["tm"]
    bm = tiles["moe_block"]
    n_assign = 2 * t_tok
    experts = route[:, :2].astype(jnp.int32).reshape(n_assign)
    rank = route[:, 4:6].astype(jnp.int32).reshape(n_assign)
    counts = counts[0, :N_EXPERTS].astype(jnp.int32)
    padded = ((counts + bm - 1) // bm) * bm
    ends = jnp.cumsum(padded)
    starts = ends - padded
    onehot = (experts[:, None] == jnp.arange(N_EXPERTS, dtype=jnp.int32)[None, :]).astype(jnp.int32)
    pos = jnp.sum(onehot * starts[None, :], axis=1) + rank
    n_blocks = n_assign // bm + N_EXPERTS
    blk_start = jnp.arange(n_blocks, dtype=jnp.int32) * bm
    blk_expert = jnp.sum((blk_start[:, None] >= ends[None, :]).astype(jnp.int32), axis=1)
    blk_expert = jnp.minimum(blk_expert, N_EXPERTS - 1)
    n_used = (ends[-1:] // bm).astype(jnp.int32)
    pos3 = pos.astype(jnp.int32).reshape(t_tok // tm, 1, 2 * tm)

    xs = _dispatch(h2t, pos3, n_blocks * bm, tm)
    ys = _expert_ffn(xs, blk_expert, n_used, lw, bm)
    return _combine(ys, pos3, x1, route, tm, out_rows)


def _head_cols(w, per_head, n_heads, offset=0):
    k = w.shape[0]
    w = w.reshape(k, n_heads, per_head)
    w = jnp.pad(w, ((0, 0), (0, 0), (offset, HEAD_PAD - per_head - offset)))
    return w.reshape(k, n_heads * HEAD_PAD)


def _swap_rope(w):
    z = jnp.zeros_like(w[..., :MLA_NOPE])
    return jnp.concatenate([z, w[..., MLA_NOPE + HALF_ROPE:], w[..., MLA_NOPE:MLA_NOPE + HALF_ROPE]], axis=-1)


def _pad_lane_block(v, offset=0):
    return jnp.pad(v, ((0, 0),) * (v.ndim - 1) + ((offset, LANES - v.shape[-1] - offset),))


def _layer_weights(i, p, rope_cos, rope_sin):
    o = IN_OFFSETS
    w_in = p["w_in"][i]
    lw = {}
    lw["g_mix"] = p["g_mix"][i].reshape(1, D_MODEL)
    lw["w_u"] = w_in[:, :o[0]].astype(BF16)
    lw["w_cq"] = w_in[:, o[0]:o[1]].astype(BF16)
    w_kr = w_in[:, o[2]:o[3]]
    w_kr_sw = jnp.concatenate([w_kr[:, HALF_ROPE:], w_kr[:, :HALF_ROPE]], axis=1)
    lw["w_ckv"] = jnp.concatenate([w_in[:, o[1]:o[2]], _pad_lane_block(w_kr, MLA_NOPE),
                                   _pad_lane_block(w_kr_sw, MLA_NOPE)], axis=1).astype(BF16)
    lw["w_qm"] = w_in[:, o[3]:o[4]].astype(BF16)
    lw["w_gate"] = w_in[:, o[4]:].astype(BF16)
    lw["g_q_lora"] = p["g_q_lora"][i].reshape(1, Q_LORA)
    w_uq = p["w_uq"][i].reshape(Q_LORA, MLA_HEADS, MLA_QK)
    lw["w_uq"] = jnp.concatenate(
        [_head_cols(w_uq.reshape(Q_LORA, -1), MLA_QK, MLA_HEADS),
         _head_cols(_swap_rope(w_uq).reshape(Q_LORA, -1), MLA_QK, MLA_HEADS)], axis=1).astype(BF16)
    lw["g_kv_lora"] = p["g_kv_lora"][i].reshape(1, KV_LORA)
    w_ukv = p["w_ukv"][i].reshape(KV_LORA, MLA_HEADS, MLA_NOPE + MLA_V)
    lw["w_kn"] = _head_cols(w_ukv[:, :, :MLA_NOPE].reshape(KV_LORA, -1), MLA_NOPE, MLA_HEADS).astype(BF16)
    w_v = w_ukv[:, :, MLA_NOPE:].reshape(KV_LORA, MLA_HEADS // 2, 2, MLA_V)
    w_v = jnp.stack([jnp.pad(w_v[:, :, 0], ((0, 0), (0, 0), (0, HEAD_PAD - MLA_V))),
                     jnp.pad(w_v[:, :, 1], ((0, 0), (0, 0), (HEAD_PAD - MLA_V, 0)))], axis=2)
    lw["w_v"] = w_v.reshape(KV_LORA, MLA_WIDTH_PAD).astype(BF16)
    one_even = jnp.zeros((HEAD_PAD,), F32).at[MLA_V].set(1.0)
    one_odd = jnp.zeros((HEAD_PAD,), F32).at[0].set(1.0)
    lw["v_one"] = jnp.tile(jnp.concatenate([one_even, one_odd]), MLA_HEADS // 2).reshape(1, MLA_WIDTH_PAD)
    for name, g in (("q", p["g_mla_q"][i]), ("k", p["g_mla_k"][i])):
        lw["rope_a" + name] = rope_cos * _pad_lane_block(g)[None, :]
        lw["rope_b" + name] = rope_sin * _pad_lane_block(_swap_rope(g))[None, :]
    lw["g_mem_q"] = jnp.tile(p["g_mem_q"][i], MEM_HEADS).reshape(1, MEM_WIDTH) * (MEM_HEAD_DIM ** -0.5)
    lw["g_mem"] = p["g_mem"][i].reshape(1, D_MODEL)
    lw["w_mem_kv"] = p["w_mem_kv"][i].astype(BF16)
    lw["g_mem_k"] = p["g_mem_k"][i].reshape(1, MEM_HEAD_DIM)
    lw["w_glu"] = p["ssm_w_glu"][i].astype(BF16)
    lw["b_glu"] = p["ssm_b_glu"][i].reshape(1, 2 * SSM_WIDTH)
    lw["w_branch"] = p["w_branch"][i].astype(BF16)
    lw["w_out"] = p["w_out"][i].astype(BF16)
    lw["g_ffn"] = p["g_ffn"][i].reshape(1, D_MODEL)
    lw["ssm"] = _ssm_tables(p["ssm_a_re"][i], p["ssm_a_im"][i], p["ssm_log_dt"][i], p["ssm_b_re"][i],
                            p["ssm_b_im"][i], p["ssm_c_re"][i], p["ssm_c_im"][i], p["ssm_d"][i])
    lw["score_bound"] = (1.02 * MLA_QK ** 0.5 * LOG2E
                         * jnp.max(jnp.abs(p["g_mla_q"][i])) * jnp.max(jnp.abs(p["g_mla_k"][i])))
    if i % 2 == 0:
        w13 = p["ffn_w13"][i // 2]
        lw["ffn_w1"] = w13[:, :D_FF].astype(BF16)
        lw["ffn_w3"] = w13[:, D_FF:].astype(BF16)
        lw["ffn_w2"] = p["ffn_w2"][i // 2].astype(BF16)
    else:
        w13 = p["moe_w13"][i // 2]
        lw["moe_w1"] = w13[:, :, :D_FF_EXPERT].astype(BF16)
        lw["moe_w3"] = w13[:, :, D_FF_EXPERT:].astype(BF16)
        lw["moe_w2"] = p["moe_w2"][i // 2].astype(BF16)
        wr = jnp.pad(p["moe_router"][i // 2], ((0, 0), (0, LANES - N_EXPERTS)))
        lw["w_router_hi"], lw["w_router_lo"] = _split_bf16(wr)
    return lw


def _rope_tables(max_len):
    inv = 1.0 / (ROPE_BASE ** (jnp.arange(0, MLA_ROPE, 2, dtype=F32) / MLA_ROPE))
    ang = jnp.arange(max_len, dtype=F32)[:, None] * inv[None, :]
    cos, sin = jnp.cos(ang), jnp.sin(ang)
    ones = jnp.ones((max_len, MLA_NOPE), F32)
    zn = jnp.zeros((max_len, MLA_NOPE), F32)
    zp = jnp.zeros((max_len, HEAD_PAD - MLA_QK), F32)
    return (jnp.concatenate([ones, cos, cos, zp], axis=1),
            jnp.concatenate([zn, -sin, sin, zp], axis=1))


def _gcd_tile(limit, *sizes):
    t = limit
    while any(s % t for s in sizes):
        t //= 2
    return t


def kernel(x_prompt, x_sample, mem_prompt, mem_sample, g_mix, w_in, ssm_a_re, ssm_a_im, ssm_log_dt, ssm_b_re, ssm_b_im, ssm_c_re, ssm_c_im, ssm_d, ssm_w_glu, ssm_b_glu, g_q_lora, w_uq, g_kv_lora, w_ukv, g_mla_q, g_mla_k, g_mem, w_mem_kv, g_mem_q, g_mem_k, w_branch, w_out, g_ffn, ffn_w13, ffn_w2, moe_router, moe_w13, moe_w2):
    params = dict(g_mix=g_mix, w_in=w_in, ssm_a_re=ssm_a_re, ssm_a_im=ssm_a_im, ssm_log_dt=ssm_log_dt,
                  ssm_b_re=ssm_b_re, ssm_b_im=ssm_b_im, ssm_c_re=ssm_c_re, ssm_c_im=ssm_c_im, ssm_d=ssm_d,
                  ssm_w_glu=ssm_w_glu, ssm_b_glu=ssm_b_glu, g_q_lora=g_q_lora, w_uq=w_uq,
                  g_kv_lora=g_kv_lora, w_ukv=w_ukv, g_mla_q=g_mla_q, g_mla_k=g_mla_k, g_mem=g_mem,
                  w_mem_kv=w_mem_kv, g_mem_q=g_mem_q, g_mem_k=g_mem_k, w_branch=w_branch, w_out=w_out,
                  g_ffn=g_ffn, ffn_w13=ffn_w13, ffn_w2=ffn_w2, moe_router=moe_router, moe_w13=moe_w13,
                  moe_w2=moe_w2)
    depth = g_mix.shape[0]
    bp, lp, _ = x_prompt.shape
    bs, ls, _ = x_sample.shape
    tp, ts = bp * lp, bs * ls
    t_tok = tp + ts

    tm = _gcd_tile(TOKEN_TILE, lp, ls)
    attn_q_tile = _gcd_tile(ATTN_Q_TILE, lp, ls)
    attn_kv_tile = _gcd_tile(ATTN_KV_TILE, lp, ls)
    ffn_tile = _gcd_tile(FFN_TILE, lp, ls)
    npt = tp // tm
    lp_t, ls_t = lp // tm, ls // tm

    def pos_map(i):
        return jnp.where(i < npt, i % lp_t, (i - npt) % ls_t)

    def seq_map(i):
        return jnp.where(i < npt, i // lp_t, bp + (i - npt) // ls_t)

    rope_cos, rope_sin = _rope_tables(max(lp, ls))
    head_ind = (jnp.arange(MLA_WIDTH_PAD)[:, None] // HEAD_PAD == jnp.arange(LANES)[None, :])
    mem_bd = (jnp.arange(MEM_WIDTH)[:, None] // MEM_HEAD_DIM == jnp.arange(MEM_WIDTH)[None, :] // MEM_HEAD_DIM)
    tiles = dict(tm=tm, ffn_tile=ffn_tile, pos_map=pos_map, seq_map=seq_map,
                 ones=jnp.ones((LANES, LANES), BF16),
                 head_ind=head_ind.astype(BF16),
                 head_ind_t=jnp.concatenate([head_ind.T, head_ind.T], axis=0).astype(BF16),
                 mem_bd=mem_bd.astype(BF16),
                 moe_block=_gcd_tile(MOE_BLOCK, 2 * t_tok))

    x_parts = [x_prompt.reshape(tp, D_MODEL), x_sample.reshape(ts, D_MODEL)]
    mem = jnp.concatenate([mem_prompt.reshape(bp * MEM_TOKENS, D_MODEL),
                           mem_sample.reshape(bs * MEM_TOKENS, D_MODEL)], axis=0)
    seqs = [(bp, lp), (bs, ls)]

    for i in range(depth):
        lw = _layer_weights(i, params, rope_cos, rope_sin)
        u, q, k, v, qm, sg = _in_projection(x_parts, lw, tiles)

        y_ssm_raw = _ssm_call(u, lw["ssm"], seqs)

        def attend(running_max, score_dtype):
            return lambda q_, k_, v_: _flash_attention(q_, k_, v_, seqs, attn_q_tile, attn_kv_tile,
                                                       running_max, score_dtype)

        y_mla = lax.cond(
            lw["score_bound"] <= MAX_FP8_LOG2_SCORE,
            attend(False, jnp.float8_e4m3fn),
            lambda q_, k_, v_: lax.cond(lw["score_bound"] <= MAX_SAFE_LOG2_SCORE,
                                        attend(False, BF16), attend(True, BF16), q_, k_, v_),
            q, k, v)
        km, vm = _memory_kv(mem, lw)

        with_router = i % 2 == 1
        outs = _merge(x_parts, y_ssm_raw, y_mla, qm, km, vm, sg, lw, tiles, with_router)
        if with_router:
            x1, h2t, route, counts = outs
            out_rows = (tp, ts) if i == depth - 1 else (t_tok,)
            x_parts = _moe_ffn(h2t, x1, route, counts, lw, tiles, out_rows)
        else:
            x1, h2 = outs
            x_parts = [_dense_ffn(h2, x1, lw, tiles)]

    if len(x_parts) == 1:
        x_parts = [x_parts[0][:tp], x_parts[0][tp:]]
    return (x_parts[0].reshape(bp, lp, D_MODEL), x_parts[1].reshape(bs, ls, D_MODEL))
```
